```python
import jax
import jax.numpy as jnp
from jax import lax
import numpy as np

D_MODEL = 1024
BATCH = 8
SEQ = 2048
DEPTH = 1

CTX_LEN = 256
GRID_W = 64
N_MOD = 6
EPS = 1e-6

LRU_WIDTH = 1024
LRU_BLOCKS = 8
LRU_BLOCK = LRU_WIDTH // LRU_BLOCKS
CONV_WIDTH = 4
CONV_LEFT = 2
LRU_C = 8.0

HG_HEADS = 8
HG_DK = 128
HG_DV = 128
HG_QK_WIDTH = HG_HEADS * HG_DK
HG_V_WIDTH = HG_HEADS * HG_DV
HG_CHUNK = 64

N_EXPERTS = 32
TOP_K = 4
D_FF = 1024
SWIGLU_LIMIT = 7.0
SWIGLU_ALPHA = 1.702
MOE_BLOCK = 128

IN_SIZES = (LRU_WIDTH, LRU_WIDTH, HG_QK_WIDTH, HG_QK_WIDTH, HG_QK_WIDTH, HG_V_WIDTH, HG_V_WIDTH, D_MODEL, D_MODEL)
IN_COLS = sum(IN_SIZES)

kernel_name = "hybrid_rglru_hgrn2_moe_dit_block"


def rms_norm(x, g):
    xf = x.astype(jnp.float32)
    y = xf * lax.rsqrt(jnp.mean(xf * xf, axis=-1, keepdims=True) + EPS)
    return (y * g.astype(jnp.float32)).astype(x.dtype)


def split_in(p):
    offs = np.cumsum(IN_SIZES)[:-1].tolist()
    return jnp.split(p, offs, axis=-1)


def split_heads(t):
    return t.reshape(t.shape[0], t.shape[1], HG_HEADS, t.shape[2] // HG_HEADS)


def short_conv(u, w, b, row_len):
    L = u.shape[1]
    up = jnp.pad(u, ((0, 0), (CONV_LEFT, CONV_WIDTH - 1 - CONV_LEFT), (0, 0)))
    col = jnp.arange(L) % row_len
    out = b
    for k in range(CONV_WIDTH):
        d = k - CONV_LEFT
        valid = ((col + d >= 0) & (col + d < row_len))[None, :, None]
        out = out + jnp.where(valid, up[:, k:k + L], 0) * w[k]
    return out


def linear_scan(a, b, h0, reverse):
    if reverse:
        a, b = a[:, ::-1], b[:, ::-1]
    b = b.at[:, 0].add(a[:, 0] * h0)
    _, h = lax.associative_scan(lambda l, r: (l[0] * r[0], r[0] * l[1] + r[1]), (a, b), axis=1)
    h_last = h[:, -1]
    if reverse:
        h = h[:, ::-1]
    return h, h_last


def rglru_direction(u, h0, wa, ba, wx, bx, lam, reverse):
    B, L, W = u.shape
    ub = u.reshape(B, L, LRU_BLOCKS, LRU_BLOCK)
    r = jax.nn.sigmoid(jnp.einsum("blgi,gij->blgj", ub, wa) + ba).reshape(B, L, W)
    i = jax.nn.sigmoid(jnp.einsum("blgi,gij->blgj", ub, wx) + bx).reshape(B, L, W)
    log_a = -LRU_C * r * jax.nn.softplus(-lam)
    a = jnp.exp(log_a)
    mult = jnp.sqrt(-jnp.expm1(2.0 * log_a))
    return linear_scan(a, mult * (i * u), h0, reverse)


def rglru_mixer(ax_c, ax_l, conv_w, conv_b, wa, ba, wx, bx, lam):
    u_c = short_conv(ax_c, conv_w, conv_b, ax_c.shape[1])
    u_l = short_conv(ax_l, conv_w, conv_b, GRID_W)
    h0 = jnp.zeros((u_c.shape[0], LRU_WIDTH), u_c.dtype)
    y_c, y_l = [], []
    for d in range(2):
        rev = d == 1
        hc, hc_last = rglru_direction(u_c, h0, wa[d], ba[d], wx[d], bx[d], lam[d], rev)
        hl, _ = rglru_direction(u_l, hc_last, wa[d], ba[d], wx[d], bx[d], lam[d], rev)
        y_c.append(hc)
        y_l.append(hl)
    return y_c[0] + y_c[1], y_l[0] + y_l[1]


def hgrn2_chunk_scan(q, logf, k, v, s0, reverse):
    if reverse:
        q, logf, k, v = (t[:, ::-1] for t in (q, logf, k, v))
    B, L, H, _ = q.shape
    n_chunks = L // HG_CHUNK
    q, logf, k, v = (t.reshape(B, n_chunks, HG_CHUNK, H, t.shape[-1]) for t in (q, logf, k, v))
    g_cum = jnp.cumsum(logf, axis=2)
    g_mid = g_cum[:, :, HG_CHUNK // 2 - 1:HG_CHUNK // 2]
    g_end = g_cum[:, :, -1]
    scores = jnp.einsum("bnthd,bnshd->bnhts", q * jnp.exp(g_cum - g_mid), k * jnp.exp(g_mid - g_cum))
    lower_tri = jnp.tril(jnp.ones((HG_CHUNK, HG_CHUNK), bool))
    scores = jnp.where(lower_tri, scores, 0)
    o_intra = jnp.einsum("bnhts,bnshe->bnthe", scores, v)
    upd = jnp.einsum("bnshd,bnshe->bnhde", k * jnp.exp(g_end[:, :, None] - g_cum), v)
    decay = jnp.exp(g_end)

    def step(s, inp):
        dec, u = inp
        return dec[..., None] * s + u, s

    s_last, s_prev = lax.scan(step, s0, (jnp.moveaxis(decay, 1, 0), jnp.moveaxis(upd, 1, 0)))
    o_inter = jnp.einsum("bnthd,nbhde->bnthe", q * jnp.exp(g_cum), s_prev)
    o = (o_intra + o_inter).reshape(B, L, H, v.shape[-1])
    if reverse:
        o = o[:, ::-1]
    return o, s_last


def hgrn2_forget(f_raw, lb):
    f = lb + (1.0 - lb) * jax.nn.sigmoid(f_raw.astype(jnp.float32))
    return split_heads(jnp.log(f)), split_heads(1.0 - f)


def hgrn2_mixer(p_c, p_l, lb, norm_g):
    q_c, ff_c, fb_c, i_c, g_c = p_c
    q_l, ff_l, fb_l, i_l, g_l = p_l
    qh_c, vh_c = split_heads(jax.nn.silu(q_c)), split_heads(i_c)
    qh_l, vh_l = split_heads(jax.nn.silu(q_l)), split_heads(i_l)
    s0 = jnp.zeros((q_c.shape[0], HG_HEADS, HG_DK, HG_DV), jnp.float32)
    o_c, o_l = [], []
    for d, (f_c, f_l) in enumerate(((ff_c, ff_l), (fb_c, fb_l))):
        rev = d == 1
        logf_c, k_c = hgrn2_forget(f_c, lb)
        logf_l, k_l = hgrn2_forget(f_l, lb)
        oc, s_ctx = hgrn2_chunk_scan(qh_c, logf_c, k_c, vh_c, s0, rev)
        ol, _ = hgrn2_chunk_scan(qh_l, logf_l, k_l, vh_l, s_ctx, rev)
        o_c.append(oc)
        o_l.append(ol)

    def readout(o, g):
        y = rms_norm(o, norm_g) * jax.nn.silu(split_heads(g))
        return y.reshape(g.shape).astype(g.dtype)

    return readout(o_c[0] + o_c[1], g_c), readout(o_l[0] + o_l[1], g_l)


def moe(t, rw, rb, w1, b1, w2, b2):
    T, D = t.shape
    logits = (t @ rw + rb).astype(jnp.float32)
    top_val, top_idx = lax.top_k(logits, TOP_K)
    gates = jax.nn.softmax(top_val, axis=-1)
    n_assign = T * TOP_K
    e_flat = top_idx.reshape(n_assign)
    order = jnp.argsort(e_flat)
    e_sorted = e_flat[order]
    tok_sorted = (order // TOP_K).astype(jnp.int32)
    gate_sorted = gates.reshape(n_assign)[order]
    counts = jnp.bincount(e_flat, length=N_EXPERTS)
    padded = (counts + MOE_BLOCK - 1) // MOE_BLOCK * MOE_BLOCK
    start = jnp.cumsum(counts) - counts
    pend = jnp.cumsum(padded)
    pstart = pend - padded
    dest = pstart[e_sorted] + jnp.arange(n_assign) - start[e_sorted]
    n_blocks = -(-n_assign // MOE_BLOCK) + N_EXPERTS
    n_slots = n_blocks * MOE_BLOCK
    slot_tok = jnp.zeros((n_slots,), jnp.int32).at[dest].set(tok_sorted)
    slot_gate = jnp.zeros((n_slots,), jnp.float32).at[dest].set(gate_sorted)
    block_start = jnp.arange(n_blocks) * MOE_BLOCK
    block_expert = jnp.minimum(jnp.sum(block_start[:, None] >= pend[None, :], axis=1), N_EXPERTS - 1)

    def expert_block(args):
        tok, e = args
        h = t[tok] @ w1[e] + b1[e]
        hg = jnp.minimum(h[:, 0::2], SWIGLU_LIMIT)
        hu = jnp.clip(h[:, 1::2], -SWIGLU_LIMIT, SWIGLU_LIMIT)
        act = hg * jax.nn.sigmoid(SWIGLU_ALPHA * hg) * (hu + 1)
        return act @ w2[e] + b2[e]

    y = lax.map(expert_block, (slot_tok.reshape(n_blocks, MOE_BLOCK), block_expert))
    y = y.reshape(n_slots, D) * slot_gate[:, None].astype(t.dtype)
    return jnp.zeros_like(t).at[slot_tok].add(y)


def mixer_sublayer(x, xc, mod_l, mod_c, norm_g, w_in, conv_w, conv_b, wa, ba, wx, bx, lam,
                   lb, hg_norm_g, w_ba, w_bb, w_out, update_ctx):
    h_l = rms_norm(x, norm_g) * (1 + mod_l[1]) + mod_l[0]
    h_c = rms_norm(xc, norm_g) * (1 + mod_c[1]) + mod_c[0]
    ax_l, ag_l, q_l, ff_l, fb_l, i_l, g_l, ma_l, mb_l = split_in(h_l @ w_in)
    ax_c, ag_c, q_c, ff_c, fb_c, i_c, g_c, ma_c, mb_c = split_in(h_c @ w_in)
    ya_c, ya_l = rglru_mixer(ax_c, ax_l, conv_w, conv_b, wa, ba, wx, bx, lam)
    yb_c, yb_l = hgrn2_mixer((q_c, ff_c, fb_c, i_c, g_c), (q_l, ff_l, fb_l, i_l, g_l), lb, hg_norm_g)

    def merge(ya, a_gate, yb, ma, mb):
        za = (ya * jax.nn.gelu(a_gate)) @ w_ba
        zb = yb @ w_bb
        return (jax.nn.sigmoid(ma) * za + jax.nn.sigmoid(mb) * zb) @ w_out

    x = x + mod_l[2] * merge(ya_l, ag_l, yb_l, ma_l, mb_l)
    if update_ctx:
        xc = xc + mod_c[2] * merge(ya_c, ag_c, yb_c, ma_c, mb_c)
    return x, xc


def moe_sublayer(x, xc, mod_l, mod_c, norm_g, rw, rb, w1, b1, w2, b2, update_ctx):
    h_l = (rms_norm(x, norm_g) * (1 + mod_l[4]) + mod_l[3]).reshape(-1, D_MODEL)
    if update_ctx:
        h_c = (rms_norm(xc, norm_g) * (1 + mod_c[4]) + mod_c[3]).reshape(-1, D_MODEL)
        y = moe(jnp.concatenate([h_l, h_c], axis=0), rw, rb, w1, b1, w2, b2)
        n_l = h_l.shape[0]
        x = x + mod_l[5] * y[:n_l].reshape(x.shape)
        xc = xc + mod_c[5] * y[n_l:].reshape(xc.shape)
    else:
        x = x + mod_l[5] * moe(h_l, rw, rb, w1, b1, w2, b2).reshape(x.shape)
    return x, xc


def setup_inputs(seed: int = 0) -> dict:
    key = jax.random.key(seed)
    ks = jax.random.split(key, 32)
    D = D_MODEL

    def nrm(k, shape, scale):
        return jax.random.normal(k, shape, jnp.float32) * scale

    lam_u = jax.random.uniform(ks[14], (DEPTH, 2, LRU_WIDTH), jnp.float32, 0.9, 0.999)
    a_base = lam_u ** (1.0 / LRU_C)
    lru_lam = jnp.log(a_base) - jnp.log1p(-a_base)
    return {
        "x": nrm(ks[0], (BATCH, SEQ, D), 1.0),
        "c": nrm(ks[1], (BATCH, D), 1.0),
        "ctx": nrm(ks[2], (BATCH, CTX_LEN, D), 1.0),
        "c_ctx": nrm(ks[3], (D,), 1.0),
        "ada_w": nrm(ks[4], (DEPTH, D, N_MOD * D), 0.5 * D ** -0.5),
        "ada_b": nrm(ks[5], (DEPTH, N_MOD * D), 0.01),
        "norm1_g": 1.0 + nrm(ks[6], (DEPTH, D), 0.01),
        "norm2_g": 1.0 + nrm(ks[7], (DEPTH, D), 0.01),
        "w_in": nrm(ks[8], (DEPTH, D, IN_COLS), D ** -0.5),
        "lru_conv_w": nrm(ks[9], (DEPTH, CONV_WIDTH, LRU_WIDTH), CONV_WIDTH ** -0.5),
        "lru_conv_b": nrm(ks[10], (DEPTH, LRU_WIDTH), 0.01),
        "lru_wa": nrm(ks[11], (DEPTH, 2, LRU_BLOCKS, LRU_BLOCK, LRU_BLOCK), LRU_BLOCK ** -0.5),
        "lru_ba": nrm(ks[12], (DEPTH, 2, LRU_BLOCKS, LRU_BLOCK), 0.01),
        "lru_wx": nrm(ks[13], (DEPTH, 2, LRU_BLOCKS, LRU_BLOCK, LRU_BLOCK), LRU_BLOCK ** -0.5),
        "lru_bx": nrm(ks[15], (DEPTH, 2, LRU_BLOCKS, LRU_BLOCK), 0.01),
        "lru_lam": lru_lam,
        "hg_lb_logits": nrm(ks[16], (DEPTH + 1, HG_QK_WIDTH), 0.1),
        "hg_norm_g": 1.0 + nrm(ks[17], (DEPTH, HG_DV), 0.01),
        "w_branch_a": nrm(ks[18], (DEPTH, LRU_WIDTH, D), LRU_WIDTH ** -0.5),
        "w_branch_b": nrm(ks[19], (DEPTH, HG_V_WIDTH, D), HG_V_WIDTH ** -0.5),
        "w_out": nrm(ks[20], (DEPTH, D, D), D ** -0.5),
        "router_w": nrm(ks[21], (DEPTH, D, N_EXPERTS), D ** -0.5),
        "router_b": nrm(ks[22], (DEPTH, N_EXPERTS), 0.01),
        "moe_w1": nrm(ks[23], (DEPTH, N_EXPERTS, D, 2 * D_FF), D ** -0.5),
        "moe_b1": nrm(ks[24], (DEPTH, N_EXPERTS, 2 * D_FF), 0.01),
        "moe_w2": nrm(ks[25], (DEPTH, N_EXPERTS, D_FF, D), D_FF ** -0.5),
        "moe_b2": nrm(ks[26], (DEPTH, N_EXPERTS, D), 0.01),
        "final_g": 1.0 + nrm(ks[27], (D,), 0.01),
    }


def reference(x, c, ctx, c_ctx, ada_w, ada_b, norm1_g, norm2_g, w_in, lru_conv_w, lru_conv_b,
              lru_wa, lru_ba, lru_wx, lru_bx, lru_lam, hg_lb_logits, hg_norm_g, w_branch_a,
              w_branch_b, w_out, router_w, router_b, moe_w1, moe_b1, moe_w2, moe_b2, final_g):
    B = x.shape[0]
    lb_all = jnp.cumsum(jax.nn.softmax(hg_lb_logits.astype(jnp.float32), axis=0), axis=0)
    xc = ctx
    for layer in range(DEPTH):
        update_ctx = layer < DEPTH - 1
        mod_l = (jax.nn.silu(c) @ ada_w[layer] + ada_b[layer]).reshape(B, N_MOD, D_MODEL)
        mod_l = jnp.transpose(mod_l, (1, 0, 2))[:, :, None, :]
        mod_c = (jax.nn.silu(c_ctx) @ ada_w[layer] + ada_b[layer]).reshape(N_MOD, D_MODEL)[:, None, None, :]
        x, xc = mixer_sublayer(x, xc, mod_l, mod_c, norm1_g[layer], w_in[layer],
                               lru_conv_w[layer], lru_conv_b[layer], lru_wa[layer], lru_ba[layer],
                               lru_wx[layer], lru_bx[layer], lru_lam[layer], lb_all[layer],
                               hg_norm_g[layer], w_branch_a[layer], w_branch_b[layer], w_out[layer],
                               update_ctx)
        x, xc = moe_sublayer(x, xc, mod_l, mod_c, norm2_g[layer], router_w[layer], router_b[layer],
                             moe_w1[layer], moe_b1[layer], moe_w2[layer], moe_b2[layer], update_ctx)
    return rms_norm(x, final_g)
```

```python
import functools

import jax
import jax.numpy as jnp
from jax import lax
from jax.experimental import pallas as pl
from jax.experimental.pallas import tpu as pltpu

F32 = jnp.float32
BF16 = jnp.bfloat16
I32 = jnp.int32
HIGHEST = lax.Precision.HIGHEST

EPS = 1e-6
N_MOD = 6
GRID_W = 64
CONV_LEFT = 2
CONV_WIDTH = 4
LRU_C = 8.0
LRU_BLOCK = 128
HG_DK = 128
HG_CHUNK = 64
TOP_K = 4
SWIGLU_LIMIT = 7.0
SWIGLU_ALPHA = 1.702
MOE_BLOCK = 256
SUBLANES = 8
HALO = 8


def _params(sem, vmem_mb):
    return pltpu.CompilerParams(dimension_semantics=sem, vmem_limit_bytes=vmem_mb * 1024 * 1024)


def _dot(a, b):
    return jnp.dot(a, b, preferred_element_type=F32)


def _dot_nt(a, b):
    return lax.dot_general(a, b, (((1,), (1,)), ((), ())), preferred_element_type=F32)


def _dot_tn(a, b):
    return lax.dot_general(a, b, (((0,), (0,)), ((), ())), preferred_element_type=F32)


def _sigmoid(x):
    return jax.nn.sigmoid(x)


def _silu(x):
    return x * jax.nn.sigmoid(x)


def _rms(x):
    return x * lax.rsqrt(jnp.mean(x * x, axis=-1, keepdims=True) + EPS)


def _mod_kernel(c_ref, w_ref, b_ref, o_ref):
    s = _silu(c_ref[...])
    o_ref[...] = jnp.dot(s, w_ref[...], preferred_element_type=F32, precision=HIGHEST) + b_ref[...]


def _modulation(c_all, ada_w, ada_b):
    m, d = c_all.shape
    n = ada_w.shape[1]
    tn = 1024
    return pl.pallas_call(
        _mod_kernel,
        grid=(n // tn,),
        in_specs=[
            pl.BlockSpec((m, d), lambda j: (0, 0)),
            pl.BlockSpec((d, tn), lambda j: (0, j)),
            pl.BlockSpec((1, tn), lambda j: (0, j)),
        ],
        out_specs=pl.BlockSpec((m, tn), lambda j: (0, j)),
        out_shape=jax.ShapeDtypeStruct((m, n), F32),
        compiler_params=_params(("arbitrary",), 32),
        name="mod",
    )(c_all, ada_w, ada_b.reshape(1, n))


def _prenorm_kernel(x_ref, shift_ref, scale_ref, g_ref, o_ref):
    y = _rms(x_ref[...]) * g_ref[...]
    o_ref[...] = (y * (1.0 + scale_ref[...]) + shift_ref[...]).astype(o_ref.dtype)


def _prenorm(x, shift, scale, g, tm):
    b, l, d = x.shape
    per_batch = shift.shape[0] == b and b > 1
    mod_map = (lambda bi, i: (bi, 0, 0)) if per_batch else (lambda bi, i: (0, 0, 0))
    return pl.pallas_call(
        _prenorm_kernel,
        grid=(b, l // tm),
        in_specs=[
            pl.BlockSpec((None, tm, d), lambda bi, i: (bi, i, 0)),
            pl.BlockSpec((None, 1, d), mod_map),
            pl.BlockSpec((None, 1, d), mod_map),
            pl.BlockSpec((1, d), lambda bi, i: (0, 0)),
        ],
        out_specs=pl.BlockSpec((None, tm, d), lambda bi, i: (bi, i, 0)),
        out_shape=jax.ShapeDtypeStruct((b, l, d), BF16),
        compiler_params=_params(("arbitrary", "arbitrary"), 32),
        name="prenorm",
    )(x, shift, scale, g.reshape(1, d))


def _mm_kernel(x_ref, w_ref, o_ref):
    o_ref[...] = _dot(x_ref[...], w_ref[...]).astype(o_ref.dtype)


def _matmul(x2d, w, tm, out_dtype):
    m, k = x2d.shape
    n = w.shape[1]
    return pl.pallas_call(
        _mm_kernel,
        grid=(m // tm,),
        in_specs=[
            pl.BlockSpec((tm, k), lambda i: (i, 0)),
            pl.BlockSpec((k, n), lambda i: (0, 0)),
        ],
        out_specs=pl.BlockSpec((tm, n), lambda i: (i, 0)),
        out_shape=jax.ShapeDtypeStruct((m, n), out_dtype),
        compiler_params=_params(("arbitrary",), 40),
        name="inproj_lru",
    )(x2d, w)


def _lru_kernel(axl_ref, agl_ref, axc_ref, cw_ref, cb_ref, wg_ref, bg_ref, lam_ref, o_ref,
                xf_ref, yacc_ref, su_ref, sa_ref, sb_ref, sh_ref, *, nb, cx, t):
    tb = GRID_W
    lat0 = cx + 2 * HALO
    ctx0 = HALO
    rows = cx + t + 3 * HALO
    gw = 2 * LRU_BLOCK

    zeros_h = jnp.zeros((HALO, LRU_BLOCK), F32)
    for b in range(nb):
        xf_ref[b, 0:HALO, :] = zeros_h
        xf_ref[b, ctx0:ctx0 + cx, :] = axc_ref[b].astype(F32)
        xf_ref[b, ctx0 + cx:lat0, :] = zeros_h
        xf_ref[b, lat0:lat0 + t, :] = axl_ref[b].astype(F32)
        xf_ref[b, lat0 + t:rows, :] = zeros_h

    cw = cw_ref[...]
    cb = cb_ref[...]
    lam = lam_ref[...]
    sp = jnp.maximum(-lam, 0.0) + jnp.log1p(jnp.exp(-jnp.abs(lam)))
    tloc = lax.broadcasted_iota(I32, (tb, LRU_BLOCK), 0)

    def fill_gates(base, d, row_local):
        for b in range(nb):
            acc = jnp.broadcast_to(cb, (tb, LRU_BLOCK))
            for k in range(CONV_WIDTH):
                off = k - CONV_LEFT
                xs = xf_ref[b, pl.ds(base + off, tb), :]
                if row_local and off != 0:
                    ok = (tloc + off >= 0) & (tloc + off < tb)
                    xs = jnp.where(ok, xs, 0.0)
                acc = acc + xs * cw[k:k + 1, :]
            su_ref[pl.ds(b, tb, stride=nb), :] = acc
        u = su_ref[...]
        z = _dot(u.astype(BF16), wg_ref[:, d * gw:(d + 1) * gw]) + bg_ref[:, d * gw:(d + 1) * gw]
        r = _sigmoid(z[:, :LRU_BLOCK])
        i = _sigmoid(z[:, LRU_BLOCK:])
        log_a = (-LRU_C) * r * sp[d:d + 1, :]
        a = jnp.exp(log_a)
        mult = jnp.sqrt(-jnp.tanh(log_a) * (a * a + 1.0))
        sa_ref[...] = a
        sb_ref[...] = mult * (i * u)

    def scan_block(h, reverse):
        order = range(tb - 1, -1, -1) if reverse else range(tb)
        for s in order:
            h = sa_ref[s * nb:(s + 1) * nb, :] * h + sb_ref[s * nb:(s + 1) * nb, :]
            sh_ref[s * nb:(s + 1) * nb, :] = h
        return h

    def sweep(d):
        reverse = d == 1
        n_c, n_l = cx // tb, t // tb

        def ctx_body(j, h):
            jj = (n_c - 1 - j) if reverse else j
            base = pl.multiple_of(ctx0 + jj * tb, SUBLANES)
            fill_gates(base, d, False)
            return scan_block(h, reverse)

        def lat_body(j, h):
            jj = (n_l - 1 - j) if reverse else j
            r0 = pl.multiple_of(jj * tb, tb)
            fill_gates(pl.multiple_of(lat0 + r0, SUBLANES), d, True)
            h = scan_block(h, reverse)
            for b in range(nb):
                yb = sh_ref[pl.ds(b, tb, stride=nb), :]
                if not reverse:
                    yacc_ref[b, pl.ds(r0, tb), :] = yb
                else:
                    gate = jax.nn.gelu(agl_ref[b, pl.ds(r0, tb), :].astype(F32))
                    o_ref[b, pl.ds(r0, tb), :] = ((yacc_ref[b, pl.ds(r0, tb), :] + yb) * gate).astype(o_ref.dtype)
            return h

        h0 = jnp.zeros((nb, LRU_BLOCK), F32)
        h1 = lax.fori_loop(0, n_c, ctx_body, h0)
        lax.fori_loop(0, n_l, lat_body, h1)

    sweep(0)
    sweep(1)


def _lru(axl, axc, conv_w, conv_b, wg, bg, lam, nb, cx, t):
    w = conv_w.shape[1]
    g = w // LRU_BLOCK
    kern = functools.partial(_lru_kernel, nb=nb, cx=cx, t=t)
    return pl.pallas_call(
        kern,
        grid=(g,),
        in_specs=[
            pl.BlockSpec((nb, t, LRU_BLOCK), lambda j: (0, 0, j)),
            pl.BlockSpec((nb, t, LRU_BLOCK), lambda j: (0, 0, g + j)),
            pl.BlockSpec((nb, cx, LRU_BLOCK), lambda j: (0, 0, j)),
            pl.BlockSpec((CONV_WIDTH, LRU_BLOCK), lambda j: (0, j)),
            pl.BlockSpec((1, LRU_BLOCK), lambda j: (0, j)),
            pl.BlockSpec((None, LRU_BLOCK, 4 * LRU_BLOCK), lambda j: (j, 0, 0)),
            pl.BlockSpec((None, 1, 4 * LRU_BLOCK), lambda j: (j, 0, 0)),
            pl.BlockSpec((2, LRU_BLOCK), lambda j: (0, j)),
        ],
        out_specs=pl.BlockSpec((nb, t, LRU_BLOCK), lambda j: (0, 0, j)),
        out_shape=jax.ShapeDtypeStruct((nb, t, w), BF16),
        scratch_shapes=[
            pltpu.VMEM((nb, cx + t + 3 * HALO, LRU_BLOCK), F32),
            pltpu.VMEM((nb, t, LRU_BLOCK), F32),
            pltpu.VMEM((GRID_W * nb, LRU_BLOCK), F32),
            pltpu.VMEM((GRID_W * nb, LRU_BLOCK), F32),
            pltpu.VMEM((GRID_W * nb, LRU_BLOCK), F32),
            pltpu.VMEM((GRID_W * nb, LRU_BLOCK), F32),
        ],
        compiler_params=_params(("arbitrary",), 56),
        name="lru",
    )(axl, axl, axc, conv_w, conv_b.reshape(1, w), wg, bg, lam)


def _hgrn_kernel(hc_ref, hl_ref, w_ref, lb_ref, ng_ref, o_ref,
                 p_ref, oacc_ref, aq_ref, upd_ref, dec_ref, *, cx, t):
    c = HG_CHUNK
    dk = HG_DK
    tc = cx + t
    n_c, n_l = cx // c, t // c
    n_all = n_c + n_l

    p_ref[0:cx, :] = _dot(hc_ref[...], w_ref[...])
    rows = 512 if t % 512 == 0 else c

    def proj_body(i, carry):
        r = pl.multiple_of(i * rows, rows)
        p_ref[pl.ds(cx + r, rows), :] = _dot(hl_ref[pl.ds(r, rows), :], w_ref[...])
        return carry

    lax.fori_loop(0, t // rows, proj_body, 0)

    lb = lb_ref[...]
    one_m_lb = 1.0 - lb
    ri = lax.broadcasted_iota(I32, (c, c), 0)
    ci = lax.broadcasted_iota(I32, (c, c), 1)
    lower = ri >= ci
    upper = ci >= ri

    def direction(d):
        reverse = d == 1
        fcol = (1 + d) * dk
        keep = upper if reverse else lower
        cum = keep.astype(BF16)
        mid = c // 2 if reverse else c // 2 - 1
        end = 0 if reverse else c - 1

        def intra(j, carry):
            r0 = pl.multiple_of(j * c, c)
            q_raw = p_ref[pl.ds(r0, c), 0:dk]
            f_raw = p_ref[pl.ds(r0, c), fcol:fcol + dk]
            v = p_ref[pl.ds(r0, c), 3 * dk:4 * dk].astype(BF16)
            q = _silu(q_raw)
            f = lb + one_m_lb * _sigmoid(f_raw)
            logf = jnp.log(f)
            k = 1.0 - f
            hi = logf.astype(BF16)
            lo = (logf - hi.astype(F32)).astype(BF16)
            gc = _dot(cum, hi) + _dot(cum, lo)
            g_mid = gc[mid:mid + 1, :]
            g_end = gc[end:end + 1, :]
            a = (q * jnp.exp(gc - g_mid)).astype(BF16)
            bm = (k * jnp.exp(g_mid - gc)).astype(BF16)
            sc = jnp.where(keep, _dot_nt(a, bm), 0.0).astype(BF16)
            o_intra = _dot(sc, v)
            bk = (k * jnp.exp(g_end - gc)).astype(BF16)
            upd_ref[j] = _dot_tn(v, bk)
            dec_ref[j] = jnp.broadcast_to(jnp.exp(g_end), (SUBLANES, dk))
            aq_ref[pl.ds(r0, c), :] = (q * jnp.exp(gc)).astype(BF16)
            if reverse:
                oacc_ref[pl.ds(r0, c), :] += o_intra
            else:
                oacc_ref[pl.ds(r0, c), :] = o_intra
            return carry

        lax.fori_loop(0, n_all, intra, 0, unroll=2)

        def inter(jj, st):
            r0 = pl.multiple_of(jj * c, c)
            oacc_ref[pl.ds(r0, c), :] += _dot_nt(aq_ref[pl.ds(r0, c), :], st.astype(BF16))
            return st * dec_ref[jj][0:1, :] + upd_ref[jj]

        def ctx_body(j, st):
            return inter((n_c - 1 - j) if reverse else j, st)

        def lat_body(j, st):
            return inter(n_c + ((n_l - 1 - j) if reverse else j), st)

        st = lax.fori_loop(0, n_c, ctx_body, jnp.zeros((dk, dk), F32), unroll=2)
        lax.fori_loop(0, n_l, lat_body, st, unroll=2)

    direction(0)
    direction(1)

    ng = ng_ref[...]

    def readout(i, carry):
        r = pl.multiple_of(i * rows, rows)
        o = oacc_ref[pl.ds(cx + r, rows), :]
        g = p_ref[pl.ds(cx + r, rows), 4 * dk:5 * dk]
        o_ref[pl.ds(r, rows), :] = (_rms(o) * ng * _silu(g)).astype(o_ref.dtype)
        return carry

    lax.fori_loop(0, t // rows, readout, 0)


def _hgrn(hc, hl, w_hg, lb, norm_g):
    b, cx, d = hc.shape
    t = hl.shape[1]
    heads = w_hg.shape[0]
    dk = HG_DK
    n_all = (cx + t) // HG_CHUNK
    kern = functools.partial(_hgrn_kernel, cx=cx, t=t)
    return pl.pallas_call(
        kern,
        grid=(b, heads),
        in_specs=[
            pl.BlockSpec((None, cx, d), lambda bi, h: (bi, 0, 0)),
            pl.BlockSpec((None, t, d), lambda bi, h: (bi, 0, 0)),
            pl.BlockSpec((None, d, 5 * dk), lambda bi, h: (h, 0, 0)),
            pl.BlockSpec((None, 1, dk), lambda bi, h: (h, 0, 0)),
            pl.BlockSpec((1, dk), lambda bi, h: (0, 0)),
        ],
        out_specs=pl.BlockSpec((None, t, dk), lambda bi, h: (bi, 0, h)),
        out_shape=jax.ShapeDtypeStruct((b, t, heads * dk), BF16),
        scratch_shapes=[
            pltpu.VMEM((cx + t, 5 * dk), F32),
            pltpu.VMEM((cx + t, dk), F32),
            pltpu.VMEM((cx + t, dk), BF16),
            pltpu.VMEM((n_all, dk, dk), F32),
            pltpu.VMEM((n_all, SUBLANES, dk), F32),
        ],
        compiler_params=_params(("arbitrary", "arbitrary"), 48),
        name="hgrn",
    )(hc, hl, w_hg, lb, norm_g.reshape(1, dk))


def _merge_kernel(x_ref, h_ref, ya_ref, yb_ref, wm_ref, wa_ref, wb_ref, wo_ref,
                  gate_ref, shift_ref, scale_ref, g2_ref, rwt_ref, rb_ref,
                  x1_ref, h2_ref, idx_ref, gates_ref, rank_ref, cnt_ref, carry_ref, *, tm, d, ne):
    first = (pl.program_id(0) == 0) & (pl.program_id(1) == 0)

    @pl.when(first)
    def _():
        carry_ref[...] = jnp.zeros_like(carry_ref)

    m = _dot(h_ref[...], wm_ref[...])
    za = _dot(ya_ref[...], wa_ref[...])
    zb = _dot(yb_ref[...], wb_ref[...])
    mix = _sigmoid(m[:, :d]) * za + _sigmoid(m[:, d:]) * zb
    out = _dot(mix.astype(BF16), wo_ref[...])
    x1 = x_ref[...] + gate_ref[...] * out
    x1_ref[...] = x1
    h2 = _rms(x1) * g2_ref[...] * (1.0 + scale_ref[...]) + shift_ref[...]
    h2_ref[...] = h2

    logits = lax.dot_general(rwt_ref[...], h2, (((1,), (1,)), ((), ())),
                             preferred_element_type=F32, precision=HIGHEST) + rb_ref[...]
    eid = lax.broadcasted_iota(I32, (ne, tm), 0)
    neg = jnp.float32(-jnp.inf)
    work = logits
    hot = jnp.zeros((ne, tm), F32)
    vals, idxs, sels = [], [], []
    for _ in range(TOP_K):
        mx = jnp.max(work, axis=0, keepdims=True)
        ix = jnp.min(jnp.where(work == mx, eid, ne), axis=0, keepdims=True)
        sel = eid == ix
        work = jnp.where(sel, neg, work)
        hot = hot + sel.astype(F32)
        vals.append(mx)
        idxs.append(ix)
        sels.append(sel)
    ex = [jnp.exp(v - vals[0]) for v in vals]
    den = ex[0] + ex[1] + ex[2] + ex[3]
    gts = [e / den for e in ex]

    ti = lax.broadcasted_iota(I32, (tm, tm), 0)
    tj = lax.broadcasted_iota(I32, (tm, tm), 1)
    strict = (ti < tj).astype(BF16)
    before = _dot(hot.astype(BF16), strict) + carry_ref[...]
    rks = [jnp.sum(jnp.where(s, before, 0.0), axis=0, keepdims=True) for s in sels]
    carry_ref[...] = carry_ref[...] + jnp.sum(hot, axis=1, keepdims=True)

    row = lax.broadcasted_iota(I32, (SUBLANES, tm), 0)

    def pack(parts, zero):
        acc = jnp.full((SUBLANES, tm), zero, parts[0].dtype)
        for k, p in enumerate(parts):
            acc = jnp.where(row == k, jnp.broadcast_to(p, (SUBLANES, tm)), acc)
        return acc

    idx_ref[...] = pack(idxs, 0)
    gates_ref[...] = pack(gts, 0.0)
    rank_ref[...] = pack(rks, 0.0).astype(I32)
    cnt_ref[...] = jnp.broadcast_to(carry_ref[...], cnt_ref.shape)


def _merge(x, hl, ya, yb, w_m, w_ba, w_bb, w_out, gate2, shift3, scale4, g2, rwt, rb, tm):
    b, t, d = x.shape
    ne = rwt.shape[0]
    nt = t // tm
    tok = b * t
    kern = functools.partial(_merge_kernel, tm=tm, d=d, ne=ne)
    row_spec = pl.BlockSpec((None, tm, d), lambda bi, i: (bi, i, 0))
    mod_spec = pl.BlockSpec((None, 1, d), lambda bi, i: (bi, 0, 0))
    flat = lambda bi, i: (bi * nt + i, 0)
    lane = lambda bi, i: (0, bi * nt + i)

    def whole(shape):
        return pl.BlockSpec(shape, lambda bi, i: (0,) * len(shape))

    return pl.pallas_call(
        kern,
        grid=(b, nt),
        in_specs=[
            row_spec, row_spec, row_spec, row_spec,
            whole((d, 2 * d)), whole((d, d)), whole((d, d)), whole((d, d)),
            mod_spec, mod_spec, mod_spec,
            whole((1, d)), whole((ne, d)), whole((ne, 1)),
        ],
        out_specs=[
            row_spec,
            pl.BlockSpec((tm, d), flat),
            pl.BlockSpec((SUBLANES, tm), lane),
            pl.BlockSpec((SUBLANES, tm), lane),
            pl.BlockSpec((SUBLANES, tm), lane),
            whole((ne, 128)),
        ],
        out_shape=[
            jax.ShapeDtypeStruct((b, t, d), F32),
            jax.ShapeDtypeStruct((tok, d), F32),
            jax.ShapeDtypeStruct((SUBLANES, tok), I32),
            jax.ShapeDtypeStruct((SUBLANES, tok), F32),
            jax.ShapeDtypeStruct((SUBLANES, tok), I32),
            jax.ShapeDtypeStruct((ne, 128), F32),
        ],
        scratch_shapes=[pltpu.VMEM((ne, 1), F32)],
        compiler_params=_params(("arbitrary", "arbitrary"), 56),
        name="merge",
    )(x, hl, ya, yb, w_m, w_ba, w_bb, w_out, gate2, shift3, scale4, g2.reshape(1, d), rwt, rb)


def _moe_kernel(bexp_ref, bact_ref, tok_ref, tokn_ref, h2_hbm, w1g_ref, w1u_ref, b1g_ref, b1u_ref,
                w2_ref, b2_ref, y_ref, xbuf, sem, *, blk, nblocks):
    j = pl.program_id(0)
    slot = lax.rem(j, 2)
    nslot = 1 - slot

    def row_copy(tok, buf_slot, r):
        return pltpu.make_async_copy(h2_hbm.at[pl.ds(tok, 1), :], xbuf.at[buf_slot, pl.ds(r, 1), :],
                                     sem.at[buf_slot])

    def start_block(idx_ref, buf_slot):
        def body(r, carry):
            row_copy(idx_ref[0, 0, r], buf_slot, r).start()
            return carry
        lax.fori_loop(0, blk, body, 0, unroll=8)

    def wait_block(buf_slot):
        def body(r, carry):
            row_copy(0, buf_slot, r).wait()
            return carry
        lax.fori_loop(0, blk, body, 0, unroll=8)

    @pl.when((j == 0) & (bact_ref[0] == 1))
    def _():
        start_block(tok_ref, 0)

    @pl.when((j + 1 < nblocks) & (bact_ref[jnp.minimum(j + 1, nblocks - 1)] == 1))
    def _():
        start_block(tokn_ref, nslot)

    @pl.when(bact_ref[j] == 1)
    def _():
        wait_block(slot)
        x = xbuf[slot].astype(BF16)
        hg = _dot(x, w1g_ref[...]) + b1g_ref[...]
        hu = _dot(x, w1u_ref[...]) + b1u_ref[...]
        hg = jnp.minimum(hg, SWIGLU_LIMIT)
        hu = jnp.clip(hu, -SWIGLU_LIMIT, SWIGLU_LIMIT)
        act = hg * _sigmoid(SWIGLU_ALPHA * hg) * (hu + 1.0)
        y_ref[...] = _dot(act.astype(BF16), w2_ref[...]) + b2_ref[...]

    @pl.when(bact_ref[j] == 0)
    def _():
        y_ref[...] = jnp.zeros_like(y_ref)


def _moe(block_expert, block_active, slot_tok3, h2, w1g, w1u, b1g, b1u, w2, b2, blk):
    nblocks = block_expert.shape[0]
    d = h2.shape[1]
    f = w1g.shape[2]
    kern = functools.partial(_moe_kernel, blk=blk, nblocks=nblocks)
    wmap = lambda j, be, ba: (be[j], 0, 0)
    grid_spec = pltpu.PrefetchScalarGridSpec(
        num_scalar_prefetch=2,
        grid=(nblocks,),
        in_specs=[
            pl.BlockSpec((1, 1, blk), lambda j, be, ba: (j, 0, 0), memory_space=pltpu.SMEM),
            pl.BlockSpec((1, 1, blk), lambda j, be, ba: (jnp.minimum(j + 1, nblocks - 1), 0, 0),
                         memory_space=pltpu.SMEM),
            pl.BlockSpec(memory_space=pl.ANY),
            pl.BlockSpec((None, d, f), wmap),
            pl.BlockSpec((None, d, f), wmap),
            pl.BlockSpec((None, 1, f), wmap),
            pl.BlockSpec((None, 1, f), wmap),
            pl.BlockSpec((None, f, d), wmap),
            pl.BlockSpec((None, 1, d), wmap),
        ],
        out_specs=pl.BlockSpec((blk, d), lambda j, be, ba: (j, 0)),
        scratch_shapes=[
            pltpu.VMEM((2, blk, d), F32),
            pltpu.SemaphoreType.DMA((2,)),
        ],
    )
    return pl.pallas_call(
        kern,
        grid_spec=grid_spec,
        out_shape=jax.ShapeDtypeStruct((nblocks * blk, d), F32),
        compiler_params=_params(("arbitrary",), 48),
        name="moe",
    )(block_expert, block_active, slot_tok3, slot_tok3, h2, w1g, w1u, b1g, b1u, w2, b2)


def _final_kernel(dest_ref, destn_ref, y_hbm, x1_ref, gt_ref, gate_ref, fg_ref, o_ref, ybuf, sem,
                  *, tm, ntiles):
    i = pl.program_id(0)
    slot = lax.rem(i, 2)
    nslot = 1 - slot
    nrows = TOP_K * tm

    def row_copy(src, buf_slot, r):
        return pltpu.make_async_copy(y_hbm.at[pl.ds(src, 1), :], ybuf.at[buf_slot, pl.ds(r, 1), :],
                                     sem.at[buf_slot])

    def start_tile(idx_ref, buf_slot):
        def body(r, carry):
            row_copy(idx_ref[0, 0, r], buf_slot, r).start()
            return carry
        lax.fori_loop(0, nrows, body, 0, unroll=8)

    @pl.when(i == 0)
    def _():
        start_tile(dest_ref, 0)

    @pl.when(i + 1 < ntiles)
    def _():
        start_tile(destn_ref, nslot)

    def wait_body(r, carry):
        row_copy(0, slot, r).wait()
        return carry

    lax.fori_loop(0, nrows, wait_body, 0, unroll=8)

    gt = gt_ref[...]
    acc = gt[:, 0:1] * ybuf[slot, 0:tm, :]
    for k in range(1, TOP_K):
        acc = acc + gt[:, k:k + 1] * ybuf[slot, k * tm:(k + 1) * tm, :]
    x2 = x1_ref[...] + gate_ref[...] * acc
    o_ref[...] = _rms(x2) * fg_ref[...]


def _final(dest3, y, x1, gates_t, gate5, final_g, tm):
    b, t, d = x1.shape
    nt = t // tm
    ntiles = b * nt
    kern = functools.partial(_final_kernel, tm=tm, ntiles=ntiles)
    row_map = lambda i: (i // nt, i % nt, 0)
    return pl.pallas_call(
        kern,
        grid=(ntiles,),
        in_specs=[
            pl.BlockSpec((1, 1, TOP_K * tm), lambda i: (i, 0, 0), memory_space=pltpu.SMEM),
            pl.BlockSpec((1, 1, TOP_K * tm), lambda i: (jnp.minimum(i + 1, ntiles - 1), 0, 0),
                         memory_space=pltpu.SMEM),
            pl.BlockSpec(memory_space=pl.ANY),
            pl.BlockSpec((None, tm, d), row_map),
            pl.BlockSpec((tm, TOP_K), lambda i: (i, 0)),
            pl.BlockSpec((None, 1, d), lambda i: (i // nt, 0, 0)),
            pl.BlockSpec((1, d), lambda i: (0, 0)),
        ],
        out_specs=pl.BlockSpec((None, tm, d), row_map),
        out_shape=jax.ShapeDtypeStruct((b, t, d), F32),
        scratch_shapes=[
            pltpu.VMEM((2, TOP_K * tm, d), F32),
            pltpu.SemaphoreType.DMA((2,)),
        ],
        compiler_params=_params(("arbitrary",), 48),
        name="final",
    )(dest3, dest3, y, x1, gates_t, gate5, final_g.reshape(1, d))


def _pick_tile(n, pref):
    tm = pref
    while n % tm:
        tm //= 2
    return tm


def kernel(x, c, ctx, c_ctx, ada_w, ada_b, norm1_g, norm2_g, w_in, lru_conv_w, lru_conv_b, lru_wa, lru_ba, lru_wx, lru_bx, lru_lam, hg_lb_logits, hg_norm_g, w_branch_a, w_branch_b, w_out, router_w, router_b, moe_w1, moe_b1, moe_w2, moe_b2, final_g):
    b, t, d = x.shape
    cx = ctx.shape[1]
    layer = 0
    w_lru = lru_conv_w.shape[2]
    qk = (w_in.shape[2] - 2 * w_lru - 2 * d) // 5
    heads = qk // HG_DK
    ne = router_w.shape[2]

    pad = (-(b + 1)) % SUBLANES
    c_all = jnp.concatenate([c, c_ctx[None, :], jnp.zeros((pad, d), F32)], axis=0)
    mod = _modulation(c_all, ada_w[layer], ada_b[layer])
    mod_l = mod[:b].reshape(b, N_MOD, 1, d)
    mod_c = mod[b].reshape(N_MOD, 1, 1, d)

    w_in_b = w_in[layer].astype(BF16)
    w_lru_in = w_in_b[:, :2 * w_lru]
    hg0 = 2 * w_lru
    w_hg = w_in_b[:, hg0:hg0 + 5 * qk].reshape(d, 5, heads, HG_DK).transpose(2, 0, 1, 3).reshape(heads, d, 5 * HG_DK)
    w_m = w_in_b[:, hg0 + 5 * qk:]
    g_blocks = w_lru // LRU_BLOCK
    wg = jnp.concatenate([lru_wa[layer, 0], lru_wx[layer, 0], lru_wa[layer, 1], lru_wx[layer, 1]], axis=-1).astype(BF16)
    bg = jnp.concatenate([lru_ba[layer, 0], lru_bx[layer, 0], lru_ba[layer, 1], lru_bx[layer, 1]], axis=-1)
    bg = bg.reshape(g_blocks, 1, 4 * LRU_BLOCK)
    lb_all = jnp.cumsum(jax.nn.softmax(hg_lb_logits.astype(F32), axis=0), axis=0)
    lb = lb_all[layer].reshape(heads, 1, HG_DK)

    tm_n = _pick_tile(t, 512)
    hl = _prenorm(x, mod_l[:, 0], mod_l[:, 1], norm1_g[layer], tm_n)
    hc = _prenorm(ctx, mod_c[0], mod_c[1], norm1_g[layer], _pick_tile(cx, 256))
    axl = _matmul(hl.reshape(b * t, d), w_lru_in, _pick_tile(b * t, 512), BF16).reshape(b, t, 2 * w_lru)
    axc = _matmul(hc.reshape(b * cx, d), w_lru_in, _pick_tile(b * cx, 512), BF16).reshape(b, cx, 2 * w_lru)
    ya = _lru(axl, axc, lru_conv_w[layer], lru_conv_b[layer], wg, bg, lru_lam[layer], b, cx, t)
    yb = _hgrn(hc, hl, w_hg, lb, hg_norm_g[layer])

    rwt = router_w[layer].T
    rb = router_b[layer].reshape(ne, 1)
    x1, h2, idx8, gates8, rank8, cnt = _merge(
        x, hl, ya, yb, w_m, w_branch_a[layer].astype(BF16), w_branch_b[layer].astype(BF16),
        w_out[layer].astype(BF16), mod_l[:, 2], mod_l[:, 3], mod_l[:, 4], norm2_g[layer], rwt, rb,
        _pick_tile(t, 512))

    blk = MOE_BLOCK
    tok = b * t
    n_assign = tok * TOP_K
    nblocks = -(-n_assign // blk) + ne
    counts = cnt[:, 0].astype(I32)
    padded = (counts + blk - 1) // blk * blk
    pend = jnp.cumsum(padded)
    pstart = pend - padded
    idx = idx8[:TOP_K]
    dest = pstart[idx] + rank8[:TOP_K]
    block_start = jnp.arange(nblocks, dtype=I32) * blk
    block_expert = jnp.minimum(jnp.sum(block_start[:, None] >= pend[None, :], axis=1), ne - 1).astype(I32)
    block_active = (block_start < pend[-1]).astype(I32)
    tok_ids = jnp.broadcast_to(jnp.arange(tok, dtype=I32)[None, :], (TOP_K, tok))
    slot_tok = jnp.zeros((nblocks * blk,), I32).at[dest.reshape(-1)].set(tok_ids.reshape(-1))

    w1 = moe_w1[layer]
    w1g = w1[:, :, 0::2].astype(BF16)
    w1u = w1[:, :, 1::2].astype(BF16)
    b1 = moe_b1[layer]
    f = w1g.shape[2]
    b1g = b1[:, 0::2].reshape(ne, 1, f)
    b1u = b1[:, 1::2].reshape(ne, 1, f)
    y = _moe(block_expert, block_active, slot_tok.reshape(nblocks, 1, blk), h2, w1g, w1u, b1g, b1u,
             moe_w2[layer].astype(BF16), moe_b2[layer].reshape(ne, 1, d), blk)

    tm_f = _pick_tile(t, 256)
    ntiles = tok // tm_f
    dest3 = dest.reshape(TOP_K, ntiles, tm_f).transpose(1, 0, 2).reshape(ntiles, 1, TOP_K * tm_f)
    gates_t = gates8[:TOP_K].T
    return _final(dest3, y, x1, gates_t, mod_l[:, 5], final_g, tm_f)
```

```python
import functools

import jax
import jax.numpy as jnp
from jax import lax
from jax.experimental import pallas as pl
from jax.experimental.pallas import tpu as pltpu

F32 = jnp.float32
BF16 = jnp.bfloat16
I32 = jnp.int32
HIGHEST = lax.Precision.HIGHEST

EPS = 1e-6
N_MOD = 6
GRID_W = 64
CONV_LEFT = 2
CONV_WIDTH = 4
LRU_C = 8.0
LRU_BLOCK = 128
HG_DK = 128
HG_CHUNK = 64
TOP_K = 4
SWIGLU_LIMIT = 7.0
SWIGLU_ALPHA = 1.702
MOE_BLOCK = 256
SUBLANES = 8
HALO = 8


def _params(sem, vmem_mb):
    return pltpu.CompilerParams(dimension_semantics=sem, vmem_limit_bytes=vmem_mb * 1024 * 1024)


def _dot(a, b):
    return jnp.dot(a, b, preferred_element_type=F32)


def _dot_nt(a, b):
    return lax.dot_general(a, b, (((1,), (1,)), ((), ())), preferred_element_type=F32)


def _dot_tn(a, b):
    return lax.dot_general(a, b, (((0,), (0,)), ((), ())), preferred_element_type=F32)


def _sigmoid(x):
    return jax.nn.sigmoid(x)


def _silu(x):
    return x * jax.nn.sigmoid(x)


def _rms(x):
    return x * lax.rsqrt(jnp.mean(x * x, axis=-1, keepdims=True) + EPS)


def _mod_kernel(c_ref, w_ref, b_ref, o_ref):
    s = _silu(c_ref[...])
    o_ref[...] = jnp.dot(s, w_ref[...], preferred_element_type=F32, precision=HIGHEST) + b_ref[...]


def _modulation(c_all, ada_w, ada_b):
    m, d = c_all.shape
    n = ada_w.shape[1]
    tn = 1024
    return pl.pallas_call(
        _mod_kernel,
        grid=(n // tn,),
        in_specs=[
            pl.BlockSpec((m, d), lambda j: (0, 0)),
            pl.BlockSpec((d, tn), lambda j: (0, j)),
            pl.BlockSpec((1, tn), lambda j: (0, j)),
        ],
        out_specs=pl.BlockSpec((m, tn), lambda j: (0, j)),
        out_shape=jax.ShapeDtypeStruct((m, n), F32),
        compiler_params=_params(("arbitrary",), 32),
        name="mod",
    )(c_all, ada_w, ada_b.reshape(1, n))


def _prenorm_kernel(x_ref, shift_ref, scale_ref, g_ref, o_ref):
    y = _rms(x_ref[...]) * g_ref[...]
    o_ref[...] = (y * (1.0 + scale_ref[...]) + shift_ref[...]).astype(o_ref.dtype)


def _prenorm(x, shift, scale, g, tm):
    b, l, d = x.shape
    per_batch = shift.shape[0] == b and b > 1
    mod_map = (lambda bi, i: (bi, 0, 0)) if per_batch else (lambda bi, i: (0, 0, 0))
    return pl.pallas_call(
        _prenorm_kernel,
        grid=(b, l // tm),
        in_specs=[
            pl.BlockSpec((None, tm, d), lambda bi, i: (bi, i, 0)),
            pl.BlockSpec((None, 1, d), mod_map),
            pl.BlockSpec((None, 1, d), mod_map),
            pl.BlockSpec((1, d), lambda bi, i: (0, 0)),
        ],
        out_specs=pl.BlockSpec((None, tm, d), lambda bi, i: (bi, i, 0)),
        out_shape=jax.ShapeDtypeStruct((b, l, d), BF16),
        compiler_params=_params(("arbitrary", "arbitrary"), 32),
        name="prenorm",
    )(x, shift, scale, g.reshape(1, d))


def _mm_kernel(x_ref, w_ref, o_ref):
    o_ref[...] = _dot(x_ref[...], w_ref[...]).astype(o_ref.dtype)


def _matmul(x2d, w, tm, out_dtype):
    m, k = x2d.shape
    n = w.shape[1]
    return pl.pallas_call(
        _mm_kernel,
        grid=(m // tm,),
        in_specs=[
            pl.BlockSpec((tm, k), lambda i: (i, 0)),
            pl.BlockSpec((k, n), lambda i: (0, 0)),
        ],
        out_specs=pl.BlockSpec((tm, n), lambda i: (i, 0)),
        out_shape=jax.ShapeDtypeStruct((m, n), out_dtype),
        compiler_params=_params(("arbitrary",), 40),
        name="inproj_lru",
    )(x2d, w)


def _lru_kernel(axl_ref, agl_ref, axc_ref, cw_ref, cb_ref, wg_ref, bg_ref, lam_ref, o_ref,
                xf_ref, yacc_ref, su_ref, sa_ref, sb_ref, sh_ref, *, nb, cx, t):
    tb = GRID_W
    lat0 = cx + 2 * HALO
    ctx0 = HALO
    rows = cx + t + 3 * HALO
    gw = 2 * LRU_BLOCK

    zeros_h = jnp.zeros((HALO, LRU_BLOCK), F32)
    for b in range(nb):
        xf_ref[b, 0:HALO, :] = zeros_h
        xf_ref[b, ctx0:ctx0 + cx, :] = axc_ref[b].astype(F32)
        xf_ref[b, ctx0 + cx:lat0, :] = zeros_h
        xf_ref[b, lat0:lat0 + t, :] = axl_ref[b].astype(F32)
        xf_ref[b, lat0 + t:rows, :] = zeros_h

    cw = cw_ref[...]
    cb = cb_ref[...]
    lam = lam_ref[...]
    sp = jnp.maximum(-lam, 0.0) + jnp.log1p(jnp.exp(-jnp.abs(lam)))
    tloc = lax.broadcasted_iota(I32, (tb, LRU_BLOCK), 0)

    def fill_gates(base, d, row_local):
        for b in range(nb):
            acc = jnp.broadcast_to(cb, (tb, LRU_BLOCK))
            for k in range(CONV_WIDTH):
                off = k - CONV_LEFT
                xs = xf_ref[b, pl.ds(base + off, tb), :]
                if row_local and off != 0:
                    ok = (tloc + off >= 0) & (tloc + off < tb)
                    xs = jnp.where(ok, xs, 0.0)
                acc = acc + xs * cw[k:k + 1, :]
            su_ref[pl.ds(b, tb, stride=nb), :] = acc
        u = su_ref[...]
        z = _dot(u.astype(BF16), wg_ref[:, d * gw:(d + 1) * gw]) + bg_ref[:, d * gw:(d + 1) * gw]
        r = _sigmoid(z[:, :LRU_BLOCK])
        i = _sigmoid(z[:, LRU_BLOCK:])
        log_a = (-LRU_C) * r * sp[d:d + 1, :]
        a = jnp.exp(log_a)
        mult = jnp.sqrt(-jnp.tanh(log_a) * (a * a + 1.0))
        sa_ref[...] = a
        sb_ref[...] = mult * (i * u)

    def scan_block(h, reverse):
        order = range(tb - 1, -1, -1) if reverse else range(tb)
        for s in order:
            h = sa_ref[s * nb:(s + 1) * nb, :] * h + sb_ref[s * nb:(s + 1) * nb, :]
            sh_ref[s * nb:(s + 1) * nb, :] = h
        return h

    def sweep(d):
        reverse = d == 1
        n_c, n_l = cx // tb, t // tb

        def ctx_body(j, h):
            jj = (n_c - 1 - j) if reverse else j
            base = pl.multiple_of(ctx0 + jj * tb, SUBLANES)
            fill_gates(base, d, False)
            return scan_block(h, reverse)

        def lat_body(j, h):
            jj = (n_l - 1 - j) if reverse else j
            r0 = pl.multiple_of(jj * tb, tb)
            fill_gates(pl.multiple_of(lat0 + r0, SUBLANES), d, True)
            h = scan_block(h, reverse)
            for b in range(nb):
                yb = sh_ref[pl.ds(b, tb, stride=nb), :]
                if not reverse:
                    yacc_ref[b, pl.ds(r0, tb), :] = yb
                else:
                    gate = jax.nn.gelu(agl_ref[b, pl.ds(r0, tb), :].astype(F32))
                    o_ref[b, pl.ds(r0, tb), :] = ((yacc_ref[b, pl.ds(r0, tb), :] + yb) * gate).astype(o_ref.dtype)
            return h

        h0 = jnp.zeros((nb, LRU_BLOCK), F32)
        h1 = lax.fori_loop(0, n_c, ctx_body, h0)
        lax.fori_loop(0, n_l, lat_body, h1)

    sweep(0)
    sweep(1)


def _lru(axl, axc, conv_w, conv_b, wg, bg, lam, nb, cx, t):
    w = conv_w.shape[1]
    g = w // LRU_BLOCK
    kern = functools.partial(_lru_kernel, nb=nb, cx=cx, t=t)
    return pl.pallas_call(
        kern,
        grid=(g,),
        in_specs=[
            pl.BlockSpec((nb, t, LRU_BLOCK), lambda j: (0, 0, j)),
            pl.BlockSpec((nb, t, LRU_BLOCK), lambda j: (0, 0, g + j)),
            pl.BlockSpec((nb, cx, LRU_BLOCK), lambda j: (0, 0, j)),
            pl.BlockSpec((CONV_WIDTH, LRU_BLOCK), lambda j: (0, j)),
            pl.BlockSpec((1, LRU_BLOCK), lambda j: (0, j)),
            pl.BlockSpec((None, LRU_BLOCK, 4 * LRU_BLOCK), lambda j: (j, 0, 0)),
            pl.BlockSpec((None, 1, 4 * LRU_BLOCK), lambda j: (j, 0, 0)),
            pl.BlockSpec((2, LRU_BLOCK), lambda j: (0, j)),
        ],
        out_specs=pl.BlockSpec((nb, t, LRU_BLOCK), lambda j: (0, 0, j)),
        out_shape=jax.ShapeDtypeStruct((nb, t, w), BF16),
        scratch_shapes=[
            pltpu.VMEM((nb, cx + t + 3 * HALO, LRU_BLOCK), F32),
            pltpu.VMEM((nb, t, LRU_BLOCK), F32),
            pltpu.VMEM((GRID_W * nb, LRU_BLOCK), F32),
            pltpu.VMEM((GRID_W * nb, LRU_BLOCK), F32),
            pltpu.VMEM((GRID_W * nb, LRU_BLOCK), F32),
            pltpu.VMEM((GRID_W * nb, LRU_BLOCK), F32),
        ],
        compiler_params=_params(("arbitrary",), 56),
        name="lru",
    )(axl, axl, axc, conv_w, conv_b.reshape(1, w), wg, bg, lam)


def _hgrn_kernel(hc_ref, hl_ref, w_ref, lb_ref, ng_ref, o_ref,
                 p_ref, oacc_ref, aq_ref, upd_ref, dec_ref, *, cx, t):
    c = HG_CHUNK
    dk = HG_DK
    tc = cx + t
    n_c, n_l = cx // c, t // c
    n_all = n_c + n_l

    p_ref[0:cx, :] = _dot(hc_ref[...], w_ref[...])
    rows = 512 if t % 512 == 0 else c

    def proj_body(i, carry):
        r = pl.multiple_of(i * rows, rows)
        p_ref[pl.ds(cx + r, rows), :] = _dot(hl_ref[pl.ds(r, rows), :], w_ref[...])
        return carry

    lax.fori_loop(0, t // rows, proj_body, 0)

    lb = lb_ref[...]
    one_m_lb = 1.0 - lb
    ri = lax.broadcasted_iota(I32, (c, c), 0)
    ci = lax.broadcasted_iota(I32, (c, c), 1)
    lower = ri >= ci
    upper = ci >= ri

    def intra(j, carry):
        r0 = pl.multiple_of(j * c, c)
        q = _silu(p_ref[pl.ds(r0, c), 0:dk])
        v = p_ref[pl.ds(r0, c), 3 * dk:4 * dk].astype(BF16)
        o_sum = None
        for d in range(2):
            reverse = d == 1
            keep = upper if reverse else lower
            cum = keep.astype(BF16)
            mid = c // 2 if reverse else c // 2 - 1
            end = 0 if reverse else c - 1
            f = lb + one_m_lb * _sigmoid(p_ref[pl.ds(r0, c), (1 + d) * dk:(2 + d) * dk])
            logf = jnp.log(f)
            k = 1.0 - f
            hi = logf.astype(BF16)
            lo = (logf - hi.astype(F32)).astype(BF16)
            gc = _dot(cum, hi) + _dot(cum, lo)
            g_mid = gc[mid:mid + 1, :]
            g_end = gc[end:end + 1, :]
            a = (q * jnp.exp(gc - g_mid)).astype(BF16)
            bm = (k * jnp.exp(g_mid - gc)).astype(BF16)
            sc = jnp.where(keep, _dot_nt(a, bm), 0.0).astype(BF16)
            o_intra = _dot(sc, v)
            o_sum = o_intra if o_sum is None else o_sum + o_intra
            bk = (k * jnp.exp(g_end - gc)).astype(BF16)
            upd_ref[d, j] = _dot_tn(v, bk)
            dec_ref[d, j] = jnp.broadcast_to(jnp.exp(g_end), (SUBLANES, dk))
            aq_ref[d, pl.ds(r0, c), :] = (q * jnp.exp(gc)).astype(BF16)
        oacc_ref[pl.ds(r0, c), :] = o_sum
        return carry

    lax.fori_loop(0, n_all, intra, 0, unroll=4)

    def inter(d, jj, st):
        r0 = pl.multiple_of(jj * c, c)
        oacc_ref[pl.ds(r0, c), :] += _dot_nt(aq_ref[d, pl.ds(r0, c), :], st.astype(BF16))
        return st * dec_ref[d, jj][0:1, :] + upd_ref[d, jj]

    def ctx_body(j, sts):
        return inter(0, j, sts[0]), inter(1, n_c - 1 - j, sts[1])

    def lat_body(j, sts):
        return inter(0, n_c + j, sts[0]), inter(1, n_c + n_l - 1 - j, sts[1])

    zero = jnp.zeros((dk, dk), F32)
    sts = lax.fori_loop(0, n_c, ctx_body, (zero, zero), unroll=2)
    lax.fori_loop(0, n_l, lat_body, sts, unroll=2)

    ng = ng_ref[...]

    def readout(i, carry):
        r = pl.multiple_of(i * rows, rows)
        o = oacc_ref[pl.ds(cx + r, rows), :]
        g = p_ref[pl.ds(cx + r, rows), 4 * dk:5 * dk]
        o_ref[pl.ds(r, rows), :] = (_rms(o) * ng * _silu(g)).astype(o_ref.dtype)
        return carry

    lax.fori_loop(0, t // rows, readout, 0)


def _hgrn(hc, hl, w_hg, lb, norm_g):
    b, cx, d = hc.shape
    t = hl.shape[1]
    heads = w_hg.shape[0]
    dk = HG_DK
    n_all = (cx + t) // HG_CHUNK
    kern = functools.partial(_hgrn_kernel, cx=cx, t=t)
    return pl.pallas_call(
        kern,
        grid=(b, heads),
        in_specs=[
            pl.BlockSpec((None, cx, d), lambda bi, h: (bi, 0, 0)),
            pl.BlockSpec((None, t, d), lambda bi, h: (bi, 0, 0)),
            pl.BlockSpec((None, d, 5 * dk), lambda bi, h: (h, 0, 0)),
            pl.BlockSpec((None, 1, dk), lambda bi, h: (h, 0, 0)),
            pl.BlockSpec((1, dk), lambda bi, h: (0, 0)),
        ],
        out_specs=pl.BlockSpec((None, t, dk), lambda bi, h: (bi, 0, h)),
        out_shape=jax.ShapeDtypeStruct((b, t, heads * dk), BF16),
        scratch_shapes=[
            pltpu.VMEM((cx + t, 5 * dk), F32),
            pltpu.VMEM((cx + t, dk), F32),
            pltpu.VMEM((2, cx + t, dk), BF16),
            pltpu.VMEM((2, n_all, dk, dk), F32),
            pltpu.VMEM((2, n_all, SUBLANES, dk), F32),
        ],
        compiler_params=_params(("arbitrary", "arbitrary"), 48),
        name="hgrn",
    )(hc, hl, w_hg, lb, norm_g.reshape(1, dk))


def _merge_kernel(x_ref, h_ref, ya_ref, yb_ref, wm_ref, wa_ref, wb_ref, wo_ref,
                  gate_ref, shift_ref, scale_ref, g2_ref, rwt_ref, rb_ref,
                  x1_ref, h2_ref, idx_ref, gates_ref, rank_ref, cnt_ref, carry_ref, *, tm, d, ne):
    first = (pl.program_id(0) == 0) & (pl.program_id(1) == 0)

    @pl.when(first)
    def _():
        carry_ref[...] = jnp.zeros_like(carry_ref)

    m = _dot(h_ref[...], wm_ref[...])
    za = _dot(ya_ref[...], wa_ref[...])
    zb = _dot(yb_ref[...], wb_ref[...])
    mix = _sigmoid(m[:, :d]) * za + _sigmoid(m[:, d:]) * zb
    out = _dot(mix.astype(BF16), wo_ref[...])
    x1 = x_ref[...] + gate_ref[...] * out
    x1_ref[...] = x1
    h2 = _rms(x1) * g2_ref[...] * (1.0 + scale_ref[...]) + shift_ref[...]
    h2_ref[...] = h2

    logits = lax.dot_general(rwt_ref[...], h2, (((1,), (1,)), ((), ())),
                             preferred_element_type=F32, precision=HIGHEST) + rb_ref[...]
    eid = lax.broadcasted_iota(I32, (ne, tm), 0)
    neg = jnp.float32(-jnp.inf)
    work = logits
    hot = jnp.zeros((ne, tm), F32)
    vals, idxs, sels = [], [], []
    for _ in range(TOP_K):
        mx = jnp.max(work, axis=0, keepdims=True)
        ix = jnp.min(jnp.where(work == mx, eid, ne), axis=0, keepdims=True)
        sel = eid == ix
        work = jnp.where(sel, neg, work)
        hot = hot + sel.astype(F32)
        vals.append(mx)
        idxs.append(ix)
        sels.append(sel)
    ex = [jnp.exp(v - vals[0]) for v in vals]
    den = ex[0] + ex[1] + ex[2] + ex[3]
    gts = [e / den for e in ex]

    ti = lax.broadcasted_iota(I32, (tm, tm), 0)
    tj = lax.broadcasted_iota(I32, (tm, tm), 1)
    strict = (ti < tj).astype(BF16)
    before = _dot(hot.astype(BF16), strict) + carry_ref[...]
    rks = [jnp.sum(jnp.where(s, before, 0.0), axis=0, keepdims=True) for s in sels]
    carry_ref[...] = carry_ref[...] + jnp.sum(hot, axis=1, keepdims=True)

    row = lax.broadcasted_iota(I32, (SUBLANES, tm), 0)

    def pack(parts, zero):
        acc = jnp.full((SUBLANES, tm), zero, parts[0].dtype)
        for k, p in enumerate(parts):
            acc = jnp.where(row == k, jnp.broadcast_to(p, (SUBLANES, tm)), acc)
        return acc

    idx_ref[...] = pack(idxs, 0)
    gates_ref[...] = pack(gts, 0.0)
    rank_ref[...] = pack(rks, 0.0).astype(I32)
    cnt_ref[...] = jnp.broadcast_to(carry_ref[...], cnt_ref.shape)


def _merge(x, hl, ya, yb, w_m, w_ba, w_bb, w_out, gate2, shift3, scale4, g2, rwt, rb, tm):
    b, t, d = x.shape
    ne = rwt.shape[0]
    nt = t // tm
    tok = b * t
    kern = functools.partial(_merge_kernel, tm=tm, d=d, ne=ne)
    row_spec = pl.BlockSpec((None, tm, d), lambda bi, i: (bi, i, 0))
    mod_spec = pl.BlockSpec((None, 1, d), lambda bi, i: (bi, 0, 0))
    flat = lambda bi, i: (bi * nt + i, 0)
    lane = lambda bi, i: (0, bi * nt + i)

    def whole(shape):
        return pl.BlockSpec(shape, lambda bi, i: (0,) * len(shape))

    return pl.pallas_call(
        kern,
        grid=(b, nt),
        in_specs=[
            row_spec, row_spec, row_spec, row_spec,
            whole((d, 2 * d)), whole((d, d)), whole((d, d)), whole((d, d)),
            mod_spec, mod_spec, mod_spec,
            whole((1, d)), whole((ne, d)), whole((ne, 1)),
        ],
        out_specs=[
            row_spec,
            pl.BlockSpec((tm, d), flat),
            pl.BlockSpec((SUBLANES, tm), lane),
            pl.BlockSpec((SUBLANES, tm), lane),
            pl.BlockSpec((SUBLANES, tm), lane),
            whole((ne, 128)),
        ],
        out_shape=[
            jax.ShapeDtypeStruct((b, t, d), F32),
            jax.ShapeDtypeStruct((tok, d), F32),
            jax.ShapeDtypeStruct((SUBLANES, tok), I32),
            jax.ShapeDtypeStruct((SUBLANES, tok), F32),
            jax.ShapeDtypeStruct((SUBLANES, tok), I32),
            jax.ShapeDtypeStruct((ne, 128), F32),
        ],
        scratch_shapes=[pltpu.VMEM((ne, 1), F32)],
        compiler_params=_params(("arbitrary", "arbitrary"), 56),
        name="merge",
    )(x, hl, ya, yb, w_m, w_ba, w_bb, w_out, gate2, shift3, scale4, g2.reshape(1, d), rwt, rb)


def _moe_kernel(bexp_ref, bact_ref, tok_ref, tokn_ref, h2_hbm, w1_ref, b1g_ref, b1u_ref,
                w2_ref, b2_ref, y_ref, xbuf, w1g_s, w1u_s, w2_s, sem, *, blk, nblocks):
    j = pl.program_id(0)
    slot = lax.rem(j, 2)
    nslot = 1 - slot
    f = w1g_s.shape[1]
    pw = 2 * LRU_BLOCK

    new_expert = (j == 0) | (bexp_ref[j] != bexp_ref[jnp.maximum(j - 1, 0)])

    @pl.when(new_expert & (bact_ref[j] == 1))
    def _():
        src = lax.broadcasted_iota(I32, (pw, pw), 0)
        dst = lax.broadcasted_iota(I32, (pw, pw), 1)
        want = jnp.where(dst < pw // 2, 2 * dst, 2 * (dst - pw // 2) + 1)
        sel = (src == want).astype(BF16)
        for cb in range(2 * f // pw):
            tcols = _dot(w1_ref[:, cb * pw:(cb + 1) * pw].astype(BF16), sel)
            w1g_s[:, cb * (pw // 2):(cb + 1) * (pw // 2)] = tcols[:, :pw // 2].astype(BF16)
            w1u_s[:, cb * (pw // 2):(cb + 1) * (pw // 2)] = tcols[:, pw // 2:].astype(BF16)
        w2_s[...] = w2_ref[...].astype(BF16)

    def row_copy(tok, buf_slot, r):
        return pltpu.make_async_copy(h2_hbm.at[pl.ds(tok, 1), :], xbuf.at[buf_slot, pl.ds(r, 1), :],
                                     sem.at[buf_slot])

    def start_block(idx_ref, buf_slot):
        def body(r, carry):
            row_copy(idx_ref[0, 0, r], buf_slot, r).start()
            return carry
        lax.fori_loop(0, blk, body, 0, unroll=8)

    def wait_block(buf_slot):
        def body(r, carry):
            row_copy(0, buf_slot, r).wait()
            return carry
        lax.fori_loop(0, blk, body, 0, unroll=8)

    @pl.when((j == 0) & (bact_ref[0] == 1))
    def _():
        start_block(tok_ref, 0)

    @pl.when((j + 1 < nblocks) & (bact_ref[jnp.minimum(j + 1, nblocks - 1)] == 1))
    def _():
        start_block(tokn_ref, nslot)

    @pl.when(bact_ref[j] == 1)
    def _():
        wait_block(slot)
        x = xbuf[slot].astype(BF16)
        hg = _dot(x, w1g_s[...]) + b1g_ref[...]
        hu = _dot(x, w1u_s[...]) + b1u_ref[...]
        hg = jnp.minimum(hg, SWIGLU_LIMIT)
        hu = jnp.clip(hu, -SWIGLU_LIMIT, SWIGLU_LIMIT)
        act = hg * _sigmoid(SWIGLU_ALPHA * hg) * (hu + 1.0)
        y_ref[...] = _dot(act.astype(BF16), w2_s[...]) + b2_ref[...]

    @pl.when(bact_ref[j] == 0)
    def _():
        y_ref[...] = jnp.zeros_like(y_ref)


def _moe(block_expert, block_active, slot_tok3, h2, w1, b1g, b1u, w2, b2, blk):
    nblocks = block_expert.shape[0]
    d = h2.shape[1]
    f = w2.shape[1]
    kern = functools.partial(_moe_kernel, blk=blk, nblocks=nblocks)
    wmap = lambda j, be, ba: (be[j], 0, 0)
    grid_spec = pltpu.PrefetchScalarGridSpec(
        num_scalar_prefetch=2,
        grid=(nblocks,),
        in_specs=[
            pl.BlockSpec((1, 1, blk), lambda j, be, ba: (j, 0, 0), memory_space=pltpu.SMEM),
            pl.BlockSpec((1, 1, blk), lambda j, be, ba: (jnp.minimum(j + 1, nblocks - 1), 0, 0),
                         memory_space=pltpu.SMEM),
            pl.BlockSpec(memory_space=pl.ANY),
            pl.BlockSpec((None, d, 2 * f), wmap),
            pl.BlockSpec((None, 1, f), wmap),
            pl.BlockSpec((None, 1, f), wmap),
            pl.BlockSpec((None, f, d), wmap),
            pl.BlockSpec((None, 1, d), wmap),
        ],
        out_specs=pl.BlockSpec((blk, d), lambda j, be, ba: (j, 0)),
        scratch_shapes=[
            pltpu.VMEM((2, blk, d), F32),
            pltpu.VMEM((d, f), BF16),
            pltpu.VMEM((d, f), BF16),
            pltpu.VMEM((f, d), BF16),
            pltpu.SemaphoreType.DMA((2,)),
        ],
    )
    return pl.pallas_call(
        kern,
        grid_spec=grid_spec,
        out_shape=jax.ShapeDtypeStruct((nblocks * blk, d), F32),
        compiler_params=_params(("arbitrary",), 56),
        name="moe",
    )(block_expert, block_active, slot_tok3, slot_tok3, h2, w1, b1g, b1u, w2, b2)


def _final_kernel(dest_ref, destn_ref, y_hbm, x1_ref, gt_ref, gate_ref, fg_ref, o_ref, ybuf, sem,
                  *, tm, ntiles):
    i = pl.program_id(0)
    slot = lax.rem(i, 2)
    nslot = 1 - slot
    nrows = TOP_K * tm

    def row_copy(src, buf_slot, r):
        return pltpu.make_async_copy(y_hbm.at[pl.ds(src, 1), :], ybuf.at[buf_slot, pl.ds(r, 1), :],
                                     sem.at[buf_slot])

    def start_tile(idx_ref, buf_slot):
        def body(r, carry):
            row_copy(idx_ref[0, 0, r], buf_slot, r).start()
            return carry
        lax.fori_loop(0, nrows, body, 0, unroll=8)

    @pl.when(i == 0)
    def _():
        start_tile(dest_ref, 0)

    @pl.when(i + 1 < ntiles)
    def _():
        start_tile(destn_ref, nslot)

    def wait_body(r, carry):
        row_copy(0, slot, r).wait()
        return carry

    lax.fori_loop(0, nrows, wait_body, 0, unroll=8)

    gt = gt_ref[...]
    acc = gt[:, 0:1] * ybuf[slot, 0:tm, :]
    for k in range(1, TOP_K):
        acc = acc + gt[:, k:k + 1] * ybuf[slot, k * tm:(k + 1) * tm, :]
    x2 = x1_ref[...] + gate_ref[...] * acc
    o_ref[...] = _rms(x2) * fg_ref[...]


def _final(dest3, y, x1, gates_t, gate5, final_g, tm):
    b, t, d = x1.shape
    nt = t // tm
    ntiles = b * nt
    kern = functools.partial(_final_kernel, tm=tm, ntiles=ntiles)
    row_map = lambda i: (i // nt, i % nt, 0)
    return pl.pallas_call(
        kern,
        grid=(ntiles,),
        in_specs=[
            pl.BlockSpec((1, 1, TOP_K * tm), lambda i: (i, 0, 0), memory_space=pltpu.SMEM),
            pl.BlockSpec((1, 1, TOP_K * tm), lambda i: (jnp.minimum(i + 1, ntiles - 1), 0, 0),
                         memory_space=pltpu.SMEM),
            pl.BlockSpec(memory_space=pl.ANY),
            pl.BlockSpec((None, tm, d), row_map),
            pl.BlockSpec((tm, TOP_K), lambda i: (i, 0)),
            pl.BlockSpec((None, 1, d), lambda i: (i // nt, 0, 0)),
            pl.BlockSpec((1, d), lambda i: (0, 0)),
        ],
        out_specs=pl.BlockSpec((None, tm, d), row_map),
        out_shape=jax.ShapeDtypeStruct((b, t, d), F32),
        scratch_shapes=[
            pltpu.VMEM((2, TOP_K * tm, d), F32),
            pltpu.SemaphoreType.DMA((2,)),
        ],
        compiler_params=_params(("arbitrary",), 48),
        name="final",
    )(dest3, dest3, y, x1, gates_t, gate5, final_g.reshape(1, d))


def _pick_tile(n, pref):
    tm = pref
    while n % tm:
        tm //= 2
    return tm


def kernel(x, c, ctx, c_ctx, ada_w, ada_b, norm1_g, norm2_g, w_in, lru_conv_w, lru_conv_b, lru_wa, lru_ba, lru_wx, lru_bx, lru_lam, hg_lb_logits, hg_norm_g, w_branch_a, w_branch_b, w_out, router_w, router_b, moe_w1, moe_b1, moe_w2, moe_b2, final_g):
    b, t, d = x.shape
    cx = ctx.shape[1]
    layer = 0
    w_lru = lru_conv_w.shape[2]
    qk = (w_in.shape[2] - 2 * w_lru - 2 * d) // 5
    heads = qk // HG_DK
    ne = router_w.shape[2]

    pad = (-(b + 1)) % SUBLANES
    c_all = jnp.concatenate([c, c_ctx[None, :], jnp.zeros((pad, d), F32)], axis=0)
    mod = _modulation(c_all, ada_w[layer], ada_b[layer])
    mod_l = mod[:b].reshape(b, N_MOD, 1, d)
    mod_c = mod[b].reshape(N_MOD, 1, 1, d)

    w_in_b = w_in[layer].astype(BF16)
    w_lru_in = w_in_b[:, :2 * w_lru]
    hg0 = 2 * w_lru
    w_hg = w_in_b[:, hg0:hg0 + 5 * qk].reshape(d, 5, heads, HG_DK).transpose(2, 0, 1, 3).reshape(heads, d, 5 * HG_DK)
    w_m = w_in_b[:, hg0 + 5 * qk:]
    g_blocks = w_lru // LRU_BLOCK
    wg = jnp.concatenate([lru_wa[layer, 0], lru_wx[layer, 0], lru_wa[layer, 1], lru_wx[layer, 1]], axis=-1).astype(BF16)
    bg = jnp.concatenate([lru_ba[layer, 0], lru_bx[layer, 0], lru_ba[layer, 1], lru_bx[layer, 1]], axis=-1)
    bg = bg.reshape(g_blocks, 1, 4 * LRU_BLOCK)
    lb_all = jnp.cumsum(jax.nn.softmax(hg_lb_logits.astype(F32), axis=0), axis=0)
    lb = lb_all[layer].reshape(heads, 1, HG_DK)

    tm_n = _pick_tile(t, 512)
    hl = _prenorm(x, mod_l[:, 0], mod_l[:, 1], norm1_g[layer], tm_n)
    hc = _prenorm(ctx, mod_c[0], mod_c[1], norm1_g[layer], _pick_tile(cx, 256))
    axl = _matmul(hl.reshape(b * t, d), w_lru_in, _pick_tile(b * t, 512), BF16).reshape(b, t, 2 * w_lru)
    axc = _matmul(hc.reshape(b * cx, d), w_lru_in, _pick_tile(b * cx, 512), BF16).reshape(b, cx, 2 * w_lru)
    ya = _lru(axl, axc, lru_conv_w[layer], lru_conv_b[layer], wg, bg, lru_lam[layer], b, cx, t)
    yb = _hgrn(hc, hl, w_hg, lb, hg_norm_g[layer])

    rwt = router_w[layer].T
    rb = router_b[layer].reshape(ne, 1)
    x1, h2, idx8, gates8, rank8, cnt = _merge(
        x, hl, ya, yb, w_m, w_branch_a[layer].astype(BF16), w_branch_b[layer].astype(BF16),
        w_out[layer].astype(BF16), mod_l[:, 2], mod_l[:, 3], mod_l[:, 4], norm2_g[layer], rwt, rb,
        _pick_tile(t, 512))

    blk = MOE_BLOCK
    tok = b * t
    n_assign = tok * TOP_K
    nblocks = -(-n_assign // blk) + ne
    counts = cnt[:, 0].astype(I32)
    padded = (counts + blk - 1) // blk * blk
    pend = jnp.cumsum(padded)
    pstart = pend - padded
    idx = idx8[:TOP_K]
    onehot = idx[:, :, None] == jnp.arange(ne, dtype=I32)[None, None, :]
    dest = rank8[:TOP_K] + jnp.sum(jnp.where(onehot, pstart[None, None, :], 0), axis=-1)
    block_start = jnp.arange(nblocks, dtype=I32) * blk
    block_active = block_start < pend[-1]
    block_expert = jnp.minimum(jnp.sum(block_start[:, None] >= pend[None, :], axis=1), ne - 1).astype(I32)
    last_expert = jnp.max(jnp.where(block_active, block_expert, 0))
    block_expert = jnp.where(block_active, block_expert, last_expert).astype(I32)
    block_active = block_active.astype(I32)
    tok_ids = jnp.broadcast_to(jnp.arange(tok, dtype=I32)[None, :], (TOP_K, tok))
    slot_tok = jnp.zeros((nblocks * blk,), I32).at[dest.reshape(-1)].set(
        tok_ids.reshape(-1), unique_indices=True, mode="promise_in_bounds")

    b1 = moe_b1[layer]
    f = b1.shape[1] // 2
    b1g = b1[:, 0::2].reshape(ne, 1, f)
    b1u = b1[:, 1::2].reshape(ne, 1, f)
    y = _moe(block_expert, block_active, slot_tok.reshape(nblocks, 1, blk), h2, moe_w1[layer], b1g, b1u,
             moe_w2[layer], moe_b2[layer].reshape(ne, 1, d), blk)

    tm_f = _pick_tile(t, 256)
    ntiles = tok // tm_f
    dest3 = dest.reshape(TOP_K, ntiles, tm_f).transpose(1, 0, 2).reshape(ntiles, 1, TOP_K * tm_f)
    gates_t = gates8[:TOP_K].T
    return _final(dest3, y, x1, gates_t, mod_l[:, 5], final_g, tm_f)
```

```python
import functools

import jax
import jax.numpy as jnp
from jax import lax
from jax.experimental import pallas as pl
from jax.experimental.pallas import tpu as pltpu

F32 = jnp.float32
BF16 = jnp.bfloat16
I32 = jnp.int32
HIGHEST = lax.Precision.HIGHEST

EPS = 1e-6
N_MOD = 6
GRID_W = 64
CONV_LEFT = 2
CONV_WIDTH = 4
LRU_C = 8.0
LRU_BLOCK = 128
HG_DK = 128
HG_CHUNK = 64
TOP_K = 4
SWIGLU_LIMIT = 7.0
SWIGLU_ALPHA = 1.702
MOE_BLOCK = 256
SUBLANES = 8
LANES = 128
HALO = 8


def _params(sem, vmem_mb):
    return pltpu.CompilerParams(dimension_semantics=sem, vmem_limit_bytes=vmem_mb * 1024 * 1024)


def _dot(a, b):
    return jnp.dot(a, b, preferred_element_type=F32)


def _dot_nt(a, b):
    return lax.dot_general(a, b, (((1,), (1,)), ((), ())), preferred_element_type=F32)


def _dot_tn(a, b):
    return lax.dot_general(a, b, (((0,), (0,)), ((), ())), preferred_element_type=F32)


def _sigmoid(x):
    return jax.nn.sigmoid(x)


def _silu(x):
    return x * jax.nn.sigmoid(x)


def _rms(x):
    return x * lax.rsqrt(jnp.mean(x * x, axis=-1, keepdims=True) + EPS)


def _mod_kernel(c_ref, w_ref, b_ref, o_ref):
    s = _silu(c_ref[...])
    o_ref[...] = jnp.dot(s, w_ref[...], preferred_element_type=F32, precision=HIGHEST) + b_ref[...]


def _modulation(c_all, ada_w, ada_b):
    m, d = c_all.shape
    n = ada_w.shape[1]
    tn = 1024
    return pl.pallas_call(
        _mod_kernel,
        grid=(n // tn,),
        in_specs=[
            pl.BlockSpec((m, d), lambda j: (0, 0)),
            pl.BlockSpec((d, tn), lambda j: (0, j)),
            pl.BlockSpec((1, tn), lambda j: (0, j)),
        ],
        out_specs=pl.BlockSpec((m, tn), lambda j: (0, j)),
        out_shape=jax.ShapeDtypeStruct((m, n), F32),
        compiler_params=_params(("arbitrary",), 32),
        name="mod",
    )(c_all, ada_w, ada_b.reshape(1, n))


def _prenorm_kernel(x_ref, shift_ref, scale_ref, g_ref, o_ref):
    y = _rms(x_ref[...]) * g_ref[...]
    o_ref[...] = (y * (1.0 + scale_ref[...]) + shift_ref[...]).astype(o_ref.dtype)


def _prenorm(x, shift, scale, g, tm):
    b, l, d = x.shape
    per_batch = shift.shape[0] == b and b > 1
    mod_map = (lambda bi, i: (bi, 0, 0)) if per_batch else (lambda bi, i: (0, 0, 0))
    return pl.pallas_call(
        _prenorm_kernel,
        grid=(b, l // tm),
        in_specs=[
            pl.BlockSpec((None, tm, d), lambda bi, i: (bi, i, 0)),
            pl.BlockSpec((None, 1, d), mod_map),
            pl.BlockSpec((None, 1, d), mod_map),
            pl.BlockSpec((1, d), lambda bi, i: (0, 0)),
        ],
        out_specs=pl.BlockSpec((None, tm, d), lambda bi, i: (bi, i, 0)),
        out_shape=jax.ShapeDtypeStruct((b, l, d), BF16),
        compiler_params=_params(("arbitrary", "arbitrary"), 32),
        name="prenorm",
    )(x, shift, scale, g.reshape(1, d))


def _mm_kernel(x_ref, w_ref, o_ref):
    o_ref[...] = _dot(x_ref[...], w_ref[...]).astype(o_ref.dtype)


def _matmul(x2d, w, tm, out_dtype):
    m, k = x2d.shape
    n = w.shape[1]
    return pl.pallas_call(
        _mm_kernel,
        grid=(m // tm,),
        in_specs=[
            pl.BlockSpec((tm, k), lambda i: (i, 0)),
            pl.BlockSpec((k, n), lambda i: (0, 0)),
        ],
        out_specs=pl.BlockSpec((tm, n), lambda i: (i, 0)),
        out_shape=jax.ShapeDtypeStruct((m, n), out_dtype),
        compiler_params=_params(("arbitrary",), 40),
        name="inproj_lru",
    )(x2d, w)


def _lru_kernel(axl_ref, agl_ref, axc_ref, cw_ref, cb_ref, wg_ref, bg_ref, lam_ref, o_ref,
                xf_ref, yacc_ref, su_ref, sa_ref, sb_ref, sh_ref, *, nb, cx, t):
    tb = GRID_W
    lat0 = cx + 2 * HALO
    ctx0 = HALO
    rows = cx + t + 3 * HALO
    gw = 2 * LRU_BLOCK

    zeros_h = jnp.zeros((HALO, LRU_BLOCK), F32)
    for b in range(nb):
        xf_ref[b, 0:HALO, :] = zeros_h
        xf_ref[b, ctx0:ctx0 + cx, :] = axc_ref[b].astype(F32)
        xf_ref[b, ctx0 + cx:lat0, :] = zeros_h
        xf_ref[b, lat0:lat0 + t, :] = axl_ref[b].astype(F32)
        xf_ref[b, lat0 + t:rows, :] = zeros_h

    cw = cw_ref[...]
    cb = cb_ref[...]
    lam = lam_ref[...]
    sp = jnp.maximum(-lam, 0.0) + jnp.log1p(jnp.exp(-jnp.abs(lam)))
    tloc = lax.broadcasted_iota(I32, (tb, LRU_BLOCK), 0)

    def fill_gates(base, d, row_local):
        for b in range(nb):
            acc = jnp.broadcast_to(cb, (tb, LRU_BLOCK))
            for k in range(CONV_WIDTH):
                off = k - CONV_LEFT
                xs = xf_ref[b, pl.ds(base + off, tb), :]
                if row_local and off != 0:
                    ok = (tloc + off >= 0) & (tloc + off < tb)
                    xs = jnp.where(ok, xs, 0.0)
                acc = acc + xs * cw[k:k + 1, :]
            su_ref[pl.ds(b, tb, stride=nb), :] = acc
        u = su_ref[...]
        z = _dot(u.astype(BF16), wg_ref[:, d * gw:(d + 1) * gw]) + bg_ref[:, d * gw:(d + 1) * gw]
        r = _sigmoid(z[:, :LRU_BLOCK])
        i = _sigmoid(z[:, LRU_BLOCK:])
        log_a = (-LRU_C) * r * sp[d:d + 1, :]
        a = jnp.exp(log_a)
        mult = jnp.sqrt(-jnp.tanh(log_a) * (a * a + 1.0))
        sa_ref[...] = a
        sb_ref[...] = mult * (i * u)

    def scan_block(h, reverse):
        order = range(tb - 1, -1, -1) if reverse else range(tb)
        for s in order:
            h = sa_ref[s * nb:(s + 1) * nb, :] * h + sb_ref[s * nb:(s + 1) * nb, :]
            sh_ref[s * nb:(s + 1) * nb, :] = h
        return h

    def sweep(d):
        reverse = d == 1
        n_c, n_l = cx // tb, t // tb

        def ctx_body(j, h):
            jj = (n_c - 1 - j) if reverse else j
            base = pl.multiple_of(ctx0 + jj * tb, SUBLANES)
            fill_gates(base, d, False)
            return scan_block(h, reverse)

        def lat_body(j, h):
            jj = (n_l - 1 - j) if reverse else j
            r0 = pl.multiple_of(jj * tb, tb)
            fill_gates(pl.multiple_of(lat0 + r0, SUBLANES), d, True)
            h = scan_block(h, reverse)
            for b in range(nb):
                yb = sh_ref[pl.ds(b, tb, stride=nb), :]
                if not reverse:
                    yacc_ref[b, pl.ds(r0, tb), :] = yb
                else:
                    gate = jax.nn.gelu(agl_ref[b, pl.ds(r0, tb), :].astype(F32))
                    o_ref[b, pl.ds(r0, tb), :] = ((yacc_ref[b, pl.ds(r0, tb), :] + yb) * gate).astype(o_ref.dtype)
            return h

        h0 = jnp.zeros((nb, LRU_BLOCK), F32)
        h1 = lax.fori_loop(0, n_c, ctx_body, h0)
        lax.fori_loop(0, n_l, lat_body, h1)

    sweep(0)
    sweep(1)


def _lru(axl, axc, conv_w, conv_b, wg, bg, lam, nb, cx, t):
    w = conv_w.shape[1]
    g = w // LRU_BLOCK
    kern = functools.partial(_lru_kernel, nb=nb, cx=cx, t=t)
    return pl.pallas_call(
        kern,
        grid=(g,),
        in_specs=[
            pl.BlockSpec((nb, t, LRU_BLOCK), lambda j: (0, 0, j)),
            pl.BlockSpec((nb, t, LRU_BLOCK), lambda j: (0, 0, g + j)),
            pl.BlockSpec((nb, cx, LRU_BLOCK), lambda j: (0, 0, j)),
            pl.BlockSpec((CONV_WIDTH, LRU_BLOCK), lambda j: (0, j)),
            pl.BlockSpec((1, LRU_BLOCK), lambda j: (0, j)),
            pl.BlockSpec((None, LRU_BLOCK, 4 * LRU_BLOCK), lambda j: (j, 0, 0)),
            pl.BlockSpec((None, 1, 4 * LRU_BLOCK), lambda j: (j, 0, 0)),
            pl.BlockSpec((2, LRU_BLOCK), lambda j: (0, j)),
        ],
        out_specs=pl.BlockSpec((nb, t, LRU_BLOCK), lambda j: (0, 0, j)),
        out_shape=jax.ShapeDtypeStruct((nb, t, w), BF16),
        scratch_shapes=[
            pltpu.VMEM((nb, cx + t + 3 * HALO, LRU_BLOCK), F32),
            pltpu.VMEM((nb, t, LRU_BLOCK), F32),
            pltpu.VMEM((GRID_W * nb, LRU_BLOCK), F32),
            pltpu.VMEM((GRID_W * nb, LRU_BLOCK), F32),
            pltpu.VMEM((GRID_W * nb, LRU_BLOCK), F32),
            pltpu.VMEM((GRID_W * nb, LRU_BLOCK), F32),
        ],
        compiler_params=_params(("arbitrary",), 56),
        name="lru",
    )(axl, axl, axc, conv_w, conv_b.reshape(1, w), wg, bg, lam)


def _hgrn_kernel(hc_ref, hl_ref, w_ref, lb_ref, ng_ref, o_ref,
                 p_ref, oacc_ref, q_s, v_s, k_s, hl_s, gc_s, bk_s, sc_s, aq_s, upd_ref, dec_ref,
                 *, cx, t):
    c = HG_CHUNK
    dk = HG_DK
    n_c, n_l = cx // c, t // c
    n_all = n_c + n_l

    p_ref[0:cx, :] = _dot(hc_ref[...], w_ref[...])
    rows = 512 if t % 512 == 0 else c

    def proj_body(i, carry):
        r = pl.multiple_of(i * rows, rows)
        p_ref[pl.ds(cx + r, rows), :] = _dot(hl_ref[pl.ds(r, rows), :], w_ref[...])
        return carry

    lax.fori_loop(0, t // rows, proj_body, 0)

    lb = lb_ref[...]
    one_m_lb = 1.0 - lb
    ri = lax.broadcasted_iota(I32, (c, c), 0)
    ci = lax.broadcasted_iota(I32, (c, c), 1)
    keep = (ri >= ci, ci >= ri)
    mid = (c // 2 - 1, c // 2)
    end = (c - 1, 0)

    def stage_gates(j, carry):
        r0 = pl.multiple_of(j * c, c)
        q_s[pl.ds(r0, c), :] = _silu(p_ref[pl.ds(r0, c), 0:dk])
        v_s[pl.ds(r0, c), :] = p_ref[pl.ds(r0, c), 3 * dk:4 * dk].astype(BF16)
        for d in range(2):
            f = lb + one_m_lb * _sigmoid(p_ref[pl.ds(r0, c), (1 + d) * dk:(2 + d) * dk])
            logf = jnp.log(f)
            hi = logf.astype(BF16)
            lo = (logf - hi.astype(F32)).astype(BF16)
            k_s[d, pl.ds(r0, c), :] = 1.0 - f
            hl_s[d, pl.ds(r0, c), :] = jnp.concatenate([hi, lo], axis=1)
        return carry

    lax.fori_loop(0, n_all, stage_gates, 0, unroll=2)

    def stage_cumsum(j, carry):
        r0 = pl.multiple_of(j * c, c)
        for d in range(2):
            s2 = _dot(keep[d].astype(BF16), hl_s[d, pl.ds(r0, c), :])
            gc_s[d, pl.ds(r0, c), :] = s2[:, :dk] + s2[:, dk:]
        return carry

    lax.fori_loop(0, n_all, stage_cumsum, 0, unroll=6)

    def stage_scores(j, carry):
        r0 = pl.multiple_of(j * c, c)
        q = q_s[pl.ds(r0, c), :]
        for d in range(2):
            gc = gc_s[d, pl.ds(r0, c), :]
            k = k_s[d, pl.ds(r0, c), :]
            g_mid = gc[mid[d]:mid[d] + 1, :]
            g_end = gc[end[d]:end[d] + 1, :]
            a = (q * jnp.exp(gc - g_mid)).astype(BF16)
            bm = (k * jnp.exp(g_mid - gc)).astype(BF16)
            sc_s[d, pl.ds(r0, c), :] = jnp.where(keep[d], _dot_nt(a, bm), 0.0).astype(BF16)
            bk_s[d, pl.ds(r0, c), :] = (k * jnp.exp(g_end - gc)).astype(BF16)
            aq_s[d, pl.ds(r0, c), :] = (q * jnp.exp(gc)).astype(BF16)
            dec_ref[d, j] = jnp.broadcast_to(jnp.exp(g_end), (SUBLANES, dk))
        return carry

    lax.fori_loop(0, n_all, stage_scores, 0, unroll=3)

    def stage_intra(j, carry):
        r0 = pl.multiple_of(j * c, c)
        v = v_s[pl.ds(r0, c), :]
        oacc_ref[pl.ds(r0, c), :] = _dot(sc_s[0, pl.ds(r0, c), :], v) + _dot(sc_s[1, pl.ds(r0, c), :], v)
        for d in range(2):
            upd_ref[d, j] = _dot_tn(v, bk_s[d, pl.ds(r0, c), :])
        return carry

    lax.fori_loop(0, n_all, stage_intra, 0, unroll=6)

    def inter(d, jj, st):
        r0 = pl.multiple_of(jj * c, c)
        oacc_ref[pl.ds(r0, c), :] += _dot_nt(aq_s[d, pl.ds(r0, c), :], st.astype(BF16))
        return st * dec_ref[d, jj][0:1, :] + upd_ref[d, jj]

    def ctx_body(j, sts):
        return inter(0, j, sts[0]), inter(1, n_c - 1 - j, sts[1])

    def lat_body(j, sts):
        return inter(0, n_c + j, sts[0]), inter(1, n_c + n_l - 1 - j, sts[1])

    zero = jnp.zeros((dk, dk), F32)
    sts = lax.fori_loop(0, n_c, ctx_body, (zero, zero), unroll=2)
    lax.fori_loop(0, n_l, lat_body, sts, unroll=4)

    ng = ng_ref[...]

    def readout(i, carry):
        r = pl.multiple_of(i * rows, rows)
        o = oacc_ref[pl.ds(cx + r, rows), :]
        g = p_ref[pl.ds(cx + r, rows), 4 * dk:5 * dk]
        o_ref[pl.ds(r, rows), :] = (_rms(o) * ng * _silu(g)).astype(o_ref.dtype)
        return carry

    lax.fori_loop(0, t // rows, readout, 0)


def _hgrn(hc, hl, w_hg, lb, norm_g):
    b, cx, d = hc.shape
    t = hl.shape[1]
    heads = w_hg.shape[0]
    dk = HG_DK
    n_all = (cx + t) // HG_CHUNK
    kern = functools.partial(_hgrn_kernel, cx=cx, t=t)
    return pl.pallas_call(
        kern,
        grid=(b, heads),
        in_specs=[
            pl.BlockSpec((None, cx, d), lambda bi, h: (bi, 0, 0)),
            pl.BlockSpec((None, t, d), lambda bi, h: (bi, 0, 0)),
            pl.BlockSpec((None, d, 5 * dk), lambda bi, h: (h, 0, 0)),
            pl.BlockSpec((None, 1, dk), lambda bi, h: (h, 0, 0)),
            pl.BlockSpec((1, dk), lambda bi, h: (0, 0)),
        ],
        out_specs=pl.BlockSpec((None, t, dk), lambda bi, h: (bi, 0, h)),
        out_shape=jax.ShapeDtypeStruct((b, t, heads * dk), BF16),
        scratch_shapes=[
            pltpu.VMEM((cx + t, 5 * dk), F32),
            pltpu.VMEM((cx + t, dk), F32),
            pltpu.VMEM((cx + t, dk), F32),
            pltpu.VMEM((cx + t, dk), BF16),
            pltpu.VMEM((2, cx + t, dk), F32),
            pltpu.VMEM((2, cx + t, 2 * dk), BF16),
            pltpu.VMEM((2, cx + t, dk), F32),
            pltpu.VMEM((2, cx + t, dk), BF16),
            pltpu.VMEM((2, cx + t, HG_CHUNK), BF16),
            pltpu.VMEM((2, cx + t, dk), BF16),
            pltpu.VMEM((2, n_all, dk, dk), F32),
            pltpu.VMEM((2, n_all, SUBLANES, dk), F32),
        ],
        compiler_params=_params(("arbitrary", "arbitrary"), 48),
        name="hgrn",
    )(hc, hl, w_hg, lb, norm_g.reshape(1, dk))


def _merge_kernel(x_ref, h_ref, ya_ref, yb_ref, wm_ref, wa_ref, wb_ref, wo_ref,
                  gate_ref, shift_ref, scale_ref, g2_ref, rwt_ref, rb_ref,
                  x1_ref, h2_ref, idx_ref, gates_ref, rank_ref, cnt_ref, carry_ref, *, tm, d, ne):
    first = (pl.program_id(0) == 0) & (pl.program_id(1) == 0)

    @pl.when(first)
    def _():
        carry_ref[...] = jnp.zeros_like(carry_ref)

    m = _dot(h_ref[...], wm_ref[...])
    za = _dot(ya_ref[...], wa_ref[...])
    zb = _dot(yb_ref[...], wb_ref[...])
    mix = _sigmoid(m[:, :d]) * za + _sigmoid(m[:, d:]) * zb
    out = _dot(mix.astype(BF16), wo_ref[...])
    x1 = x_ref[...] + gate_ref[...] * out
    x1_ref[...] = x1
    h2 = _rms(x1) * g2_ref[...] * (1.0 + scale_ref[...]) + shift_ref[...]
    for s in range(d // LANES):
        h2_ref[pl.ds(s, tm, stride=d // LANES), :] = h2[:, s * LANES:(s + 1) * LANES]

    logits = lax.dot_general(rwt_ref[...], h2, (((1,), (1,)), ((), ())),
                             preferred_element_type=F32, precision=HIGHEST) + rb_ref[...]
    eid = lax.broadcasted_iota(I32, (ne, tm), 0)
    neg = jnp.float32(-jnp.inf)
    work = logits
    hot = jnp.zeros((ne, tm), F32)
    vals, idxs, sels = [], [], []
    for _ in range(TOP_K):
        mx = jnp.max(work, axis=0, keepdims=True)
        ix = jnp.min(jnp.where(work == mx, eid, ne), axis=0, keepdims=True)
        sel = eid == ix
        work = jnp.where(sel, neg, work)
        hot = hot + sel.astype(F32)
        vals.append(mx)
        idxs.append(ix)
        sels.append(sel)
    ex = [jnp.exp(v - vals[0]) for v in vals]
    den = ex[0] + ex[1] + ex[2] + ex[3]
    gts = [e / den for e in ex]

    ti = lax.broadcasted_iota(I32, (tm, tm), 0)
    tj = lax.broadcasted_iota(I32, (tm, tm), 1)
    strict = (ti < tj).astype(BF16)
    before = _dot(hot.astype(BF16), strict) + carry_ref[...]
    rks = [jnp.sum(jnp.where(s, before, 0.0), axis=0, keepdims=True) for s in sels]
    carry_ref[...] = carry_ref[...] + jnp.sum(hot, axis=1, keepdims=True)

    row = lax.broadcasted_iota(I32, (SUBLANES, tm), 0)

    def pack(parts, zero):
        acc = jnp.full((SUBLANES, tm), zero, parts[0].dtype)
        for k, p in enumerate(parts):
            acc = jnp.where(row == k, jnp.broadcast_to(p, (SUBLANES, tm)), acc)
        return acc

    idx_ref[...] = pack(idxs, 0)
    gates_ref[...] = pack(gts, 0.0)
    rank_ref[...] = pack(rks, 0.0).astype(I32)
    cnt_ref[...] = jnp.broadcast_to(carry_ref[...], cnt_ref.shape)


def _merge(x, hl, ya, yb, w_m, w_ba, w_bb, w_out, gate2, shift3, scale4, g2, rwt, rb, tm):
    b, t, d = x.shape
    ne = rwt.shape[0]
    nt = t // tm
    tok = b * t
    kern = functools.partial(_merge_kernel, tm=tm, d=d, ne=ne)
    row_spec = pl.BlockSpec((None, tm, d), lambda bi, i: (bi, i, 0))
    mod_spec = pl.BlockSpec((None, 1, d), lambda bi, i: (bi, 0, 0))
    flat = lambda bi, i: (bi * nt + i, 0)
    lane = lambda bi, i: (0, bi * nt + i)

    def whole(shape):
        return pl.BlockSpec(shape, lambda bi, i: (0,) * len(shape))

    return pl.pallas_call(
        kern,
        grid=(b, nt),
        in_specs=[
            row_spec, row_spec, row_spec, row_spec,
            whole((d, 2 * d)), whole((d, d)), whole((d, d)), whole((d, d)),
            mod_spec, mod_spec, mod_spec,
            whole((1, d)), whole((ne, d)), whole((ne, 1)),
        ],
        out_specs=[
            row_spec,
            pl.BlockSpec((tm * (d // LANES), LANES), flat),
            pl.BlockSpec((SUBLANES, tm), lane),
            pl.BlockSpec((SUBLANES, tm), lane),
            pl.BlockSpec((SUBLANES, tm), lane),
            whole((ne, 128)),
        ],
        out_shape=[
            jax.ShapeDtypeStruct((b, t, d), F32),
            jax.ShapeDtypeStruct((tok * (d // LANES), LANES), F32),
            jax.ShapeDtypeStruct((SUBLANES, tok), I32),
            jax.ShapeDtypeStruct((SUBLANES, tok), F32),
            jax.ShapeDtypeStruct((SUBLANES, tok), I32),
            jax.ShapeDtypeStruct((ne, 128), F32),
        ],
        scratch_shapes=[pltpu.VMEM((ne, 1), F32)],
        compiler_params=_params(("arbitrary", "arbitrary"), 56),
        name="merge",
    )(x, hl, ya, yb, w_m, w_ba, w_bb, w_out, gate2, shift3, scale4, g2.reshape(1, d), rwt, rb)


def _moe_kernel(bexp_ref, bact_ref, tok_ref, tokn_ref, dstp_ref, h2_hbm, w1_ref, b1_ref, w2_ref, b2_ref,
                yo_hbm, xbuf, ybuf, xb_s, act_s, w1p_s, w2_s, gsem, ssem, *, blk, nblocks):
    j = pl.program_id(0)
    slot = lax.rem(j, 2)
    nslot = 1 - slot
    d = xb_s.shape[1]
    f = w2_s.shape[1]
    ts = d // LANES
    pw = 2 * LANES
    active = bact_ref[j] == 1

    def gather_copy(tok, buf_slot, r):
        src = pl.multiple_of(tok * ts, ts)
        dst = pl.multiple_of((buf_slot * blk + r) * ts, ts)
        return pltpu.make_async_copy(h2_hbm.at[pl.ds(src, ts), :], xbuf.at[pl.ds(dst, ts), :], gsem.at[buf_slot])

    def scatter_copy(row, buf_slot, r):
        src = pl.multiple_of((buf_slot * blk + r) * ts, ts)
        dst = pl.multiple_of(row * ts, ts)
        return pltpu.make_async_copy(ybuf.at[pl.ds(src, ts), :], yo_hbm.at[pl.ds(dst, ts), :], ssem.at[buf_slot])

    def gather_loop(idx_ref, buf_slot):
        def body(r, carry):
            gather_copy(idx_ref[0, 0, r], buf_slot, r).start()
            return carry
        lax.fori_loop(0, blk, body, 0, unroll=8)

    def scatter_loop(buf_slot):
        def body(r, carry):
            scatter_copy(dstp_ref[0, 0, r], buf_slot, r).start()
            return carry
        lax.fori_loop(0, blk, body, 0, unroll=8)

    def scatter_wait_loop(buf_slot):
        def body(r, carry):
            scatter_copy(0, buf_slot, r).wait()
            return carry
        lax.fori_loop(0, blk, body, 0, unroll=8)

    @pl.when(j == 0)
    def _():
        gather_loop(tok_ref, 0)
        ybuf[...] = jnp.zeros_like(ybuf)
        n_scratch = 2 * blk * ts
        fill = pltpu.make_async_copy(ybuf, yo_hbm.at[pl.ds(yo_hbm.shape[0] - n_scratch, n_scratch), :],
                                     ssem.at[0])
        fill.start()
        fill.wait()

    def gather_wait(r, carry):
        gather_copy(0, slot, r).wait()
        return carry

    lax.fori_loop(0, blk, gather_wait, 0, unroll=8)

    prev_active = bact_ref[jnp.maximum(j - 1, 0)] == 1
    @pl.when((j == 1) | ((j >= 2) & (bact_ref[jnp.maximum(j - 2, 0)] == 1)))
    def _():
        scatter_wait_loop(slot)

    new_expert = (j == 0) | (bexp_ref[j] != bexp_ref[jnp.maximum(j - 1, 0)])

    @pl.when(new_expert & active)
    def _():
        src = lax.broadcasted_iota(I32, (pw, pw), 0)
        dst = lax.broadcasted_iota(I32, (pw, pw), 1)
        want = jnp.where(dst < pw // 2, 2 * dst, 2 * (dst - pw // 2) + 1)
        sel = (src == want).astype(BF16)
        for cb in range(2 * f // pw):
            w1p_s[cb] = _dot(w1_ref[:, cb * pw:(cb + 1) * pw].astype(BF16), sel).astype(BF16)
        for cb in range(d // pw):
            w2_s[cb] = w2_ref[:, cb * pw:(cb + 1) * pw].astype(BF16)

    n1 = 2 * f // pw
    n2 = d // pw
    per1 = (blk * 5 // 8) // n1
    per2 = (blk - per1 * n1) // n2
    assert per1 * n1 + per2 * n2 == blk

    xbase = pl.multiple_of(slot * blk * ts, ts)

    @pl.when(active)
    def _():
        for s in range(ts):
            xb_s[:, s * LANES:(s + 1) * LANES] = xbuf[pl.ds(xbase + s, blk, stride=ts), :].astype(BF16)

    def transfers(r0, count):
        for r in range(r0, r0 + count):
            gather_copy(tokn_ref[0, 0, r], nslot, r).start()
        for r in range(r0, r0 + count):
            scatter_copy(dstp_ref[0, 0, r], nslot, r).start()

    @pl.when(active)
    def _():
        for n in range(n1):
            h = _dot(xb_s[...], w1p_s[n]) + b1_ref[n]
            hg = jnp.minimum(h[:, :LANES], SWIGLU_LIMIT)
            hu = jnp.clip(h[:, LANES:], -SWIGLU_LIMIT, SWIGLU_LIMIT)
            act_s[n] = (hg * _sigmoid(SWIGLU_ALPHA * hg) * (hu + 1.0)).astype(BF16)
            transfers(n * per1, per1)
        act = jnp.concatenate([act_s[i] for i in range(n1)], axis=1)
        for n in range(n2):
            y = _dot(act, w2_s[n]) + b2_ref[n]
            ybuf[pl.ds(xbase + 2 * n, blk, stride=ts), :] = y[:, :LANES]
            ybuf[pl.ds(xbase + 2 * n + 1, blk, stride=ts), :] = y[:, LANES:]
            transfers(n1 * per1 + n * per2, per2)

    @pl.when(jnp.logical_not(active) & (j + 1 < nblocks))
    def _():
        gather_loop(tokn_ref, nslot)

    @pl.when(jnp.logical_not(active) & prev_active)
    def _():
        scatter_loop(nslot)

    @pl.when((j == nblocks - 1) & prev_active)
    def _():
        scatter_wait_loop(nslot)


def _moe(block_expert, block_active, slot_tok3, slot_dst3, h2t, w1, b1p, w2, b2, n_out_rows, blk):
    nblocks = block_expert.shape[0]
    f, d = w2.shape[1], w2.shape[2]
    ts = d // LANES
    pw = 2 * LANES
    n1, n2 = 2 * f // pw, d // pw
    b1p = b1p.reshape(-1, n1, 1, pw)
    b2 = b2.reshape(-1, n2, 1, pw)
    kern = functools.partial(_moe_kernel, blk=blk, nblocks=nblocks)
    wmap = lambda j, be, ba: (be[j], 0, 0)
    grid_spec = pltpu.PrefetchScalarGridSpec(
        num_scalar_prefetch=2,
        grid=(nblocks,),
        in_specs=[
            pl.BlockSpec((1, 1, blk), lambda j, be, ba: (j, 0, 0), memory_space=pltpu.SMEM),
            pl.BlockSpec((1, 1, blk), lambda j, be, ba: (jnp.minimum(j + 1, nblocks - 1), 0, 0),
                         memory_space=pltpu.SMEM),
            pl.BlockSpec((1, 1, blk), lambda j, be, ba: (j, 0, 0), memory_space=pltpu.SMEM),
            pl.BlockSpec(memory_space=pl.ANY),
            pl.BlockSpec((None, d, 2 * f), wmap),
            pl.BlockSpec((None, n1, 1, pw), lambda j, be, ba: (be[j], 0, 0, 0)),
            pl.BlockSpec((None, f, d), wmap),
            pl.BlockSpec((None, n2, 1, pw), lambda j, be, ba: (be[j], 0, 0, 0)),
        ],
        out_specs=pl.BlockSpec(memory_space=pl.ANY),
        scratch_shapes=[
            pltpu.VMEM((2 * blk * ts, LANES), F32),
            pltpu.VMEM((2 * blk * ts, LANES), F32),
            pltpu.VMEM((blk, d), BF16),
            pltpu.VMEM((n1, blk, LANES), BF16),
            pltpu.VMEM((n1, d, pw), BF16),
            pltpu.VMEM((n2, f, pw), BF16),
            pltpu.SemaphoreType.DMA((2,)),
            pltpu.SemaphoreType.DMA((2,)),
        ],
    )
    return pl.pallas_call(
        kern,
        grid_spec=grid_spec,
        out_shape=jax.ShapeDtypeStruct((n_out_rows * ts, LANES), F32),
        compiler_params=_params(("arbitrary",), 56),
        name="moe",
    )(block_expert, block_active, slot_tok3, slot_tok3, slot_dst3, h2t, w1, b1p, w2, b2)


def _final_kernel(*refs, tm, d):
    y_refs = refs[:TOP_K]
    x1_ref, gt_ref, gate_ref, fg_ref, o_ref = refs[TOP_K:]
    ts = d // LANES
    gt = gt_ref[...]
    acc = None
    for k in range(TOP_K):
        yk = jnp.concatenate([y_refs[k][pl.ds(s, tm, stride=ts), :] for s in range(ts)], axis=1)
        term = gt[:, k:k + 1] * yk
        acc = term if acc is None else acc + term
    x2 = x1_ref[...] + gate_ref[...] * acc
    o_ref[...] = _rms(x2) * fg_ref[...]


def _final(yo, x1, gates_t, gate5, final_g, tm):
    b, t, d = x1.shape
    ts = d // LANES
    nt = t // tm
    ntiles = b * nt
    kern = functools.partial(_final_kernel, tm=tm, d=d)
    row_map = lambda i: (i // nt, i % nt, 0)
    y_specs = [pl.BlockSpec((tm * ts, LANES), functools.partial(lambda i, k: (k * ntiles + i, 0), k=k))
               for k in range(TOP_K)]
    return pl.pallas_call(
        kern,
        grid=(ntiles,),
        in_specs=y_specs + [
            pl.BlockSpec((None, tm, d), row_map),
            pl.BlockSpec((tm, TOP_K), lambda i: (i, 0)),
            pl.BlockSpec((None, 1, d), lambda i: (i // nt, 0, 0)),
            pl.BlockSpec((1, d), lambda i: (0, 0)),
        ],
        out_specs=pl.BlockSpec((None, tm, d), row_map),
        out_shape=jax.ShapeDtypeStruct((b, t, d), F32),
        compiler_params=_params(("arbitrary",), 48),
        name="final",
    )(*([yo] * TOP_K), x1, gates_t, gate5, final_g.reshape(1, d))


def _pick_tile(n, pref):
    tm = pref
    while n % tm:
        tm //= 2
    return tm


def kernel(x, c, ctx, c_ctx, ada_w, ada_b, norm1_g, norm2_g, w_in, lru_conv_w, lru_conv_b, lru_wa, lru_ba, lru_wx, lru_bx, lru_lam, hg_lb_logits, hg_norm_g, w_branch_a, w_branch_b, w_out, router_w, router_b, moe_w1, moe_b1, moe_w2, moe_b2, final_g):
    b, t, d = x.shape
    cx = ctx.shape[1]
    layer = 0
    w_lru = lru_conv_w.shape[2]
    qk = (w_in.shape[2] - 2 * w_lru - 2 * d) // 5
    heads = qk // HG_DK
    ne = router_w.shape[2]

    pad = (-(b + 1)) % SUBLANES
    c_all = jnp.concatenate([c, c_ctx[None, :], jnp.zeros((pad, d), F32)], axis=0)
    mod = _modulation(c_all, ada_w[layer], ada_b[layer])
    mod_l = mod[:b].reshape(b, N_MOD, 1, d)
    mod_c = mod[b].reshape(N_MOD, 1, 1, d)

    w_in_b = w_in[layer].astype(BF16)
    w_lru_in = w_in_b[:, :2 * w_lru]
    hg0 = 2 * w_lru
    w_hg = w_in_b[:, hg0:hg0 + 5 * qk].reshape(d, 5, heads, HG_DK).transpose(2, 0, 1, 3).reshape(heads, d, 5 * HG_DK)
    w_m = w_in_b[:, hg0 + 5 * qk:]
    g_blocks = w_lru // LRU_BLOCK
    wg = jnp.concatenate([lru_wa[layer, 0], lru_wx[layer, 0], lru_wa[layer, 1], lru_wx[layer, 1]], axis=-1).astype(BF16)
    bg = jnp.concatenate([lru_ba[layer, 0], lru_bx[layer, 0], lru_ba[layer, 1], lru_bx[layer, 1]], axis=-1)
    bg = bg.reshape(g_blocks, 1, 4 * LRU_BLOCK)
    lb_all = jnp.cumsum(jax.nn.softmax(hg_lb_logits.astype(F32), axis=0), axis=0)
    lb = lb_all[layer].reshape(heads, 1, HG_DK)

    tm_n = _pick_tile(t, 512)
    hl = _prenorm(x, mod_l[:, 0], mod_l[:, 1], norm1_g[layer], tm_n)
    hc = _prenorm(ctx, mod_c[0], mod_c[1], norm1_g[layer], _pick_tile(cx, 256))
    axl = _matmul(hl.reshape(b * t, d), w_lru_in, _pick_tile(b * t, 512), BF16).reshape(b, t, 2 * w_lru)
    axc = _matmul(hc.reshape(b * cx, d), w_lru_in, _pick_tile(b * cx, 512), BF16).reshape(b, cx, 2 * w_lru)
    ya = _lru(axl, axc, lru_conv_w[layer], lru_conv_b[layer], wg, bg, lru_lam[layer], b, cx, t)
    yb = _hgrn(hc, hl, w_hg, lb, hg_norm_g[layer])

    rwt = router_w[layer].T
    rb = router_b[layer].reshape(ne, 1)
    x1, h2, idx8, gates8, rank8, cnt = _merge(
        x, hl, ya, yb, w_m, w_branch_a[layer].astype(BF16), w_branch_b[layer].astype(BF16),
        w_out[layer].astype(BF16), mod_l[:, 2], mod_l[:, 3], mod_l[:, 4], norm2_g[layer], rwt, rb,
        _pick_tile(t, 512))

    blk = MOE_BLOCK
    tok = b * t
    n_assign = tok * TOP_K
    nblocks = -(-n_assign // blk) + ne
    counts = cnt[:, 0].astype(I32)
    padded = (counts + blk - 1) // blk * blk
    pend = jnp.cumsum(padded)
    pstart = pend - padded
    idx = idx8[:TOP_K]
    onehot = idx[:, :, None] == jnp.arange(ne, dtype=I32)[None, None, :]
    dest = rank8[:TOP_K] + jnp.sum(jnp.where(onehot, pstart[None, None, :], 0), axis=-1)
    block_start = jnp.arange(nblocks, dtype=I32) * blk
    block_active = block_start < pend[-1]
    block_expert = jnp.minimum(jnp.sum(block_start[:, None] >= pend[None, :], axis=1), ne - 1).astype(I32)
    last_expert = jnp.max(jnp.where(block_active, block_expert, 0))
    block_expert = jnp.where(block_active, block_expert, last_expert).astype(I32)
    block_active = block_active.astype(I32)
    codes = jnp.arange(n_assign, dtype=I32)
    slot_code = jnp.full((nblocks * blk,), -1, I32).at[dest.reshape(-1)].set(
        codes, unique_indices=True, mode="promise_in_bounds")
    slot_tok = jnp.where(slot_code >= 0, slot_code % tok, 0)
    slot_ids = jnp.arange(nblocks * blk, dtype=I32)
    scratch_row = n_assign + (slot_ids // blk % 2) * blk + slot_ids % blk
    slot_dst = jnp.where(slot_code >= 0, slot_code, scratch_row)
    first_dst = n_assign + blk + jnp.arange(blk, dtype=I32)
    slot_dst = jnp.concatenate([first_dst, slot_dst[:-blk]])

    b1 = moe_b1[layer]
    f = b1.shape[1] // 2
    b1p = b1.reshape(ne, f // LANES, LANES, 2).transpose(0, 1, 3, 2).reshape(ne, 1, 2 * f)
    yo = _moe(block_expert, block_active, slot_tok.reshape(nblocks, 1, blk), slot_dst.reshape(nblocks, 1, blk),
              h2, moe_w1[layer], b1p, moe_w2[layer], moe_b2[layer].reshape(ne, 1, d), n_assign + 2 * blk, blk)

    gates_t = gates8[:TOP_K].T
    return _final(yo, x1, gates_t, mod_l[:, 5], final_g, _pick_tile(t, 256))
```

```python
import functools

import jax
import jax.numpy as jnp
from jax import lax
from jax.experimental import pallas as pl
from jax.experimental.pallas import tpu as pltpu

F32 = jnp.float32
BF16 = jnp.bfloat16
I32 = jnp.int32
HIGHEST = lax.Precision.HIGHEST

EPS = 1e-6
N_MOD = 6
GRID_W = 64
CONV_LEFT = 2
CONV_WIDTH = 4
LRU_C = 8.0
LRU_BLOCK = 128
HG_DK = 128
HG_CHUNK = 64
TOP_K = 4
SWIGLU_LIMIT = 7.0
SWIGLU_ALPHA = 1.702
MOE_BLOCK = 256
MOE_RING = 3
SUBLANES = 8
LANES = 128
HALO = 8


def _params(sem, vmem_mb):
    return pltpu.CompilerParams(dimension_semantics=sem, vmem_limit_bytes=vmem_mb * 1024 * 1024)


def _dot(a, b):
    return jnp.dot(a, b, preferred_element_type=F32)


def _dot_nt(a, b):
    return lax.dot_general(a, b, (((1,), (1,)), ((), ())), preferred_element_type=F32)


def _dot_tn(a, b):
    return lax.dot_general(a, b, (((0,), (0,)), ((), ())), preferred_element_type=F32)


def _sigmoid(x):
    return jax.nn.sigmoid(x)


def _silu(x):
    return x * jax.nn.sigmoid(x)


def _rms(x):
    return x * lax.rsqrt(jnp.mean(x * x, axis=-1, keepdims=True) + EPS)


def _mod_kernel(c_ref, w_ref, b_ref, o_ref):
    s = _silu(c_ref[...])
    o_ref[...] = jnp.dot(s, w_ref[...], preferred_element_type=F32, precision=HIGHEST) + b_ref[...]


def _modulation(c_all, ada_w, ada_b):
    m, d = c_all.shape
    n = ada_w.shape[1]
    tn = 1024
    return pl.pallas_call(
        _mod_kernel,
        grid=(n // tn,),
        in_specs=[
            pl.BlockSpec((m, d), lambda j: (0, 0)),
            pl.BlockSpec((d, tn), lambda j: (0, j)),
            pl.BlockSpec((1, tn), lambda j: (0, j)),
        ],
        out_specs=pl.BlockSpec((m, tn), lambda j: (0, j)),
        out_shape=jax.ShapeDtypeStruct((m, n), F32),
        compiler_params=_params(("arbitrary",), 32),
        name="mod",
    )(c_all, ada_w, ada_b.reshape(1, n))


def _prenorm_kernel(x_ref, shift_ref, scale_ref, g_ref, o_ref):
    y = _rms(x_ref[...]) * g_ref[...]
    o_ref[...] = (y * (1.0 + scale_ref[...]) + shift_ref[...]).astype(o_ref.dtype)


def _prenorm(x, shift, scale, g, tm):
    b, l, d = x.shape
    per_batch = shift.shape[0] == b and b > 1
    mod_map = (lambda bi, i: (bi, 0, 0)) if per_batch else (lambda bi, i: (0, 0, 0))
    return pl.pallas_call(
        _prenorm_kernel,
        grid=(b, l // tm),
        in_specs=[
            pl.BlockSpec((None, tm, d), lambda bi, i: (bi, i, 0)),
            pl.BlockSpec((None, 1, d), mod_map),
            pl.BlockSpec((None, 1, d), mod_map),
            pl.BlockSpec((1, d), lambda bi, i: (0, 0)),
        ],
        out_specs=pl.BlockSpec((None, tm, d), lambda bi, i: (bi, i, 0)),
        out_shape=jax.ShapeDtypeStruct((b, l, d), BF16),
        compiler_params=_params(("arbitrary", "arbitrary"), 32),
        name="prenorm",
    )(x, shift, scale, g.reshape(1, d))


def _mm_kernel(x_ref, w_ref, o_ref):
    o_ref[...] = _dot(x_ref[...], w_ref[...]).astype(o_ref.dtype)


def _matmul(x2d, w, tm, out_dtype):
    m, k = x2d.shape
    n = w.shape[1]
    return pl.pallas_call(
        _mm_kernel,
        grid=(m // tm,),
        in_specs=[
            pl.BlockSpec((tm, k), lambda i: (i, 0)),
            pl.BlockSpec((k, n), lambda i: (0, 0)),
        ],
        out_specs=pl.BlockSpec((tm, n), lambda i: (i, 0)),
        out_shape=jax.ShapeDtypeStruct((m, n), out_dtype),
        compiler_params=_params(("arbitrary",), 40),
        name="inproj_lru",
    )(x2d, w)


def _lru_kernel(axl_ref, agl_ref, axc_ref, cw_ref, cb_ref, wg_ref, bg_ref, lam_ref, o_ref,
                xf_ref, yacc_ref, su_ref, sa_ref, sb_ref, sh_ref, *, nb, cx, t):
    tb = GRID_W
    lat0 = cx + 2 * HALO
    ctx0 = HALO
    rows = cx + t + 3 * HALO
    gw = 2 * LRU_BLOCK

    zeros_h = jnp.zeros((HALO, LRU_BLOCK), F32)
    for b in range(nb):
        xf_ref[b, 0:HALO, :] = zeros_h
        xf_ref[b, ctx0:ctx0 + cx, :] = axc_ref[b].astype(F32)
        xf_ref[b, ctx0 + cx:lat0, :] = zeros_h
        xf_ref[b, lat0:lat0 + t, :] = axl_ref[b].astype(F32)
        xf_ref[b, lat0 + t:rows, :] = zeros_h

    cw = cw_ref[...]
    cb = cb_ref[...]
    lam = lam_ref[...]
    sp = jnp.maximum(-lam, 0.0) + jnp.log1p(jnp.exp(-jnp.abs(lam)))
    tloc = lax.broadcasted_iota(I32, (tb, LRU_BLOCK), 0)

    def fill_gates(base, d, row_local):
        for b in range(nb):
            acc = jnp.broadcast_to(cb, (tb, LRU_BLOCK))
            for k in range(CONV_WIDTH):
                off = k - CONV_LEFT
                xs = xf_ref[b, pl.ds(base + off, tb), :]
                if row_local and off != 0:
                    ok = (tloc + off >= 0) & (tloc + off < tb)
                    xs = jnp.where(ok, xs, 0.0)
                acc = acc + xs * cw[k:k + 1, :]
            su_ref[pl.ds(b, tb, stride=nb), :] = acc
        u = su_ref[...]
        z = _dot(u.astype(BF16), wg_ref[:, d * gw:(d + 1) * gw]) + bg_ref[:, d * gw:(d + 1) * gw]
        r = _sigmoid(z[:, :LRU_BLOCK])
        i = _sigmoid(z[:, LRU_BLOCK:])
        log_a = (-LRU_C) * r * sp[d:d + 1, :]
        a = jnp.exp(log_a)
        mult = jnp.sqrt(-jnp.tanh(log_a) * (a * a + 1.0))
        sa_ref[...] = a
        sb_ref[...] = mult * (i * u)

    def scan_block(h, reverse):
        order = range(tb - 1, -1, -1) if reverse else range(tb)
        for s in order:
            h = sa_ref[s * nb:(s + 1) * nb, :] * h + sb_ref[s * nb:(s + 1) * nb, :]
            sh_ref[s * nb:(s + 1) * nb, :] = h
        return h

    def sweep(d):
        reverse = d == 1
        n_c, n_l = cx // tb, t // tb

        def ctx_body(j, h):
            jj = (n_c - 1 - j) if reverse else j
            base = pl.multiple_of(ctx0 + jj * tb, SUBLANES)
            fill_gates(base, d, False)
            return scan_block(h, reverse)

        def lat_body(j, h):
            jj = (n_l - 1 - j) if reverse else j
            r0 = pl.multiple_of(jj * tb, tb)
            fill_gates(pl.multiple_of(lat0 + r0, SUBLANES), d, True)
            h = scan_block(h, reverse)
            for b in range(nb):
                yb = sh_ref[pl.ds(b, tb, stride=nb), :]
                if not reverse:
                    yacc_ref[b, pl.ds(r0, tb), :] = yb
                else:
                    gate = jax.nn.gelu(agl_ref[b, pl.ds(r0, tb), :].astype(F32))
                    o_ref[b, pl.ds(r0, tb), :] = ((yacc_ref[b, pl.ds(r0, tb), :] + yb) * gate).astype(o_ref.dtype)
            return h

        h0 = jnp.zeros((nb, LRU_BLOCK), F32)
        h1 = lax.fori_loop(0, n_c, ctx_body, h0)
        lax.fori_loop(0, n_l, lat_body, h1)

    sweep(0)
    sweep(1)


def _lru(axl, axc, conv_w, conv_b, wg, bg, lam, nb, cx, t):
    w = conv_w.shape[1]
    g = w // LRU_BLOCK
    kern = functools.partial(_lru_kernel, nb=nb, cx=cx, t=t)
    return pl.pallas_call(
        kern,
        grid=(g,),
        in_specs=[
            pl.BlockSpec((nb, t, LRU_BLOCK), lambda j: (0, 0, j)),
            pl.BlockSpec((nb, t, LRU_BLOCK), lambda j: (0, 0, g + j)),
            pl.BlockSpec((nb, cx, LRU_BLOCK), lambda j: (0, 0, j)),
            pl.BlockSpec((CONV_WIDTH, LRU_BLOCK), lambda j: (0, j)),
            pl.BlockSpec((1, LRU_BLOCK), lambda j: (0, j)),
            pl.BlockSpec((None, LRU_BLOCK, 4 * LRU_BLOCK), lambda j: (j, 0, 0)),
            pl.BlockSpec((None, 1, 4 * LRU_BLOCK), lambda j: (j, 0, 0)),
            pl.BlockSpec((2, LRU_BLOCK), lambda j: (0, j)),
        ],
        out_specs=pl.BlockSpec((nb, t, LRU_BLOCK), lambda j: (0, 0, j)),
        out_shape=jax.ShapeDtypeStruct((nb, t, w), BF16),
        scratch_shapes=[
            pltpu.VMEM((nb, cx + t + 3 * HALO, LRU_BLOCK), F32),
            pltpu.VMEM((nb, t, LRU_BLOCK), F32),
            pltpu.VMEM((GRID_W * nb, LRU_BLOCK), F32),
            pltpu.VMEM((GRID_W * nb, LRU_BLOCK), F32),
            pltpu.VMEM((GRID_W * nb, LRU_BLOCK), F32),
            pltpu.VMEM((GRID_W * nb, LRU_BLOCK), F32),
        ],
        compiler_params=_params(("arbitrary",), 56),
        name="lru",
    )(axl, axl, axc, conv_w, conv_b.reshape(1, w), wg, bg, lam)


def _hgrn_kernel(hc_ref, hl_ref, w_ref, lb_ref, ng_ref, o_ref,
                 p_ref, oacc_ref, q_s, v_s, k_s, hl_s, gc_s, bk_s, sc_s, aq_s, upd_ref, dec_ref,
                 *, cx, t):
    c = HG_CHUNK
    dk = HG_DK
    n_c, n_l = cx // c, t // c
    n_all = n_c + n_l

    p_ref[0:cx, :] = _dot(hc_ref[...], w_ref[...])
    rows = 512 if t % 512 == 0 else c

    def proj_body(i, carry):
        r = pl.multiple_of(i * rows, rows)
        p_ref[pl.ds(cx + r, rows), :] = _dot(hl_ref[pl.ds(r, rows), :], w_ref[...])
        return carry

    lax.fori_loop(0, t // rows, proj_body, 0)

    lb = lb_ref[...]
    one_m_lb = 1.0 - lb
    ri = lax.broadcasted_iota(I32, (c, c), 0)
    ci = lax.broadcasted_iota(I32, (c, c), 1)
    keep = (ri >= ci, ci >= ri)
    mid = (c // 2 - 1, c // 2)
    end = (c - 1, 0)

    def stage_gates(j, carry):
        r0 = pl.multiple_of(j * c, c)
        q_s[pl.ds(r0, c), :] = _silu(p_ref[pl.ds(r0, c), 0:dk])
        v_s[pl.ds(r0, c), :] = p_ref[pl.ds(r0, c), 3 * dk:4 * dk].astype(BF16)
        for d in range(2):
            f = lb + one_m_lb * _sigmoid(p_ref[pl.ds(r0, c), (1 + d) * dk:(2 + d) * dk])
            logf = jnp.log(f)
            hi = logf.astype(BF16)
            lo = (logf - hi.astype(F32)).astype(BF16)
            k_s[d, pl.ds(r0, c), :] = 1.0 - f
            hl_s[d, pl.ds(r0, c), :] = jnp.concatenate([hi, lo], axis=1)
        return carry

    lax.fori_loop(0, n_all, stage_gates, 0, unroll=2)

    def stage_cumsum(j, carry):
        r0 = pl.multiple_of(j * c, c)
        for d in range(2):
            s2 = _dot(keep[d].astype(BF16), hl_s[d, pl.ds(r0, c), :])
            gc_s[d, pl.ds(r0, c), :] = s2[:, :dk] + s2[:, dk:]
        return carry

    lax.fori_loop(0, n_all, stage_cumsum, 0, unroll=6)

    def stage_scores(j, carry):
        r0 = pl.multiple_of(j * c, c)
        q = q_s[pl.ds(r0, c), :]
        for d in range(2):
            gc = gc_s[d, pl.ds(r0, c), :]
            k = k_s[d, pl.ds(r0, c), :]
            g_mid = gc[mid[d]:mid[d] + 1, :]
            g_end = gc[end[d]:end[d] + 1, :]
            a = (q * jnp.exp(gc - g_mid)).astype(BF16)
            bm = (k * jnp.exp(g_mid - gc)).astype(BF16)
            sc_s[d, pl.ds(r0, c), :] = jnp.where(keep[d], _dot_nt(a, bm), 0.0).astype(BF16)
            bk_s[d, pl.ds(r0, c), :] = (k * jnp.exp(g_end - gc)).astype(BF16)
            aq_s[d, pl.ds(r0, c), :] = (q * jnp.exp(gc)).astype(BF16)
            dec_ref[d, j] = jnp.broadcast_to(jnp.exp(g_end), (SUBLANES, dk))
        return carry

    lax.fori_loop(0, n_all, stage_scores, 0, unroll=3)

    def stage_intra(j, carry):
        r0 = pl.multiple_of(j * c, c)
        v = v_s[pl.ds(r0, c), :]
        oacc_ref[pl.ds(r0, c), :] = _dot(sc_s[0, pl.ds(r0, c), :], v) + _dot(sc_s[1, pl.ds(r0, c), :], v)
        for d in range(2):
            upd_ref[d, j] = _dot_tn(v, bk_s[d, pl.ds(r0, c), :])
        return carry

    lax.fori_loop(0, n_all, stage_intra, 0, unroll=6)

    def inter(d, jj, st):
        r0 = pl.multiple_of(jj * c, c)
        oacc_ref[pl.ds(r0, c), :] += _dot_nt(aq_s[d, pl.ds(r0, c), :], st.astype(BF16))
        return st * dec_ref[d, jj][0:1, :] + upd_ref[d, jj]

    def ctx_body(j, sts):
        return inter(0, j, sts[0]), inter(1, n_c - 1 - j, sts[1])

    def lat_body(j, sts):
        return inter(0, n_c + j, sts[0]), inter(1, n_c + n_l - 1 - j, sts[1])

    zero = jnp.zeros((dk, dk), F32)
    sts = lax.fori_loop(0, n_c, ctx_body, (zero, zero), unroll=2)
    lax.fori_loop(0, n_l, lat_body, sts, unroll=4)

    ng = ng_ref[...]

    def readout(i, carry):
        r = pl.multiple_of(i * rows, rows)
        o = oacc_ref[pl.ds(cx + r, rows), :]
        g = p_ref[pl.ds(cx + r, rows), 4 * dk:5 * dk]
        o_ref[pl.ds(r, rows), :] = (_rms(o) * ng * _silu(g)).astype(o_ref.dtype)
        return carry

    lax.fori_loop(0, t // rows, readout, 0)


def _hgrn(hc, hl, w_hg, lb, norm_g):
    b, cx, d = hc.shape
    t = hl.shape[1]
    heads = w_hg.shape[0]
    dk = HG_DK
    n_all = (cx + t) // HG_CHUNK
    kern = functools.partial(_hgrn_kernel, cx=cx, t=t)
    return pl.pallas_call(
        kern,
        grid=(b, heads),
        in_specs=[
            pl.BlockSpec((None, cx, d), lambda bi, h: (bi, 0, 0)),
            pl.BlockSpec((None, t, d), lambda bi, h: (bi, 0, 0)),
            pl.BlockSpec((None, d, 5 * dk), lambda bi, h: (h, 0, 0)),
            pl.BlockSpec((None, 1, dk), lambda bi, h: (h, 0, 0)),
            pl.BlockSpec((1, dk), lambda bi, h: (0, 0)),
        ],
        out_specs=pl.BlockSpec((None, t, dk), lambda bi, h: (bi, 0, h)),
        out_shape=jax.ShapeDtypeStruct((b, t, heads * dk), BF16),
        scratch_shapes=[
            pltpu.VMEM((cx + t, 5 * dk), F32),
            pltpu.VMEM((cx + t, dk), F32),
            pltpu.VMEM((cx + t, dk), F32),
            pltpu.VMEM((cx + t, dk), BF16),
            pltpu.VMEM((2, cx + t, dk), F32),
            pltpu.VMEM((2, cx + t, 2 * dk), BF16),
            pltpu.VMEM((2, cx + t, dk), F32),
            pltpu.VMEM((2, cx + t, dk), BF16),
            pltpu.VMEM((2, cx + t, HG_CHUNK), BF16),
            pltpu.VMEM((2, cx + t, dk), BF16),
            pltpu.VMEM((2, n_all, dk, dk), F32),
            pltpu.VMEM((2, n_all, SUBLANES, dk), F32),
        ],
        compiler_params=_params(("arbitrary", "arbitrary"), 48),
        name="hgrn",
    )(hc, hl, w_hg, lb, norm_g.reshape(1, dk))


def _merge_kernel(x_ref, h_ref, ya_ref, yb_ref, wm_ref, wa_ref, wb_ref, wo_ref,
                  gate_ref, shift_ref, scale_ref, g2_ref, rwt_ref, rb_ref,
                  x1_ref, h2_ref, idx_ref, gates_ref, rank_ref, cnt_ref, carry_ref, *, tm, d, ne):
    first = (pl.program_id(0) == 0) & (pl.program_id(1) == 0)

    @pl.when(first)
    def _():
        carry_ref[...] = jnp.zeros_like(carry_ref)

    m = _dot(h_ref[...], wm_ref[...])
    za = _dot(ya_ref[...], wa_ref[...])
    zb = _dot(yb_ref[...], wb_ref[...])
    mix = _sigmoid(m[:, :d]) * za + _sigmoid(m[:, d:]) * zb
    out = _dot(mix.astype(BF16), wo_ref[...])
    x1 = x_ref[...] + gate_ref[...] * out
    x1_ref[...] = x1
    h2 = _rms(x1) * g2_ref[...] * (1.0 + scale_ref[...]) + shift_ref[...]
    for s in range(d // LANES):
        h2_ref[pl.ds(s, tm, stride=d // LANES), :] = h2[:, s * LANES:(s + 1) * LANES]

    logits = lax.dot_general(rwt_ref[...], h2, (((1,), (1,)), ((), ())),
                             preferred_element_type=F32, precision=HIGHEST) + rb_ref[...]
    eid = lax.broadcasted_iota(I32, (ne, tm), 0)
    neg = jnp.float32(-jnp.inf)
    work = logits
    hot = jnp.zeros((ne, tm), F32)
    vals, idxs, sels = [], [], []
    for _ in range(TOP_K):
        mx = jnp.max(work, axis=0, keepdims=True)
        ix = jnp.min(jnp.where(work == mx, eid, ne), axis=0, keepdims=True)
        sel = eid == ix
        work = jnp.where(sel, neg, work)
        hot = hot + sel.astype(F32)
        vals.append(mx)
        idxs.append(ix)
        sels.append(sel)
    ex = [jnp.exp(v - vals[0]) for v in vals]
    den = ex[0] + ex[1] + ex[2] + ex[3]
    gts = [e / den for e in ex]

    ti = lax.broadcasted_iota(I32, (tm, tm), 0)
    tj = lax.broadcasted_iota(I32, (tm, tm), 1)
    strict = (ti < tj).astype(BF16)
    before = _dot(hot.astype(BF16), strict) + carry_ref[...]
    rks = [jnp.sum(jnp.where(s, before, 0.0), axis=0, keepdims=True) for s in sels]
    carry_ref[...] = carry_ref[...] + jnp.sum(hot, axis=1, keepdims=True)

    row = lax.broadcasted_iota(I32, (SUBLANES, tm), 0)

    def pack(parts, zero):
        acc = jnp.full((SUBLANES, tm), zero, parts[0].dtype)
        for k, p in enumerate(parts):
            acc = jnp.where(row == k, jnp.broadcast_to(p, (SUBLANES, tm)), acc)
        return acc

    idx_ref[...] = pack(idxs, 0)
    gates_ref[...] = pack(gts, 0.0)
    rank_ref[...] = pack(rks, 0.0).astype(I32)
    cnt_ref[...] = jnp.broadcast_to(carry_ref[...], cnt_ref.shape)


def _merge(x, hl, ya, yb, w_m, w_ba, w_bb, w_out, gate2, shift3, scale4, g2, rwt, rb, tm):
    b, t, d = x.shape
    ne = rwt.shape[0]
    nt = t // tm
    tok = b * t
    kern = functools.partial(_merge_kernel, tm=tm, d=d, ne=ne)
    row_spec = pl.BlockSpec((None, tm, d), lambda bi, i: (bi, i, 0))
    mod_spec = pl.BlockSpec((None, 1, d), lambda bi, i: (bi, 0, 0))
    flat = lambda bi, i: (bi * nt + i, 0)
    lane = lambda bi, i: (0, bi * nt + i)

    def whole(shape):
        return pl.BlockSpec(shape, lambda bi, i: (0,) * len(shape))

    return pl.pallas_call(
        kern,
        grid=(b, nt),
        in_specs=[
            row_spec, row_spec, row_spec, row_spec,
            whole((d, 2 * d)), whole((d, d)), whole((d, d)), whole((d, d)),
            mod_spec, mod_spec, mod_spec,
            whole((1, d)), whole((ne, d)), whole((ne, 1)),
        ],
        out_specs=[
            row_spec,
            pl.BlockSpec((tm * (d // LANES), LANES), flat),
            pl.BlockSpec((SUBLANES, tm), lane),
            pl.BlockSpec((SUBLANES, tm), lane),
            pl.BlockSpec((SUBLANES, tm), lane),
            whole((ne, 128)),
        ],
        out_shape=[
            jax.ShapeDtypeStruct((b, t, d), F32),
            jax.ShapeDtypeStruct((tok * (d // LANES), LANES), F32),
            jax.ShapeDtypeStruct((SUBLANES, tok), I32),
            jax.ShapeDtypeStruct((SUBLANES, tok), F32),
            jax.ShapeDtypeStruct((SUBLANES, tok), I32),
            jax.ShapeDtypeStruct((ne, 128), F32),
        ],
        scratch_shapes=[pltpu.VMEM((ne, 1), F32)],
        compiler_params=_params(("arbitrary", "arbitrary"), 56),
        name="merge",
    )(x, hl, ya, yb, w_m, w_ba, w_bb, w_out, gate2, shift3, scale4, g2.reshape(1, d), rwt, rb)


def _slots_kernel(dest_ref, code_ref, *, chunk, nslots):
    i = pl.program_id(0)

    @pl.when(i == 0)
    def _():
        def fill(s, carry):
            code_ref[s] = -1
            return carry
        lax.fori_loop(0, nslots, fill, 0, unroll=8)

    base = i * chunk

    def body(a, carry):
        code_ref[dest_ref[0, 0, a]] = base + a
        return carry

    lax.fori_loop(0, chunk, body, 0, unroll=8)


def _slot_table(dest_flat, nslots):
    n = dest_flat.shape[0]
    chunk = _pick_tile(n, 8192)
    kern = functools.partial(_slots_kernel, chunk=chunk, nslots=nslots)
    return pl.pallas_call(
        kern,
        grid=(n // chunk,),
        in_specs=[pl.BlockSpec((1, 1, chunk), lambda i: (i, 0, 0), memory_space=pltpu.SMEM)],
        out_specs=pl.BlockSpec(memory_space=pltpu.SMEM),
        out_shape=jax.ShapeDtypeStruct((nslots,), I32),
        compiler_params=_params(("arbitrary",), 32),
        name="slots",
    )(dest_flat.reshape(n // chunk, 1, chunk))


def _moe_kernel(bexp_ref, bact_ref, tok_ref, tok1_ref, tok2_ref, dstp_ref, h2_hbm, w1_ref, b1_ref, w2_ref, b2_ref,
                yo_hbm, xbuf, ybuf, xb_s, act_s, w1p_s, w2_s, gsem, ssem, *, blk, nblocks):
    j = pl.program_id(0)
    slot = lax.rem(j, MOE_RING)
    far = lax.rem(j + 2, MOE_RING)
    d = xb_s.shape[1]
    f = w2_s.shape[1]
    ts = d // LANES
    pw = 2 * LANES
    active = bact_ref[j] == 1

    def gather_copy(tok, buf_slot, r):
        src = pl.multiple_of(tok * ts, ts)
        dst = pl.multiple_of((buf_slot * blk + r) * ts, ts)
        return pltpu.make_async_copy(h2_hbm.at[pl.ds(src, ts), :], xbuf.at[pl.ds(dst, ts), :], gsem.at[buf_slot])

    def scatter_copy(row, buf_slot, r):
        src = pl.multiple_of((buf_slot * blk + r) * ts, ts)
        dst = pl.multiple_of(row * ts, ts)
        return pltpu.make_async_copy(ybuf.at[pl.ds(src, ts), :], yo_hbm.at[pl.ds(dst, ts), :], ssem.at[buf_slot])

    def gather_loop(idx_ref, buf_slot):
        def body(i, carry):
            gather_copy(idx_ref[0, 0, 2 * i], buf_slot, 2 * i).start(priority=0)
            gather_copy(idx_ref[0, 0, 2 * i + 1], buf_slot, 2 * i + 1).start(priority=1)
            return carry
        lax.fori_loop(0, blk // 2, body, 0, unroll=4)

    def scatter_loop(buf_slot):
        def body(i, carry):
            scatter_copy(dstp_ref[0, 0, 2 * i], buf_slot, 2 * i).start(priority=0)
            scatter_copy(dstp_ref[0, 0, 2 * i + 1], buf_slot, 2 * i + 1).start(priority=1)
            return carry
        lax.fori_loop(0, blk // 2, body, 0, unroll=4)

    def scatter_wait_loop(buf_slot):
        def body(r, carry):
            scatter_copy(0, buf_slot, r).wait()
            return carry
        lax.fori_loop(0, blk, body, 0, unroll=8)

    @pl.when(j == 0)
    def _():
        gather_loop(tok_ref, 0)
        gather_loop(tok1_ref, 1)
        ybuf[...] = jnp.zeros_like(ybuf)
        n_scratch = 2 * blk * ts
        fill = pltpu.make_async_copy(ybuf.at[pl.ds(0, n_scratch), :],
                                     yo_hbm.at[pl.ds(yo_hbm.shape[0] - n_scratch, n_scratch), :], ssem.at[0])
        fill.start()
        fill.wait()

    def gather_wait(r, carry):
        gather_copy(0, slot, r).wait()
        return carry

    lax.fori_loop(0, blk, gather_wait, 0, unroll=8)

    prev_active = bact_ref[jnp.maximum(j - 1, 0)] == 1
    @pl.when((j == 2) | ((j >= 3) & (bact_ref[jnp.maximum(j - 3, 0)] == 1)))
    def _():
        scatter_wait_loop(slot)

    new_expert = (j == 0) | (bexp_ref[j] != bexp_ref[jnp.maximum(j - 1, 0)])

    @pl.when(new_expert & active)
    def _():
        src = lax.broadcasted_iota(I32, (pw, pw), 0)
        dst = lax.broadcasted_iota(I32, (pw, pw), 1)
        want = jnp.where(dst < pw // 2, 2 * dst, 2 * (dst - pw // 2) + 1)
        sel = (src == want).astype(BF16)
        for cb in range(2 * f // pw):
            w1p_s[cb] = _dot(w1_ref[:, cb * pw:(cb + 1) * pw].astype(BF16), sel).astype(BF16)
        for cb in range(d // pw):
            w2_s[cb] = w2_ref[:, cb * pw:(cb + 1) * pw].astype(BF16)

    n1 = 2 * f // pw
    n2 = d // pw
    per1 = (blk * 5 // 8) // n1
    per2 = (blk - per1 * n1) // n2
    assert per1 * n1 + per2 * n2 == blk

    xbase = pl.multiple_of(slot * blk * ts, ts)

    @pl.when(active)
    def _():
        for s in range(ts):
            xb_s[:, s * LANES:(s + 1) * LANES] = xbuf[pl.ds(xbase + s, blk, stride=ts), :].astype(BF16)

    def transfers(r0, count):
        for r in range(r0, r0 + count):
            gather_copy(tok2_ref[0, 0, r], far, r).start(priority=r % 2)
        for r in range(r0, r0 + count):
            scatter_copy(dstp_ref[0, 0, r], far, r).start(priority=r % 2)

    @pl.when(active)
    def _():
        for n in range(n1):
            h = _dot(xb_s[...], w1p_s[n]) + b1_ref[n]
            hg = jnp.minimum(h[:, :LANES], SWIGLU_LIMIT)
            hu = jnp.clip(h[:, LANES:], -SWIGLU_LIMIT, SWIGLU_LIMIT)
            act_s[n] = (hg * _sigmoid(SWIGLU_ALPHA * hg) * (hu + 1.0)).astype(BF16)
            transfers(n * per1, per1)
        act = jnp.concatenate([act_s[i] for i in range(n1)], axis=1)
        for n in range(n2):
            y = _dot(act, w2_s[n]) + b2_ref[n]
            ybuf[pl.ds(xbase + 2 * n, blk, stride=ts), :] = y[:, :LANES]
            ybuf[pl.ds(xbase + 2 * n + 1, blk, stride=ts), :] = y[:, LANES:]
            transfers(n1 * per1 + n * per2, per2)

    @pl.when(jnp.logical_not(active) & (j + 2 < nblocks))
    def _():
        gather_loop(tok2_ref, far)

    @pl.when(jnp.logical_not(active) & prev_active)
    def _():
        scatter_loop(far)


def _moe(block_expert, block_active, slot_tok3, slot_dst3, h2t, w1, b1p, w2, b2, n_out_rows, blk):
    nblocks = block_expert.shape[0]
    f, d = w2.shape[1], w2.shape[2]
    ts = d // LANES
    pw = 2 * LANES
    n1, n2 = 2 * f // pw, d // pw
    b1p = b1p.reshape(-1, n1, 1, pw)
    b2 = b2.reshape(-1, n2, 1, pw)
    kern = functools.partial(_moe_kernel, blk=blk, nblocks=nblocks)
    wmap = lambda j, be, ba: (be[j], 0, 0)
    grid_spec = pltpu.PrefetchScalarGridSpec(
        num_scalar_prefetch=2,
        grid=(nblocks,),
        in_specs=[
            pl.BlockSpec((1, 1, blk), lambda j, be, ba: (j, 0, 0), memory_space=pltpu.SMEM),
            pl.BlockSpec((1, 1, blk), lambda j, be, ba: (jnp.minimum(j + 1, nblocks - 1), 0, 0),
                         memory_space=pltpu.SMEM),
            pl.BlockSpec((1, 1, blk), lambda j, be, ba: (jnp.minimum(j + 2, nblocks - 1), 0, 0),
                         memory_space=pltpu.SMEM),
            pl.BlockSpec((1, 1, blk), lambda j, be, ba: (j, 0, 0), memory_space=pltpu.SMEM),
            pl.BlockSpec(memory_space=pl.ANY),
            pl.BlockSpec((None, d, 2 * f), wmap),
            pl.BlockSpec((None, n1, 1, pw), lambda j, be, ba: (be[j], 0, 0, 0)),
            pl.BlockSpec((None, f, d), wmap),
            pl.BlockSpec((None, n2, 1, pw), lambda j, be, ba: (be[j], 0, 0, 0)),
        ],
        out_specs=pl.BlockSpec(memory_space=pl.ANY),
        scratch_shapes=[
            pltpu.VMEM((MOE_RING * blk * ts, LANES), F32),
            pltpu.VMEM((MOE_RING * blk * ts, LANES), F32),
            pltpu.VMEM((blk, d), BF16),
            pltpu.VMEM((n1, blk, LANES), BF16),
            pltpu.VMEM((n1, d, pw), BF16),
            pltpu.VMEM((n2, f, pw), BF16),
            pltpu.SemaphoreType.DMA((MOE_RING,)),
            pltpu.SemaphoreType.DMA((MOE_RING,)),
        ],
    )
    return pl.pallas_call(
        kern,
        grid_spec=grid_spec,
        out_shape=jax.ShapeDtypeStruct((n_out_rows * ts, LANES), F32),
        compiler_params=_params(("arbitrary",), 56),
        name="moe",
    )(block_expert, block_active, slot_tok3, slot_tok3, slot_tok3, slot_dst3, h2t, w1, b1p, w2, b2)


def _final_kernel(*refs, tm, d):
    y_refs = refs[:TOP_K]
    x1_ref, gt_ref, gate_ref, fg_ref, o_ref = refs[TOP_K:]
    ts = d // LANES
    gt = gt_ref[...]
    acc = None
    for k in range(TOP_K):
        yk = jnp.concatenate([y_refs[k][pl.ds(s, tm, stride=ts), :] for s in range(ts)], axis=1)
        term = gt[:, k:k + 1] * yk
        acc = term if acc is None else acc + term
    x2 = x1_ref[...] + gate_ref[...] * acc
    o_ref[...] = _rms(x2) * fg_ref[...]


def _final(yo, x1, gates_t, gate5, final_g, tm):
    b, t, d = x1.shape
    ts = d // LANES
    nt = t // tm
    ntiles = b * nt
    kern = functools.partial(_final_kernel, tm=tm, d=d)
    row_map = lambda i: (i // nt, i % nt, 0)
    y_specs = [pl.BlockSpec((tm * ts, LANES), functools.partial(lambda i, k: (k * ntiles + i, 0), k=k))
               for k in range(TOP_K)]
    return pl.pallas_call(
        kern,
        grid=(ntiles,),
        in_specs=y_specs + [
            pl.BlockSpec((None, tm, d), row_map),
            pl.BlockSpec((tm, TOP_K), lambda i: (i, 0)),
            pl.BlockSpec((None, 1, d), lambda i: (i // nt, 0, 0)),
            pl.BlockSpec((1, d), lambda i: (0, 0)),
        ],
        out_specs=pl.BlockSpec((None, tm, d), row_map),
        out_shape=jax.ShapeDtypeStruct((b, t, d), F32),
        compiler_params=_params(("arbitrary",), 48),
        name="final",
    )(*([yo] * TOP_K), x1, gates_t, gate5, final_g.reshape(1, d))


def _pick_tile(n, pref):
    tm = pref
    while n % tm:
        tm //= 2
    return tm


def kernel(x, c, ctx, c_ctx, ada_w, ada_b, norm1_g, norm2_g, w_in, lru_conv_w, lru_conv_b, lru_wa, lru_ba, lru_wx, lru_bx, lru_lam, hg_lb_logits, hg_norm_g, w_branch_a, w_branch_b, w_out, router_w, router_b, moe_w1, moe_b1, moe_w2, moe_b2, final_g):
    b, t, d = x.shape
    cx = ctx.shape[1]
    layer = 0
    w_lru = lru_conv_w.shape[2]
    qk = (w_in.shape[2] - 2 * w_lru - 2 * d) // 5
    heads = qk // HG_DK
    ne = router_w.shape[2]

    pad = (-(b + 1)) % SUBLANES
    c_all = jnp.concatenate([c, c_ctx[None, :], jnp.zeros((pad, d), F32)], axis=0)
    mod = _modulation(c_all, ada_w[layer], ada_b[layer])
    mod_l = mod[:b].reshape(b, N_MOD, 1, d)
    mod_c = mod[b].reshape(N_MOD, 1, 1, d)

    w_in_b = w_in[layer].astype(BF16)
    w_lru_in = w_in_b[:, :2 * w_lru]
    hg0 = 2 * w_lru
    w_hg = w_in_b[:, hg0:hg0 + 5 * qk].reshape(d, 5, heads, HG_DK).transpose(2, 0, 1, 3).reshape(heads, d, 5 * HG_DK)
    w_m = w_in_b[:, hg0 + 5 * qk:]
    g_blocks = w_lru // LRU_BLOCK
    wg = jnp.concatenate([lru_wa[layer, 0], lru_wx[layer, 0], lru_wa[layer, 1], lru_wx[layer, 1]], axis=-1).astype(BF16)
    bg = jnp.concatenate([lru_ba[layer, 0], lru_bx[layer, 0], lru_ba[layer, 1], lru_bx[layer, 1]], axis=-1)
    bg = bg.reshape(g_blocks, 1, 4 * LRU_BLOCK)
    lb_all = jnp.cumsum(jax.nn.softmax(hg_lb_logits.astype(F32), axis=0), axis=0)
    lb = lb_all[layer].reshape(heads, 1, HG_DK)

    tm_n = _pick_tile(t, 512)
    hl = _prenorm(x, mod_l[:, 0], mod_l[:, 1], norm1_g[layer], tm_n)
    hc = _prenorm(ctx, mod_c[0], mod_c[1], norm1_g[layer], _pick_tile(cx, 256))
    axl = _matmul(hl.reshape(b * t, d), w_lru_in, _pick_tile(b * t, 512), BF16).reshape(b, t, 2 * w_lru)
    axc = _matmul(hc.reshape(b * cx, d), w_lru_in, _pick_tile(b * cx, 512), BF16).reshape(b, cx, 2 * w_lru)
    ya = _lru(axl, axc, lru_conv_w[layer], lru_conv_b[layer], wg, bg, lru_lam[layer], b, cx, t)
    yb = _hgrn(hc, hl, w_hg, lb, hg_norm_g[layer])

    rwt = router_w[layer].T
    rb = router_b[layer].reshape(ne, 1)
    x1, h2, idx8, gates8, rank8, cnt = _merge(
        x, hl, ya, yb, w_m, w_branch_a[layer].astype(BF16), w_branch_b[layer].astype(BF16),
        w_out[layer].astype(BF16), mod_l[:, 2], mod_l[:, 3], mod_l[:, 4], norm2_g[layer], rwt, rb,
        _pick_tile(t, 512))

    blk = MOE_BLOCK
    tok = b * t
    n_assign = tok * TOP_K
    nblocks = -(-n_assign // blk) + ne + MOE_RING - 1
    counts = cnt[:, 0].astype(I32)
    padded = (counts + blk - 1) // blk * blk
    pend = jnp.cumsum(padded)
    pstart = pend - padded
    idx = idx8[:TOP_K]
    onehot = idx[:, :, None] == jnp.arange(ne, dtype=I32)[None, None, :]
    dest = rank8[:TOP_K] + jnp.sum(jnp.where(onehot, pstart[None, None, :], 0), axis=-1)
    block_start = jnp.arange(nblocks, dtype=I32) * blk
    block_active = block_start < pend[-1]
    block_expert = jnp.minimum(jnp.sum(block_start[:, None] >= pend[None, :], axis=1), ne - 1).astype(I32)
    last_expert = jnp.max(jnp.where(block_active, block_expert, 0))
    block_expert = jnp.where(block_active, block_expert, last_expert).astype(I32)
    block_active = block_active.astype(I32)
    slot_code = _slot_table(dest.reshape(-1), nblocks * blk)
    slot_tok = jnp.where(slot_code >= 0, slot_code % tok, 0)
    slot_ids = jnp.arange(nblocks * blk, dtype=I32)
    scratch_row = n_assign + (slot_ids // blk % 2) * blk + slot_ids % blk
    slot_dst = jnp.where(slot_code >= 0, slot_code, scratch_row)
    first_dst = n_assign + blk + jnp.arange(blk, dtype=I32)
    slot_dst = jnp.concatenate([first_dst, slot_dst[:-blk]])

    b1 = moe_b1[layer]
    f = b1.shape[1] // 2
    b1p = b1.reshape(ne, f // LANES, LANES, 2).transpose(0, 1, 3, 2).reshape(ne, 1, 2 * f)
    yo = _moe(block_expert, block_active, slot_tok.reshape(nblocks, 1, blk), slot_dst.reshape(nblocks, 1, blk),
              h2, moe_w1[layer], b1p, moe_w2[layer], moe_b2[layer].reshape(ne, 1, d), n_assign + 2 * blk, blk)

    gates_t = gates8[:TOP_K].T
    return _final(yo, x1, gates_t, mod_l[:, 5], final_g, _pick_tile(t, 256))
```

```python
import functools

import jax
import jax.numpy as jnp
from jax import lax
from jax.experimental import pallas as pl
from jax.experimental.pallas import tpu as pltpu

F32 = jnp.float32
BF16 = jnp.bfloat16
I32 = jnp.int32
HIGHEST = lax.Precision.HIGHEST

EPS = 1e-6
N_MOD = 6
GRID_W = 64
CONV_LEFT = 2
CONV_WIDTH = 4
LRU_C = 8.0
LRU_BLOCK = 128
HG_DK = 128
HG_CHUNK = 64
TOP_K = 4
SWIGLU_LIMIT = 7.0
SWIGLU_ALPHA = 1.702
MOE_BLOCK = 256
MOE_RING = 3
SUBLANES = 8
LANES = 128
HALO = 8


def _params(sem, vmem_mb):
    return pltpu.CompilerParams(dimension_semantics=sem, vmem_limit_bytes=vmem_mb * 1024 * 1024)


def _dot(a, b):
    return jnp.dot(a, b, preferred_element_type=F32)


def _dot_nt(a, b):
    return lax.dot_general(a, b, (((1,), (1,)), ((), ())), preferred_element_type=F32)


def _dot_tn(a, b):
    return lax.dot_general(a, b, (((0,), (0,)), ((), ())), preferred_element_type=F32)


def _sigmoid(x):
    return jax.nn.sigmoid(x)


def _silu(x):
    return x * jax.nn.sigmoid(x)


def _rms(x):
    return x * lax.rsqrt(jnp.mean(x * x, axis=-1, keepdims=True) + EPS)


def _mod_kernel(c_ref, w_ref, b_ref, o_ref):
    s = _silu(c_ref[...])
    o_ref[...] = jnp.dot(s, w_ref[...], preferred_element_type=F32, precision=HIGHEST) + b_ref[...]


def _modulation(c_all, ada_w, ada_b):
    m, d = c_all.shape
    n = ada_w.shape[1]
    tn = 1024
    return pl.pallas_call(
        _mod_kernel,
        grid=(n // tn,),
        in_specs=[
            pl.BlockSpec((m, d), lambda j: (0, 0)),
            pl.BlockSpec((d, tn), lambda j: (0, j)),
            pl.BlockSpec((1, tn), lambda j: (0, j)),
        ],
        out_specs=pl.BlockSpec((m, tn), lambda j: (0, j)),
        out_shape=jax.ShapeDtypeStruct((m, n), F32),
        compiler_params=_params(("arbitrary",), 32),
        name="mod",
    )(c_all, ada_w, ada_b.reshape(1, n))


def _prenorm_kernel(x_ref, shift_ref, scale_ref, g_ref, o_ref):
    y = _rms(x_ref[...]) * g_ref[...]
    o_ref[...] = (y * (1.0 + scale_ref[...]) + shift_ref[...]).astype(o_ref.dtype)


def _prenorm(x, shift, scale, g, tm):
    b, l, d = x.shape
    per_batch = shift.shape[0] == b and b > 1
    mod_map = (lambda bi, i: (bi, 0, 0)) if per_batch else (lambda bi, i: (0, 0, 0))
    return pl.pallas_call(
        _prenorm_kernel,
        grid=(b, l // tm),
        in_specs=[
            pl.BlockSpec((None, tm, d), lambda bi, i: (bi, i, 0)),
            pl.BlockSpec((None, 1, d), mod_map),
            pl.BlockSpec((None, 1, d), mod_map),
            pl.BlockSpec((1, d), lambda bi, i: (0, 0)),
        ],
        out_specs=pl.BlockSpec((None, tm, d), lambda bi, i: (bi, i, 0)),
        out_shape=jax.ShapeDtypeStruct((b, l, d), BF16),
        compiler_params=_params(("arbitrary", "arbitrary"), 32),
        name="prenorm",
    )(x, shift, scale, g.reshape(1, d))


def _mm_kernel(x_ref, w_ref, o_ref):
    o_ref[...] = _dot(x_ref[...], w_ref[...]).astype(o_ref.dtype)


def _matmul(x2d, w, tm, out_dtype):
    m, k = x2d.shape
    n = w.shape[1]
    return pl.pallas_call(
        _mm_kernel,
        grid=(m // tm,),
        in_specs=[
            pl.BlockSpec((tm, k), lambda i: (i, 0)),
            pl.BlockSpec((k, n), lambda i: (0, 0)),
        ],
        out_specs=pl.BlockSpec((tm, n), lambda i: (i, 0)),
        out_shape=jax.ShapeDtypeStruct((m, n), out_dtype),
        compiler_params=_params(("arbitrary",), 40),
        name="inproj_lru",
    )(x2d, w)


def _lru_kernel(axl_ref, agl_ref, axc_ref, cw_ref, cb_ref, wg_ref, bg_ref, lam_ref, o_ref,
                xc_ref, yacc_ref, sx_ref, sa_ref, sb_ref, sh_ref, *, nb, cx, t):
    tb = GRID_W
    ctx0 = HALO
    gw = 2 * LRU_BLOCK
    pre = CONV_LEFT
    steps = pre + tb + (HALO - pre)

    zeros_h = jnp.zeros((HALO, LRU_BLOCK), F32)
    for b in range(nb):
        xc_ref[b, 0:HALO, :] = zeros_h
        xc_ref[b, ctx0:ctx0 + cx, :] = axc_ref[b].astype(F32)
        xc_ref[b, ctx0 + cx:ctx0 + cx + HALO, :] = zeros_h

    cw = cw_ref[...]
    cb = cb_ref[...]
    lam = lam_ref[...]
    sp = jnp.maximum(-lam, 0.0) + jnp.log1p(jnp.exp(-jnp.abs(lam)))

    def zero_history():
        sx_ref[0:pre * nb, :] = jnp.zeros((pre * nb, LRU_BLOCK), F32)
        sx_ref[(pre + tb) * nb:steps * nb, :] = jnp.zeros(((steps - pre - tb) * nb, LRU_BLOCK), F32)

    def load_ctx_block(base):
        for b in range(nb):
            sx_ref[pl.ds(b, steps, stride=nb), :] = xc_ref[b, pl.ds(base - pre, steps), :]

    def load_latent_block(r0):
        for b in range(nb):
            sx_ref[pl.ds(pre * nb + b, tb, stride=nb), :] = axl_ref[b, pl.ds(r0, tb), :].astype(F32)

    def fill_gates(d):
        u = jnp.broadcast_to(cb, (tb * nb, LRU_BLOCK))
        for k in range(CONV_WIDTH):
            u = u + sx_ref[k * nb:(k + tb) * nb, :] * cw[k:k + 1, :]
        z = _dot(u.astype(BF16), wg_ref[:, d * gw:(d + 1) * gw]) + bg_ref[:, d * gw:(d + 1) * gw]
        r = _sigmoid(z[:, :LRU_BLOCK])
        i = _sigmoid(z[:, LRU_BLOCK:])
        log_a = (-LRU_C) * r * sp[d:d + 1, :]
        a = jnp.exp(log_a)
        mult = jnp.sqrt(-jnp.tanh(log_a) * (a * a + 1.0))
        sa_ref[...] = a
        sb_ref[...] = mult * (i * u)

    def scan_block(h, reverse):
        order = range(tb - 1, -1, -1) if reverse else range(tb)
        for s in order:
            h = sa_ref[s * nb:(s + 1) * nb, :] * h + sb_ref[s * nb:(s + 1) * nb, :]
            sh_ref[s * nb:(s + 1) * nb, :] = h
        return h

    def sweep(d):
        reverse = d == 1
        n_c, n_l = cx // tb, t // tb

        def ctx_body(j, h):
            jj = (n_c - 1 - j) if reverse else j
            load_ctx_block(pl.multiple_of(ctx0 + jj * tb, SUBLANES))
            fill_gates(d)
            return scan_block(h, reverse)

        def lat_body(j, h):
            jj = (n_l - 1 - j) if reverse else j
            r0 = pl.multiple_of(jj * tb, tb)
            load_latent_block(r0)
            fill_gates(d)
            h = scan_block(h, reverse)
            for b in range(nb):
                yb = sh_ref[pl.ds(b, tb, stride=nb), :]
                if not reverse:
                    yacc_ref[b, pl.ds(r0, tb), :] = yb
                else:
                    gate = jax.nn.gelu(agl_ref[b, pl.ds(r0, tb), :].astype(F32))
                    o_ref[b, pl.ds(r0, tb), :] = ((yacc_ref[b, pl.ds(r0, tb), :] + yb) * gate).astype(o_ref.dtype)
            return h

        h0 = jnp.zeros((nb, LRU_BLOCK), F32)
        h1 = lax.fori_loop(0, n_c, ctx_body, h0)
        zero_history()
        lax.fori_loop(0, n_l, lat_body, h1)

    sweep(0)
    sweep(1)


def _lru(axl, axc, conv_w, conv_b, wg, bg, lam, nb, cx, t):
    w = conv_w.shape[1]
    g = w // LRU_BLOCK
    kern = functools.partial(_lru_kernel, nb=nb, cx=cx, t=t)
    return pl.pallas_call(
        kern,
        grid=(g,),
        in_specs=[
            pl.BlockSpec((nb, t, LRU_BLOCK), lambda j: (0, 0, j)),
            pl.BlockSpec((nb, t, LRU_BLOCK), lambda j: (0, 0, g + j)),
            pl.BlockSpec((nb, cx, LRU_BLOCK), lambda j: (0, 0, j)),
            pl.BlockSpec((CONV_WIDTH, LRU_BLOCK), lambda j: (0, j)),
            pl.BlockSpec((1, LRU_BLOCK), lambda j: (0, j)),
            pl.BlockSpec((None, LRU_BLOCK, 4 * LRU_BLOCK), lambda j: (j, 0, 0)),
            pl.BlockSpec((None, 1, 4 * LRU_BLOCK), lambda j: (j, 0, 0)),
            pl.BlockSpec((2, LRU_BLOCK), lambda j: (0, j)),
        ],
        out_specs=pl.BlockSpec((nb, t, LRU_BLOCK), lambda j: (0, 0, j)),
        out_shape=jax.ShapeDtypeStruct((nb, t, w), BF16),
        scratch_shapes=[
            pltpu.VMEM((nb, cx + 2 * HALO, LRU_BLOCK), F32),
            pltpu.VMEM((nb, t, LRU_BLOCK), F32),
            pltpu.VMEM(((GRID_W + HALO) * nb, LRU_BLOCK), F32),
            pltpu.VMEM((GRID_W * nb, LRU_BLOCK), F32),
            pltpu.VMEM((GRID_W * nb, LRU_BLOCK), F32),
            pltpu.VMEM((GRID_W * nb, LRU_BLOCK), F32),
        ],
        compiler_params=_params(("arbitrary",), 56),
        name="lru",
    )(axl, axl, axc, conv_w, conv_b.reshape(1, w), wg, bg, lam)


def _hgrn_kernel(hc_ref, hl_ref, w_ref, lb_ref, ng_ref, o_ref,
                 p_ref, oacc_ref, q_s, v_s, k_s, hl_s, gc_s, bk_s, sc_s, aq_s, upd_ref, dec_ref,
                 *, cx, t):
    c = HG_CHUNK
    dk = HG_DK
    n_c, n_l = cx // c, t // c
    n_all = n_c + n_l

    p_ref[0:cx, :] = _dot(hc_ref[...], w_ref[...])
    rows = 512 if t % 512 == 0 else c

    def proj_body(i, carry):
        r = pl.multiple_of(i * rows, rows)
        p_ref[pl.ds(cx + r, rows), :] = _dot(hl_ref[pl.ds(r, rows), :], w_ref[...])
        return carry

    lax.fori_loop(0, t // rows, proj_body, 0)

    lb = lb_ref[...]
    one_m_lb = 1.0 - lb
    ri = lax.broadcasted_iota(I32, (c, c), 0)
    ci = lax.broadcasted_iota(I32, (c, c), 1)
    keep = (ri >= ci, ci >= ri)
    mid = (c // 2 - 1, c // 2)
    end = (c - 1, 0)

    def stage_gates(j, carry):
        r0 = pl.multiple_of(j * c, c)
        q_s[pl.ds(r0, c), :] = _silu(p_ref[pl.ds(r0, c), 0:dk])
        v_s[pl.ds(r0, c), :] = p_ref[pl.ds(r0, c), 3 * dk:4 * dk].astype(BF16)
        for d in range(2):
            f = lb + one_m_lb * _sigmoid(p_ref[pl.ds(r0, c), (1 + d) * dk:(2 + d) * dk])
            logf = jnp.log(f)
            hi = logf.astype(BF16)
            lo = (logf - hi.astype(F32)).astype(BF16)
            k_s[d, pl.ds(r0, c), :] = 1.0 - f
            hl_s[d, pl.ds(r0, c), :] = jnp.concatenate([hi, lo], axis=1)
        return carry

    lax.fori_loop(0, n_all, stage_gates, 0, unroll=2)

    def stage_cumsum(j, carry):
        r0 = pl.multiple_of(j * c, c)
        for d in range(2):
            s2 = _dot(keep[d].astype(BF16), hl_s[d, pl.ds(r0, c), :])
            gc_s[d, pl.ds(r0, c), :] = s2[:, :dk] + s2[:, dk:]
        return carry

    lax.fori_loop(0, n_all, stage_cumsum, 0, unroll=6)

    def stage_scores(j, carry):
        r0 = pl.multiple_of(j * c, c)
        q = q_s[pl.ds(r0, c), :]
        for d in range(2):
            gc = gc_s[d, pl.ds(r0, c), :]
            k = k_s[d, pl.ds(r0, c), :]
            g_mid = gc[mid[d]:mid[d] + 1, :]
            g_end = gc[end[d]:end[d] + 1, :]
            a = (q * jnp.exp(gc - g_mid)).astype(BF16)
            bm = (k * jnp.exp(g_mid - gc)).astype(BF16)
            sc_s[d, pl.ds(r0, c), :] = jnp.where(keep[d], _dot_nt(a, bm), 0.0).astype(BF16)
            bk_s[d, pl.ds(r0, c), :] = (k * jnp.exp(g_end - gc)).astype(BF16)
            aq_s[d, pl.ds(r0, c), :] = (q * jnp.exp(gc)).astype(BF16)
            dec_ref[d, j] = jnp.broadcast_to(jnp.exp(g_end), (SUBLANES, dk))
        return carry

    lax.fori_loop(0, n_all, stage_scores, 0, unroll=3)

    def stage_intra(j, carry):
        r0 = pl.multiple_of(j * c, c)
        v = v_s[pl.ds(r0, c), :]
        oacc_ref[pl.ds(r0, c), :] = _dot(sc_s[0, pl.ds(r0, c), :], v) + _dot(sc_s[1, pl.ds(r0, c), :], v)
        for d in range(2):
            upd_ref[d, j] = _dot_tn(v, bk_s[d, pl.ds(r0, c), :])
        return carry

    lax.fori_loop(0, n_all, stage_intra, 0, unroll=6)

    def inter(d, jj, st):
        r0 = pl.multiple_of(jj * c, c)
        oacc_ref[pl.ds(r0, c), :] += _dot_nt(aq_s[d, pl.ds(r0, c), :], st.astype(BF16))
        return st * dec_ref[d, jj][0:1, :] + upd_ref[d, jj]

    def ctx_body(j, sts):
        return inter(0, j, sts[0]), inter(1, n_c - 1 - j, sts[1])

    def lat_body(j, sts):
        return inter(0, n_c + j, sts[0]), inter(1, n_c + n_l - 1 - j, sts[1])

    zero = jnp.zeros((dk, dk), F32)
    sts = lax.fori_loop(0, n_c, ctx_body, (zero, zero), unroll=2)
    lax.fori_loop(0, n_l, lat_body, sts, unroll=4)

    ng = ng_ref[...]

    def readout(i, carry):
        r = pl.multiple_of(i * rows, rows)
        o = oacc_ref[pl.ds(cx + r, rows), :]
        g = p_ref[pl.ds(cx + r, rows), 4 * dk:5 * dk]
        o_ref[pl.ds(r, rows), :] = (_rms(o) * ng * _silu(g)).astype(o_ref.dtype)
        return carry

    lax.fori_loop(0, t // rows, readout, 0)


def _hgrn(hc, hl, w_hg, lb, norm_g):
    b, cx, d = hc.shape
    t = hl.shape[1]
    heads = w_hg.shape[0]
    dk = HG_DK
    n_all = (cx + t) // HG_CHUNK
    kern = functools.partial(_hgrn_kernel, cx=cx, t=t)
    return pl.pallas_call(
        kern,
        grid=(b, heads),
        in_specs=[
            pl.BlockSpec((None, cx, d), lambda bi, h: (bi, 0, 0)),
            pl.BlockSpec((None, t, d), lambda bi, h: (bi, 0, 0)),
            pl.BlockSpec((None, d, 5 * dk), lambda bi, h: (h, 0, 0)),
            pl.BlockSpec((None, 1, dk), lambda bi, h: (h, 0, 0)),
            pl.BlockSpec((1, dk), lambda bi, h: (0, 0)),
        ],
        out_specs=pl.BlockSpec((None, t, dk), lambda bi, h: (bi, 0, h)),
        out_shape=jax.ShapeDtypeStruct((b, t, heads * dk), BF16),
        scratch_shapes=[
            pltpu.VMEM((cx + t, 5 * dk), F32),
            pltpu.VMEM((cx + t, dk), F32),
            pltpu.VMEM((cx + t, dk), F32),
            pltpu.VMEM((cx + t, dk), BF16),
            pltpu.VMEM((2, cx + t, dk), F32),
            pltpu.VMEM((2, cx + t, 2 * dk), BF16),
            pltpu.VMEM((2, cx + t, dk), F32),
            pltpu.VMEM((2, cx + t, dk), BF16),
            pltpu.VMEM((2, cx + t, HG_CHUNK), BF16),
            pltpu.VMEM((2, cx + t, dk), BF16),
            pltpu.VMEM((2, n_all, dk, dk), F32),
            pltpu.VMEM((2, n_all, SUBLANES, dk), F32),
        ],
        compiler_params=_params(("arbitrary", "arbitrary"), 48),
        name="hgrn",
    )(hc, hl, w_hg, lb, norm_g.reshape(1, dk))


def _merge_kernel(x_ref, h_ref, ya_ref, yb_ref, wm_ref, wa_ref, wb_ref, wo_ref,
                  gate_ref, shift_ref, scale_ref, g2_ref, rwt_ref, rb_ref,
                  x1_ref, h2_ref, idx_ref, gates_ref, rank_ref, cnt_ref, carry_ref, *, tm, d, ne):
    first = (pl.program_id(0) == 0) & (pl.program_id(1) == 0)

    @pl.when(first)
    def _():
        carry_ref[...] = jnp.zeros_like(carry_ref)

    m = _dot(h_ref[...], wm_ref[...])
    za = _dot(ya_ref[...], wa_ref[...])
    zb = _dot(yb_ref[...], wb_ref[...])
    mix = _sigmoid(m[:, :d]) * za + _sigmoid(m[:, d:]) * zb
    out = _dot(mix.astype(BF16), wo_ref[...])
    x1 = x_ref[...] + gate_ref[...] * out
    x1_ref[...] = x1
    h2 = _rms(x1) * g2_ref[...] * (1.0 + scale_ref[...]) + shift_ref[...]
    for s in range(d // LANES):
        h2_ref[pl.ds(s, tm, stride=d // LANES), :] = h2[:, s * LANES:(s + 1) * LANES]

    logits = lax.dot_general(rwt_ref[...], h2, (((1,), (1,)), ((), ())),
                             preferred_element_type=F32, precision=HIGHEST) + rb_ref[...]
    eid = lax.broadcasted_iota(I32, (ne, tm), 0)
    neg = jnp.float32(-jnp.inf)
    work = logits
    hot = jnp.zeros((ne, tm), F32)
    vals, idxs, sels = [], [], []
    for _ in range(TOP_K):
        mx = jnp.max(work, axis=0, keepdims=True)
        ix = jnp.min(jnp.where(work == mx, eid, ne), axis=0, keepdims=True)
        sel = eid == ix
        work = jnp.where(sel, neg, work)
        hot = hot + sel.astype(F32)
        vals.append(mx)
        idxs.append(ix)
        sels.append(sel)
    ex = [jnp.exp(v - vals[0]) for v in vals]
    den = ex[0] + ex[1] + ex[2] + ex[3]
    gts = [e / den for e in ex]

    ti = lax.broadcasted_iota(I32, (tm, tm), 0)
    tj = lax.broadcasted_iota(I32, (tm, tm), 1)
    strict = (ti < tj).astype(BF16)
    before = _dot(hot.astype(BF16), strict) + carry_ref[...]
    rks = [jnp.sum(jnp.where(s, before, 0.0), axis=0, keepdims=True) for s in sels]
    carry_ref[...] = carry_ref[...] + jnp.sum(hot, axis=1, keepdims=True)

    row = lax.broadcasted_iota(I32, (SUBLANES, tm), 0)

    def pack(parts, zero):
        acc = jnp.full((SUBLANES, tm), zero, parts[0].dtype)
        for k, p in enumerate(parts):
            acc = jnp.where(row == k, jnp.broadcast_to(p, (SUBLANES, tm)), acc)
        return acc

    idx_ref[...] = pack(idxs, 0)
    gates_ref[...] = pack(gts, 0.0)
    rank_ref[...] = pack(rks, 0.0).astype(I32)
    cnt_ref[...] = jnp.broadcast_to(carry_ref[...], cnt_ref.shape)


def _merge(x, hl, ya, yb, w_m, w_ba, w_bb, w_out, gate2, shift3, scale4, g2, rwt, rb, tm):
    b, t, d = x.shape
    ne = rwt.shape[0]
    nt = t // tm
    tok = b * t
    kern = functools.partial(_merge_kernel, tm=tm, d=d, ne=ne)
    row_spec = pl.BlockSpec((None, tm, d), lambda bi, i: (bi, i, 0))
    mod_spec = pl.BlockSpec((None, 1, d), lambda bi, i: (bi, 0, 0))
    flat = lambda bi, i: (bi * nt + i, 0)
    lane = lambda bi, i: (0, bi * nt + i)

    def whole(shape):
        return pl.BlockSpec(shape, lambda bi, i: (0,) * len(shape))

    return pl.pallas_call(
        kern,
        grid=(b, nt),
        in_specs=[
            row_spec, row_spec, row_spec, row_spec,
            whole((d, 2 * d)), whole((d, d)), whole((d, d)), whole((d, d)),
            mod_spec, mod_spec, mod_spec,
            whole((1, d)), whole((ne, d)), whole((ne, 1)),
        ],
        out_specs=[
            row_spec,
            pl.BlockSpec((tm * (d // LANES), LANES), flat),
            pl.BlockSpec((SUBLANES, tm), lane),
            pl.BlockSpec((SUBLANES, tm), lane),
            pl.BlockSpec((SUBLANES, tm), lane),
            whole((ne, 128)),
        ],
        out_shape=[
            jax.ShapeDtypeStruct((b, t, d), F32),
            jax.ShapeDtypeStruct((tok * (d // LANES), LANES), F32),
            jax.ShapeDtypeStruct((SUBLANES, tok), I32),
            jax.ShapeDtypeStruct((SUBLANES, tok), F32),
            jax.ShapeDtypeStruct((SUBLANES, tok), I32),
            jax.ShapeDtypeStruct((ne, 128), F32),
        ],
        scratch_shapes=[pltpu.VMEM((ne, 1), F32)],
        compiler_params=_params(("arbitrary", "arbitrary"), 56),
        name="merge",
    )(x, hl, ya, yb, w_m, w_ba, w_bb, w_out, gate2, shift3, scale4, g2.reshape(1, d), rwt, rb)


def _slots_kernel(used_ref, pend_ref, dest_ref, code_ref, *, chunk, nslots, ne):
    i = pl.program_id(0)

    @pl.when(i == 0)
    def _():
        def fill(s, carry):
            code_ref[s] = -1
            return carry

        for e in range(ne):
            lax.fori_loop(used_ref[e], pend_ref[e], fill, 0)
        lax.fori_loop(pend_ref[ne - 1], nslots, fill, 0)

    base = i * chunk

    def body(a, carry):
        code_ref[dest_ref[0, 0, a]] = base + a
        return carry

    lax.fori_loop(0, chunk, body, 0, unroll=8)


def _slot_table(dest_flat, used_end, pend, nslots):
    n = dest_flat.shape[0]
    ne = pend.shape[0]
    chunk = _pick_tile(n, 8192)
    kern = functools.partial(_slots_kernel, chunk=chunk, nslots=nslots, ne=ne)
    grid_spec = pltpu.PrefetchScalarGridSpec(
        num_scalar_prefetch=2,
        grid=(n // chunk,),
        in_specs=[pl.BlockSpec((1, 1, chunk), lambda i, u, p: (i, 0, 0), memory_space=pltpu.SMEM)],
        out_specs=pl.BlockSpec(memory_space=pltpu.SMEM),
    )
    return pl.pallas_call(
        kern,
        grid_spec=grid_spec,
        out_shape=jax.ShapeDtypeStruct((nslots,), I32),
        compiler_params=_params(("arbitrary",), 32),
        name="slots",
    )(used_end, pend, dest_flat.reshape(n // chunk, 1, chunk))


def _moe_kernel(bexp_ref, bact_ref, tok_ref, tok1_ref, tok2_ref, dstp_ref, h2_hbm, w1_ref, b1_ref, w2_ref, b2_ref,
                yo_hbm, xbuf, ybuf, xb_s, act_s, w1p_s, w2_s, gsem, ssem, *, blk, nblocks):
    j = pl.program_id(0)
    slot = lax.rem(j, MOE_RING)
    far = lax.rem(j + 2, MOE_RING)
    d = xb_s.shape[1]
    f = w2_s.shape[1]
    ts = d // LANES
    pw = 2 * LANES
    active = bact_ref[j] == 1

    def gather_copy(tok, buf_slot, r):
        src = pl.multiple_of(tok * ts, ts)
        dst = pl.multiple_of((buf_slot * blk + r) * ts, ts)
        return pltpu.make_async_copy(h2_hbm.at[pl.ds(src, ts), :], xbuf.at[pl.ds(dst, ts), :], gsem.at[buf_slot])

    def scatter_copy(row, buf_slot, r):
        src = pl.multiple_of((buf_slot * blk + r) * ts, ts)
        dst = pl.multiple_of(row * ts, ts)
        return pltpu.make_async_copy(ybuf.at[pl.ds(src, ts), :], yo_hbm.at[pl.ds(dst, ts), :], ssem.at[buf_slot])

    def gather_loop(idx_ref, buf_slot):
        def body(i, carry):
            gather_copy(idx_ref[0, 0, 2 * i], buf_slot, 2 * i).start(priority=0)
            gather_copy(idx_ref[0, 0, 2 * i + 1], buf_slot, 2 * i + 1).start(priority=1)
            return carry
        lax.fori_loop(0, blk // 2, body, 0, unroll=4)

    def scatter_loop(buf_slot):
        def body(i, carry):
            scatter_copy(dstp_ref[0, 0, 2 * i], buf_slot, 2 * i).start(priority=0)
            scatter_copy(dstp_ref[0, 0, 2 * i + 1], buf_slot, 2 * i + 1).start(priority=1)
            return carry
        lax.fori_loop(0, blk // 2, body, 0, unroll=4)

    def block_rows(buf_slot):
        return pl.ds(pl.multiple_of(buf_slot * blk * ts, ts), blk * ts)

    def gather_wait_block(buf_slot):
        pltpu.make_async_copy(h2_hbm.at[pl.ds(0, blk * ts), :], xbuf.at[block_rows(buf_slot), :],
                              gsem.at[buf_slot]).wait()

    def scatter_wait_block(buf_slot):
        pltpu.make_async_copy(ybuf.at[block_rows(buf_slot), :], yo_hbm.at[pl.ds(0, blk * ts), :],
                              ssem.at[buf_slot]).wait()

    @pl.when(j == 0)
    def _():
        gather_loop(tok_ref, 0)
        gather_loop(tok1_ref, 1)
        ybuf[...] = jnp.zeros_like(ybuf)
        n_scratch = 2 * blk * ts
        fill = pltpu.make_async_copy(ybuf.at[pl.ds(0, n_scratch), :],
                                     yo_hbm.at[pl.ds(yo_hbm.shape[0] - n_scratch, n_scratch), :], ssem.at[0])
        fill.start()
        fill.wait()

    gather_wait_block(slot)

    prev_active = bact_ref[jnp.maximum(j - 1, 0)] == 1

    @pl.when((j == 2) | ((j >= 3) & (bact_ref[jnp.maximum(j - 3, 0)] == 1)))
    def _():
        scatter_wait_block(slot)

    new_expert = (j == 0) | (bexp_ref[j] != bexp_ref[jnp.maximum(j - 1, 0)])

    @pl.when(new_expert & active)
    def _():
        src = lax.broadcasted_iota(I32, (pw, pw), 0)
        dst = lax.broadcasted_iota(I32, (pw, pw), 1)
        want = jnp.where(dst < pw // 2, 2 * dst, 2 * (dst - pw // 2) + 1)
        sel = (src == want).astype(BF16)
        for cb in range(2 * f // pw):
            w1p_s[cb] = _dot(w1_ref[:, cb * pw:(cb + 1) * pw].astype(BF16), sel).astype(BF16)
        for cb in range(d // pw):
            w2_s[cb] = w2_ref[:, cb * pw:(cb + 1) * pw].astype(BF16)

    n1 = 2 * f // pw
    n2 = d // pw
    per1 = (blk * 5 // 8) // n1
    per2 = (blk - per1 * n1) // n2
    assert per1 * n1 + per2 * n2 == blk

    xbase = pl.multiple_of(slot * blk * ts, ts)

    @pl.when(active)
    def _():
        for s in range(ts):
            xb_s[:, s * LANES:(s + 1) * LANES] = xbuf[pl.ds(xbase + s, blk, stride=ts), :].astype(BF16)

    def transfers(r0, count):
        for r in range(r0, r0 + count):
            gather_copy(tok2_ref[0, 0, r], far, r).start(priority=r % 2)
        for r in range(r0, r0 + count):
            scatter_copy(dstp_ref[0, 0, r], far, r).start(priority=r % 2)

    @pl.when(active)
    def _():
        for n in range(n1):
            h = _dot(xb_s[...], w1p_s[n]) + b1_ref[n]
            hg = jnp.minimum(h[:, :LANES], SWIGLU_LIMIT)
            hu = jnp.clip(h[:, LANES:], -SWIGLU_LIMIT, SWIGLU_LIMIT)
            act_s[n] = (hg * _sigmoid(SWIGLU_ALPHA * hg) * (hu + 1.0)).astype(BF16)
            transfers(n * per1, per1)
        act = jnp.concatenate([act_s[i] for i in range(n1)], axis=1)
        for n in range(n2):
            y = _dot(act, w2_s[n]) + b2_ref[n]
            ybuf[pl.ds(xbase + 2 * n, blk, stride=ts), :] = y[:, :LANES]
            ybuf[pl.ds(xbase + 2 * n + 1, blk, stride=ts), :] = y[:, LANES:]
            transfers(n1 * per1 + n * per2, per2)

    @pl.when(jnp.logical_not(active) & (j + 2 < nblocks))
    def _():
        gather_loop(tok2_ref, far)

    @pl.when(jnp.logical_not(active) & prev_active)
    def _():
        scatter_loop(far)


def _moe(block_expert, block_active, slot_tok3, slot_dst3, h2t, w1, b1p, w2, b2, n_out_rows, blk):
    nblocks = block_expert.shape[0]
    f, d = w2.shape[1], w2.shape[2]
    ts = d // LANES
    pw = 2 * LANES
    n1, n2 = 2 * f // pw, d // pw
    b1p = b1p.reshape(-1, n1, 1, pw)
    b2 = b2.reshape(-1, n2, 1, pw)
    kern = functools.partial(_moe_kernel, blk=blk, nblocks=nblocks)
    wmap = lambda j, be, ba: (be[j], 0, 0)
    grid_spec = pltpu.PrefetchScalarGridSpec(
        num_scalar_prefetch=2,
        grid=(nblocks,),
        in_specs=[
            pl.BlockSpec((1, 1, blk), lambda j, be, ba: (j, 0, 0), memory_space=pltpu.SMEM),
            pl.BlockSpec((1, 1, blk), lambda j, be, ba: (jnp.minimum(j + 1, nblocks - 1), 0, 0),
                         memory_space=pltpu.SMEM),
            pl.BlockSpec((1, 1, blk), lambda j, be, ba: (jnp.minimum(j + 2, nblocks - 1), 0, 0),
                         memory_space=pltpu.SMEM),
            pl.BlockSpec((1, 1, blk), lambda j, be, ba: (j, 0, 0), memory_space=pltpu.SMEM),
            pl.BlockSpec(memory_space=pl.ANY),
            pl.BlockSpec((None, d, 2 * f), wmap),
            pl.BlockSpec((None, n1, 1, pw), lambda j, be, ba: (be[j], 0, 0, 0)),
            pl.BlockSpec((None, f, d), wmap),
            pl.BlockSpec((None, n2, 1, pw), lambda j, be, ba: (be[j], 0, 0, 0)),
        ],
        out_specs=pl.BlockSpec(memory_space=pl.ANY),
        scratch_shapes=[
            pltpu.VMEM((MOE_RING * blk * ts, LANES), F32),
            pltpu.VMEM((MOE_RING * blk * ts, LANES), F32),
            pltpu.VMEM((blk, d), BF16),
            pltpu.VMEM((n1, blk, LANES), BF16),
            pltpu.VMEM((n1, d, pw), BF16),
            pltpu.VMEM((n2, f, pw), BF16),
            pltpu.SemaphoreType.DMA((MOE_RING,)),
            pltpu.SemaphoreType.DMA((MOE_RING,)),
        ],
    )
    return pl.pallas_call(
        kern,
        grid_spec=grid_spec,
        out_shape=jax.ShapeDtypeStruct((n_out_rows * ts, LANES), F32),
        compiler_params=_params(("arbitrary",), 56),
        name="moe",
    )(block_expert, block_active, slot_tok3, slot_tok3, slot_tok3, slot_dst3, h2t, w1, b1p, w2, b2)


def _final_kernel(*refs, tm, d):
    y_refs = refs[:TOP_K]
    x1_ref, gt_ref, gate_ref, fg_ref, o_ref = refs[TOP_K:]
    ts = d // LANES
    gt = gt_ref[...]
    acc = None
    for k in range(TOP_K):
        yk = jnp.concatenate([y_refs[k][pl.ds(s, tm, stride=ts), :] for s in range(ts)], axis=1)
        term = gt[:, k:k + 1] * yk
        acc = term if acc is None else acc + term
    x2 = x1_ref[...] + gate_ref[...] * acc
    o_ref[...] = _rms(x2) * fg_ref[...]


def _final(yo, x1, gates_t, gate5, final_g, tm):
    b, t, d = x1.shape
    ts = d // LANES
    nt = t // tm
    ntiles = b * nt
    kern = functools.partial(_final_kernel, tm=tm, d=d)
    row_map = lambda i: (i // nt, i % nt, 0)
    y_specs = [pl.BlockSpec((tm * ts, LANES), functools.partial(lambda i, k: (k * ntiles + i, 0), k=k))
               for k in range(TOP_K)]
    return pl.pallas_call(
        kern,
        grid=(ntiles,),
        in_specs=y_specs + [
            pl.BlockSpec((None, tm, d), row_map),
            pl.BlockSpec((tm, TOP_K), lambda i: (i, 0)),
            pl.BlockSpec((None, 1, d), lambda i: (i // nt, 0, 0)),
            pl.BlockSpec((1, d), lambda i: (0, 0)),
        ],
        out_specs=pl.BlockSpec((None, tm, d), row_map),
        out_shape=jax.ShapeDtypeStruct((b, t, d), F32),
        compiler_params=_params(("arbitrary",), 48),
        name="final",
    )(*([yo] * TOP_K), x1, gates_t, gate5, final_g.reshape(1, d))


def _pick_tile(n, pref):
    tm = pref
    while n % tm:
        tm //= 2
    return tm


def kernel(x, c, ctx, c_ctx, ada_w, ada_b, norm1_g, norm2_g, w_in, lru_conv_w, lru_conv_b, lru_wa, lru_ba, lru_wx, lru_bx, lru_lam, hg_lb_logits, hg_norm_g, w_branch_a, w_branch_b, w_out, router_w, router_b, moe_w1, moe_b1, moe_w2, moe_b2, final_g):
    b, t, d = x.shape
    cx = ctx.shape[1]
    layer = 0
    w_lru = lru_conv_w.shape[2]
    qk = (w_in.shape[2] - 2 * w_lru - 2 * d) // 5
    heads = qk // HG_DK
    ne = router_w.shape[2]

    pad = (-(b + 1)) % SUBLANES
    c_all = jnp.concatenate([c, c_ctx[None, :], jnp.zeros((pad, d), F32)], axis=0)
    mod = _modulation(c_all, ada_w[layer], ada_b[layer])
    mod_l = mod[:b].reshape(b, N_MOD, 1, d)
    mod_c = mod[b].reshape(N_MOD, 1, 1, d)

    w_in_b = w_in[layer].astype(BF16)
    w_lru_in = w_in_b[:, :2 * w_lru]
    hg0 = 2 * w_lru
    w_hg = w_in_b[:, hg0:hg0 + 5 * qk].reshape(d, 5, heads, HG_DK).transpose(2, 0, 1, 3).reshape(heads, d, 5 * HG_DK)
    w_m = w_in_b[:, hg0 + 5 * qk:]
    g_blocks = w_lru // LRU_BLOCK
    wg = jnp.concatenate([lru_wa[layer, 0], lru_wx[layer, 0], lru_wa[layer, 1], lru_wx[layer, 1]], axis=-1).astype(BF16)
    bg = jnp.concatenate([lru_ba[layer, 0], lru_bx[layer, 0], lru_ba[layer, 1], lru_bx[layer, 1]], axis=-1)
    bg = bg.reshape(g_blocks, 1, 4 * LRU_BLOCK)
    lb_all = jnp.cumsum(jax.nn.softmax(hg_lb_logits.astype(F32), axis=0), axis=0)
    lb = lb_all[layer].reshape(heads, 1, HG_DK)

    tm_n = _pick_tile(t, 512)
    hl = _prenorm(x, mod_l[:, 0], mod_l[:, 1], norm1_g[layer], tm_n)
    hc = _prenorm(ctx, mod_c[0], mod_c[1], norm1_g[layer], _pick_tile(cx, 256))
    axl = _matmul(hl.reshape(b * t, d), w_lru_in, _pick_tile(b * t, 512), BF16).reshape(b, t, 2 * w_lru)
    axc = _matmul(hc.reshape(b * cx, d), w_lru_in, _pick_tile(b * cx, 512), BF16).reshape(b, cx, 2 * w_lru)
    ya = _lru(axl, axc, lru_conv_w[layer], lru_conv_b[layer], wg, bg, lru_lam[layer], b, cx, t)
    yb = _hgrn(hc, hl, w_hg, lb, hg_norm_g[layer])

    rwt = router_w[layer].T
    rb = router_b[layer].reshape(ne, 1)
    x1, h2, idx8, gates8, rank8, cnt = _merge(
        x, hl, ya, yb, w_m, w_branch_a[layer].astype(BF16), w_branch_b[layer].astype(BF16),
        w_out[layer].astype(BF16), mod_l[:, 2], mod_l[:, 3], mod_l[:, 4], norm2_g[layer], rwt, rb,
        _pick_tile(t, 512))

    blk = MOE_BLOCK
    tok = b * t
    n_assign = tok * TOP_K
    nblocks = -(-n_assign // blk) + ne + MOE_RING - 1
    counts = cnt[:, 0].astype(I32)
    padded = (counts + blk - 1) // blk * blk
    pend = jnp.cumsum(padded)
    pstart = pend - padded
    idx = idx8[:TOP_K]
    onehot = idx[:, :, None] == jnp.arange(ne, dtype=I32)[None, None, :]
    dest = rank8[:TOP_K] + jnp.sum(jnp.where(onehot, pstart[None, None, :], 0), axis=-1)
    block_start = jnp.arange(nblocks, dtype=I32) * blk
    block_active = block_start < pend[-1]
    block_expert = jnp.minimum(jnp.sum(block_start[:, None] >= pend[None, :], axis=1), ne - 1).astype(I32)
    last_expert = jnp.max(jnp.where(block_active, block_expert, 0))
    block_expert = jnp.where(block_active, block_expert, last_expert).astype(I32)
    block_active = block_active.astype(I32)
    slot_code = _slot_table(dest.reshape(-1), (pstart + counts).astype(I32), pend.astype(I32), nblocks * blk)
    slot_tok = jnp.where(slot_code >= 0, slot_code % tok, 0)
    slot_ids = jnp.arange(nblocks * blk, dtype=I32)
    scratch_row = n_assign + (slot_ids // blk % 2) * blk + slot_ids % blk
    slot_dst = jnp.where(slot_code >= 0, slot_code, scratch_row)
    first_dst = n_assign + blk + jnp.arange(blk, dtype=I32)
    slot_dst = jnp.concatenate([first_dst, slot_dst[:-blk]])

    b1 = moe_b1[layer]
    f = b1.shape[1] // 2
    b1p = b1.reshape(ne, f // LANES, LANES, 2).transpose(0, 1, 3, 2).reshape(ne, 1, 2 * f)
    yo = _moe(block_expert, block_active, slot_tok.reshape(nblocks, 1, blk), slot_dst.reshape(nblocks, 1, blk),
              h2, moe_w1[layer], b1p, moe_w2[layer], moe_b2[layer].reshape(ne, 1, d), n_assign + 2 * blk, blk)

    gates_t = gates8[:TOP_K].T
    return _final(yo, x1, gates_t, mod_l[:, 5], final_g, _pick_tile(t, 256))
```

```python
import functools

import jax
import jax.numpy as jnp
from jax import lax
from jax.experimental import pallas as pl
from jax.experimental.pallas import tpu as pltpu

F32 = jnp.float32
BF16 = jnp.bfloat16
I32 = jnp.int32
HIGHEST = lax.Precision.HIGHEST

EPS = 1e-6
N_MOD = 6
GRID_W = 64
CONV_LEFT = 2
CONV_WIDTH = 4
LRU_C = 8.0
LRU_BLOCK = 128
HG_DK = 128
HG_CHUNK = 64
TOP_K = 4
SWIGLU_LIMIT = 7.0
SWIGLU_ALPHA = 1.702
MOE_BLOCK = 256
MOE_CHUNK = 16
SORT_ROWS = 256
MOE_RING = 3
SUBLANES = 8
LANES = 128
HALO = 8


def _params(sem, vmem_mb):
    return pltpu.CompilerParams(dimension_semantics=sem, vmem_limit_bytes=vmem_mb * 1024 * 1024)


def _dot(a, b):
    return jnp.dot(a, b, preferred_element_type=F32)


def _dot_nt(a, b):
    return lax.dot_general(a, b, (((1,), (1,)), ((), ())), preferred_element_type=F32)


def _dot_tn(a, b):
    return lax.dot_general(a, b, (((0,), (0,)), ((), ())), preferred_element_type=F32)


def _sigmoid(x):
    return jax.nn.sigmoid(x)


def _silu(x):
    return x * jax.nn.sigmoid(x)


def _rms(x):
    return x * lax.rsqrt(jnp.mean(x * x, axis=-1, keepdims=True) + EPS)


def _mod_kernel(c_ref, w_ref, b_ref, o_ref):
    s = _silu(c_ref[...])
    o_ref[...] = jnp.dot(s, w_ref[...], preferred_element_type=F32, precision=HIGHEST) + b_ref[...]


def _modulation(c_all, ada_w, ada_b):
    m, d = c_all.shape
    n = ada_w.shape[1]
    tn = 1024
    return pl.pallas_call(
        _mod_kernel,
        grid=(n // tn,),
        in_specs=[
            pl.BlockSpec((m, d), lambda j: (0, 0)),
            pl.BlockSpec((d, tn), lambda j: (0, j)),
            pl.BlockSpec((1, tn), lambda j: (0, j)),
        ],
        out_specs=pl.BlockSpec((m, tn), lambda j: (0, j)),
        out_shape=jax.ShapeDtypeStruct((m, n), F32),
        compiler_params=_params(("arbitrary",), 32),
        name="mod",
    )(c_all, ada_w, ada_b.reshape(1, n))


def _prenorm_kernel(x_ref, shift_ref, scale_ref, g_ref, o_ref):
    y = _rms(x_ref[...]) * g_ref[...]
    o_ref[...] = (y * (1.0 + scale_ref[...]) + shift_ref[...]).astype(o_ref.dtype)


def _prenorm(x, shift, scale, g, tm):
    b, l, d = x.shape
    per_batch = shift.shape[0] == b and b > 1
    mod_map = (lambda bi, i: (bi, 0, 0)) if per_batch else (lambda bi, i: (0, 0, 0))
    return pl.pallas_call(
        _prenorm_kernel,
        grid=(b, l // tm),
        in_specs=[
            pl.BlockSpec((None, tm, d), lambda bi, i: (bi, i, 0)),
            pl.BlockSpec((None, 1, d), mod_map),
            pl.BlockSpec((None, 1, d), mod_map),
            pl.BlockSpec((1, d), lambda bi, i: (0, 0)),
        ],
        out_specs=pl.BlockSpec((None, tm, d), lambda bi, i: (bi, i, 0)),
        out_shape=jax.ShapeDtypeStruct((b, l, d), BF16),
        compiler_params=_params(("arbitrary", "arbitrary"), 32),
        name="prenorm",
    )(x, shift, scale, g.reshape(1, d))


def _mm_kernel(x_ref, w_ref, o_ref):
    o_ref[...] = _dot(x_ref[...], w_ref[...]).astype(o_ref.dtype)


def _matmul(x2d, w, tm, out_dtype):
    m, k = x2d.shape
    n = w.shape[1]
    return pl.pallas_call(
        _mm_kernel,
        grid=(m // tm,),
        in_specs=[
            pl.BlockSpec((tm, k), lambda i: (i, 0)),
            pl.BlockSpec((k, n), lambda i: (0, 0)),
        ],
        out_specs=pl.BlockSpec((tm, n), lambda i: (i, 0)),
        out_shape=jax.ShapeDtypeStruct((m, n), out_dtype),
        compiler_params=_params(("arbitrary",), 40),
        name="inproj_lru",
    )(x2d, w)


def _lru_kernel(axl_ref, agl_ref, axc_ref, cw_ref, cb_ref, wg_ref, bg_ref, lam_ref, o_ref,
                xc_ref, yacc_ref, sx_ref, sa_ref, sb_ref, sh_ref, *, nb, cx, t):
    tb = GRID_W
    ctx0 = HALO
    gw = 2 * LRU_BLOCK
    pre = CONV_LEFT
    steps = pre + tb + (HALO - pre)

    zeros_h = jnp.zeros((HALO, LRU_BLOCK), F32)
    for b in range(nb):
        xc_ref[b, 0:HALO, :] = zeros_h
        xc_ref[b, ctx0:ctx0 + cx, :] = axc_ref[b].astype(F32)
        xc_ref[b, ctx0 + cx:ctx0 + cx + HALO, :] = zeros_h

    cw = cw_ref[...]
    cb = cb_ref[...]
    lam = lam_ref[...]
    sp = jnp.maximum(-lam, 0.0) + jnp.log1p(jnp.exp(-jnp.abs(lam)))

    def zero_history():
        sx_ref[0:pre * nb, :] = jnp.zeros((pre * nb, LRU_BLOCK), F32)
        sx_ref[(pre + tb) * nb:steps * nb, :] = jnp.zeros(((steps - pre - tb) * nb, LRU_BLOCK), F32)

    def load_ctx_block(base):
        for b in range(nb):
            sx_ref[pl.ds(b, steps, stride=nb), :] = xc_ref[b, pl.ds(base - pre, steps), :]

    def load_latent_block(r0):
        for b in range(nb):
            sx_ref[pl.ds(pre * nb + b, tb, stride=nb), :] = axl_ref[b, pl.ds(r0, tb), :].astype(F32)

    def fill_gates(d):
        u = jnp.broadcast_to(cb, (tb * nb, LRU_BLOCK))
        for k in range(CONV_WIDTH):
            u = u + sx_ref[k * nb:(k + tb) * nb, :] * cw[k:k + 1, :]
        z = _dot(u.astype(BF16), wg_ref[:, d * gw:(d + 1) * gw]) + bg_ref[:, d * gw:(d + 1) * gw]
        r = _sigmoid(z[:, :LRU_BLOCK])
        i = _sigmoid(z[:, LRU_BLOCK:])
        log_a = (-LRU_C) * r * sp[d:d + 1, :]
        a = jnp.exp(log_a)
        mult = jnp.sqrt(-jnp.tanh(log_a) * (a * a + 1.0))
        sa_ref[...] = a
        sb_ref[...] = mult * (i * u)

    def scan_block(h, reverse):
        order = range(tb - 1, -1, -1) if reverse else range(tb)
        for s in order:
            h = sa_ref[s * nb:(s + 1) * nb, :] * h + sb_ref[s * nb:(s + 1) * nb, :]
            sh_ref[s * nb:(s + 1) * nb, :] = h
        return h

    def sweep(d):
        reverse = d == 1
        n_c, n_l = cx // tb, t // tb

        def ctx_body(j, h):
            jj = (n_c - 1 - j) if reverse else j
            load_ctx_block(pl.multiple_of(ctx0 + jj * tb, SUBLANES))
            fill_gates(d)
            return scan_block(h, reverse)

        def lat_body(j, h):
            jj = (n_l - 1 - j) if reverse else j
            r0 = pl.multiple_of(jj * tb, tb)
            load_latent_block(r0)
            fill_gates(d)
            h = scan_block(h, reverse)
            for b in range(nb):
                yb = sh_ref[pl.ds(b, tb, stride=nb), :]
                if not reverse:
                    yacc_ref[b, pl.ds(r0, tb), :] = yb
                else:
                    gate = jax.nn.gelu(agl_ref[b, pl.ds(r0, tb), :].astype(F32))
                    o_ref[b, pl.ds(r0, tb), :] = ((yacc_ref[b, pl.ds(r0, tb), :] + yb) * gate).astype(o_ref.dtype)
            return h

        h0 = jnp.zeros((nb, LRU_BLOCK), F32)
        h1 = lax.fori_loop(0, n_c, ctx_body, h0)
        zero_history()
        lax.fori_loop(0, n_l, lat_body, h1)

    sweep(0)
    sweep(1)


def _lru(axl, axc, conv_w, conv_b, wg, bg, lam, nb, cx, t):
    w = conv_w.shape[1]
    g = w // LRU_BLOCK
    kern = functools.partial(_lru_kernel, nb=nb, cx=cx, t=t)
    return pl.pallas_call(
        kern,
        grid=(g,),
        in_specs=[
            pl.BlockSpec((nb, t, LRU_BLOCK), lambda j: (0, 0, j)),
            pl.BlockSpec((nb, t, LRU_BLOCK), lambda j: (0, 0, g + j)),
            pl.BlockSpec((nb, cx, LRU_BLOCK), lambda j: (0, 0, j)),
            pl.BlockSpec((CONV_WIDTH, LRU_BLOCK), lambda j: (0, j)),
            pl.BlockSpec((1, LRU_BLOCK), lambda j: (0, j)),
            pl.BlockSpec((None, LRU_BLOCK, 4 * LRU_BLOCK), lambda j: (j, 0, 0)),
            pl.BlockSpec((None, 1, 4 * LRU_BLOCK), lambda j: (j, 0, 0)),
            pl.BlockSpec((2, LRU_BLOCK), lambda j: (0, j)),
        ],
        out_specs=pl.BlockSpec((nb, t, LRU_BLOCK), lambda j: (0, 0, j)),
        out_shape=jax.ShapeDtypeStruct((nb, t, w), BF16),
        scratch_shapes=[
            pltpu.VMEM((nb, cx + 2 * HALO, LRU_BLOCK), F32),
            pltpu.VMEM((nb, t, LRU_BLOCK), F32),
            pltpu.VMEM(((GRID_W + HALO) * nb, LRU_BLOCK), F32),
            pltpu.VMEM((GRID_W * nb, LRU_BLOCK), F32),
            pltpu.VMEM((GRID_W * nb, LRU_BLOCK), F32),
            pltpu.VMEM((GRID_W * nb, LRU_BLOCK), F32),
        ],
        compiler_params=_params(("arbitrary",), 56),
        name="lru",
    )(axl, axl, axc, conv_w, conv_b.reshape(1, w), wg, bg, lam)


def _hgrn_kernel(hc_ref, hl_ref, w_ref, lb_ref, ng_ref, o_ref,
                 p_ref, oacc_ref, q_s, v_s, k_s, hl_s, gc_s, bk_s, sc_s, aq_s, upd_ref, dec_ref,
                 *, cx, t):
    c = HG_CHUNK
    dk = HG_DK
    n_c, n_l = cx // c, t // c
    n_all = n_c + n_l

    p_ref[0:cx, :] = _dot(hc_ref[...], w_ref[...])
    rows = 512 if t % 512 == 0 else c

    def proj_body(i, carry):
        r = pl.multiple_of(i * rows, rows)
        p_ref[pl.ds(cx + r, rows), :] = _dot(hl_ref[pl.ds(r, rows), :], w_ref[...])
        return carry

    lax.fori_loop(0, t // rows, proj_body, 0)

    lb = lb_ref[...]
    one_m_lb = 1.0 - lb
    ri = lax.broadcasted_iota(I32, (c, c), 0)
    ci = lax.broadcasted_iota(I32, (c, c), 1)
    keep = (ri >= ci, ci >= ri)
    mid = (c // 2 - 1, c // 2)
    end = (c - 1, 0)

    def stage_gates(j, carry):
        r0 = pl.multiple_of(j * c, c)
        q_s[pl.ds(r0, c), :] = _silu(p_ref[pl.ds(r0, c), 0:dk])
        v_s[pl.ds(r0, c), :] = p_ref[pl.ds(r0, c), 3 * dk:4 * dk].astype(BF16)
        for d in range(2):
            f = lb + one_m_lb * _sigmoid(p_ref[pl.ds(r0, c), (1 + d) * dk:(2 + d) * dk])
            logf = jnp.log(f)
            hi = logf.astype(BF16)
            lo = (logf - hi.astype(F32)).astype(BF16)
            k_s[d, pl.ds(r0, c), :] = 1.0 - f
            hl_s[d, pl.ds(r0, c), :] = jnp.concatenate([hi, lo], axis=1)
        return carry

    lax.fori_loop(0, n_all, stage_gates, 0, unroll=2)

    def stage_cumsum(j, carry):
        r0 = pl.multiple_of(j * c, c)
        for d in range(2):
            s2 = _dot(keep[d].astype(BF16), hl_s[d, pl.ds(r0, c), :])
            gc_s[d, pl.ds(r0, c), :] = s2[:, :dk] + s2[:, dk:]
        return carry

    lax.fori_loop(0, n_all, stage_cumsum, 0, unroll=6)

    def stage_scores(j, carry):
        r0 = pl.multiple_of(j * c, c)
        q = q_s[pl.ds(r0, c), :]
        for d in range(2):
            gc = gc_s[d, pl.ds(r0, c), :]
            k = k_s[d, pl.ds(r0, c), :]
            g_mid = gc[mid[d]:mid[d] + 1, :]
            g_end = gc[end[d]:end[d] + 1, :]
            a = (q * jnp.exp(gc - g_mid)).astype(BF16)
            bm = (k * jnp.exp(g_mid - gc)).astype(BF16)
            sc_s[d, pl.ds(r0, c), :] = jnp.where(keep[d], _dot_nt(a, bm), 0.0).astype(BF16)
            bk_s[d, pl.ds(r0, c), :] = (k * jnp.exp(g_end - gc)).astype(BF16)
            aq_s[d, pl.ds(r0, c), :] = (q * jnp.exp(gc)).astype(BF16)
            dec_ref[d, j] = jnp.broadcast_to(jnp.exp(g_end), (SUBLANES, dk))
        return carry

    lax.fori_loop(0, n_all, stage_scores, 0, unroll=3)

    def stage_intra(j, carry):
        r0 = pl.multiple_of(j * c, c)
        v = v_s[pl.ds(r0, c), :]
        oacc_ref[pl.ds(r0, c), :] = _dot(sc_s[0, pl.ds(r0, c), :], v) + _dot(sc_s[1, pl.ds(r0, c), :], v)
        for d in range(2):
            upd_ref[d, j] = _dot_tn(v, bk_s[d, pl.ds(r0, c), :])
        return carry

    lax.fori_loop(0, n_all, stage_intra, 0, unroll=6)

    def inter(d, jj, st):
        r0 = pl.multiple_of(jj * c, c)
        oacc_ref[pl.ds(r0, c), :] += _dot_nt(aq_s[d, pl.ds(r0, c), :], st.astype(BF16))
        return st * dec_ref[d, jj][0:1, :] + upd_ref[d, jj]

    def ctx_body(j, sts):
        return inter(0, j, sts[0]), inter(1, n_c - 1 - j, sts[1])

    def lat_body(j, sts):
        return inter(0, n_c + j, sts[0]), inter(1, n_c + n_l - 1 - j, sts[1])

    zero = jnp.zeros((dk, dk), F32)
    sts = lax.fori_loop(0, n_c, ctx_body, (zero, zero), unroll=2)
    lax.fori_loop(0, n_l, lat_body, sts, unroll=4)

    ng = ng_ref[...]

    def readout(i, carry):
        r = pl.multiple_of(i * rows, rows)
        o = oacc_ref[pl.ds(cx + r, rows), :]
        g = p_ref[pl.ds(cx + r, rows), 4 * dk:5 * dk]
        o_ref[pl.ds(r, rows), :] = (_rms(o) * ng * _silu(g)).astype(o_ref.dtype)
        return carry

    lax.fori_loop(0, t // rows, readout, 0)


def _hgrn(hc, hl, w_hg, lb, norm_g):
    b, cx, d = hc.shape
    t = hl.shape[1]
    heads = w_hg.shape[0]
    dk = HG_DK
    n_all = (cx + t) // HG_CHUNK
    kern = functools.partial(_hgrn_kernel, cx=cx, t=t)
    return pl.pallas_call(
        kern,
        grid=(b, heads),
        in_specs=[
            pl.BlockSpec((None, cx, d), lambda bi, h: (bi, 0, 0)),
            pl.BlockSpec((None, t, d), lambda bi, h: (bi, 0, 0)),
            pl.BlockSpec((None, d, 5 * dk), lambda bi, h: (h, 0, 0)),
            pl.BlockSpec((None, 1, dk), lambda bi, h: (h, 0, 0)),
            pl.BlockSpec((1, dk), lambda bi, h: (0, 0)),
        ],
        out_specs=pl.BlockSpec((None, t, dk), lambda bi, h: (bi, 0, h)),
        out_shape=jax.ShapeDtypeStruct((b, t, heads * dk), BF16),
        scratch_shapes=[
            pltpu.VMEM((cx + t, 5 * dk), F32),
            pltpu.VMEM((cx + t, dk), F32),
            pltpu.VMEM((cx + t, dk), F32),
            pltpu.VMEM((cx + t, dk), BF16),
            pltpu.VMEM((2, cx + t, dk), F32),
            pltpu.VMEM((2, cx + t, 2 * dk), BF16),
            pltpu.VMEM((2, cx + t, dk), F32),
            pltpu.VMEM((2, cx + t, dk), BF16),
            pltpu.VMEM((2, cx + t, HG_CHUNK), BF16),
            pltpu.VMEM((2, cx + t, dk), BF16),
            pltpu.VMEM((2, n_all, dk, dk), F32),
            pltpu.VMEM((2, n_all, SUBLANES, dk), F32),
        ],
        compiler_params=_params(("arbitrary", "arbitrary"), 48),
        name="hgrn",
    )(hc, hl, w_hg, lb, norm_g.reshape(1, dk))


def _merge_kernel(x_ref, h_ref, ya_ref, yb_ref, wm_ref, wa_ref, wb_ref, wo_ref,
                  gate_ref, shift_ref, scale_ref, g2_ref, rwt_ref, rb_ref,
                  x1_ref, hs_ref, pos_ref, cnt_ref, h2b_s, *, tm, d, ne, cap):
    m = _dot(h_ref[...], wm_ref[...])
    za = _dot(ya_ref[...], wa_ref[...])
    zb = _dot(yb_ref[...], wb_ref[...])
    mix = _sigmoid(m[:, :d]) * za + _sigmoid(m[:, d:]) * zb
    out = _dot(mix.astype(BF16), wo_ref[...])
    x1 = x_ref[...] + gate_ref[...] * out
    x1_ref[...] = x1
    h2 = _rms(x1) * g2_ref[...] * (1.0 + scale_ref[...]) + shift_ref[...]
    h2b_s[...] = h2.astype(BF16)

    logits = lax.dot_general(rwt_ref[...], h2, (((1,), (1,)), ((), ())),
                             preferred_element_type=F32, precision=HIGHEST) + rb_ref[...]
    eid = lax.broadcasted_iota(I32, (ne, tm), 0)
    neg = jnp.float32(-jnp.inf)
    work = logits
    hot = jnp.zeros((ne, tm), F32)
    vals, idxs, sels = [], [], []
    for _ in range(TOP_K):
        mx = jnp.max(work, axis=0, keepdims=True)
        ix = jnp.min(jnp.where(work == mx, eid, ne), axis=0, keepdims=True)
        sel = eid == ix
        work = jnp.where(sel, neg, work)
        hot = hot + sel.astype(F32)
        vals.append(mx)
        idxs.append(ix)
        sels.append(sel)
    ex = [jnp.exp(v - vals[0]) for v in vals]
    den = ex[0] + ex[1] + ex[2] + ex[3]
    gts = [e / den for e in ex]

    ti = lax.broadcasted_iota(I32, (tm, tm), 0)
    tj = lax.broadcasted_iota(I32, (tm, tm), 1)
    strict = (ti < tj).astype(BF16)
    before = _dot(hot.astype(BF16), strict)
    cnt = jnp.sum(hot, axis=1, keepdims=True)
    nchunks = jnp.ceil(cnt * (1.0 / MOE_CHUNK))
    ei = lax.broadcasted_iota(I32, (ne, ne), 0)
    ej = lax.broadcasted_iota(I32, (ne, ne), 1)
    earlier = (ej < ei).astype(BF16)
    start = _dot(earlier, jnp.broadcast_to(nchunks, (ne, LANES)).astype(BF16))[:, 0:1] * MOE_CHUNK
    where_to = before + start
    poss = [jnp.sum(jnp.where(s, where_to, 0.0), axis=0, keepdims=True).astype(I32) for s in sels]

    rb_rows = SORT_ROWS
    lane = lax.broadcasted_iota(I32, (rb_rows, LANES), 1)

    def sort_rows(i, carry):
        r0 = pl.multiple_of(i * rb_rows, rb_rows)
        pid = lax.broadcasted_iota(I32, (rb_rows, tm), 0) + r0
        place = jnp.zeros((rb_rows, tm), F32)
        gsel = jnp.zeros((rb_rows, tm), F32)
        for k in range(TOP_K):
            hit = pid == poss[k]
            place = jnp.where(hit, 1.0, place)
            gsel = jnp.where(hit, gts[k], gsel)
        hs_ref[pl.ds(r0, rb_rows), 0:d] = _dot(place.astype(BF16), h2b_s[...]).astype(BF16)
        g = jnp.sum(gsel, axis=1, keepdims=True)
        g_hi = g.astype(BF16).astype(F32)
        hs_ref[pl.ds(r0, rb_rows), d:d + LANES] = jnp.where(lane < LANES // 2, g_hi, g - g_hi).astype(BF16)
        return carry

    lax.fori_loop(0, cap // rb_rows, sort_rows, 0)

    row = lax.broadcasted_iota(I32, (SUBLANES, tm), 0)
    packed = jnp.zeros((SUBLANES, tm), I32)
    for k, p in enumerate(poss):
        packed = jnp.where(row == k, jnp.broadcast_to(p, (SUBLANES, tm)), packed)
    pos_ref[...] = packed
    cnt_ref[...] = jnp.broadcast_to(cnt, cnt_ref.shape)


def _merge(x, hl, ya, yb, w_m, w_ba, w_bb, w_out, gate2, shift3, scale4, g2, rwt, rb, tm):
    b, t, d = x.shape
    ne = rwt.shape[0]
    nt = t // tm
    tok = b * t
    cap = _sorted_rows(tm, ne)
    kern = functools.partial(_merge_kernel, tm=tm, d=d, ne=ne, cap=cap)
    row_spec = pl.BlockSpec((None, tm, d), lambda bi, i: (bi, i, 0))
    mod_spec = pl.BlockSpec((None, 1, d), lambda bi, i: (bi, 0, 0))
    flat = lambda bi, i: (bi * nt + i, 0)
    lane = lambda bi, i: (0, bi * nt + i)

    def whole(shape):
        return pl.BlockSpec(shape, lambda bi, i: (0,) * len(shape))

    return pl.pallas_call(
        kern,
        grid=(b, nt),
        in_specs=[
            row_spec, row_spec, row_spec, row_spec,
            whole((d, 2 * d)), whole((d, d)), whole((d, d)), whole((d, d)),
            mod_spec, mod_spec, mod_spec,
            whole((1, d)), whole((ne, d)), whole((ne, 1)),
        ],
        out_specs=[
            row_spec,
            pl.BlockSpec((cap, d + LANES), flat),
            pl.BlockSpec((SUBLANES, tm), lane),
            pl.BlockSpec((None, ne, LANES), lambda bi, i: (bi * nt + i, 0, 0)),
        ],
        out_shape=[
            jax.ShapeDtypeStruct((b, t, d), F32),
            jax.ShapeDtypeStruct((b * nt * cap, d + LANES), BF16),
            jax.ShapeDtypeStruct((SUBLANES, tok), I32),
            jax.ShapeDtypeStruct((b * nt, ne, LANES), F32),
        ],
        scratch_shapes=[pltpu.VMEM((tm, d), BF16)],
        compiler_params=_params(("arbitrary", "arbitrary"), 56),
        name="merge",
    )(x, hl, ya, yb, w_m, w_ba, w_bb, w_out, gate2, shift3, scale4, g2.reshape(1, d), rwt, rb)


def _experts_kernel(bexp_ref, bact_ref, src_ref, src1_ref, src2_ref, hs_hbm, w1_ref, b1_ref, w2_ref, b2_ref,
                    y_ref, xbuf, act_s, w1p_s, w2_s, gsem, *, blk, nblocks, d):
    j = pl.program_id(0)
    slot = lax.rem(j, MOE_RING)
    far = lax.rem(j + 2, MOE_RING)
    f = w2_s.shape[1]
    pw = 2 * LANES
    n1 = 2 * f // pw
    n2 = d // pw
    nch = blk // MOE_CHUNK
    active = bact_ref[j] == 1

    def request(idx_ref, buf_slot):
        for c in range(nch):
            src = pl.multiple_of(idx_ref[0, 0, c] * MOE_CHUNK, MOE_CHUNK)
            pltpu.make_async_copy(hs_hbm.at[pl.ds(src, MOE_CHUNK), :],
                                  xbuf.at[buf_slot, pl.ds(c * MOE_CHUNK, MOE_CHUNK), :], gsem.at[buf_slot]).start()

    @pl.when(j == 0)
    def _():
        request(src_ref, 0)
        request(src1_ref, 1)

    @pl.when(j + 2 < nblocks)
    def _():
        request(src2_ref, far)

    pltpu.make_async_copy(hs_hbm.at[pl.ds(0, blk), :], xbuf.at[slot], gsem.at[slot]).wait()

    new_expert = (j == 0) | (bexp_ref[j] != bexp_ref[jnp.maximum(j - 1, 0)])

    @pl.when(new_expert & active)
    def _():
        src = lax.broadcasted_iota(I32, (pw, pw), 0)
        dst = lax.broadcasted_iota(I32, (pw, pw), 1)
        want = jnp.where(dst < pw // 2, 2 * dst, 2 * (dst - pw // 2) + 1)
        sel = (src == want).astype(BF16)
        for cb in range(n1):
            w1p_s[cb] = _dot(w1_ref[:, cb * pw:(cb + 1) * pw].astype(BF16), sel).astype(BF16)
        for cb in range(n2):
            w2_s[cb] = w2_ref[:, cb * pw:(cb + 1) * pw].astype(BF16)

    @pl.when(active)
    def _():
        x = xbuf[slot, :, 0:d]
        gate = (xbuf[slot, :, d:d + 1].astype(F32) + xbuf[slot, :, d + LANES // 2:d + LANES // 2 + 1].astype(F32))
        for n in range(n1):
            h = _dot(x, w1p_s[n]) + b1_ref[n]
            hg = jnp.minimum(h[:, :LANES], SWIGLU_LIMIT)
            hu = jnp.clip(h[:, LANES:], -SWIGLU_LIMIT, SWIGLU_LIMIT)
            act_s[n] = (hg * _sigmoid(SWIGLU_ALPHA * hg) * (hu + 1.0)).astype(BF16)
        act = jnp.concatenate([act_s[i] for i in range(n1)], axis=1)
        for n in range(n2):
            y = _dot(act, w2_s[n]) + b2_ref[n]
            y_ref[:, n * pw:(n + 1) * pw] = (y * gate).astype(y_ref.dtype)

    @pl.when(jnp.logical_not(active))
    def _():
        y_ref[...] = jnp.zeros_like(y_ref)


def _experts(block_expert, block_active, src3, hs, w1, b1p, w2, b2, blk):
    nblocks = block_expert.shape[0]
    f, d = w2.shape[1], w2.shape[2]
    pw = 2 * LANES
    n1, n2 = 2 * f // pw, d // pw
    nch = blk // MOE_CHUNK
    b1p = b1p.reshape(-1, n1, 1, pw)
    b2 = b2.reshape(-1, n2, 1, pw)
    kern = functools.partial(_experts_kernel, blk=blk, nblocks=nblocks, d=d)
    wmap = lambda j, be, ba: (be[j], 0, 0)

    def idx_spec(ahead):
        return pl.BlockSpec((1, 1, nch), lambda j, be, ba: (jnp.minimum(j + ahead, nblocks - 1), 0, 0),
                            memory_space=pltpu.SMEM)

    grid_spec = pltpu.PrefetchScalarGridSpec(
        num_scalar_prefetch=2,
        grid=(nblocks,),
        in_specs=[
            idx_spec(0), idx_spec(1), idx_spec(2),
            pl.BlockSpec(memory_space=pl.ANY),
            pl.BlockSpec((None, d, 2 * f), wmap),
            pl.BlockSpec((None, n1, 1, pw), lambda j, be, ba: (be[j], 0, 0, 0)),
            pl.BlockSpec((None, f, d), wmap),
            pl.BlockSpec((None, n2, 1, pw), lambda j, be, ba: (be[j], 0, 0, 0)),
        ],
        out_specs=pl.BlockSpec((blk, d), lambda j, be, ba: (j, 0)),
        scratch_shapes=[
            pltpu.VMEM((MOE_RING, blk, d + LANES), BF16),
            pltpu.VMEM((n1, blk, LANES), BF16),
            pltpu.VMEM((n1, d, pw), BF16),
            pltpu.VMEM((n2, f, pw), BF16),
            pltpu.SemaphoreType.DMA((MOE_RING,)),
        ],
    )
    return pl.pallas_call(
        kern,
        grid_spec=grid_spec,
        out_shape=jax.ShapeDtypeStruct((nblocks * blk, d), BF16),
        compiler_params=_params(("arbitrary",), 56),
        name="experts",
    )(block_expert, block_active, src3, src3, src3, hs, w1, b1p, w2, b2)


def _combine_kernel(dst_ref, dstn_ref, y_hbm, x1_ref, pos_ref, gate_ref, fg_ref, o_ref, ybuf, sem,
                    *, tm, cap, ntiles):
    i = pl.program_id(0)
    slot = lax.rem(i, 2)
    nch = cap // MOE_CHUNK

    def request(idx_ref, buf_slot):
        def body(c, carry):
            src = pl.multiple_of(idx_ref[0, 0, c] * MOE_CHUNK, MOE_CHUNK)
            dst = pl.multiple_of(c * MOE_CHUNK, MOE_CHUNK)
            pltpu.make_async_copy(y_hbm.at[pl.ds(src, MOE_CHUNK), :],
                                  ybuf.at[buf_slot, pl.ds(dst, MOE_CHUNK), :], sem.at[buf_slot]).start()
            return carry
        lax.fori_loop(0, nch, body, 0, unroll=8)

    @pl.when(i == 0)
    def _():
        request(dst_ref, 0)

    @pl.when(i + 1 < ntiles)
    def _():
        request(dstn_ref, 1 - slot)

    pltpu.make_async_copy(y_hbm.at[pl.ds(0, cap), :], ybuf.at[slot], sem.at[slot]).wait()

    pos = pos_ref[...]
    acc = None
    for c0 in range(0, cap, SORT_ROWS):
        col = lax.broadcasted_iota(I32, (tm, SORT_ROWS), 1) + c0
        pick = jnp.zeros((tm, SORT_ROWS), F32)
        for k in range(TOP_K):
            pick = jnp.where(col == pos[:, k:k + 1], 1.0, pick)
        part = _dot(pick.astype(BF16), ybuf[slot, c0:c0 + SORT_ROWS, :])
        acc = part if acc is None else acc + part
    x2 = x1_ref[...] + gate_ref[...] * acc
    o_ref[...] = _rms(x2) * fg_ref[...]


def _combine(dst3, y, x1, pos_t, gate5, final_g, tm, cap):
    b, t, d = x1.shape
    nt = t // tm
    ntiles = b * nt
    nch = cap // MOE_CHUNK
    kern = functools.partial(_combine_kernel, tm=tm, cap=cap, ntiles=ntiles)
    row_map = lambda i: (i // nt, i % nt, 0)
    return pl.pallas_call(
        kern,
        grid=(ntiles,),
        in_specs=[
            pl.BlockSpec((1, 1, nch), lambda i: (i, 0, 0), memory_space=pltpu.SMEM),
            pl.BlockSpec((1, 1, nch), lambda i: (jnp.minimum(i + 1, ntiles - 1), 0, 0), memory_space=pltpu.SMEM),
            pl.BlockSpec(memory_space=pl.ANY),
            pl.BlockSpec((None, tm, d), row_map),
            pl.BlockSpec((tm, TOP_K), lambda i: (i, 0)),
            pl.BlockSpec((None, 1, d), lambda i: (i // nt, 0, 0)),
            pl.BlockSpec((1, d), lambda i: (0, 0)),
        ],
        out_specs=pl.BlockSpec((None, tm, d), row_map),
        out_shape=jax.ShapeDtypeStruct((b, t, d), F32),
        scratch_shapes=[
            pltpu.VMEM((2, cap, d), BF16),
            pltpu.SemaphoreType.DMA((2,)),
        ],
        compiler_params=_params(("arbitrary",), 48),
        name="combine",
    )(dst3, dst3, y, x1, pos_t, gate5, final_g.reshape(1, d))


def _sorted_rows(tm, ne):
    need = tm * TOP_K + ne * (MOE_CHUNK - 1) + MOE_CHUNK
    return -(-need // SORT_ROWS) * SORT_ROWS


def _pick_tile(n, pref):
    tm = pref
    while n % tm:
        tm //= 2
    return tm


def kernel(x, c, ctx, c_ctx, ada_w, ada_b, norm1_g, norm2_g, w_in, lru_conv_w, lru_conv_b, lru_wa, lru_ba, lru_wx, lru_bx, lru_lam, hg_lb_logits, hg_norm_g, w_branch_a, w_branch_b, w_out, router_w, router_b, moe_w1, moe_b1, moe_w2, moe_b2, final_g):
    b, t, d = x.shape
    cx = ctx.shape[1]
    layer = 0
    w_lru = lru_conv_w.shape[2]
    qk = (w_in.shape[2] - 2 * w_lru - 2 * d) // 5
    heads = qk // HG_DK
    ne = router_w.shape[2]

    pad = (-(b + 1)) % SUBLANES
    c_all = jnp.concatenate([c, c_ctx[None, :], jnp.zeros((pad, d), F32)], axis=0)
    mod = _modulation(c_all, ada_w[layer], ada_b[layer])
    mod_l = mod[:b].reshape(b, N_MOD, 1, d)
    mod_c = mod[b].reshape(N_MOD, 1, 1, d)

    w_in_b = w_in[layer].astype(BF16)
    w_lru_in = w_in_b[:, :2 * w_lru]
    hg0 = 2 * w_lru
    w_hg = w_in_b[:, hg0:hg0 + 5 * qk].reshape(d, 5, heads, HG_DK).transpose(2, 0, 1, 3).reshape(heads, d, 5 * HG_DK)
    w_m = w_in_b[:, hg0 + 5 * qk:]
    g_blocks = w_lru // LRU_BLOCK
    wg = jnp.concatenate([lru_wa[layer, 0], lru_wx[layer, 0], lru_wa[layer, 1], lru_wx[layer, 1]], axis=-1).astype(BF16)
    bg = jnp.concatenate([lru_ba[layer, 0], lru_bx[layer, 0], lru_ba[layer, 1], lru_bx[layer, 1]], axis=-1)
    bg = bg.reshape(g_blocks, 1, 4 * LRU_BLOCK)
    lb_all = jnp.cumsum(jax.nn.softmax(hg_lb_logits.astype(F32), axis=0), axis=0)
    lb = lb_all[layer].reshape(heads, 1, HG_DK)

    tm_n = _pick_tile(t, 512)
    hl = _prenorm(x, mod_l[:, 0], mod_l[:, 1], norm1_g[layer], tm_n)
    hc = _prenorm(ctx, mod_c[0], mod_c[1], norm1_g[layer], _pick_tile(cx, 256))
    axl = _matmul(hl.reshape(b * t, d), w_lru_in, _pick_tile(b * t, 512), BF16).reshape(b, t, 2 * w_lru)
    axc = _matmul(hc.reshape(b * cx, d), w_lru_in, _pick_tile(b * cx, 512), BF16).reshape(b, cx, 2 * w_lru)
    ya = _lru(axl, axc, lru_conv_w[layer], lru_conv_b[layer], wg, bg, lru_lam[layer], b, cx, t)
    yb = _hgrn(hc, hl, w_hg, lb, hg_norm_g[layer])

    rwt = router_w[layer].T
    rb = router_b[layer].reshape(ne, 1)
    tm = _pick_tile(t, 512)
    x1, hs, pos8, cnt = _merge(
        x, hl, ya, yb, w_m, w_branch_a[layer].astype(BF16), w_branch_b[layer].astype(BF16),
        w_out[layer].astype(BF16), mod_l[:, 2], mod_l[:, 3], mod_l[:, 4], norm2_g[layer], rwt, rb, tm)

    blk = MOE_BLOCK
    ch = MOE_CHUNK
    bpc = blk // ch
    tok = b * t
    ntile = tok // tm
    cap = _sorted_rows(tm, ne)
    tch = cap // ch
    run = (cnt[:, :, 0].astype(I32) + ch - 1) // ch
    loc_end = jnp.cumsum(run, axis=1)
    loc_start = loc_end - run
    upto = jnp.cumsum(run, axis=0)
    total = upto[-1]
    padded = (total + bpc - 1) // bpc * bpc
    gend = jnp.cumsum(padded)
    gstart = gend - padded
    where_run = gstart[None, :] + upto - run
    max_chunks = tok * TOP_K // ch + ntile * ne + ne * bpc
    nblocks = -(-max_chunks // bpc)

    eids = jnp.arange(ne, dtype=I32)
    tids = jnp.arange(ntile, dtype=I32)
    g = jnp.arange(nblocks * bpc, dtype=I32)
    g_e = jnp.minimum(jnp.sum(g[:, None] >= gend[None, :], axis=1), ne - 1).astype(I32)
    is_e = g_e[:, None] == eids[None, :]
    off = g - jnp.sum(jnp.where(is_e, gstart[None, :], 0), axis=1)
    g_total = jnp.sum(jnp.where(is_e, total[None, :], 0), axis=1)
    upto_g = jnp.sum(jnp.where(is_e[:, None, :], upto[None, :, :], 0), axis=2)
    g_i = jnp.minimum(jnp.sum(off[:, None] >= upto_g, axis=1), ntile - 1).astype(I32)
    both = (g_i[:, None] == tids[None, :])[:, :, None] & is_e[:, None, :]

    def at_run(table):
        return jnp.sum(jnp.where(both, table[None, :, :], 0), axis=(1, 2))

    within = off - (at_run(upto) - at_run(run))
    zero_chunk = tch - 1
    src = jnp.where((off < g_total) & (g < gend[-1]), g_i * tch + at_run(loc_start) + within, zero_chunk)

    block_first = jnp.arange(nblocks, dtype=I32) * bpc
    block_active = block_first < gend[-1]
    block_expert = jnp.minimum(jnp.sum(block_first[:, None] >= gend[None, :], axis=1), ne - 1).astype(I32)
    last_expert = jnp.max(jnp.where(block_active, block_expert, 0))
    block_expert = jnp.where(block_active, block_expert, last_expert).astype(I32)

    c = jnp.arange(tch, dtype=I32)
    c_e = jnp.minimum(jnp.sum(c[None, :, None] >= loc_end[:, None, :], axis=2), ne - 1).astype(I32)
    is_ce = c_e[:, :, None] == eids[None, None, :]
    back = jnp.sum(jnp.where(is_ce, (where_run - loc_start)[:, None, :], 0), axis=2) + c[None, :]
    back = jnp.where(c[None, :] < loc_end[:, -1:], back, 0)

    b1 = moe_b1[layer]
    f = b1.shape[1] // 2
    b1p = b1.reshape(ne, f // LANES, LANES, 2).transpose(0, 1, 3, 2).reshape(ne, 1, 2 * f)
    y = _experts(block_expert, block_active.astype(I32), src.reshape(nblocks, 1, bpc), hs, moe_w1[layer], b1p,
                 moe_w2[layer], moe_b2[layer].reshape(ne, 1, d), blk)
    return _combine(back.reshape(ntile, 1, tch), y, x1, pos8[:TOP_K].T, mod_l[:, 5], final_g, tm, cap)
```

```python
import functools

import jax
import jax.numpy as jnp
from jax import lax
from jax.experimental import pallas as pl
from jax.experimental.pallas import tpu as pltpu

F32 = jnp.float32
BF16 = jnp.bfloat16
I32 = jnp.int32
HIGHEST = lax.Precision.HIGHEST

EPS = 1e-6
N_MOD = 6
GRID_W = 64
CONV_LEFT = 2
CONV_WIDTH = 4
LRU_C = 8.0
LRU_BLOCK = 128
HG_DK = 128
HG_CHUNK = 64
TOP_K = 4
SWIGLU_LIMIT = 7.0
SWIGLU_ALPHA = 1.702
MOE_BLOCK = 512
MOE_CHUNK = 16
SORT_ROWS = 256
MOE_RING = 3
SUBLANES = 8
LANES = 128
HALO = 8


def _params(sem, vmem_mb):
    return pltpu.CompilerParams(dimension_semantics=sem, vmem_limit_bytes=vmem_mb * 1024 * 1024)


def _dot(a, b):
    return jnp.dot(a, b, preferred_element_type=F32)


def _dot_nt(a, b):
    return lax.dot_general(a, b, (((1,), (1,)), ((), ())), preferred_element_type=F32)


def _dot_tn(a, b):
    return lax.dot_general(a, b, (((0,), (0,)), ((), ())), preferred_element_type=F32)


def _sigmoid(x):
    return jax.nn.sigmoid(x)


def _silu(x):
    return x * jax.nn.sigmoid(x)


def _rms(x):
    return x * lax.rsqrt(jnp.mean(x * x, axis=-1, keepdims=True) + EPS)


def _mod_kernel(c_ref, w_ref, b_ref, o_ref):
    s = _silu(c_ref[...])
    o_ref[...] = jnp.dot(s, w_ref[...], preferred_element_type=F32, precision=HIGHEST) + b_ref[...]


def _modulation(c_all, ada_w, ada_b):
    m, d = c_all.shape
    n = ada_w.shape[1]
    tn = 1024
    return pl.pallas_call(
        _mod_kernel,
        grid=(n // tn,),
        in_specs=[
            pl.BlockSpec((m, d), lambda j: (0, 0)),
            pl.BlockSpec((d, tn), lambda j: (0, j)),
            pl.BlockSpec((1, tn), lambda j: (0, j)),
        ],
        out_specs=pl.BlockSpec((m, tn), lambda j: (0, j)),
        out_shape=jax.ShapeDtypeStruct((m, n), F32),
        compiler_params=_params(("arbitrary",), 32),
        name="mod",
    )(c_all, ada_w, ada_b.reshape(1, n))


def _prenorm_kernel(x_ref, shift_ref, scale_ref, g_ref, o_ref):
    y = _rms(x_ref[...]) * g_ref[...]
    o_ref[...] = (y * (1.0 + scale_ref[...]) + shift_ref[...]).astype(o_ref.dtype)


def _prenorm(x, shift, scale, g, tm):
    b, l, d = x.shape
    per_batch = shift.shape[0] == b and b > 1
    mod_map = (lambda bi, i: (bi, 0, 0)) if per_batch else (lambda bi, i: (0, 0, 0))
    return pl.pallas_call(
        _prenorm_kernel,
        grid=(b, l // tm),
        in_specs=[
            pl.BlockSpec((None, tm, d), lambda bi, i: (bi, i, 0)),
            pl.BlockSpec((None, 1, d), mod_map),
            pl.BlockSpec((None, 1, d), mod_map),
            pl.BlockSpec((1, d), lambda bi, i: (0, 0)),
        ],
        out_specs=pl.BlockSpec((None, tm, d), lambda bi, i: (bi, i, 0)),
        out_shape=jax.ShapeDtypeStruct((b, l, d), BF16),
        compiler_params=_params(("arbitrary", "arbitrary"), 32),
        name="prenorm",
    )(x, shift, scale, g.reshape(1, d))


def _mm_kernel(x_ref, w_ref, o_ref):
    o_ref[...] = _dot(x_ref[...], w_ref[...]).astype(o_ref.dtype)


def _matmul(x2d, w, tm, out_dtype):
    m, k = x2d.shape
    n = w.shape[1]
    return pl.pallas_call(
        _mm_kernel,
        grid=(m // tm,),
        in_specs=[
            pl.BlockSpec((tm, k), lambda i: (i, 0)),
            pl.BlockSpec((k, n), lambda i: (0, 0)),
        ],
        out_specs=pl.BlockSpec((tm, n), lambda i: (i, 0)),
        out_shape=jax.ShapeDtypeStruct((m, n), out_dtype),
        compiler_params=_params(("arbitrary",), 40),
        name="inproj_lru",
    )(x2d, w)


def _lru_kernel(axl_ref, agl_ref, axc_ref, cw_ref, cb_ref, wg_ref, bg_ref, lam_ref, o_ref,
                xc_ref, yacc_ref, sx_ref, sa_ref, sb_ref, sh_ref, *, nb, cx, t):
    tb = GRID_W
    ctx0 = HALO
    gw = 2 * LRU_BLOCK
    pre = CONV_LEFT
    steps = pre + tb + (HALO - pre)

    zeros_h = jnp.zeros((HALO, LRU_BLOCK), F32)
    for b in range(nb):
        xc_ref[b, 0:HALO, :] = zeros_h
        xc_ref[b, ctx0:ctx0 + cx, :] = axc_ref[b].astype(F32)
        xc_ref[b, ctx0 + cx:ctx0 + cx + HALO, :] = zeros_h

    cw = cw_ref[...]
    cb = cb_ref[...]
    lam = lam_ref[...]
    sp = jnp.maximum(-lam, 0.0) + jnp.log1p(jnp.exp(-jnp.abs(lam)))

    def zero_history():
        sx_ref[0:pre * nb, :] = jnp.zeros((pre * nb, LRU_BLOCK), F32)
        sx_ref[(pre + tb) * nb:steps * nb, :] = jnp.zeros(((steps - pre - tb) * nb, LRU_BLOCK), F32)

    def load_ctx_block(base):
        for b in range(nb):
            sx_ref[pl.ds(b, steps, stride=nb), :] = xc_ref[b, pl.ds(base - pre, steps), :]

    def load_latent_block(r0):
        for b in range(nb):
            sx_ref[pl.ds(pre * nb + b, tb, stride=nb), :] = axl_ref[b, pl.ds(r0, tb), :].astype(F32)

    def fill_gates(d):
        u = jnp.broadcast_to(cb, (tb * nb, LRU_BLOCK))
        for k in range(CONV_WIDTH):
            u = u + sx_ref[k * nb:(k + tb) * nb, :] * cw[k:k + 1, :]
        z = _dot(u.astype(BF16), wg_ref[:, d * gw:(d + 1) * gw]) + bg_ref[:, d * gw:(d + 1) * gw]
        r = _sigmoid(z[:, :LRU_BLOCK])
        i = _sigmoid(z[:, LRU_BLOCK:])
        log_a = (-LRU_C) * r * sp[d:d + 1, :]
        a = jnp.exp(log_a)
        mult = jnp.sqrt(-jnp.tanh(log_a) * (a * a + 1.0))
        sa_ref[...] = a
        sb_ref[...] = mult * (i * u)

    def scan_block(h, reverse):
        order = range(tb - 1, -1, -1) if reverse else range(tb)
        for s in order:
            h = sa_ref[s * nb:(s + 1) * nb, :] * h + sb_ref[s * nb:(s + 1) * nb, :]
            sh_ref[s * nb:(s + 1) * nb, :] = h
        return h

    def sweep(d):
        reverse = d == 1
        n_c, n_l = cx // tb, t // tb

        def ctx_body(j, h):
            jj = (n_c - 1 - j) if reverse else j
            load_ctx_block(pl.multiple_of(ctx0 + jj * tb, SUBLANES))
            fill_gates(d)
            return scan_block(h, reverse)

        def lat_body(j, h):
            jj = (n_l - 1 - j) if reverse else j
            r0 = pl.multiple_of(jj * tb, tb)
            load_latent_block(r0)
            fill_gates(d)
            h = scan_block(h, reverse)
            for b in range(nb):
                yb = sh_ref[pl.ds(b, tb, stride=nb), :]
                if not reverse:
                    yacc_ref[b, pl.ds(r0, tb), :] = yb
                else:
                    gate = jax.nn.gelu(agl_ref[b, pl.ds(r0, tb), :].astype(F32))
                    o_ref[b, pl.ds(r0, tb), :] = ((yacc_ref[b, pl.ds(r0, tb), :] + yb) * gate).astype(o_ref.dtype)
            return h

        h0 = jnp.zeros((nb, LRU_BLOCK), F32)
        h1 = lax.fori_loop(0, n_c, ctx_body, h0)
        zero_history()
        lax.fori_loop(0, n_l, lat_body, h1)

    sweep(0)
    sweep(1)


def _lru(axl, axc, conv_w, conv_b, wg, bg, lam, nb, cx, t):
    w = conv_w.shape[1]
    g = w // LRU_BLOCK
    kern = functools.partial(_lru_kernel, nb=nb, cx=cx, t=t)
    return pl.pallas_call(
        kern,
        grid=(g,),
        in_specs=[
            pl.BlockSpec((nb, t, LRU_BLOCK), lambda j: (0, 0, j)),
            pl.BlockSpec((nb, t, LRU_BLOCK), lambda j: (0, 0, g + j)),
            pl.BlockSpec((nb, cx, LRU_BLOCK), lambda j: (0, 0, j)),
            pl.BlockSpec((CONV_WIDTH, LRU_BLOCK), lambda j: (0, j)),
            pl.BlockSpec((1, LRU_BLOCK), lambda j: (0, j)),
            pl.BlockSpec((None, LRU_BLOCK, 4 * LRU_BLOCK), lambda j: (j, 0, 0)),
            pl.BlockSpec((None, 1, 4 * LRU_BLOCK), lambda j: (j, 0, 0)),
            pl.BlockSpec((2, LRU_BLOCK), lambda j: (0, j)),
        ],
        out_specs=pl.BlockSpec((nb, t, LRU_BLOCK), lambda j: (0, 0, j)),
        out_shape=jax.ShapeDtypeStruct((nb, t, w), BF16),
        scratch_shapes=[
            pltpu.VMEM((nb, cx + 2 * HALO, LRU_BLOCK), F32),
            pltpu.VMEM((nb, t, LRU_BLOCK), F32),
            pltpu.VMEM(((GRID_W + HALO) * nb, LRU_BLOCK), F32),
            pltpu.VMEM((GRID_W * nb, LRU_BLOCK), F32),
            pltpu.VMEM((GRID_W * nb, LRU_BLOCK), F32),
            pltpu.VMEM((GRID_W * nb, LRU_BLOCK), F32),
        ],
        compiler_params=_params(("arbitrary",), 56),
        name="lru",
    )(axl, axl, axc, conv_w, conv_b.reshape(1, w), wg, bg, lam)


def _hgrn_kernel(hc_ref, hl_ref, w_ref, lb_ref, ng_ref, o_ref,
                 p_ref, oacc_ref, q_s, v_s, k_s, hl_s, gc_s, bk_s, sc_s, aq_s, upd_ref, dec_ref,
                 *, cx, t):
    c = HG_CHUNK
    dk = HG_DK
    n_c, n_l = cx // c, t // c
    n_all = n_c + n_l

    p_ref[0:cx, :] = _dot(hc_ref[...], w_ref[...])
    p_ref[cx:cx + t, :] = _dot(hl_ref[...], w_ref[...])
    rows = 512 if t % 512 == 0 else c

    lb = lb_ref[...]
    one_m_lb = 1.0 - lb
    ri = lax.broadcasted_iota(I32, (c, c), 0)
    ci = lax.broadcasted_iota(I32, (c, c), 1)
    keep = (ri >= ci, ci >= ri)
    mid = (c // 2 - 1, c // 2)
    end = (c - 1, 0)

    def stage_gates(j, carry):
        r0 = pl.multiple_of(j * c, c)
        q_s[pl.ds(r0, c), :] = _silu(p_ref[pl.ds(r0, c), 0:dk])
        v_s[pl.ds(r0, c), :] = p_ref[pl.ds(r0, c), 3 * dk:4 * dk].astype(BF16)
        for d in range(2):
            f = lb + one_m_lb * _sigmoid(p_ref[pl.ds(r0, c), (1 + d) * dk:(2 + d) * dk])
            logf = jnp.log(f)
            hi = logf.astype(BF16)
            lo = (logf - hi.astype(F32)).astype(BF16)
            k_s[d, pl.ds(r0, c), :] = 1.0 - f
            hl_s[d, pl.ds(r0, c), :] = jnp.concatenate([hi, lo], axis=1)
        return carry

    lax.fori_loop(0, n_all, stage_gates, 0, unroll=4)

    def stage_cumsum(j, carry):
        r0 = pl.multiple_of(j * c, c)
        for d in range(2):
            s2 = _dot(keep[d].astype(BF16), hl_s[d, pl.ds(r0, c), :])
            gc_s[d, pl.ds(r0, c), :] = s2[:, :dk] + s2[:, dk:]
        return carry

    lax.fori_loop(0, n_all, stage_cumsum, 0, unroll=12)

    def stage_scores(j, carry):
        r0 = pl.multiple_of(j * c, c)
        q = q_s[pl.ds(r0, c), :]
        for d in range(2):
            gc = gc_s[d, pl.ds(r0, c), :]
            k = k_s[d, pl.ds(r0, c), :]
            g_mid = gc[mid[d]:mid[d] + 1, :]
            g_end = gc[end[d]:end[d] + 1, :]
            qa = q * jnp.exp(gc - g_mid)
            kb = k * jnp.exp(g_mid - gc)
            sc_s[d, pl.ds(r0, c), :] = jnp.where(keep[d], _dot_nt(qa.astype(BF16), kb.astype(BF16)), 0.0).astype(BF16)
            bk_s[d, pl.ds(r0, c), :] = (kb * jnp.exp(g_end - g_mid)).astype(BF16)
            aq_s[d, pl.ds(r0, c), :] = (qa * jnp.exp(g_mid)).astype(BF16)
            dec_ref[d, j] = jnp.broadcast_to(jnp.exp(g_end), (SUBLANES, dk))
        return carry

    lax.fori_loop(0, n_all, stage_scores, 0, unroll=6)

    def stage_intra(j, carry):
        r0 = pl.multiple_of(j * c, c)
        v = v_s[pl.ds(r0, c), :]
        oacc_ref[pl.ds(r0, c), :] = _dot(sc_s[0, pl.ds(r0, c), :], v) + _dot(sc_s[1, pl.ds(r0, c), :], v)
        for d in range(2):
            upd_ref[d, j] = _dot_tn(v, bk_s[d, pl.ds(r0, c), :])
        return carry

    lax.fori_loop(0, n_all, stage_intra, 0, unroll=12)

    def inter(d, jj, st):
        r0 = pl.multiple_of(jj * c, c)
        oacc_ref[pl.ds(r0, c), :] += _dot_nt(aq_s[d, pl.ds(r0, c), :], st.astype(BF16))
        return st * dec_ref[d, jj][0:1, :] + upd_ref[d, jj]

    def ctx_body(j, sts):
        return inter(0, j, sts[0]), inter(1, n_c - 1 - j, sts[1])

    def lat_body(j, sts):
        return inter(0, n_c + j, sts[0]), inter(1, n_c + n_l - 1 - j, sts[1])

    zero = jnp.zeros((dk, dk), F32)
    sts = lax.fori_loop(0, n_c, ctx_body, (zero, zero), unroll=2)
    lax.fori_loop(0, n_l, lat_body, sts, unroll=8)

    ng = ng_ref[...]

    def readout(i, carry):
        r = pl.multiple_of(i * rows, rows)
        o = oacc_ref[pl.ds(cx + r, rows), :]
        g = p_ref[pl.ds(cx + r, rows), 4 * dk:5 * dk]
        o_ref[pl.ds(r, rows), :] = (_rms(o) * ng * _silu(g)).astype(o_ref.dtype)
        return carry

    lax.fori_loop(0, t // rows, readout, 0)


def _hgrn(hc, hl, w_hg, lb, norm_g):
    b, cx, d = hc.shape
    t = hl.shape[1]
    heads = w_hg.shape[0]
    dk = HG_DK
    n_all = (cx + t) // HG_CHUNK
    kern = functools.partial(_hgrn_kernel, cx=cx, t=t)
    return pl.pallas_call(
        kern,
        grid=(b, heads),
        in_specs=[
            pl.BlockSpec((None, cx, d), lambda bi, h: (bi, 0, 0)),
            pl.BlockSpec((None, t, d), lambda bi, h: (bi, 0, 0)),
            pl.BlockSpec((None, d, 5 * dk), lambda bi, h: (h, 0, 0)),
            pl.BlockSpec((None, 1, dk), lambda bi, h: (h, 0, 0)),
            pl.BlockSpec((1, dk), lambda bi, h: (0, 0)),
        ],
        out_specs=pl.BlockSpec((None, t, dk), lambda bi, h: (bi, 0, h)),
        out_shape=jax.ShapeDtypeStruct((b, t, heads * dk), BF16),
        scratch_shapes=[
            pltpu.VMEM((cx + t, 5 * dk), F32),
            pltpu.VMEM((cx + t, dk), F32),
            pltpu.VMEM((cx + t, dk), F32),
            pltpu.VMEM((cx + t, dk), BF16),
            pltpu.VMEM((2, cx + t, dk), F32),
            pltpu.VMEM((2, cx + t, 2 * dk), BF16),
            pltpu.VMEM((2, cx + t, dk), F32),
            pltpu.VMEM((2, cx + t, dk), BF16),
            pltpu.VMEM((2, cx + t, HG_CHUNK), BF16),
            pltpu.VMEM((2, cx + t, dk), BF16),
            pltpu.VMEM((2, n_all, dk, dk), F32),
            pltpu.VMEM((2, n_all, SUBLANES, dk), F32),
        ],
        compiler_params=_params(("arbitrary", "arbitrary"), 48),
        name="hgrn",
    )(hc, hl, w_hg, lb, norm_g.reshape(1, dk))


def _merge_kernel(x_ref, h_ref, ya_ref, yb_ref, wm_ref, wa_ref, wb_ref, wo_ref,
                  gate_ref, shift_ref, scale_ref, g2_ref, rwt_ref, rb_ref,
                  x1_ref, hs_ref, pos_ref, cnt_ref, h2b_s, *, tm, d, ne, cap):
    m = _dot(h_ref[...], wm_ref[...])
    za = _dot(ya_ref[...], wa_ref[...])
    zb = _dot(yb_ref[...], wb_ref[...])
    mix = _sigmoid(m[:, :d]) * za + _sigmoid(m[:, d:]) * zb
    out = _dot(mix.astype(BF16), wo_ref[...])
    x1 = x_ref[...] + gate_ref[...] * out
    x1_ref[...] = x1
    h2 = _rms(x1) * g2_ref[...] * (1.0 + scale_ref[...]) + shift_ref[...]
    h2b_s[...] = h2.astype(BF16)

    logits = lax.dot_general(rwt_ref[...], h2, (((1,), (1,)), ((), ())),
                             preferred_element_type=F32, precision=HIGHEST) + rb_ref[...]
    eid = lax.broadcasted_iota(I32, (ne, tm), 0)
    neg = jnp.float32(-jnp.inf)
    work = logits
    hot = jnp.zeros((ne, tm), F32)
    vals, idxs, sels = [], [], []
    for _ in range(TOP_K):
        mx = jnp.max(work, axis=0, keepdims=True)
        ix = jnp.min(jnp.where(work == mx, eid, ne), axis=0, keepdims=True)
        sel = eid == ix
        work = jnp.where(sel, neg, work)
        hot = hot + sel.astype(F32)
        vals.append(mx)
        idxs.append(ix)
        sels.append(sel)
    ex = [jnp.exp(v - vals[0]) for v in vals]
    den = ex[0] + ex[1] + ex[2] + ex[3]
    gts = [e / den for e in ex]

    ti = lax.broadcasted_iota(I32, (tm, tm), 0)
    tj = lax.broadcasted_iota(I32, (tm, tm), 1)
    strict = (ti < tj).astype(BF16)
    before = _dot(hot.astype(BF16), strict)
    cnt = jnp.sum(hot, axis=1, keepdims=True)
    nchunks = jnp.ceil(cnt * (1.0 / MOE_CHUNK))
    ei = lax.broadcasted_iota(I32, (ne, ne), 0)
    ej = lax.broadcasted_iota(I32, (ne, ne), 1)
    earlier = (ej < ei).astype(BF16)
    start = _dot(earlier, jnp.broadcast_to(nchunks, (ne, LANES)).astype(BF16))[:, 0:1] * MOE_CHUNK
    where_to = before + start
    poss = [jnp.sum(jnp.where(s, where_to, 0.0), axis=0, keepdims=True).astype(I32) for s in sels]

    rb_rows = SORT_ROWS
    lane = lax.broadcasted_iota(I32, (rb_rows, LANES), 1)

    def sort_rows(i, carry):
        r0 = pl.multiple_of(i * rb_rows, rb_rows)
        pid = lax.broadcasted_iota(I32, (rb_rows, tm), 0) + r0
        place = jnp.zeros((rb_rows, tm), F32)
        gsel = jnp.zeros((rb_rows, tm), F32)
        for k in range(TOP_K):
            hit = pid == poss[k]
            place = jnp.where(hit, 1.0, place)
            gsel = jnp.where(hit, gts[k], gsel)
        hs_ref[pl.ds(r0, rb_rows), 0:d] = _dot(place.astype(BF16), h2b_s[...]).astype(BF16)
        g = jnp.sum(gsel, axis=1, keepdims=True)
        g_hi = g.astype(BF16).astype(F32)
        hs_ref[pl.ds(r0, rb_rows), d:d + LANES] = jnp.where(lane < LANES // 2, g_hi, g - g_hi).astype(BF16)
        return carry

    lax.fori_loop(0, cap // rb_rows, sort_rows, 0)

    row = lax.broadcasted_iota(I32, (SUBLANES, tm), 0)
    packed = jnp.zeros((SUBLANES, tm), I32)
    for k, p in enumerate(poss):
        packed = jnp.where(row == k, jnp.broadcast_to(p, (SUBLANES, tm)), packed)
    pos_ref[...] = packed
    cnt_ref[...] = jnp.broadcast_to(cnt, cnt_ref.shape)


def _merge(x, hl, ya, yb, w_m, w_ba, w_bb, w_out, gate2, shift3, scale4, g2, rwt, rb, tm):
    b, t, d = x.shape
    ne = rwt.shape[0]
    nt = t // tm
    tok = b * t
    cap = _sorted_rows(tm, ne)
    kern = functools.partial(_merge_kernel, tm=tm, d=d, ne=ne, cap=cap)
    row_spec = pl.BlockSpec((None, tm, d), lambda bi, i: (bi, i, 0))
    mod_spec = pl.BlockSpec((None, 1, d), lambda bi, i: (bi, 0, 0))
    flat = lambda bi, i: (bi * nt + i, 0)
    lane = lambda bi, i: (0, bi * nt + i)

    def whole(shape):
        return pl.BlockSpec(shape, lambda bi, i: (0,) * len(shape))

    return pl.pallas_call(
        kern,
        grid=(b, nt),
        in_specs=[
            row_spec, row_spec, row_spec, row_spec,
            whole((d, 2 * d)), whole((d, d)), whole((d, d)), whole((d, d)),
            mod_spec, mod_spec, mod_spec,
            whole((1, d)), whole((ne, d)), whole((ne, 1)),
        ],
        out_specs=[
            row_spec,
            pl.BlockSpec((cap, d + LANES), flat),
            pl.BlockSpec((SUBLANES, tm), lane),
            pl.BlockSpec((None, ne, LANES), lambda bi, i: (bi * nt + i, 0, 0)),
        ],
        out_shape=[
            jax.ShapeDtypeStruct((b, t, d), F32),
            jax.ShapeDtypeStruct((b * nt * cap, d + LANES), BF16),
            jax.ShapeDtypeStruct((SUBLANES, tok), I32),
            jax.ShapeDtypeStruct((b * nt, ne, LANES), F32),
        ],
        scratch_shapes=[pltpu.VMEM((tm, d), BF16)],
        compiler_params=_params(("arbitrary", "arbitrary"), 56),
        name="merge",
    )(x, hl, ya, yb, w_m, w_ba, w_bb, w_out, gate2, shift3, scale4, g2.reshape(1, d), rwt, rb)


def _experts_kernel(bexp_ref, bact_ref, src_ref, src1_ref, src2_ref, hs_hbm, w1_ref, b1_ref, w2_ref, b2_ref,
                    y_ref, xbuf, act_s, w1p_s, w2_s, gsem, *, blk, nblocks, d):
    j = pl.program_id(0)
    slot = lax.rem(j, MOE_RING)
    far = lax.rem(j + 2, MOE_RING)
    f = w2_s.shape[1]
    pw = 2 * LANES
    n1 = 2 * f // pw
    n2 = d // pw
    nch = blk // MOE_CHUNK
    active = bact_ref[j] == 1

    def request(idx_ref, buf_slot):
        for c in range(nch):
            src = pl.multiple_of(idx_ref[0, 0, c] * MOE_CHUNK, MOE_CHUNK)
            pltpu.make_async_copy(hs_hbm.at[pl.ds(src, MOE_CHUNK), :],
                                  xbuf.at[buf_slot, pl.ds(c * MOE_CHUNK, MOE_CHUNK), :], gsem.at[buf_slot]).start()

    @pl.when(j == 0)
    def _():
        request(src_ref, 0)
        request(src1_ref, 1)

    @pl.when(j + 2 < nblocks)
    def _():
        request(src2_ref, far)

    pltpu.make_async_copy(hs_hbm.at[pl.ds(0, blk), :], xbuf.at[slot], gsem.at[slot]).wait()

    new_expert = (j == 0) | (bexp_ref[j] != bexp_ref[jnp.maximum(j - 1, 0)])

    @pl.when(new_expert & active)
    def _():
        src = lax.broadcasted_iota(I32, (pw, pw), 0)
        dst = lax.broadcasted_iota(I32, (pw, pw), 1)
        want = jnp.where(dst < pw // 2, 2 * dst, 2 * (dst - pw // 2) + 1)
        sel = (src == want).astype(BF16)
        for cb in range(n1):
            w1p_s[cb] = _dot(w1_ref[:, cb * pw:(cb + 1) * pw].astype(BF16), sel).astype(BF16)
        for cb in range(n2):
            w2_s[cb] = w2_ref[:, cb * pw:(cb + 1) * pw].astype(BF16)

    @pl.when(active)
    def _():
        x = xbuf[slot, :, 0:d]
        gate = (xbuf[slot, :, d:d + 1].astype(F32) + xbuf[slot, :, d + LANES // 2:d + LANES // 2 + 1].astype(F32))
        for n in range(n1):
            h = _dot(x, w1p_s[n]) + b1_ref[n]
            hg = jnp.minimum(h[:, :LANES], SWIGLU_LIMIT)
            hu = jnp.clip(h[:, LANES:], -SWIGLU_LIMIT, SWIGLU_LIMIT)
            act_s[n] = (hg * _sigmoid(SWIGLU_ALPHA * hg) * (hu + 1.0)).astype(BF16)
        act = jnp.concatenate([act_s[i] for i in range(n1)], axis=1)
        for n in range(n2):
            y = _dot(act, w2_s[n]) + b2_ref[n]
            y_ref[:, n * pw:(n + 1) * pw] = (y * gate).astype(y_ref.dtype)

    @pl.when(jnp.logical_not(active))
    def _():
        y_ref[...] = jnp.zeros_like(y_ref)


def _experts(block_expert, block_active, src3, hs, w1, b1p, w2, b2, blk):
    nblocks = block_expert.shape[0]
    f, d = w2.shape[1], w2.shape[2]
    pw = 2 * LANES
    n1, n2 = 2 * f // pw, d // pw
    nch = blk // MOE_CHUNK
    b1p = b1p.reshape(-1, n1, 1, pw)
    b2 = b2.reshape(-1, n2, 1, pw)
    kern = functools.partial(_experts_kernel, blk=blk, nblocks=nblocks, d=d)
    wmap = lambda j, be, ba: (be[j], 0, 0)

    def idx_spec(ahead):
        return pl.BlockSpec((1, 1, nch), lambda j, be, ba: (jnp.minimum(j + ahead, nblocks - 1), 0, 0),
                            memory_space=pltpu.SMEM)

    grid_spec = pltpu.PrefetchScalarGridSpec(
        num_scalar_prefetch=2,
        grid=(nblocks,),
        in_specs=[
            idx_spec(0), idx_spec(1), idx_spec(2),
            pl.BlockSpec(memory_space=pl.ANY),
            pl.BlockSpec((None, d, 2 * f), wmap),
            pl.BlockSpec((None, n1, 1, pw), lambda j, be, ba: (be[j], 0, 0, 0)),
            pl.BlockSpec((None, f, d), wmap),
            pl.BlockSpec((None, n2, 1, pw), lambda j, be, ba: (be[j], 0, 0, 0)),
        ],
        out_specs=pl.BlockSpec((blk, d), lambda j, be, ba: (j, 0)),
        scratch_shapes=[
            pltpu.VMEM((MOE_RING, blk, d + LANES), BF16),
            pltpu.VMEM((n1, blk, LANES), BF16),
            pltpu.VMEM((n1, d, pw), BF16),
            pltpu.VMEM((n2, f, pw), BF16),
            pltpu.SemaphoreType.DMA((MOE_RING,)),
        ],
    )
    return pl.pallas_call(
        kern,
        grid_spec=grid_spec,
        out_shape=jax.ShapeDtypeStruct((nblocks * blk, d), BF16),
        compiler_params=_params(("arbitrary",), 56),
        name="experts",
    )(block_expert, block_active, src3, src3, src3, hs, w1, b1p, w2, b2)


def _combine_kernel(dst_ref, dstn_ref, y_hbm, x1_ref, pos_ref, gate_ref, fg_ref, o_ref, ybuf, sem,
                    *, tm, cap, ntiles):
    i = pl.program_id(0)
    slot = lax.rem(i, 2)
    nch = cap // MOE_CHUNK

    def request(idx_ref, buf_slot):
        def body(c, carry):
            src = pl.multiple_of(idx_ref[0, 0, c] * MOE_CHUNK, MOE_CHUNK)
            dst = pl.multiple_of(c * MOE_CHUNK, MOE_CHUNK)
            pltpu.make_async_copy(y_hbm.at[pl.ds(src, MOE_CHUNK), :],
                                  ybuf.at[buf_slot, pl.ds(dst, MOE_CHUNK), :], sem.at[buf_slot]).start()
            return carry
        lax.fori_loop(0, nch, body, 0, unroll=8)

    @pl.when(i == 0)
    def _():
        request(dst_ref, 0)

    @pl.when(i + 1 < ntiles)
    def _():
        request(dstn_ref, 1 - slot)

    pltpu.make_async_copy(y_hbm.at[pl.ds(0, cap), :], ybuf.at[slot], sem.at[slot]).wait()

    pos = pos_ref[...]
    acc = None
    for c0 in range(0, cap, SORT_ROWS):
        col = lax.broadcasted_iota(I32, (tm, SORT_ROWS), 1) + c0
        pick = jnp.zeros((tm, SORT_ROWS), F32)
        for k in range(TOP_K):
            pick = jnp.where(col == pos[:, k:k + 1], 1.0, pick)
        part = _dot(pick.astype(BF16), ybuf[slot, c0:c0 + SORT_ROWS, :])
        acc = part if acc is None else acc + part
    x2 = x1_ref[...] + gate_ref[...] * acc
    o_ref[...] = _rms(x2) * fg_ref[...]


def _combine(dst3, y, x1, pos_t, gate5, final_g, tm, cap):
    b, t, d = x1.shape
    nt = t // tm
    ntiles = b * nt
    nch = cap // MOE_CHUNK
    kern = functools.partial(_combine_kernel, tm=tm, cap=cap, ntiles=ntiles)
    row_map = lambda i: (i // nt, i % nt, 0)
    return pl.pallas_call(
        kern,
        grid=(ntiles,),
        in_specs=[
            pl.BlockSpec((1, 1, nch), lambda i: (i, 0, 0), memory_space=pltpu.SMEM),
            pl.BlockSpec((1, 1, nch), lambda i: (jnp.minimum(i + 1, ntiles - 1), 0, 0), memory_space=pltpu.SMEM),
            pl.BlockSpec(memory_space=pl.ANY),
            pl.BlockSpec((None, tm, d), row_map),
            pl.BlockSpec((tm, TOP_K), lambda i: (i, 0)),
            pl.BlockSpec((None, 1, d), lambda i: (i // nt, 0, 0)),
            pl.BlockSpec((1, d), lambda i: (0, 0)),
        ],
        out_specs=pl.BlockSpec((None, tm, d), row_map),
        out_shape=jax.ShapeDtypeStruct((b, t, d), F32),
        scratch_shapes=[
            pltpu.VMEM((2, cap, d), BF16),
            pltpu.SemaphoreType.DMA((2,)),
        ],
        compiler_params=_params(("arbitrary",), 48),
        name="combine",
    )(dst3, dst3, y, x1, pos_t, gate5, final_g.reshape(1, d))


def _sorted_rows(tm, ne):
    need = tm * TOP_K + ne * (MOE_CHUNK - 1) + MOE_CHUNK
    return -(-need // SORT_ROWS) * SORT_ROWS


def _pick_tile(n, pref):
    tm = pref
    while n % tm:
        tm //= 2
    return tm


def kernel(x, c, ctx, c_ctx, ada_w, ada_b, norm1_g, norm2_g, w_in, lru_conv_w, lru_conv_b, lru_wa, lru_ba, lru_wx, lru_bx, lru_lam, hg_lb_logits, hg_norm_g, w_branch_a, w_branch_b, w_out, router_w, router_b, moe_w1, moe_b1, moe_w2, moe_b2, final_g):
    b, t, d = x.shape
    cx = ctx.shape[1]
    layer = 0
    w_lru = lru_conv_w.shape[2]
    qk = (w_in.shape[2] - 2 * w_lru - 2 * d) // 5
    heads = qk // HG_DK
    ne = router_w.shape[2]

    pad = (-(b + 1)) % SUBLANES
    c_all = jnp.concatenate([c, c_ctx[None, :], jnp.zeros((pad, d), F32)], axis=0)
    mod = _modulation(c_all, ada_w[layer], ada_b[layer])
    mod_l = mod[:b].reshape(b, N_MOD, 1, d)
    mod_c = mod[b].reshape(N_MOD, 1, 1, d)

    w_in_b = w_in[layer].astype(BF16)
    w_lru_in = w_in_b[:, :2 * w_lru]
    hg0 = 2 * w_lru
    w_hg = w_in_b[:, hg0:hg0 + 5 * qk].reshape(d, 5, heads, HG_DK).transpose(2, 0, 1, 3).reshape(heads, d, 5 * HG_DK)
    w_m = w_in_b[:, hg0 + 5 * qk:]
    g_blocks = w_lru // LRU_BLOCK
    wg = jnp.concatenate([lru_wa[layer, 0], lru_wx[layer, 0], lru_wa[layer, 1], lru_wx[layer, 1]], axis=-1).astype(BF16)
    bg = jnp.concatenate([lru_ba[layer, 0], lru_bx[layer, 0], lru_ba[layer, 1], lru_bx[layer, 1]], axis=-1)
    bg = bg.reshape(g_blocks, 1, 4 * LRU_BLOCK)
    lb_all = jnp.cumsum(jax.nn.softmax(hg_lb_logits.astype(F32), axis=0), axis=0)
    lb = lb_all[layer].reshape(heads, 1, HG_DK)

    tm_n = _pick_tile(t, 512)
    hl = _prenorm(x, mod_l[:, 0], mod_l[:, 1], norm1_g[layer], tm_n)
    hc = _prenorm(ctx, mod_c[0], mod_c[1], norm1_g[layer], _pick_tile(cx, 256))
    axl = _matmul(hl.reshape(b * t, d), w_lru_in, _pick_tile(b * t, 512), BF16).reshape(b, t, 2 * w_lru)
    axc = _matmul(hc.reshape(b * cx, d), w_lru_in, _pick_tile(b * cx, 512), BF16).reshape(b, cx, 2 * w_lru)
    ya = _lru(axl, axc, lru_conv_w[layer], lru_conv_b[layer], wg, bg, lru_lam[layer], b, cx, t)
    yb = _hgrn(hc, hl, w_hg, lb, hg_norm_g[layer])

    rwt = router_w[layer].T
    rb = router_b[layer].reshape(ne, 1)
    tm = _pick_tile(t, 512)
    x1, hs, pos8, cnt = _merge(
        x, hl, ya, yb, w_m, w_branch_a[layer].astype(BF16), w_branch_b[layer].astype(BF16),
        w_out[layer].astype(BF16), mod_l[:, 2], mod_l[:, 3], mod_l[:, 4], norm2_g[layer], rwt, rb, tm)

    blk = MOE_BLOCK
    ch = MOE_CHUNK
    bpc = blk // ch
    tok = b * t
    ntile = tok // tm
    cap = _sorted_rows(tm, ne)
    tch = cap // ch
    run = (cnt[:, :, 0].astype(I32) + ch - 1) // ch
    loc_end = jnp.cumsum(run, axis=1)
    loc_start = loc_end - run
    upto = jnp.cumsum(run, axis=0)
    total = upto[-1]
    padded = (total + bpc - 1) // bpc * bpc
    gend = jnp.cumsum(padded)
    gstart = gend - padded
    where_run = gstart[None, :] + upto - run
    max_chunks = tok * TOP_K // ch + ntile * ne + ne * bpc
    nblocks = -(-max_chunks // bpc)

    eids = jnp.arange(ne, dtype=I32)
    tids = jnp.arange(ntile, dtype=I32)
    g = jnp.arange(nblocks * bpc, dtype=I32)
    g_e = jnp.minimum(jnp.sum(g[:, None] >= gend[None, :], axis=1), ne - 1).astype(I32)
    is_e = g_e[:, None] == eids[None, :]
    off = g - jnp.sum(jnp.where(is_e, gstart[None, :], 0), axis=1)
    g_total = jnp.sum(jnp.where(is_e, total[None, :], 0), axis=1)
    upto_g = jnp.sum(jnp.where(is_e[:, None, :], upto[None, :, :], 0), axis=2)
    g_i = jnp.minimum(jnp.sum(off[:, None] >= upto_g, axis=1), ntile - 1).astype(I32)
    both = (g_i[:, None] == tids[None, :])[:, :, None] & is_e[:, None, :]

    def at_run(table):
        return jnp.sum(jnp.where(both, table[None, :, :], 0), axis=(1, 2))

    within = off - (at_run(upto) - at_run(run))
    zero_chunk = tch - 1
    src = jnp.where((off < g_total) & (g < gend[-1]), g_i * tch + at_run(loc_start) + within, zero_chunk)

    block_first = jnp.arange(nblocks, dtype=I32) * bpc
    block_active = block_first < gend[-1]
    block_expert = jnp.minimum(jnp.sum(block_first[:, None] >= gend[None, :], axis=1), ne - 1).astype(I32)
    last_expert = jnp.max(jnp.where(block_active, block_expert, 0))
    block_expert = jnp.where(block_active, block_expert, last_expert).astype(I32)

    c = jnp.arange(tch, dtype=I32)
    c_e = jnp.minimum(jnp.sum(c[None, :, None] >= loc_end[:, None, :], axis=2), ne - 1).astype(I32)
    is_ce = c_e[:, :, None] == eids[None, None, :]
    back = jnp.sum(jnp.where(is_ce, (where_run - loc_start)[:, None, :], 0), axis=2) + c[None, :]
    back = jnp.where(c[None, :] < loc_end[:, -1:], back, 0)

    b1 = moe_b1[layer]
    f = b1.shape[1] // 2
    b1p = b1.reshape(ne, f // LANES, LANES, 2).transpose(0, 1, 3, 2).reshape(ne, 1, 2 * f)
    y = _experts(block_expert, block_active.astype(I32), src.reshape(nblocks, 1, bpc), hs, moe_w1[layer], b1p,
                 moe_w2[layer], moe_b2[layer].reshape(ne, 1, d), blk)
    return _combine(back.reshape(ntile, 1, tch), y, x1, pos8[:TOP_K].T, mod_l[:, 5], final_g, tm, cap)
```

```python
import functools

import jax
import jax.numpy as jnp
from jax import lax
from jax.experimental import pallas as pl
from jax.experimental.pallas import tpu as pltpu

F32 = jnp.float32
BF16 = jnp.bfloat16
I32 = jnp.int32
HIGHEST = lax.Precision.HIGHEST

EPS = 1e-6
N_MOD = 6
GRID_W = 64
CONV_LEFT = 2
CONV_WIDTH = 4
LRU_C = 8.0
LRU_BLOCK = 128
HG_DK = 128
HG_CHUNK = 64
TOP_K = 4
SWIGLU_LIMIT = 7.0
SWIGLU_ALPHA = 1.702
MOE_BLOCK = 512
MOE_CHUNK = 16
SORT_ROWS = 256
MOE_RING = 3
SUBLANES = 8
LANES = 128
HALO = 8


def _params(sem, vmem_mb):
    return pltpu.CompilerParams(dimension_semantics=sem, vmem_limit_bytes=vmem_mb * 1024 * 1024)


def _dot(a, b):
    return jnp.dot(a, b, preferred_element_type=F32)


def _dot_nt(a, b):
    return lax.dot_general(a, b, (((1,), (1,)), ((), ())), preferred_element_type=F32)


def _dot_tn(a, b):
    return lax.dot_general(a, b, (((0,), (0,)), ((), ())), preferred_element_type=F32)


def _sigmoid(x):
    return 0.5 * jnp.tanh(0.5 * x) + 0.5


def _silu(x):
    return x * _sigmoid(x)


def _rms(x):
    return x * lax.rsqrt(jnp.mean(x * x, axis=-1, keepdims=True) + EPS)


def _mod_kernel(c_ref, w_ref, b_ref, o_ref):
    s = _silu(c_ref[...])
    o_ref[...] = jnp.dot(s, w_ref[...], preferred_element_type=F32, precision=HIGHEST) + b_ref[...]


def _modulation(c_all, ada_w, ada_b):
    m, d = c_all.shape
    n = ada_w.shape[1]
    tn = 1024
    return pl.pallas_call(
        _mod_kernel,
        grid=(n // tn,),
        in_specs=[
            pl.BlockSpec((m, d), lambda j: (0, 0)),
            pl.BlockSpec((d, tn), lambda j: (0, j)),
            pl.BlockSpec((1, tn), lambda j: (0, j)),
        ],
        out_specs=pl.BlockSpec((m, tn), lambda j: (0, j)),
        out_shape=jax.ShapeDtypeStruct((m, n), F32),
        compiler_params=_params(("arbitrary",), 32),
        name="mod",
    )(c_all, ada_w, ada_b.reshape(1, n))


def _prenorm_kernel(x_ref, shift_ref, scale_ref, g_ref, o_ref):
    y = _rms(x_ref[...]) * g_ref[...]
    o_ref[...] = (y * (1.0 + scale_ref[...]) + shift_ref[...]).astype(o_ref.dtype)


def _prenorm(x, shift, scale, g, tm):
    b, l, d = x.shape
    per_batch = shift.shape[0] == b and b > 1
    mod_map = (lambda bi, i: (bi, 0, 0)) if per_batch else (lambda bi, i: (0, 0, 0))
    return pl.pallas_call(
        _prenorm_kernel,
        grid=(b, l // tm),
        in_specs=[
            pl.BlockSpec((None, tm, d), lambda bi, i: (bi, i, 0)),
            pl.BlockSpec((None, 1, d), mod_map),
            pl.BlockSpec((None, 1, d), mod_map),
            pl.BlockSpec((1, d), lambda bi, i: (0, 0)),
        ],
        out_specs=pl.BlockSpec((None, tm, d), lambda bi, i: (bi, i, 0)),
        out_shape=jax.ShapeDtypeStruct((b, l, d), BF16),
        compiler_params=_params(("arbitrary", "arbitrary"), 32),
        name="prenorm",
    )(x, shift, scale, g.reshape(1, d))


def _mm_kernel(x_ref, w_ref, o_ref):
    o_ref[...] = _dot(x_ref[...], w_ref[...]).astype(o_ref.dtype)


def _matmul(x2d, w, tm, out_dtype):
    m, k = x2d.shape
    n = w.shape[1]
    return pl.pallas_call(
        _mm_kernel,
        grid=(m // tm,),
        in_specs=[
            pl.BlockSpec((tm, k), lambda i: (i, 0)),
            pl.BlockSpec((k, n), lambda i: (0, 0)),
        ],
        out_specs=pl.BlockSpec((tm, n), lambda i: (i, 0)),
        out_shape=jax.ShapeDtypeStruct((m, n), out_dtype),
        compiler_params=_params(("arbitrary",), 40),
        name="inproj_lru",
    )(x2d, w)


def _lru_kernel(axl_ref, agl_ref, axc_ref, cw_ref, cb_ref, wg_ref, bg_ref, lam_ref, o_ref,
                xc_ref, yacc_ref, sx_ref, sa_ref, sb_ref, sh_ref, *, nb, cx, t):
    tb = GRID_W
    ctx0 = HALO
    gw = 2 * LRU_BLOCK
    pre = CONV_LEFT
    steps = pre + tb + (HALO - pre)

    zeros_h = jnp.zeros((HALO, LRU_BLOCK), F32)
    for b in range(nb):
        xc_ref[b, 0:HALO, :] = zeros_h
        xc_ref[b, ctx0:ctx0 + cx, :] = axc_ref[b].astype(F32)
        xc_ref[b, ctx0 + cx:ctx0 + cx + HALO, :] = zeros_h

    cw = cw_ref[...]
    cb = cb_ref[...]
    lam = lam_ref[...]
    sp = jnp.maximum(-lam, 0.0) + jnp.log1p(jnp.exp(-jnp.abs(lam)))

    def zero_history(d):
        sx_ref[d, 0:pre * nb, :] = jnp.zeros((pre * nb, LRU_BLOCK), F32)
        sx_ref[d, (pre + tb) * nb:steps * nb, :] = jnp.zeros(((steps - pre - tb) * nb, LRU_BLOCK), F32)

    def load_ctx_block(d, base):
        for b in range(nb):
            sx_ref[d, pl.ds(b, steps, stride=nb), :] = xc_ref[b, pl.ds(base - pre, steps), :]

    def load_latent_block(d, r0):
        for b in range(nb):
            sx_ref[d, pl.ds(pre * nb + b, tb, stride=nb), :] = axl_ref[b, pl.ds(r0, tb), :].astype(F32)

    def fill_gates(d):
        u = jnp.broadcast_to(cb, (tb * nb, LRU_BLOCK))
        for k in range(CONV_WIDTH):
            u = u + sx_ref[d, k * nb:(k + tb) * nb, :] * cw[k:k + 1, :]
        z = _dot(u.astype(BF16), wg_ref[:, d * gw:(d + 1) * gw]) + bg_ref[:, d * gw:(d + 1) * gw]
        r = _sigmoid(z[:, :LRU_BLOCK])
        i = _sigmoid(z[:, LRU_BLOCK:])
        log_a = (-LRU_C) * r * sp[d:d + 1, :]
        a = jnp.exp(log_a)
        mult = jnp.sqrt(-jnp.tanh(log_a) * (a * a + 1.0))
        sa_ref[d] = a
        sb_ref[d] = mult * (i * u)

    def scan_pair(hf, hr):
        for s in range(tb):
            sr = tb - 1 - s
            hf = sa_ref[0, s * nb:(s + 1) * nb, :] * hf + sb_ref[0, s * nb:(s + 1) * nb, :]
            sh_ref[0, s * nb:(s + 1) * nb, :] = hf
            hr = sa_ref[1, sr * nb:(sr + 1) * nb, :] * hr + sb_ref[1, sr * nb:(sr + 1) * nb, :]
            sh_ref[1, sr * nb:(sr + 1) * nb, :] = hr
        return hf, hr

    n_c, n_l = cx // tb, t // tb

    def ctx_body(j, hs):
        load_ctx_block(0, pl.multiple_of(ctx0 + j * tb, SUBLANES))
        load_ctx_block(1, pl.multiple_of(ctx0 + (n_c - 1 - j) * tb, SUBLANES))
        fill_gates(0)
        fill_gates(1)
        return scan_pair(*hs)

    def emit(d, r0, final):
        for b in range(nb):
            yb = sh_ref[d, pl.ds(b, tb, stride=nb), :]
            if final:
                gate = jax.nn.gelu(agl_ref[b, pl.ds(r0, tb), :].astype(F32))
                o_ref[b, pl.ds(r0, tb), :] = ((yacc_ref[b, pl.ds(r0, tb), :] + yb) * gate).astype(o_ref.dtype)
            else:
                yacc_ref[b, pl.ds(r0, tb), :] = yb

    def lat_body(j, hs, final_f, final_r):
        rf = pl.multiple_of(j * tb, tb)
        rr = pl.multiple_of((n_l - 1 - j) * tb, tb)
        load_latent_block(0, rf)
        load_latent_block(1, rr)
        fill_gates(0)
        fill_gates(1)
        hs = scan_pair(*hs)
        emit(0, rf, final_f)
        emit(1, rr, final_r)
        return hs

    h0 = jnp.zeros((nb, LRU_BLOCK), F32)
    hs = lax.fori_loop(0, n_c, ctx_body, (h0, h0))
    zero_history(0)
    zero_history(1)
    half = n_l // 2
    hs = lax.fori_loop(0, half, functools.partial(lat_body, final_f=False, final_r=False), hs)
    if n_l % 2:
        hs = lat_body(jnp.int32(half), hs, False, True)
    lax.fori_loop(n_l - half, n_l, functools.partial(lat_body, final_f=True, final_r=True), hs)


def _lru(axl, axc, conv_w, conv_b, wg, bg, lam, nb, cx, t):
    w = conv_w.shape[1]
    g = w // LRU_BLOCK
    kern = functools.partial(_lru_kernel, nb=nb, cx=cx, t=t)
    return pl.pallas_call(
        kern,
        grid=(g,),
        in_specs=[
            pl.BlockSpec((nb, t, LRU_BLOCK), lambda j: (0, 0, j)),
            pl.BlockSpec((nb, t, LRU_BLOCK), lambda j: (0, 0, g + j)),
            pl.BlockSpec((nb, cx, LRU_BLOCK), lambda j: (0, 0, j)),
            pl.BlockSpec((CONV_WIDTH, LRU_BLOCK), lambda j: (0, j)),
            pl.BlockSpec((1, LRU_BLOCK), lambda j: (0, j)),
            pl.BlockSpec((None, LRU_BLOCK, 4 * LRU_BLOCK), lambda j: (j, 0, 0)),
            pl.BlockSpec((None, 1, 4 * LRU_BLOCK), lambda j: (j, 0, 0)),
            pl.BlockSpec((2, LRU_BLOCK), lambda j: (0, j)),
        ],
        out_specs=pl.BlockSpec((nb, t, LRU_BLOCK), lambda j: (0, 0, j)),
        out_shape=jax.ShapeDtypeStruct((nb, t, w), BF16),
        scratch_shapes=[
            pltpu.VMEM((nb, cx + 2 * HALO, LRU_BLOCK), F32),
            pltpu.VMEM((nb, t, LRU_BLOCK), F32),
            pltpu.VMEM((2, (GRID_W + HALO) * nb, LRU_BLOCK), F32),
            pltpu.VMEM((2, GRID_W * nb, LRU_BLOCK), F32),
            pltpu.VMEM((2, GRID_W * nb, LRU_BLOCK), F32),
            pltpu.VMEM((2, GRID_W * nb, LRU_BLOCK), F32),
        ],
        compiler_params=_params(("arbitrary",), 56),
        name="lru",
    )(axl, axl, axc, conv_w, conv_b.reshape(1, w), wg, bg, lam)


def _hgrn_kernel(hc_ref, hl_ref, wq_ref, wff_ref, wfb_ref, wv_ref, wg_ref, lb_ref, ng_ref, o_ref,
                 p_ref, oacc_ref, q_s, v_s, k_s, hl_s, gc_s, bk_s, sc_s, aq_s, upd_ref, dec_ref,
                 *, cx, t):
    c = HG_CHUNK
    dk = HG_DK
    n_c, n_l = cx // c, t // c
    n_all = n_c + n_l

    w = jnp.concatenate([wq_ref[...], wff_ref[...], wfb_ref[...], wv_ref[...], wg_ref[...]], axis=1)
    p_ref[0:cx, :] = _dot(hc_ref[...], w)
    p_ref[cx:cx + t, :] = _dot(hl_ref[...], w)
    rows = 512 if t % 512 == 0 else c

    lb = lb_ref[...]
    one_m_lb = 1.0 - lb
    ri = lax.broadcasted_iota(I32, (c, c), 0)
    ci = lax.broadcasted_iota(I32, (c, c), 1)
    keep = (ri >= ci, ci >= ri)
    mid = (c // 2 - 1, c // 2)
    end = (c - 1, 0)

    def stage_gates(j, carry):
        r0 = pl.multiple_of(j * c, c)
        q_s[pl.ds(r0, c), :] = _silu(p_ref[pl.ds(r0, c), 0:dk])
        v_s[pl.ds(r0, c), :] = p_ref[pl.ds(r0, c), 3 * dk:4 * dk].astype(BF16)
        for d in range(2):
            f = lb + one_m_lb * _sigmoid(p_ref[pl.ds(r0, c), (1 + d) * dk:(2 + d) * dk])
            logf = jnp.log(f)
            hi = logf.astype(BF16)
            lo = (logf - hi.astype(F32)).astype(BF16)
            k_s[d, pl.ds(r0, c), :] = 1.0 - f
            hl_s[d, pl.ds(r0, c), :] = jnp.concatenate([hi, lo], axis=1)
        return carry

    lax.fori_loop(0, n_all, stage_gates, 0, unroll=4)

    def stage_cumsum(j, carry):
        r0 = pl.multiple_of(j * c, c)
        for d in range(2):
            s2 = _dot(keep[d].astype(BF16), hl_s[d, pl.ds(r0, c), :])
            gc_s[d, pl.ds(r0, c), :] = s2[:, :dk] + s2[:, dk:]
        return carry

    lax.fori_loop(0, n_all, stage_cumsum, 0, unroll=12)

    def stage_scores(j, carry):
        r0 = pl.multiple_of(j * c, c)
        q = q_s[pl.ds(r0, c), :]
        for d in range(2):
            gc = gc_s[d, pl.ds(r0, c), :]
            k = k_s[d, pl.ds(r0, c), :]
            g_mid = gc[mid[d]:mid[d] + 1, :]
            g_end = gc[end[d]:end[d] + 1, :]
            qa = q * jnp.exp(gc - g_mid)
            kb = k * jnp.exp(g_mid - gc)
            sc_s[d, pl.ds(r0, c), :] = jnp.where(keep[d], _dot_nt(qa.astype(BF16), kb.astype(BF16)), 0.0).astype(BF16)
            bk_s[d, pl.ds(r0, c), :] = (kb * jnp.exp(g_end - g_mid)).astype(BF16)
            aq_s[d, pl.ds(r0, c), :] = (qa * jnp.exp(g_mid)).astype(BF16)
            dec_ref[d, j] = jnp.broadcast_to(jnp.exp(g_end), (SUBLANES, dk))
        return carry

    lax.fori_loop(0, n_all, stage_scores, 0, unroll=6)

    def stage_intra(j, carry):
        r0 = pl.multiple_of(j * c, c)
        v = v_s[pl.ds(r0, c), :]
        oacc_ref[pl.ds(r0, c), :] = _dot(sc_s[0, pl.ds(r0, c), :], v) + _dot(sc_s[1, pl.ds(r0, c), :], v)
        for d in range(2):
            upd_ref[d, j] = _dot_tn(v, bk_s[d, pl.ds(r0, c), :])
        return carry

    lax.fori_loop(0, n_all, stage_intra, 0, unroll=12)

    def inter(d, jj, st):
        r0 = pl.multiple_of(jj * c, c)
        oacc_ref[pl.ds(r0, c), :] += _dot_nt(aq_s[d, pl.ds(r0, c), :], st.astype(BF16))
        return st * dec_ref[d, jj][0:1, :] + upd_ref[d, jj]

    def ctx_body(j, sts):
        return inter(0, j, sts[0]), inter(1, n_c - 1 - j, sts[1])

    def lat_body(j, sts):
        return inter(0, n_c + j, sts[0]), inter(1, n_c + n_l - 1 - j, sts[1])

    zero = jnp.zeros((dk, dk), F32)
    sts = lax.fori_loop(0, n_c, ctx_body, (zero, zero), unroll=2)
    lax.fori_loop(0, n_l, lat_body, sts, unroll=8)

    ng = ng_ref[...]

    def readout(i, carry):
        r = pl.multiple_of(i * rows, rows)
        o = oacc_ref[pl.ds(cx + r, rows), :]
        g = p_ref[pl.ds(cx + r, rows), 4 * dk:5 * dk]
        o_ref[pl.ds(r, rows), :] = (_rms(o) * ng * _silu(g)).astype(o_ref.dtype)
        return carry

    lax.fori_loop(0, t // rows, readout, 0)


def _hgrn(hc, hl, w_in_b, col0, heads, lb, norm_g):
    b, cx, d = hc.shape
    t = hl.shape[1]
    dk = HG_DK
    n_all = (cx + t) // HG_CHUNK
    kern = functools.partial(_hgrn_kernel, cx=cx, t=t)

    def group(gi):
        return pl.BlockSpec((d, dk), lambda bi, h: (0, col0 // dk + gi * heads + h))

    return pl.pallas_call(
        kern,
        grid=(b, heads),
        in_specs=[
            pl.BlockSpec((None, cx, d), lambda bi, h: (bi, 0, 0)),
            pl.BlockSpec((None, t, d), lambda bi, h: (bi, 0, 0)),
            group(0), group(1), group(2), group(3), group(4),
            pl.BlockSpec((None, 1, dk), lambda bi, h: (h, 0, 0)),
            pl.BlockSpec((1, dk), lambda bi, h: (0, 0)),
        ],
        out_specs=pl.BlockSpec((None, t, dk), lambda bi, h: (bi, 0, h)),
        out_shape=jax.ShapeDtypeStruct((b, t, heads * dk), BF16),
        scratch_shapes=[
            pltpu.VMEM((cx + t, 5 * dk), F32),
            pltpu.VMEM((cx + t, dk), F32),
            pltpu.VMEM((cx + t, dk), F32),
            pltpu.VMEM((cx + t, dk), BF16),
            pltpu.VMEM((2, cx + t, dk), F32),
            pltpu.VMEM((2, cx + t, 2 * dk), BF16),
            pltpu.VMEM((2, cx + t, dk), F32),
            pltpu.VMEM((2, cx + t, dk), BF16),
            pltpu.VMEM((2, cx + t, HG_CHUNK), BF16),
            pltpu.VMEM((2, cx + t, dk), BF16),
            pltpu.VMEM((2, n_all, dk, dk), F32),
            pltpu.VMEM((2, n_all, SUBLANES, dk), F32),
        ],
        compiler_params=_params(("arbitrary", "arbitrary"), 48),
        name="hgrn",
    )(hc, hl, w_in_b, w_in_b, w_in_b, w_in_b, w_in_b, lb, norm_g.reshape(1, dk))


def _merge_kernel(x_ref, h_ref, ya_ref, yb_ref, wm_ref, wa_ref, wb_ref, wo_ref,
                  gate_ref, shift_ref, scale_ref, g2_ref, rwt_ref, rb_ref,
                  x1_ref, hs_ref, pos_ref, cnt_ref, h2b_s, *, tm, d, ne, cap):
    m = _dot(h_ref[...], wm_ref[...])
    za = _dot(ya_ref[...], wa_ref[...])
    zb = _dot(yb_ref[...], wb_ref[...])
    mix = _sigmoid(m[:, :d]) * za + _sigmoid(m[:, d:]) * zb
    out = _dot(mix.astype(BF16), wo_ref[...])
    x1 = x_ref[...] + gate_ref[...] * out
    x1_ref[...] = x1
    h2 = _rms(x1) * g2_ref[...] * (1.0 + scale_ref[...]) + shift_ref[...]
    h2b_s[...] = h2.astype(BF16)

    logits = lax.dot_general(rwt_ref[...], h2, (((1,), (1,)), ((), ())),
                             preferred_element_type=F32, precision=HIGHEST) + rb_ref[...]
    eid = lax.broadcasted_iota(I32, (ne, tm), 0)
    neg = jnp.float32(-jnp.inf)
    work = logits
    hot = jnp.zeros((ne, tm), F32)
    vals, idxs, sels = [], [], []
    for _ in range(TOP_K):
        mx = jnp.max(work, axis=0, keepdims=True)
        ix = jnp.min(jnp.where(work == mx, eid, ne), axis=0, keepdims=True)
        sel = eid == ix
        work = jnp.where(sel, neg, work)
        hot = hot + sel.astype(F32)
        vals.append(mx)
        idxs.append(ix)
        sels.append(sel)
    ex = [jnp.exp(v - vals[0]) for v in vals]
    den = ex[0] + ex[1] + ex[2] + ex[3]
    gts = [e / den for e in ex]

    ti = lax.broadcasted_iota(I32, (tm, tm), 0)
    tj = lax.broadcasted_iota(I32, (tm, tm), 1)
    strict = (ti < tj).astype(BF16)
    before = _dot(hot.astype(BF16), strict)
    cnt = jnp.sum(hot, axis=1, keepdims=True)
    nchunks = jnp.ceil(cnt * (1.0 / MOE_CHUNK))
    ei = lax.broadcasted_iota(I32, (ne, ne), 0)
    ej = lax.broadcasted_iota(I32, (ne, ne), 1)
    earlier = (ej < ei).astype(BF16)
    start = _dot(earlier, jnp.broadcast_to(nchunks, (ne, LANES)).astype(BF16))[:, 0:1] * MOE_CHUNK
    where_to = before + start
    poss = [jnp.sum(jnp.where(s, where_to, 0.0), axis=0, keepdims=True).astype(I32) for s in sels]

    rb_rows = SORT_ROWS
    lane = lax.broadcasted_iota(I32, (rb_rows, LANES), 1)

    def sort_rows(i, carry):
        r0 = pl.multiple_of(i * rb_rows, rb_rows)
        pid = lax.broadcasted_iota(I32, (rb_rows, tm), 0) + r0
        place = jnp.zeros((rb_rows, tm), F32)
        gsel = jnp.zeros((rb_rows, tm), F32)
        for k in range(TOP_K):
            hit = pid == poss[k]
            place = jnp.where(hit, 1.0, place)
            gsel = jnp.where(hit, gts[k], gsel)
        hs_ref[pl.ds(r0, rb_rows), 0:d] = _dot(place.astype(BF16), h2b_s[...]).astype(BF16)
        g = jnp.sum(gsel, axis=1, keepdims=True)
        g_hi = g.astype(BF16).astype(F32)
        hs_ref[pl.ds(r0, rb_rows), d:d + LANES] = jnp.where(lane < LANES // 2, g_hi, g - g_hi).astype(BF16)
        return carry

    lax.fori_loop(0, cap // rb_rows, sort_rows, 0)

    row = lax.broadcasted_iota(I32, (SUBLANES, tm), 0)
    packed = jnp.zeros((SUBLANES, tm), I32)
    for k, p in enumerate(poss):
        packed = jnp.where(row == k, jnp.broadcast_to(p, (SUBLANES, tm)), packed)
    pos_ref[...] = packed
    cnt_ref[...] = jnp.broadcast_to(cnt, cnt_ref.shape)


def _merge(x, hl, ya, yb, w_m, w_ba, w_bb, w_out, gate2, shift3, scale4, g2, rwt, rb, tm):
    b, t, d = x.shape
    ne = rwt.shape[0]
    nt = t // tm
    tok = b * t
    cap = _sorted_rows(tm, ne)
    kern = functools.partial(_merge_kernel, tm=tm, d=d, ne=ne, cap=cap)
    row_spec = pl.BlockSpec((None, tm, d), lambda bi, i: (bi, i, 0))
    mod_spec = pl.BlockSpec((None, 1, d), lambda bi, i: (bi, 0, 0))
    flat = lambda bi, i: (bi * nt + i, 0)
    lane = lambda bi, i: (0, bi * nt + i)

    def whole(shape):
        return pl.BlockSpec(shape, lambda bi, i: (0,) * len(shape))

    return pl.pallas_call(
        kern,
        grid=(b, nt),
        in_specs=[
            row_spec, row_spec, row_spec, row_spec,
            whole((d, 2 * d)), whole((d, d)), whole((d, d)), whole((d, d)),
            mod_spec, mod_spec, mod_spec,
            whole((1, d)), whole((ne, d)), whole((ne, 1)),
        ],
        out_specs=[
            row_spec,
            pl.BlockSpec((cap, d + LANES), flat),
            pl.BlockSpec((SUBLANES, tm), lane),
            pl.BlockSpec((None, ne, LANES), lambda bi, i: (bi * nt + i, 0, 0)),
        ],
        out_shape=[
            jax.ShapeDtypeStruct((b, t, d), F32),
            jax.ShapeDtypeStruct((b * nt * cap, d + LANES), BF16),
            jax.ShapeDtypeStruct((SUBLANES, tok), I32),
            jax.ShapeDtypeStruct((b * nt, ne, LANES), F32),
        ],
        scratch_shapes=[pltpu.VMEM((tm, d), BF16)],
        compiler_params=_params(("arbitrary", "arbitrary"), 56),
        name="merge",
    )(x, hl, ya, yb, w_m, w_ba, w_bb, w_out, gate2, shift3, scale4, g2.reshape(1, d), rwt, rb)


def _experts_kernel(bexp_ref, bact_ref, src_ref, src1_ref, src2_ref, hs_hbm, w1_ref, b1_ref, w2_ref, b2_ref,
                    y_ref, xbuf, act_s, w1p_s, w2_s, gsem, *, blk, nblocks, d):
    j = pl.program_id(0)
    slot = lax.rem(j, MOE_RING)
    far = lax.rem(j + 2, MOE_RING)
    f = w2_s.shape[1]
    pw = 2 * LANES
    n1 = 2 * f // pw
    n2 = d // pw
    nch = blk // MOE_CHUNK
    active = bact_ref[j] == 1

    def request(idx_ref, buf_slot):
        for c in range(nch):
            src = pl.multiple_of(idx_ref[0, 0, c] * MOE_CHUNK, MOE_CHUNK)
            pltpu.make_async_copy(hs_hbm.at[pl.ds(src, MOE_CHUNK), :],
                                  xbuf.at[buf_slot, pl.ds(c * MOE_CHUNK, MOE_CHUNK), :], gsem.at[buf_slot]).start()

    @pl.when(j == 0)
    def _():
        request(src_ref, 0)
        request(src1_ref, 1)

    @pl.when(j + 2 < nblocks)
    def _():
        request(src2_ref, far)

    pltpu.make_async_copy(hs_hbm.at[pl.ds(0, blk), :], xbuf.at[slot], gsem.at[slot]).wait()

    new_expert = (j == 0) | (bexp_ref[j] != bexp_ref[jnp.maximum(j - 1, 0)])

    @pl.when(new_expert & active)
    def _():
        src = lax.broadcasted_iota(I32, (pw, pw), 0)
        dst = lax.broadcasted_iota(I32, (pw, pw), 1)
        want = jnp.where(dst < pw // 2, 2 * dst, 2 * (dst - pw // 2) + 1)
        sel = (src == want).astype(BF16)
        for cb in range(n1):
            w1p_s[cb] = _dot(w1_ref[:, cb * pw:(cb + 1) * pw].astype(BF16), sel).astype(BF16)
        for cb in range(n2):
            w2_s[cb] = w2_ref[:, cb * pw:(cb + 1) * pw].astype(BF16)

    @pl.when(active)
    def _():
        x = xbuf[slot, :, 0:d]
        gate = (xbuf[slot, :, d:d + 1].astype(F32) + xbuf[slot, :, d + LANES // 2:d + LANES // 2 + 1].astype(F32))
        for n in range(n1):
            h = _dot(x, w1p_s[n]) + b1_ref[n]
            hg = jnp.minimum(h[:, :LANES], SWIGLU_LIMIT)
            hu = jnp.clip(h[:, LANES:], -SWIGLU_LIMIT, SWIGLU_LIMIT)
            act_s[n] = (hg * _sigmoid(SWIGLU_ALPHA * hg) * (hu + 1.0)).astype(BF16)
        act = jnp.concatenate([act_s[i] for i in range(n1)], axis=1)
        for n in range(n2):
            y = _dot(act, w2_s[n]) + b2_ref[n]
            y_ref[:, n * pw:(n + 1) * pw] = (y * gate).astype(y_ref.dtype)

    @pl.when(jnp.logical_not(active))
    def _():
        y_ref[...] = jnp.zeros_like(y_ref)


def _experts(block_expert, block_active, src3, hs, w1, b1p, w2, b2, blk):
    nblocks = block_expert.shape[0]
    f, d = w2.shape[1], w2.shape[2]
    pw = 2 * LANES
    n1, n2 = 2 * f // pw, d // pw
    nch = blk // MOE_CHUNK
    b1p = b1p.reshape(-1, n1, 1, pw)
    b2 = b2.reshape(-1, n2, 1, pw)
    kern = functools.partial(_experts_kernel, blk=blk, nblocks=nblocks, d=d)
    wmap = lambda j, be, ba: (be[j], 0, 0)

    def idx_spec(ahead):
        return pl.BlockSpec((1, 1, nch), lambda j, be, ba: (jnp.minimum(j + ahead, nblocks - 1), 0, 0),
                            memory_space=pltpu.SMEM)

    grid_spec = pltpu.PrefetchScalarGridSpec(
        num_scalar_prefetch=2,
        grid=(nblocks,),
        in_specs=[
            idx_spec(0), idx_spec(1), idx_spec(2),
            pl.BlockSpec(memory_space=pl.ANY),
            pl.BlockSpec((None, d, 2 * f), wmap),
            pl.BlockSpec((None, n1, 1, pw), lambda j, be, ba: (be[j], 0, 0, 0)),
            pl.BlockSpec((None, f, d), wmap),
            pl.BlockSpec((None, n2, 1, pw), lambda j, be, ba: (be[j], 0, 0, 0)),
        ],
        out_specs=pl.BlockSpec((blk, d), lambda j, be, ba: (j, 0)),
        scratch_shapes=[
            pltpu.VMEM((MOE_RING, blk, d + LANES), BF16),
            pltpu.VMEM((n1, blk, LANES), BF16),
            pltpu.VMEM((n1, d, pw), BF16),
            pltpu.VMEM((n2, f, pw), BF16),
            pltpu.SemaphoreType.DMA((MOE_RING,)),
        ],
    )
    return pl.pallas_call(
        kern,
        grid_spec=grid_spec,
        out_shape=jax.ShapeDtypeStruct((nblocks * blk, d), BF16),
        compiler_params=_params(("arbitrary",), 56),
        name="experts",
    )(block_expert, block_active, src3, src3, src3, hs, w1, b1p, w2, b2)


def _combine_kernel(dst_ref, dstn_ref, y_hbm, x1_ref, pos_ref, gate_ref, fg_ref, o_ref, ybuf, sem,
                    *, tm, cap, ntiles):
    i = pl.program_id(0)
    slot = lax.rem(i, 2)
    nch = cap // MOE_CHUNK

    def request(idx_ref, buf_slot):
        def body(c, carry):
            src = pl.multiple_of(idx_ref[0, 0, c] * MOE_CHUNK, MOE_CHUNK)
            dst = pl.multiple_of(c * MOE_CHUNK, MOE_CHUNK)
            pltpu.make_async_copy(y_hbm.at[pl.ds(src, MOE_CHUNK), :],
                                  ybuf.at[buf_slot, pl.ds(dst, MOE_CHUNK), :], sem.at[buf_slot]).start()
            return carry
        lax.fori_loop(0, nch, body, 0, unroll=8)

    @pl.when(i == 0)
    def _():
        request(dst_ref, 0)

    @pl.when(i + 1 < ntiles)
    def _():
        request(dstn_ref, 1 - slot)

    pltpu.make_async_copy(y_hbm.at[pl.ds(0, cap), :], ybuf.at[slot], sem.at[slot]).wait()

    pos = pos_ref[...]
    acc = None
    for c0 in range(0, cap, SORT_ROWS):
        col = lax.broadcasted_iota(I32, (tm, SORT_ROWS), 1) + c0
        pick = jnp.zeros((tm, SORT_ROWS), F32)
        for k in range(TOP_K):
            pick = jnp.where(col == pos[:, k:k + 1], 1.0, pick)
        part = _dot(pick.astype(BF16), ybuf[slot, c0:c0 + SORT_ROWS, :])
        acc = part if acc is None else acc + part
    x2 = x1_ref[...] + gate_ref[...] * acc
    o_ref[...] = _rms(x2) * fg_ref[...]


def _combine(dst3, y, x1, pos_t, gate5, final_g, tm, cap):
    b, t, d = x1.shape
    nt = t // tm
    ntiles = b * nt
    nch = cap // MOE_CHUNK
    kern = functools.partial(_combine_kernel, tm=tm, cap=cap, ntiles=ntiles)
    row_map = lambda i: (i // nt, i % nt, 0)
    return pl.pallas_call(
        kern,
        grid=(ntiles,),
        in_specs=[
            pl.BlockSpec((1, 1, nch), lambda i: (i, 0, 0), memory_space=pltpu.SMEM),
            pl.BlockSpec((1, 1, nch), lambda i: (jnp.minimum(i + 1, ntiles - 1), 0, 0), memory_space=pltpu.SMEM),
            pl.BlockSpec(memory_space=pl.ANY),
            pl.BlockSpec((None, tm, d), row_map),
            pl.BlockSpec((tm, TOP_K), lambda i: (i, 0)),
            pl.BlockSpec((None, 1, d), lambda i: (i // nt, 0, 0)),
            pl.BlockSpec((1, d), lambda i: (0, 0)),
        ],
        out_specs=pl.BlockSpec((None, tm, d), row_map),
        out_shape=jax.ShapeDtypeStruct((b, t, d), F32),
        scratch_shapes=[
            pltpu.VMEM((2, cap, d), BF16),
            pltpu.SemaphoreType.DMA((2,)),
        ],
        compiler_params=_params(("arbitrary",), 48),
        name="combine",
    )(dst3, dst3, y, x1, pos_t, gate5, final_g.reshape(1, d))


def _sorted_rows(tm, ne):
    need = tm * TOP_K + ne * (MOE_CHUNK - 1) + MOE_CHUNK
    return -(-need // SORT_ROWS) * SORT_ROWS


def _pick_tile(n, pref):
    tm = pref
    while n % tm:
        tm //= 2
    return tm


def kernel(x, c, ctx, c_ctx, ada_w, ada_b, norm1_g, norm2_g, w_in, lru_conv_w, lru_conv_b, lru_wa, lru_ba, lru_wx, lru_bx, lru_lam, hg_lb_logits, hg_norm_g, w_branch_a, w_branch_b, w_out, router_w, router_b, moe_w1, moe_b1, moe_w2, moe_b2, final_g):
    b, t, d = x.shape
    cx = ctx.shape[1]
    layer = 0
    w_lru = lru_conv_w.shape[2]
    qk = (w_in.shape[2] - 2 * w_lru - 2 * d) // 5
    heads = qk // HG_DK
    ne = router_w.shape[2]

    pad = (-(b + 1)) % SUBLANES
    c_all = jnp.concatenate([c, c_ctx[None, :], jnp.zeros((pad, d), F32)], axis=0)
    mod = _modulation(c_all, ada_w[layer], ada_b[layer])
    mod_l = mod[:b].reshape(b, N_MOD, 1, d)
    mod_c = mod[b].reshape(N_MOD, 1, 1, d)

    w_in_b = w_in[layer].astype(BF16)
    w_lru_in = w_in_b[:, :2 * w_lru]
    hg0 = 2 * w_lru
    w_m = w_in_b[:, hg0 + 5 * qk:]
    g_blocks = w_lru // LRU_BLOCK
    wg = jnp.concatenate([lru_wa[layer, 0], lru_wx[layer, 0], lru_wa[layer, 1], lru_wx[layer, 1]], axis=-1).astype(BF16)
    bg = jnp.concatenate([lru_ba[layer, 0], lru_bx[layer, 0], lru_ba[layer, 1], lru_bx[layer, 1]], axis=-1)
    bg = bg.reshape(g_blocks, 1, 4 * LRU_BLOCK)
    lb_all = jnp.cumsum(jax.nn.softmax(hg_lb_logits.astype(F32), axis=0), axis=0)
    lb = lb_all[layer].reshape(heads, 1, HG_DK)

    tm_n = _pick_tile(t, 512)
    hl = _prenorm(x, mod_l[:, 0], mod_l[:, 1], norm1_g[layer], tm_n)
    hc = _prenorm(ctx, mod_c[0], mod_c[1], norm1_g[layer], _pick_tile(cx, 256))
    axl = _matmul(hl.reshape(b * t, d), w_lru_in, _pick_tile(b * t, 512), BF16).reshape(b, t, 2 * w_lru)
    axc = _matmul(hc.reshape(b * cx, d), w_lru_in, _pick_tile(b * cx, 512), BF16).reshape(b, cx, 2 * w_lru)
    ya = _lru(axl, axc, lru_conv_w[layer], lru_conv_b[layer], wg, bg, lru_lam[layer], b, cx, t)
    yb = _hgrn(hc, hl, w_in_b, hg0, heads, lb, hg_norm_g[layer])

    rwt = router_w[layer].T
    rb = router_b[layer].reshape(ne, 1)
    tm = _pick_tile(t, 512)
    x1, hs, pos8, cnt = _merge(
        x, hl, ya, yb, w_m, w_branch_a[layer].astype(BF16), w_branch_b[layer].astype(BF16),
        w_out[layer].astype(BF16), mod_l[:, 2], mod_l[:, 3], mod_l[:, 4], norm2_g[layer], rwt, rb, tm)

    blk = MOE_BLOCK
    ch = MOE_CHUNK
    bpc = blk // ch
    tok = b * t
    ntile = tok // tm
    cap = _sorted_rows(tm, ne)
    tch = cap // ch
    run = (cnt[:, :, 0].astype(I32) + ch - 1) // ch
    loc_end = jnp.cumsum(run, axis=1)
    loc_start = loc_end - run
    upto = jnp.cumsum(run, axis=0)
    total = upto[-1]
    padded = (total + bpc - 1) // bpc * bpc
    gend = jnp.cumsum(padded)
    gstart = gend - padded
    where_run = gstart[None, :] + upto - run
    max_chunks = tok * TOP_K // ch + ntile * ne + ne * bpc
    nblocks = -(-max_chunks // bpc)

    eids = jnp.arange(ne, dtype=I32)
    tids = jnp.arange(ntile, dtype=I32)
    g = jnp.arange(nblocks * bpc, dtype=I32)
    g_e = jnp.minimum(jnp.sum(g[:, None] >= gend[None, :], axis=1), ne - 1).astype(I32)
    is_e = g_e[:, None] == eids[None, :]
    off = g - jnp.sum(jnp.where(is_e, gstart[None, :], 0), axis=1)
    g_total = jnp.sum(jnp.where(is_e, total[None, :], 0), axis=1)
    upto_g = jnp.sum(jnp.where(is_e[:, None, :], upto[None, :, :], 0), axis=2)
    g_i = jnp.minimum(jnp.sum(off[:, None] >= upto_g, axis=1), ntile - 1).astype(I32)
    both = (g_i[:, None] == tids[None, :])[:, :, None] & is_e[:, None, :]

    def at_run(table):
        return jnp.sum(jnp.where(both, table[None, :, :], 0), axis=(1, 2))

    within = off - (at_run(upto) - at_run(run))
    zero_chunk = tch - 1
    src = jnp.where((off < g_total) & (g < gend[-1]), g_i * tch + at_run(loc_start) + within, zero_chunk)

    block_first = jnp.arange(nblocks, dtype=I32) * bpc
    block_active = block_first < gend[-1]
    block_expert = jnp.minimum(jnp.sum(block_first[:, None] >= gend[None, :], axis=1), ne - 1).astype(I32)
    last_expert = jnp.max(jnp.where(block_active, block_expert, 0))
    block_expert = jnp.where(block_active, block_expert, last_expert).astype(I32)

    c = jnp.arange(tch, dtype=I32)
    c_e = jnp.minimum(jnp.sum(c[None, :, None] >= loc_end[:, None, :], axis=2), ne - 1).astype(I32)
    is_ce = c_e[:, :, None] == eids[None, None, :]
    back = jnp.sum(jnp.where(is_ce, (where_run - loc_start)[:, None, :], 0), axis=2) + c[None, :]
    back = jnp.where(c[None, :] < loc_end[:, -1:], back, 0)

    b1 = moe_b1[layer]
    f = b1.shape[1] // 2
    b1p = b1.reshape(ne, f // LANES, LANES, 2).transpose(0, 1, 3, 2).reshape(ne, 1, 2 * f)
    y = _experts(block_expert, block_active.astype(I32), src.reshape(nblocks, 1, bpc), hs, moe_w1[layer], b1p,
                 moe_w2[layer], moe_b2[layer].reshape(ne, 1, d), blk)
    return _combine(back.reshape(ntile, 1, tch), y, x1, pos8[:TOP_K].T, mod_l[:, 5], final_g, tm, cap)
```

```python
import functools

import jax
import jax.numpy as jnp
from jax import lax
from jax.experimental import pallas as pl
from jax.experimental.pallas import tpu as pltpu

F32 = jnp.float32
BF16 = jnp.bfloat16
I32 = jnp.int32
HIGHEST = lax.Precision.HIGHEST

EPS = 1e-6
N_MOD = 6
GRID_W = 64
CONV_LEFT = 2
CONV_WIDTH = 4
LRU_C = 8.0
LRU_BLOCK = 128
HG_DK = 128
HG_CHUNK = 64
TOP_K = 4
SWIGLU_LIMIT = 7.0
SWIGLU_ALPHA = 1.702
MOE_BLOCK = 512
MOE_CHUNK = 16
SORT_ROWS = 256
MOE_RING = 3
SUBLANES = 8
LANES = 128
HALO = 8


def _params(sem, vmem_mb):
    return pltpu.CompilerParams(dimension_semantics=sem, vmem_limit_bytes=vmem_mb * 1024 * 1024)


def _dot(a, b):
    return jnp.dot(a, b, preferred_element_type=F32)


def _dot_nt(a, b):
    return lax.dot_general(a, b, (((1,), (1,)), ((), ())), preferred_element_type=F32)


def _dot_tn(a, b):
    return lax.dot_general(a, b, (((0,), (0,)), ((), ())), preferred_element_type=F32)


def _sigmoid(x):
    return 0.5 * jnp.tanh(0.5 * x) + 0.5


def _silu(x):
    return x * _sigmoid(x)


def _rms(x):
    return x * lax.rsqrt(jnp.mean(x * x, axis=-1, keepdims=True) + EPS)


def _mod_kernel(c_ref, w_ref, b_ref, o_ref):
    s = _silu(c_ref[...])
    o_ref[...] = jnp.dot(s, w_ref[...], preferred_element_type=F32, precision=HIGHEST) + b_ref[...]


def _modulation(c_all, ada_w, ada_b):
    m, d = c_all.shape
    n = ada_w.shape[1]
    tn = 1024
    return pl.pallas_call(
        _mod_kernel,
        grid=(n // tn,),
        in_specs=[
            pl.BlockSpec((m, d), lambda j: (0, 0)),
            pl.BlockSpec((d, tn), lambda j: (0, j)),
            pl.BlockSpec((1, tn), lambda j: (0, j)),
        ],
        out_specs=pl.BlockSpec((m, tn), lambda j: (0, j)),
        out_shape=jax.ShapeDtypeStruct((m, n), F32),
        compiler_params=_params(("arbitrary",), 32),
        name="mod",
    )(c_all, ada_w, ada_b.reshape(1, n))


def _norm_proj_kernel(x_ref, shift_ref, scale_ref, g_ref, w_ref, h_ref, o_ref):
    y = _rms(x_ref[...]) * g_ref[...]
    h = (y * (1.0 + scale_ref[...]) + shift_ref[...]).astype(h_ref.dtype)
    h_ref[...] = h
    o_ref[...] = _dot(h, w_ref[...]).astype(o_ref.dtype)


def _norm_proj(x, shift, scale, g, w_in_b, n_out, tm):
    b, l, d = x.shape
    per_batch = shift.shape[0] == b and b > 1
    mod_map = (lambda bi, i: (bi, 0, 0)) if per_batch else (lambda bi, i: (0, 0, 0))
    return pl.pallas_call(
        _norm_proj_kernel,
        grid=(b, l // tm),
        in_specs=[
            pl.BlockSpec((None, tm, d), lambda bi, i: (bi, i, 0)),
            pl.BlockSpec((None, 1, d), mod_map),
            pl.BlockSpec((None, 1, d), mod_map),
            pl.BlockSpec((1, d), lambda bi, i: (0, 0)),
            pl.BlockSpec((d, n_out), lambda bi, i: (0, 0)),
        ],
        out_specs=[
            pl.BlockSpec((None, tm, d), lambda bi, i: (bi, i, 0)),
            pl.BlockSpec((None, tm, n_out), lambda bi, i: (bi, i, 0)),
        ],
        out_shape=[
            jax.ShapeDtypeStruct((b, l, d), BF16),
            jax.ShapeDtypeStruct((b, l, n_out), BF16),
        ],
        compiler_params=_params(("arbitrary", "arbitrary"), 40),
        name="norm_proj",
    )(x, shift, scale, g.reshape(1, d), w_in_b)


def _lru_kernel(axl_ref, agl_ref, axc_ref, cw_ref, cb_ref, wg_ref, bg_ref, lam_ref, o_ref,
                xc_ref, yacc_ref, sx_ref, sa_ref, sb_ref, sh_ref, *, nb, cx, t):
    tb = GRID_W
    ctx0 = HALO
    gw = 2 * LRU_BLOCK
    pre = CONV_LEFT
    steps = pre + tb + (HALO - pre)

    zeros_h = jnp.zeros((HALO, LRU_BLOCK), F32)
    for b in range(nb):
        xc_ref[b, 0:HALO, :] = zeros_h
        xc_ref[b, ctx0:ctx0 + cx, :] = axc_ref[b].astype(F32)
        xc_ref[b, ctx0 + cx:ctx0 + cx + HALO, :] = zeros_h

    cw = cw_ref[...]
    cb = cb_ref[...]
    lam = lam_ref[...]
    sp = jnp.maximum(-lam, 0.0) + jnp.log1p(jnp.exp(-jnp.abs(lam)))
    half_neg_c_sp = (-0.5 * LRU_C) * sp

    def zero_history(d):
        sx_ref[d, 0:pre * nb, :] = jnp.zeros((pre * nb, LRU_BLOCK), F32)
        sx_ref[d, (pre + tb) * nb:steps * nb, :] = jnp.zeros(((steps - pre - tb) * nb, LRU_BLOCK), F32)

    def load_ctx_block(d, base):
        for b in range(nb):
            sx_ref[d, pl.ds(b, steps, stride=nb), :] = xc_ref[b, pl.ds(base - pre, steps), :]

    def load_latent_block(d, r0):
        for b in range(nb):
            sx_ref[d, pl.ds(pre * nb + b, tb, stride=nb), :] = axl_ref[b, pl.ds(r0, tb), :].astype(F32)

    def fill_gates(d):
        u = jnp.broadcast_to(cb, (tb * nb, LRU_BLOCK))
        for k in range(CONV_WIDTH):
            u = u + sx_ref[d, k * nb:(k + tb) * nb, :] * cw[k:k + 1, :]
        th = jnp.tanh(_dot(u.astype(BF16), wg_ref[:, d * gw:(d + 1) * gw]) + bg_ref[:, d * gw:(d + 1) * gw])
        r2 = th[:, :LRU_BLOCK] + 1.0
        i2 = th[:, LRU_BLOCK:] + 1.0
        log_a = r2 * half_neg_c_sp[d:d + 1, :]
        a = jnp.exp(log_a)
        half_mult = jnp.sqrt(jnp.tanh(log_a) * (a * a + 1.0) * (-0.25))
        sa_ref[d] = a
        sb_ref[d] = half_mult * (i2 * u)

    def scan_pair(hf, hr):
        for s in range(tb):
            sr = tb - 1 - s
            hf = sa_ref[0, s * nb:(s + 1) * nb, :] * hf + sb_ref[0, s * nb:(s + 1) * nb, :]
            sh_ref[0, s * nb:(s + 1) * nb, :] = hf
            hr = sa_ref[1, sr * nb:(sr + 1) * nb, :] * hr + sb_ref[1, sr * nb:(sr + 1) * nb, :]
            sh_ref[1, sr * nb:(sr + 1) * nb, :] = hr
        return hf, hr

    n_c, n_l = cx // tb, t // tb

    def ctx_body(j, hs):
        load_ctx_block(0, pl.multiple_of(ctx0 + j * tb, SUBLANES))
        load_ctx_block(1, pl.multiple_of(ctx0 + (n_c - 1 - j) * tb, SUBLANES))
        fill_gates(0)
        fill_gates(1)
        return scan_pair(*hs)

    def emit(d, r0, final):
        for b in range(nb):
            yb = sh_ref[d, pl.ds(b, tb, stride=nb), :]
            if final:
                gate = jax.nn.gelu(agl_ref[b, pl.ds(r0, tb), :].astype(F32))
                o_ref[b, pl.ds(r0, tb), :] = ((yacc_ref[b, pl.ds(r0, tb), :] + yb) * gate).astype(o_ref.dtype)
            else:
                yacc_ref[b, pl.ds(r0, tb), :] = yb

    def lat_body(j, hs, final_f, final_r):
        rf = pl.multiple_of(j * tb, tb)
        rr = pl.multiple_of((n_l - 1 - j) * tb, tb)
        load_latent_block(0, rf)
        load_latent_block(1, rr)
        fill_gates(0)
        fill_gates(1)
        hs = scan_pair(*hs)
        emit(0, rf, final_f)
        emit(1, rr, final_r)
        return hs

    h0 = jnp.zeros((nb, LRU_BLOCK), F32)
    hs = lax.fori_loop(0, n_c, ctx_body, (h0, h0))
    zero_history(0)
    zero_history(1)
    half = n_l // 2
    hs = lax.fori_loop(0, half, functools.partial(lat_body, final_f=False, final_r=False), hs)
    if n_l % 2:
        hs = lat_body(jnp.int32(half), hs, False, True)
    lax.fori_loop(n_l - half, n_l, functools.partial(lat_body, final_f=True, final_r=True), hs)


def _lru(axl, axc, conv_w, conv_b, wg, bg, lam, nb, cx, t):
    w = conv_w.shape[1]
    g = w // LRU_BLOCK
    kern = functools.partial(_lru_kernel, nb=nb, cx=cx, t=t)
    return pl.pallas_call(
        kern,
        grid=(g,),
        in_specs=[
            pl.BlockSpec((nb, t, LRU_BLOCK), lambda j: (0, 0, j)),
            pl.BlockSpec((nb, t, LRU_BLOCK), lambda j: (0, 0, g + j)),
            pl.BlockSpec((nb, cx, LRU_BLOCK), lambda j: (0, 0, j)),
            pl.BlockSpec((CONV_WIDTH, LRU_BLOCK), lambda j: (0, j)),
            pl.BlockSpec((1, LRU_BLOCK), lambda j: (0, j)),
            pl.BlockSpec((None, LRU_BLOCK, 4 * LRU_BLOCK), lambda j: (j, 0, 0)),
            pl.BlockSpec((None, 1, 4 * LRU_BLOCK), lambda j: (j, 0, 0)),
            pl.BlockSpec((2, LRU_BLOCK), lambda j: (0, j)),
        ],
        out_specs=pl.BlockSpec((nb, t, LRU_BLOCK), lambda j: (0, 0, j)),
        out_shape=jax.ShapeDtypeStruct((nb, t, w), BF16),
        scratch_shapes=[
            pltpu.VMEM((nb, cx + 2 * HALO, LRU_BLOCK), F32),
            pltpu.VMEM((nb, t, LRU_BLOCK), F32),
            pltpu.VMEM((2, (GRID_W + HALO) * nb, LRU_BLOCK), F32),
            pltpu.VMEM((2, GRID_W * nb, LRU_BLOCK), F32),
            pltpu.VMEM((2, GRID_W * nb, LRU_BLOCK), F32),
            pltpu.VMEM((2, GRID_W * nb, LRU_BLOCK), F32),
        ],
        compiler_params=_params(("arbitrary",), 56),
        name="lru",
    )(axl, axl, axc, conv_w, conv_b.reshape(1, w), wg, bg, lam)


def _hgrn_kernel(hc_ref, hl_ref, wq_ref, wff_ref, wfb_ref, wv_ref, wg_ref, lb_ref, ng_ref, o_ref,
                 p_ref, sprev_s, q_s, v_s, k_s, hl_s, gc_s, bk_s, sc_s, aq_s, upd_ref, dec_ref,
                 *, cx, t):
    c = HG_CHUNK
    dk = HG_DK
    n_c, n_l = cx // c, t // c
    n_all = n_c + n_l

    w = jnp.concatenate([wq_ref[...], wff_ref[...], wfb_ref[...], wv_ref[...], wg_ref[...]], axis=1)
    p_ref[0:cx, :] = _dot(hc_ref[...], w)
    p_ref[cx:cx + t, :] = _dot(hl_ref[...], w)

    lb = lb_ref[...]
    f_half = 0.5 * (1.0 - lb)
    f_mid = lb + f_half
    ri = lax.broadcasted_iota(I32, (c, c), 0)
    ci = lax.broadcasted_iota(I32, (c, c), 1)
    keep = (ri >= ci, ci >= ri)
    mid = (c // 2 - 1, c // 2)
    end = (c - 1, 0)

    def stage_gates(j, carry):
        r0 = pl.multiple_of(j * c, c)
        q_s[pl.ds(r0, c), :] = _silu(p_ref[pl.ds(r0, c), 0:dk])
        v_s[pl.ds(r0, c), :] = p_ref[pl.ds(r0, c), 3 * dk:4 * dk].astype(BF16)
        for d in range(2):
            f = f_mid + f_half * jnp.tanh(0.5 * p_ref[pl.ds(r0, c), (1 + d) * dk:(2 + d) * dk])
            logf = jnp.log(f)
            hi = logf.astype(BF16)
            lo = (logf - hi.astype(F32)).astype(BF16)
            k_s[d, pl.ds(r0, c), :] = 1.0 - f
            hl_s[d, pl.ds(r0, c), :] = jnp.concatenate([hi, lo], axis=1)
        return carry

    lax.fori_loop(0, n_all, stage_gates, 0, unroll=4)

    def stage_cumsum(j, carry):
        r0 = pl.multiple_of(j * c, c)
        for d in range(2):
            s2 = _dot(keep[d].astype(BF16), hl_s[d, pl.ds(r0, c), :])
            gc_s[d, pl.ds(r0, c), :] = s2[:, :dk] + s2[:, dk:]
        return carry

    lax.fori_loop(0, n_all, stage_cumsum, 0, unroll=12)

    def stage_scores(j, carry):
        r0 = pl.multiple_of(j * c, c)
        q = q_s[pl.ds(r0, c), :]
        scores = None
        for d in range(2):
            gc = gc_s[d, pl.ds(r0, c), :]
            k = k_s[d, pl.ds(r0, c), :]
            g_mid = gc[mid[d]:mid[d] + 1, :]
            g_end = gc[end[d]:end[d] + 1, :]
            qa = q * jnp.exp(gc - g_mid)
            kb = k * jnp.exp(g_mid - gc)
            sc = jnp.where(keep[d], _dot_nt(qa.astype(BF16), kb.astype(BF16)), 0.0)
            scores = sc if scores is None else scores + sc
            bk_s[d, pl.ds(r0, c), :] = (kb * jnp.exp(g_end - g_mid)).astype(BF16)
            aq_s[pl.ds(r0, c), d * dk:(d + 1) * dk] = (qa * jnp.exp(g_mid)).astype(BF16)
            dec_ref[d, j] = jnp.broadcast_to(jnp.exp(g_end), (SUBLANES, dk))
        sc_s[pl.ds(r0, c), :] = scores.astype(BF16)
        return carry

    lax.fori_loop(0, n_all, stage_scores, 0, unroll=6)

    def stage_updates(j, carry):
        r0 = pl.multiple_of(j * c, c)
        v = v_s[pl.ds(r0, c), :]
        for d in range(2):
            upd_ref[d, j] = _dot_tn(v, bk_s[d, pl.ds(r0, c), :])
        return carry

    lax.fori_loop(0, n_all, stage_updates, 0, unroll=12)

    def advance(d, jj, st):
        return st * dec_ref[d, jj][0:1, :] + upd_ref[d, jj]

    def ctx_body(j, sts):
        return advance(0, j, sts[0]), advance(1, n_c - 1 - j, sts[1])

    def lat_body(j, sts):
        jf, jr = j, n_l - 1 - j
        sprev_s[jf, :, 0:dk] = sts[0].astype(BF16)
        sprev_s[jr, :, dk:2 * dk] = sts[1].astype(BF16)
        return advance(0, n_c + jf, sts[0]), advance(1, n_c + jr, sts[1])

    zero = jnp.zeros((dk, dk), F32)
    sts = lax.fori_loop(0, n_c, ctx_body, (zero, zero), unroll=2)
    lax.fori_loop(0, n_l, lat_body, sts, unroll=4)

    ng = ng_ref[...]

    def stage_outputs(j, carry):
        r0 = pl.multiple_of(j * c, c)
        rc = pl.multiple_of(cx + r0, c)
        o = _dot(sc_s[pl.ds(rc, c), :], v_s[pl.ds(rc, c), :]) + _dot_nt(aq_s[pl.ds(rc, c), :], sprev_s[j])
        g = p_ref[pl.ds(rc, c), 4 * dk:5 * dk]
        o_ref[pl.ds(r0, c), :] = (_rms(o) * ng * _silu(g)).astype(o_ref.dtype)
        return carry

    lax.fori_loop(0, n_l, stage_outputs, 0, unroll=16)


def _hgrn(hc, hl, w_in_b, col0, heads, lb, norm_g):
    b, cx, d = hc.shape
    t = hl.shape[1]
    dk = HG_DK
    n_all = (cx + t) // HG_CHUNK
    kern = functools.partial(_hgrn_kernel, cx=cx, t=t)

    def group(gi):
        return pl.BlockSpec((d, dk), lambda bi, h: (0, col0 // dk + gi * heads + h))

    return pl.pallas_call(
        kern,
        grid=(b, heads),
        in_specs=[
            pl.BlockSpec((None, cx, d), lambda bi, h: (bi, 0, 0)),
            pl.BlockSpec((None, t, d), lambda bi, h: (bi, 0, 0)),
            group(0), group(1), group(2), group(3), group(4),
            pl.BlockSpec((None, 1, dk), lambda bi, h: (h, 0, 0)),
            pl.BlockSpec((1, dk), lambda bi, h: (0, 0)),
        ],
        out_specs=pl.BlockSpec((None, t, dk), lambda bi, h: (bi, 0, h)),
        out_shape=jax.ShapeDtypeStruct((b, t, heads * dk), BF16),
        scratch_shapes=[
            pltpu.VMEM((cx + t, 5 * dk), F32),
            pltpu.VMEM((t // HG_CHUNK, dk, 2 * dk), BF16),
            pltpu.VMEM((cx + t, dk), F32),
            pltpu.VMEM((cx + t, dk), BF16),
            pltpu.VMEM((2, cx + t, dk), F32),
            pltpu.VMEM((2, cx + t, 2 * dk), BF16),
            pltpu.VMEM((2, cx + t, dk), F32),
            pltpu.VMEM((2, cx + t, dk), BF16),
            pltpu.VMEM((cx + t, HG_CHUNK), BF16),
            pltpu.VMEM((cx + t, 2 * dk), BF16),
            pltpu.VMEM((2, n_all, dk, dk), F32),
            pltpu.VMEM((2, n_all, SUBLANES, dk), F32),
        ],
        compiler_params=_params(("arbitrary", "arbitrary"), 48),
        name="hgrn",
    )(hc, hl, w_in_b, w_in_b, w_in_b, w_in_b, w_in_b, lb, norm_g.reshape(1, dk))


def _merge_kernel(x_ref, h_ref, ya_ref, yb_ref, wm_ref, wa_ref, wb_ref, wo_ref,
                  gate_ref, shift_ref, scale_ref, g2_ref, rwt_ref, rb_ref,
                  x1_ref, hs_ref, pos_ref, cnt_ref, h2b_s, *, tm, d, ne, cap):
    m = _dot(h_ref[...], wm_ref[...])
    za = _dot(ya_ref[...], wa_ref[...])
    zb = _dot(yb_ref[...], wb_ref[...])
    mix = _sigmoid(m[:, :d]) * za + _sigmoid(m[:, d:]) * zb
    out = _dot(mix.astype(BF16), wo_ref[...])
    x1 = x_ref[...] + gate_ref[...] * out
    x1_ref[...] = x1
    h2 = _rms(x1) * g2_ref[...] * (1.0 + scale_ref[...]) + shift_ref[...]
    h2b_s[...] = h2.astype(BF16)

    logits = lax.dot_general(rwt_ref[...], h2, (((1,), (1,)), ((), ())),
                             preferred_element_type=F32, precision=HIGHEST) + rb_ref[...]
    eid = lax.broadcasted_iota(I32, (ne, tm), 0)
    neg = jnp.float32(-jnp.inf)
    work = logits
    hot = jnp.zeros((ne, tm), F32)
    vals, idxs, sels = [], [], []
    for _ in range(TOP_K):
        mx = jnp.max(work, axis=0, keepdims=True)
        ix = jnp.min(jnp.where(work == mx, eid, ne), axis=0, keepdims=True)
        sel = eid == ix
        work = jnp.where(sel, neg, work)
        hot = hot + sel.astype(F32)
        vals.append(mx)
        idxs.append(ix)
        sels.append(sel)
    ex = [jnp.exp(v - vals[0]) for v in vals]
    den = ex[0] + ex[1] + ex[2] + ex[3]
    gts = [e / den for e in ex]

    ti = lax.broadcasted_iota(I32, (tm, tm), 0)
    tj = lax.broadcasted_iota(I32, (tm, tm), 1)
    strict = (ti < tj).astype(BF16)
    before = _dot(hot.astype(BF16), strict)
    cnt = jnp.sum(hot, axis=1, keepdims=True)
    nchunks = jnp.ceil(cnt * (1.0 / MOE_CHUNK))
    ei = lax.broadcasted_iota(I32, (ne, ne), 0)
    ej = lax.broadcasted_iota(I32, (ne, ne), 1)
    earlier = (ej < ei).astype(BF16)
    start = _dot(earlier, jnp.broadcast_to(nchunks, (ne, LANES)).astype(BF16))[:, 0:1] * MOE_CHUNK
    where_to = before + start
    poss = [jnp.sum(jnp.where(s, where_to, 0.0), axis=0, keepdims=True).astype(I32) for s in sels]

    rb_rows = SORT_ROWS
    lane = lax.broadcasted_iota(I32, (rb_rows, LANES), 1)

    def sort_rows(i, carry):
        r0 = pl.multiple_of(i * rb_rows, rb_rows)
        pid = lax.broadcasted_iota(I32, (rb_rows, tm), 0)
        gsel = jnp.zeros((rb_rows, tm), F32)
        for k in range(TOP_K):
            gsel = jnp.where(pid == poss[k] - r0, gts[k], gsel)
        place = jnp.where(gsel != 0.0, 1.0, 0.0)
        hs_ref[pl.ds(r0, rb_rows), 0:d] = _dot(place.astype(BF16), h2b_s[...]).astype(BF16)
        g = jnp.sum(gsel, axis=1, keepdims=True)
        g_hi = g.astype(BF16).astype(F32)
        hs_ref[pl.ds(r0, rb_rows), d:d + LANES] = jnp.where(lane < LANES // 2, g_hi, g - g_hi).astype(BF16)
        return carry

    lax.fori_loop(0, cap // rb_rows, sort_rows, 0)

    row = lax.broadcasted_iota(I32, (SUBLANES, tm), 0)
    packed = jnp.zeros((SUBLANES, tm), I32)
    for k, p in enumerate(poss):
        packed = jnp.where(row == k, jnp.broadcast_to(p, (SUBLANES, tm)), packed)
    pos_ref[...] = packed
    cnt_ref[...] = jnp.broadcast_to(cnt, cnt_ref.shape)


def _merge(x, hl, ya, yb, w_m, w_ba, w_bb, w_out, gate2, shift3, scale4, g2, rwt, rb, tm):
    b, t, d = x.shape
    ne = rwt.shape[0]
    nt = t // tm
    tok = b * t
    cap = _sorted_rows(tm, ne)
    kern = functools.partial(_merge_kernel, tm=tm, d=d, ne=ne, cap=cap)
    row_spec = pl.BlockSpec((None, tm, d), lambda bi, i: (bi, i, 0))
    mod_spec = pl.BlockSpec((None, 1, d), lambda bi, i: (bi, 0, 0))
    flat = lambda bi, i: (bi * nt + i, 0)
    lane = lambda bi, i: (0, bi * nt + i)

    def whole(shape):
        return pl.BlockSpec(shape, lambda bi, i: (0,) * len(shape))

    return pl.pallas_call(
        kern,
        grid=(b, nt),
        in_specs=[
            row_spec, row_spec, row_spec, row_spec,
            whole((d, 2 * d)), whole((d, d)), whole((d, d)), whole((d, d)),
            mod_spec, mod_spec, mod_spec,
            whole((1, d)), whole((ne, d)), whole((ne, 1)),
        ],
        out_specs=[
            row_spec,
            pl.BlockSpec((cap, d + LANES), flat),
            pl.BlockSpec((SUBLANES, tm), lane),
            pl.BlockSpec((None, ne, LANES), lambda bi, i: (bi * nt + i, 0, 0)),
        ],
        out_shape=[
            jax.ShapeDtypeStruct((b, t, d), F32),
            jax.ShapeDtypeStruct((b * nt * cap, d + LANES), BF16),
            jax.ShapeDtypeStruct((SUBLANES, tok), I32),
            jax.ShapeDtypeStruct((b * nt, ne, LANES), F32),
        ],
        scratch_shapes=[pltpu.VMEM((tm, d), BF16)],
        compiler_params=_params(("arbitrary", "arbitrary"), 56),
        name="merge",
    )(x, hl, ya, yb, w_m, w_ba, w_bb, w_out, gate2, shift3, scale4, g2.reshape(1, d), rwt, rb)


def _experts_kernel(bexp_ref, bact_ref, src_ref, src1_ref, src2_ref, hs_hbm, w1_ref, b1_ref, w2_ref, b2_ref,
                    y_ref, xbuf, act_s, w1p_s, w2_s, gsem, *, blk, nblocks, d):
    j = pl.program_id(0)
    slot = lax.rem(j, MOE_RING)
    far = lax.rem(j + 2, MOE_RING)
    f = w2_s.shape[1]
    pw = 2 * LANES
    n1 = 2 * f // pw
    n2 = d // pw
    nch = blk // MOE_CHUNK
    active = bact_ref[j] == 1

    def request(idx_ref, buf_slot):
        for c in range(nch):
            src = pl.multiple_of(idx_ref[0, 0, c] * MOE_CHUNK, MOE_CHUNK)
            pltpu.make_async_copy(hs_hbm.at[pl.ds(src, MOE_CHUNK), :],
                                  xbuf.at[buf_slot, pl.ds(c * MOE_CHUNK, MOE_CHUNK), :], gsem.at[buf_slot]).start()

    @pl.when(j == 0)
    def _():
        request(src_ref, 0)
        request(src1_ref, 1)

    @pl.when(j + 2 < nblocks)
    def _():
        request(src2_ref, far)

    pltpu.make_async_copy(hs_hbm.at[pl.ds(0, blk), :], xbuf.at[slot], gsem.at[slot]).wait()

    new_expert = (j == 0) | (bexp_ref[j] != bexp_ref[jnp.maximum(j - 1, 0)])

    @pl.when(new_expert & active)
    def _():
        src = lax.broadcasted_iota(I32, (pw, pw), 0)
        dst = lax.broadcasted_iota(I32, (pw, pw), 1)
        want = jnp.where(dst < pw // 2, 2 * dst, 2 * (dst - pw // 2) + 1)
        sel = (src == want).astype(BF16)
        for cb in range(n1):
            w1p_s[cb] = _dot(w1_ref[:, cb * pw:(cb + 1) * pw].astype(BF16), sel).astype(BF16)
        for cb in range(n2):
            w2_s[cb] = w2_ref[:, cb * pw:(cb + 1) * pw].astype(BF16)

    @pl.when(active)
    def _():
        x = xbuf[slot, :, 0:d]
        gate = (xbuf[slot, :, d:d + 1].astype(F32) + xbuf[slot, :, d + LANES // 2:d + LANES // 2 + 1].astype(F32))
        for n in range(n1):
            h = _dot(x, w1p_s[n]) + b1_ref[n]
            hg = jnp.minimum(h[:, :LANES], SWIGLU_LIMIT)
            hu = jnp.clip(h[:, LANES:], -SWIGLU_LIMIT, SWIGLU_LIMIT)
            act_s[n] = (hg * _sigmoid(SWIGLU_ALPHA * hg) * (hu + 1.0)).astype(BF16)
        act = jnp.concatenate([act_s[i] for i in range(n1)], axis=1)
        for n in range(n2):
            y = _dot(act, w2_s[n]) + b2_ref[n]
            y_ref[:, n * pw:(n + 1) * pw] = (y * gate).astype(y_ref.dtype)

    @pl.when(jnp.logical_not(active))
    def _():
        y_ref[...] = jnp.zeros_like(y_ref)


def _experts(block_expert, block_active, src3, hs, w1, b1p, w2, b2, blk):
    nblocks = block_expert.shape[0]
    f, d = w2.shape[1], w2.shape[2]
    pw = 2 * LANES
    n1, n2 = 2 * f // pw, d // pw
    nch = blk // MOE_CHUNK
    b1p = b1p.reshape(-1, n1, 1, pw)
    b2 = b2.reshape(-1, n2, 1, pw)
    kern = functools.partial(_experts_kernel, blk=blk, nblocks=nblocks, d=d)
    wmap = lambda j, be, ba: (be[j], 0, 0)

    def idx_spec(ahead):
        return pl.BlockSpec((1, 1, nch), lambda j, be, ba: (jnp.minimum(j + ahead, nblocks - 1), 0, 0),
                            memory_space=pltpu.SMEM)

    grid_spec = pltpu.PrefetchScalarGridSpec(
        num_scalar_prefetch=2,
        grid=(nblocks,),
        in_specs=[
            idx_spec(0), idx_spec(1), idx_spec(2),
            pl.BlockSpec(memory_space=pl.ANY),
            pl.BlockSpec((None, d, 2 * f), wmap),
            pl.BlockSpec((None, n1, 1, pw), lambda j, be, ba: (be[j], 0, 0, 0)),
            pl.BlockSpec((None, f, d), wmap),
            pl.BlockSpec((None, n2, 1, pw), lambda j, be, ba: (be[j], 0, 0, 0)),
        ],
        out_specs=pl.BlockSpec((blk, d), lambda j, be, ba: (j, 0)),
        scratch_shapes=[
            pltpu.VMEM((MOE_RING, blk, d + LANES), BF16),
            pltpu.VMEM((n1, blk, LANES), BF16),
            pltpu.VMEM((n1, d, pw), BF16),
            pltpu.VMEM((n2, f, pw), BF16),
            pltpu.SemaphoreType.DMA((MOE_RING,)),
        ],
    )
    return pl.pallas_call(
        kern,
        grid_spec=grid_spec,
        out_shape=jax.ShapeDtypeStruct((nblocks * blk, d), BF16),
        compiler_params=_params(("arbitrary",), 56),
        name="experts",
    )(block_expert, block_active, src3, src3, src3, hs, w1, b1p, w2, b2)


def _combine_kernel(dst_ref, dstn_ref, y_hbm, x1_ref, pos_ref, gate_ref, fg_ref, o_ref, ybuf, sem,
                    *, tm, cap, ntiles):
    i = pl.program_id(0)
    slot = lax.rem(i, 2)
    nch = cap // MOE_CHUNK

    def request(idx_ref, buf_slot):
        def body(c, carry):
            src = pl.multiple_of(idx_ref[0, 0, c] * MOE_CHUNK, MOE_CHUNK)
            dst = pl.multiple_of(c * MOE_CHUNK, MOE_CHUNK)
            pltpu.make_async_copy(y_hbm.at[pl.ds(src, MOE_CHUNK), :],
                                  ybuf.at[buf_slot, pl.ds(dst, MOE_CHUNK), :], sem.at[buf_slot]).start()
            return carry
        lax.fori_loop(0, nch, body, 0, unroll=8)

    @pl.when(i == 0)
    def _():
        request(dst_ref, 0)

    @pl.when(i + 1 < ntiles)
    def _():
        request(dstn_ref, 1 - slot)

    pltpu.make_async_copy(y_hbm.at[pl.ds(0, cap), :], ybuf.at[slot], sem.at[slot]).wait()

    pos = pos_ref[...]
    acc = None
    for c0 in range(0, cap, SORT_ROWS):
        col = lax.broadcasted_iota(I32, (tm, SORT_ROWS), 1) + c0
        pick = jnp.zeros((tm, SORT_ROWS), F32)
        for k in range(TOP_K):
            pick = jnp.where(col == pos[:, k:k + 1], 1.0, pick)
        part = _dot(pick.astype(BF16), ybuf[slot, c0:c0 + SORT_ROWS, :])
        acc = part if acc is None else acc + part
    x2 = x1_ref[...] + gate_ref[...] * acc
    o_ref[...] = _rms(x2) * fg_ref[...]


def _combine(dst3, y, x1, pos_t, gate5, final_g, tm, cap):
    b, t, d = x1.shape
    nt = t // tm
    ntiles = b * nt
    nch = cap // MOE_CHUNK
    kern = functools.partial(_combine_kernel, tm=tm, cap=cap, ntiles=ntiles)
    row_map = lambda i: (i // nt, i % nt, 0)
    return pl.pallas_call(
        kern,
        grid=(ntiles,),
        in_specs=[
            pl.BlockSpec((1, 1, nch), lambda i: (i, 0, 0), memory_space=pltpu.SMEM),
            pl.BlockSpec((1, 1, nch), lambda i: (jnp.minimum(i + 1, ntiles - 1), 0, 0), memory_space=pltpu.SMEM),
            pl.BlockSpec(memory_space=pl.ANY),
            pl.BlockSpec((None, tm, d), row_map),
            pl.BlockSpec((tm, TOP_K), lambda i: (i, 0)),
            pl.BlockSpec((None, 1, d), lambda i: (i // nt, 0, 0)),
            pl.BlockSpec((1, d), lambda i: (0, 0)),
        ],
        out_specs=pl.BlockSpec((None, tm, d), row_map),
        out_shape=jax.ShapeDtypeStruct((b, t, d), F32),
        scratch_shapes=[
            pltpu.VMEM((2, cap, d), BF16),
            pltpu.SemaphoreType.DMA((2,)),
        ],
        compiler_params=_params(("arbitrary",), 48),
        name="combine",
    )(dst3, dst3, y, x1, pos_t, gate5, final_g.reshape(1, d))


def _sorted_rows(tm, ne):
    need = tm * TOP_K + ne * (MOE_CHUNK - 1) + MOE_CHUNK
    return -(-need // SORT_ROWS) * SORT_ROWS


def _pick_tile(n, pref):
    tm = pref
    while n % tm:
        tm //= 2
    return tm


def kernel(x, c, ctx, c_ctx, ada_w, ada_b, norm1_g, norm2_g, w_in, lru_conv_w, lru_conv_b, lru_wa, lru_ba, lru_wx, lru_bx, lru_lam, hg_lb_logits, hg_norm_g, w_branch_a, w_branch_b, w_out, router_w, router_b, moe_w1, moe_b1, moe_w2, moe_b2, final_g):
    b, t, d = x.shape
    cx = ctx.shape[1]
    layer = 0
    w_lru = lru_conv_w.shape[2]
    qk = (w_in.shape[2] - 2 * w_lru - 2 * d) // 5
    heads = qk // HG_DK
    ne = router_w.shape[2]

    pad = (-(b + 1)) % SUBLANES
    c_all = jnp.concatenate([c, c_ctx[None, :], jnp.zeros((pad, d), F32)], axis=0)
    mod = _modulation(c_all, ada_w[layer], ada_b[layer])
    mod_l = mod[:b].reshape(b, N_MOD, 1, d)
    mod_c = mod[b].reshape(N_MOD, 1, 1, d)

    w_in_b = w_in[layer].astype(BF16)
    hg0 = 2 * w_lru
    w_m = w_in_b[:, hg0 + 5 * qk:]
    g_blocks = w_lru // LRU_BLOCK
    wg = jnp.concatenate([lru_wa[layer, 0], lru_wx[layer, 0], lru_wa[layer, 1], lru_wx[layer, 1]], axis=-1)
    wg = (0.5 * wg).astype(BF16)
    bg = jnp.concatenate([lru_ba[layer, 0], lru_bx[layer, 0], lru_ba[layer, 1], lru_bx[layer, 1]], axis=-1)
    bg = (0.5 * bg).reshape(g_blocks, 1, 4 * LRU_BLOCK)
    lb_all = jnp.cumsum(jax.nn.softmax(hg_lb_logits.astype(F32), axis=0), axis=0)
    lb = lb_all[layer].reshape(heads, 1, HG_DK)

    tm_n = _pick_tile(t, 512)
    hl, axl = _norm_proj(x, mod_l[:, 0], mod_l[:, 1], norm1_g[layer], w_in_b, 2 * w_lru, tm_n)
    hc, axc = _norm_proj(ctx, mod_c[0], mod_c[1], norm1_g[layer], w_in_b, 2 * w_lru, _pick_tile(cx, 256))
    ya = _lru(axl, axc, lru_conv_w[layer], lru_conv_b[layer], wg, bg, lru_lam[layer], b, cx, t)
    yb = _hgrn(hc, hl, w_in_b, hg0, heads, lb, hg_norm_g[layer])

    rwt = router_w[layer].T
    rb = router_b[layer].reshape(ne, 1)
    tm = _pick_tile(t, 512)
    x1, hs, pos8, cnt = _merge(
        x, hl, ya, yb, w_m, w_branch_a[layer].astype(BF16), w_branch_b[layer].astype(BF16),
        w_out[layer].astype(BF16), mod_l[:, 2], mod_l[:, 3], mod_l[:, 4], norm2_g[layer], rwt, rb, tm)

    blk = MOE_BLOCK
    ch = MOE_CHUNK
    bpc = blk // ch
    tok = b * t
    ntile = tok // tm
    cap = _sorted_rows(tm, ne)
    tch = cap // ch
    run = (cnt[:, :, 0].astype(I32) + ch - 1) // ch
    loc_end = jnp.cumsum(run, axis=1)
    loc_start = loc_end - run
    upto = jnp.cumsum(run, axis=0)
    total = upto[-1]
    padded = (total + bpc - 1) // bpc * bpc
    gend = jnp.cumsum(padded)
    gstart = gend - padded
    where_run = gstart[None, :] + upto - run
    max_chunks = tok * TOP_K // ch + ntile * ne + ne * bpc
    nblocks = -(-max_chunks // bpc)

    eids = jnp.arange(ne, dtype=I32)
    tids = jnp.arange(ntile, dtype=I32)
    g = jnp.arange(nblocks * bpc, dtype=I32)
    g_e = jnp.minimum(jnp.sum(g[:, None] >= gend[None, :], axis=1), ne - 1).astype(I32)
    is_e = g_e[:, None] == eids[None, :]
    off = g - jnp.sum(jnp.where(is_e, gstart[None, :], 0), axis=1)
    g_total = jnp.sum(jnp.where(is_e, total[None, :], 0), axis=1)
    upto_g = jnp.sum(jnp.where(is_e[:, None, :], upto[None, :, :], 0), axis=2)
    g_i = jnp.minimum(jnp.sum(off[:, None] >= upto_g, axis=1), ntile - 1).astype(I32)
    both = (g_i[:, None] == tids[None, :])[:, :, None] & is_e[:, None, :]

    def at_run(table):
        return jnp.sum(jnp.where(both, table[None, :, :], 0), axis=(1, 2))

    within = off - (at_run(upto) - at_run(run))
    zero_chunk = tch - 1
    src = jnp.where((off < g_total) & (g < gend[-1]), g_i * tch + at_run(loc_start) + within, zero_chunk)

    block_first = jnp.arange(nblocks, dtype=I32) * bpc
    block_active = block_first < gend[-1]
    block_expert = jnp.minimum(jnp.sum(block_first[:, None] >= gend[None, :], axis=1), ne - 1).astype(I32)
    last_expert = jnp.max(jnp.where(block_active, block_expert, 0))
    block_expert = jnp.where(block_active, block_expert, last_expert).astype(I32)

    c = jnp.arange(tch, dtype=I32)
    c_e = jnp.minimum(jnp.sum(c[None, :, None] >= loc_end[:, None, :], axis=2), ne - 1).astype(I32)
    is_ce = c_e[:, :, None] == eids[None, None, :]
    back = jnp.sum(jnp.where(is_ce, (where_run - loc_start)[:, None, :], 0), axis=2) + c[None, :]
    back = jnp.where(c[None, :] < loc_end[:, -1:], back, 0)

    b1 = moe_b1[layer]
    f = b1.shape[1] // 2
    b1p = b1.reshape(ne, f // LANES, LANES, 2).transpose(0, 1, 3, 2).reshape(ne, 1, 2 * f)
    y = _experts(block_expert, block_active.astype(I32), src.reshape(nblocks, 1, bpc), hs, moe_w1[layer], b1p,
                 moe_w2[layer], moe_b2[layer].reshape(ne, 1, d), blk)
    return _combine(back.reshape(ntile, 1, tch), y, x1, pos8[:TOP_K].T, mod_l[:, 5], final_g, tm, cap)
```

```python
import functools

import jax
import jax.numpy as jnp
from jax import lax
from jax.experimental import pallas as pl
from jax.experimental.pallas import tpu as pltpu

F32 = jnp.float32
BF16 = jnp.bfloat16
I32 = jnp.int32
HIGHEST = lax.Precision.HIGHEST

EPS = 1e-6
N_MOD = 6
GRID_W = 64
CONV_LEFT = 2
CONV_WIDTH = 4
LRU_C = 8.0
LRU_BLOCK = 128
HG_DK = 128
HG_CHUNK = 64
TOP_K = 4
SWIGLU_LIMIT = 7.0
SWIGLU_ALPHA = 1.702
MOE_BLOCK = 512
MOE_CHUNK = 16
SORT_ROWS = 512
MOE_RING = 3
SUBLANES = 8
LANES = 128
HALO = 8


def _params(sem, vmem_mb):
    return pltpu.CompilerParams(dimension_semantics=sem, vmem_limit_bytes=vmem_mb * 1024 * 1024)


def _dot(a, b):
    return jnp.dot(a, b, preferred_element_type=F32)


def _dot_nt(a, b):
    return lax.dot_general(a, b, (((1,), (1,)), ((), ())), preferred_element_type=F32)


def _dot_tn(a, b):
    return lax.dot_general(a, b, (((0,), (0,)), ((), ())), preferred_element_type=F32)


def _sigmoid(x):
    return 0.5 * jnp.tanh(0.5 * x) + 0.5


def _silu(x):
    return x * _sigmoid(x)


def _rms(x):
    return x * lax.rsqrt(jnp.mean(x * x, axis=-1, keepdims=True) + EPS)


def _mod_kernel(c_ref, w_ref, b_ref, o_ref):
    s = _silu(c_ref[...])
    o_ref[...] = jnp.dot(s, w_ref[...], preferred_element_type=F32, precision=HIGHEST) + b_ref[...]


def _modulation(c_all, ada_w, ada_b):
    m, d = c_all.shape
    n = ada_w.shape[1]
    tn = 1024
    return pl.pallas_call(
        _mod_kernel,
        grid=(n // tn,),
        in_specs=[
            pl.BlockSpec((m, d), lambda j: (0, 0)),
            pl.BlockSpec((d, tn), lambda j: (0, j)),
            pl.BlockSpec((1, tn), lambda j: (0, j)),
        ],
        out_specs=pl.BlockSpec((m, tn), lambda j: (0, j)),
        out_shape=jax.ShapeDtypeStruct((m, n), F32),
        compiler_params=_params(("arbitrary",), 32),
        name="mod",
    )(c_all, ada_w, ada_b.reshape(1, n))


def _norm_proj_kernel(x_ref, shift_ref, scale_ref, g_ref, w_ref, h_ref, o_ref):
    y = _rms(x_ref[...]) * g_ref[...]
    h = (y * (1.0 + scale_ref[...]) + shift_ref[...]).astype(h_ref.dtype)
    h_ref[...] = h
    o_ref[...] = _dot(h, w_ref[...]).astype(o_ref.dtype)


def _norm_proj(x, shift, scale, g, w_in_b, n_out, tm):
    b, l, d = x.shape
    per_batch = shift.shape[0] == b and b > 1
    mod_map = (lambda bi, i: (bi, 0, 0)) if per_batch else (lambda bi, i: (0, 0, 0))
    return pl.pallas_call(
        _norm_proj_kernel,
        grid=(b, l // tm),
        in_specs=[
            pl.BlockSpec((None, tm, d), lambda bi, i: (bi, i, 0)),
            pl.BlockSpec((None, 1, d), mod_map),
            pl.BlockSpec((None, 1, d), mod_map),
            pl.BlockSpec((1, d), lambda bi, i: (0, 0)),
            pl.BlockSpec((d, n_out), lambda bi, i: (0, 0)),
        ],
        out_specs=[
            pl.BlockSpec((None, tm, d), lambda bi, i: (bi, i, 0)),
            pl.BlockSpec((None, tm, n_out), lambda bi, i: (bi, i, 0)),
        ],
        out_shape=[
            jax.ShapeDtypeStruct((b, l, d), BF16),
            jax.ShapeDtypeStruct((b, l, n_out), BF16),
        ],
        compiler_params=_params(("arbitrary", "arbitrary"), 40),
        name="norm_proj",
    )(x, shift, scale, g.reshape(1, d), w_in_b)


def _lru_kernel(axl_ref, agl_ref, axc_ref, cw_ref, cb_ref, wg_ref, bg_ref, lam_ref, o_ref,
                xc_ref, yacc_ref, sx_ref, sa_ref, sb_ref, sh_ref, *, nb, cx, t):
    tb = GRID_W
    ctx0 = HALO
    gw = 2 * LRU_BLOCK
    pre = CONV_LEFT
    steps = pre + tb + (HALO - pre)

    zeros_h = jnp.zeros((HALO, LRU_BLOCK), F32)
    for b in range(nb):
        xc_ref[b, 0:HALO, :] = zeros_h
        xc_ref[b, ctx0:ctx0 + cx, :] = axc_ref[b].astype(F32)
        xc_ref[b, ctx0 + cx:ctx0 + cx + HALO, :] = zeros_h

    cw = cw_ref[...]
    cb = cb_ref[...]
    lam = lam_ref[...]
    sp = jnp.maximum(-lam, 0.0) + jnp.log1p(jnp.exp(-jnp.abs(lam)))
    half_neg_c_sp = (-0.5 * LRU_C) * sp

    def zero_history(d):
        sx_ref[d, 0:pre * nb, :] = jnp.zeros((pre * nb, LRU_BLOCK), F32)
        sx_ref[d, (pre + tb) * nb:steps * nb, :] = jnp.zeros(((steps - pre - tb) * nb, LRU_BLOCK), F32)

    def load_ctx_block(d, base):
        for b in range(nb):
            sx_ref[d, pl.ds(b, steps, stride=nb), :] = xc_ref[b, pl.ds(base - pre, steps), :]

    def load_latent_block(d, r0):
        for b in range(nb):
            sx_ref[d, pl.ds(pre * nb + b, tb, stride=nb), :] = axl_ref[b, pl.ds(r0, tb), :].astype(F32)

    def fill_gates(d):
        u = jnp.broadcast_to(cb, (tb * nb, LRU_BLOCK))
        for k in range(CONV_WIDTH):
            u = u + sx_ref[d, k * nb:(k + tb) * nb, :] * cw[k:k + 1, :]
        th = jnp.tanh(_dot(u.astype(BF16), wg_ref[:, d * gw:(d + 1) * gw]) + bg_ref[:, d * gw:(d + 1) * gw])
        r2 = th[:, :LRU_BLOCK] + 1.0
        i2 = th[:, LRU_BLOCK:] + 1.0
        log_a = r2 * half_neg_c_sp[d:d + 1, :]
        a = jnp.exp(log_a)
        half_mult = jnp.sqrt(jnp.tanh(log_a) * (a * a + 1.0) * (-0.25))
        sa_ref[d] = a
        sb_ref[d] = half_mult * (i2 * u)

    def scan_pair(hf, hr):
        for s in range(tb):
            sr = tb - 1 - s
            hf = sa_ref[0, s * nb:(s + 1) * nb, :] * hf + sb_ref[0, s * nb:(s + 1) * nb, :]
            sh_ref[0, s * nb:(s + 1) * nb, :] = hf
            hr = sa_ref[1, sr * nb:(sr + 1) * nb, :] * hr + sb_ref[1, sr * nb:(sr + 1) * nb, :]
            sh_ref[1, sr * nb:(sr + 1) * nb, :] = hr
        return hf, hr

    n_c, n_l = cx // tb, t // tb

    def ctx_body(j, hs):
        load_ctx_block(0, pl.multiple_of(ctx0 + j * tb, SUBLANES))
        load_ctx_block(1, pl.multiple_of(ctx0 + (n_c - 1 - j) * tb, SUBLANES))
        fill_gates(0)
        fill_gates(1)
        return scan_pair(*hs)

    def emit(d, r0, final):
        for b in range(nb):
            yb = sh_ref[d, pl.ds(b, tb, stride=nb), :]
            if final:
                gate = jax.nn.gelu(agl_ref[b, pl.ds(r0, tb), :].astype(F32))
                o_ref[b, pl.ds(r0, tb), :] = ((yacc_ref[b, pl.ds(r0, tb), :] + yb) * gate).astype(o_ref.dtype)
            else:
                yacc_ref[b, pl.ds(r0, tb), :] = yb

    def lat_body(j, hs, final_f, final_r):
        rf = pl.multiple_of(j * tb, tb)
        rr = pl.multiple_of((n_l - 1 - j) * tb, tb)
        load_latent_block(0, rf)
        load_latent_block(1, rr)
        fill_gates(0)
        fill_gates(1)
        hs = scan_pair(*hs)
        emit(0, rf, final_f)
        emit(1, rr, final_r)
        return hs

    h0 = jnp.zeros((nb, LRU_BLOCK), F32)
    hs = lax.fori_loop(0, n_c, ctx_body, (h0, h0))
    zero_history(0)
    zero_history(1)
    half = n_l // 2
    hs = lax.fori_loop(0, half, functools.partial(lat_body, final_f=False, final_r=False), hs)
    if n_l % 2:
        hs = lat_body(jnp.int32(half), hs, False, True)
    lax.fori_loop(n_l - half, n_l, functools.partial(lat_body, final_f=True, final_r=True), hs)


def _lru(axl, axc, conv_w, conv_b, wg, bg, lam, nb, cx, t):
    w = conv_w.shape[1]
    g = w // LRU_BLOCK
    kern = functools.partial(_lru_kernel, nb=nb, cx=cx, t=t)
    return pl.pallas_call(
        kern,
        grid=(g,),
        in_specs=[
            pl.BlockSpec((nb, t, LRU_BLOCK), lambda j: (0, 0, j)),
            pl.BlockSpec((nb, t, LRU_BLOCK), lambda j: (0, 0, g + j)),
            pl.BlockSpec((nb, cx, LRU_BLOCK), lambda j: (0, 0, j)),
            pl.BlockSpec((CONV_WIDTH, LRU_BLOCK), lambda j: (0, j)),
            pl.BlockSpec((1, LRU_BLOCK), lambda j: (0, j)),
            pl.BlockSpec((None, LRU_BLOCK, 4 * LRU_BLOCK), lambda j: (j, 0, 0)),
            pl.BlockSpec((None, 1, 4 * LRU_BLOCK), lambda j: (j, 0, 0)),
            pl.BlockSpec((2, LRU_BLOCK), lambda j: (0, j)),
        ],
        out_specs=pl.BlockSpec((nb, t, LRU_BLOCK), lambda j: (0, 0, j)),
        out_shape=jax.ShapeDtypeStruct((nb, t, w), BF16),
        scratch_shapes=[
            pltpu.VMEM((nb, cx + 2 * HALO, LRU_BLOCK), F32),
            pltpu.VMEM((nb, t, LRU_BLOCK), F32),
            pltpu.VMEM((2, (GRID_W + HALO) * nb, LRU_BLOCK), F32),
            pltpu.VMEM((2, GRID_W * nb, LRU_BLOCK), F32),
            pltpu.VMEM((2, GRID_W * nb, LRU_BLOCK), F32),
            pltpu.VMEM((2, GRID_W * nb, LRU_BLOCK), F32),
        ],
        compiler_params=_params(("arbitrary",), 56),
        name="lru",
    )(axl, axl, axc, conv_w, conv_b.reshape(1, w), wg, bg, lam)


def _hgrn_kernel(hc_ref, hl_ref, wq_ref, wff_ref, wfb_ref, wv_ref, wg_ref, lb_ref, ng_ref, o_ref,
                 p_ref, sprev_s, q_s, v_s, k_s, hl_s, gc_s, bk_s, sc_s, aq_s, upd_ref, dec_ref,
                 *, cx, t):
    c = HG_CHUNK
    dk = HG_DK
    n_c, n_l = cx // c, t // c
    n_all = n_c + n_l

    w = jnp.concatenate([wq_ref[...], wff_ref[...], wfb_ref[...], wv_ref[...], wg_ref[...]], axis=1)
    p_ref[0:cx, :] = _dot(hc_ref[...], w)
    p_ref[cx:cx + t, :] = _dot(hl_ref[...], w)

    lb = lb_ref[...]
    f_half = 0.5 * (1.0 - lb)
    f_mid = lb + f_half
    ri = lax.broadcasted_iota(I32, (c, c), 0)
    ci = lax.broadcasted_iota(I32, (c, c), 1)
    keep = (ri >= ci, ci >= ri)
    mid = (c // 2 - 1, c // 2)
    end = (c - 1, 0)

    def stage_gates(j, carry):
        r0 = pl.multiple_of(j * c, c)
        q_s[pl.ds(r0, c), :] = _silu(p_ref[pl.ds(r0, c), 0:dk])
        v_s[pl.ds(r0, c), :] = p_ref[pl.ds(r0, c), 3 * dk:4 * dk].astype(BF16)
        for d in range(2):
            f = f_mid + f_half * jnp.tanh(0.5 * p_ref[pl.ds(r0, c), (1 + d) * dk:(2 + d) * dk])
            logf = jnp.log(f)
            hi = logf.astype(BF16)
            lo = (logf - hi.astype(F32)).astype(BF16)
            k_s[d, pl.ds(r0, c), :] = 1.0 - f
            hl_s[d, pl.ds(r0, c), :] = jnp.concatenate([hi, lo], axis=1)
        return carry

    lax.fori_loop(0, n_all, stage_gates, 0, unroll=4)

    def stage_cumsum(j, carry):
        r0 = pl.multiple_of(j * c, c)
        for d in range(2):
            s2 = _dot(keep[d].astype(BF16), hl_s[d, pl.ds(r0, c), :])
            gc_s[d, pl.ds(r0, c), :] = s2[:, :dk] + s2[:, dk:]
        return carry

    lax.fori_loop(0, n_all, stage_cumsum, 0, unroll=12)

    def stage_scores(j, carry):
        r0 = pl.multiple_of(j * c, c)
        q = q_s[pl.ds(r0, c), :]
        scores = None
        for d in range(2):
            gc = gc_s[d, pl.ds(r0, c), :]
            k = k_s[d, pl.ds(r0, c), :]
            g_mid = gc[mid[d]:mid[d] + 1, :]
            g_end = gc[end[d]:end[d] + 1, :]
            qa = q * jnp.exp(gc - g_mid)
            kb = k * jnp.exp(g_mid - gc)
            sc = jnp.where(keep[d], _dot_nt(qa.astype(BF16), kb.astype(BF16)), 0.0)
            scores = sc if scores is None else scores + sc
            bk_s[d, pl.ds(r0, c), :] = (kb * jnp.exp(g_end - g_mid)).astype(BF16)
            aq_s[pl.ds(r0, c), d * dk:(d + 1) * dk] = (qa * jnp.exp(g_mid)).astype(BF16)
            dec_ref[d, j] = jnp.broadcast_to(jnp.exp(g_end), (SUBLANES, dk))
        sc_s[pl.ds(r0, c), :] = scores.astype(BF16)
        return carry

    lax.fori_loop(0, n_all, stage_scores, 0, unroll=6)

    def stage_updates(j, carry):
        r0 = pl.multiple_of(j * c, c)
        v = v_s[pl.ds(r0, c), :]
        for d in range(2):
            upd_ref[d, j] = _dot_tn(v, bk_s[d, pl.ds(r0, c), :])
        return carry

    lax.fori_loop(0, n_all, stage_updates, 0, unroll=12)

    def advance(d, jj, st):
        return st * dec_ref[d, jj][0:1, :] + upd_ref[d, jj]

    def ctx_body(j, sts):
        return advance(0, j, sts[0]), advance(1, n_c - 1 - j, sts[1])

    def lat_body(j, sts):
        jf, jr = j, n_l - 1 - j
        sprev_s[jf, :, 0:dk] = sts[0].astype(BF16)
        sprev_s[jr, :, dk:2 * dk] = sts[1].astype(BF16)
        return advance(0, n_c + jf, sts[0]), advance(1, n_c + jr, sts[1])

    zero = jnp.zeros((dk, dk), F32)
    sts = lax.fori_loop(0, n_c, ctx_body, (zero, zero), unroll=2)
    lax.fori_loop(0, n_l, lat_body, sts, unroll=4)

    ng = ng_ref[...]

    def stage_outputs(j, carry):
        r0 = pl.multiple_of(j * c, c)
        rc = pl.multiple_of(cx + r0, c)
        o = _dot(sc_s[pl.ds(rc, c), :], v_s[pl.ds(rc, c), :]) + _dot_nt(aq_s[pl.ds(rc, c), :], sprev_s[j])
        g = p_ref[pl.ds(rc, c), 4 * dk:5 * dk]
        o_ref[pl.ds(r0, c), :] = (_rms(o) * ng * _silu(g)).astype(o_ref.dtype)
        return carry

    lax.fori_loop(0, n_l, stage_outputs, 0, unroll=16)


def _hgrn(hc, hl, w_in_b, col0, heads, lb, norm_g):
    b, cx, d = hc.shape
    t = hl.shape[1]
    dk = HG_DK
    n_all = (cx + t) // HG_CHUNK
    kern = functools.partial(_hgrn_kernel, cx=cx, t=t)

    def group(gi):
        return pl.BlockSpec((d, dk), lambda bi, h: (0, col0 // dk + gi * heads + h))

    return pl.pallas_call(
        kern,
        grid=(b, heads),
        in_specs=[
            pl.BlockSpec((None, cx, d), lambda bi, h: (bi, 0, 0)),
            pl.BlockSpec((None, t, d), lambda bi, h: (bi, 0, 0)),
            group(0), group(1), group(2), group(3), group(4),
            pl.BlockSpec((None, 1, dk), lambda bi, h: (h, 0, 0)),
            pl.BlockSpec((1, dk), lambda bi, h: (0, 0)),
        ],
        out_specs=pl.BlockSpec((None, t, dk), lambda bi, h: (bi, 0, h)),
        out_shape=jax.ShapeDtypeStruct((b, t, heads * dk), BF16),
        scratch_shapes=[
            pltpu.VMEM((cx + t, 5 * dk), F32),
            pltpu.VMEM((t // HG_CHUNK, dk, 2 * dk), BF16),
            pltpu.VMEM((cx + t, dk), F32),
            pltpu.VMEM((cx + t, dk), BF16),
            pltpu.VMEM((2, cx + t, dk), F32),
            pltpu.VMEM((2, cx + t, 2 * dk), BF16),
            pltpu.VMEM((2, cx + t, dk), F32),
            pltpu.VMEM((2, cx + t, dk), BF16),
            pltpu.VMEM((cx + t, HG_CHUNK), BF16),
            pltpu.VMEM((cx + t, 2 * dk), BF16),
            pltpu.VMEM((2, n_all, dk, dk), F32),
            pltpu.VMEM((2, n_all, SUBLANES, dk), F32),
        ],
        compiler_params=_params(("arbitrary", "arbitrary"), 48),
        name="hgrn",
    )(hc, hl, w_in_b, w_in_b, w_in_b, w_in_b, w_in_b, lb, norm_g.reshape(1, dk))


def _merge_kernel(x_ref, h_ref, ya_ref, yb_ref, wm_ref, wa_ref, wb_ref, wo_ref,
                  gate_ref, shift_ref, scale_ref, g2_ref, rwt_ref, rb_ref,
                  x1_ref, hs_ref, pos_ref, cnt_ref, h2b_s, *, tm, d, ne, cap):
    m = _dot(h_ref[...], wm_ref[...])
    za = _dot(ya_ref[...], wa_ref[...])
    zb = _dot(yb_ref[...], wb_ref[...])
    mix = _sigmoid(m[:, :d]) * za + _sigmoid(m[:, d:]) * zb
    out = _dot(mix.astype(BF16), wo_ref[...])
    x1 = x_ref[...] + gate_ref[...] * out
    x1_ref[...] = x1
    h2 = _rms(x1) * g2_ref[...] * (1.0 + scale_ref[...]) + shift_ref[...]
    h2b_s[...] = h2.astype(BF16)

    logits = lax.dot_general(rwt_ref[...], h2, (((1,), (1,)), ((), ())),
                             preferred_element_type=F32, precision=HIGHEST) + rb_ref[...]
    eid = lax.broadcasted_iota(I32, (ne, tm), 0)
    neg = jnp.float32(-jnp.inf)
    work = logits
    hot = jnp.zeros((ne, tm), F32)
    vals, idxs, sels = [], [], []
    for _ in range(TOP_K):
        mx = jnp.max(work, axis=0, keepdims=True)
        ix = jnp.min(jnp.where(work == mx, eid, ne), axis=0, keepdims=True)
        sel = eid == ix
        work = jnp.where(sel, neg, work)
        hot = hot + sel.astype(F32)
        vals.append(mx)
        idxs.append(ix)
        sels.append(sel)
    ex = [jnp.exp(v - vals[0]) for v in vals]
    den = ex[0] + ex[1] + ex[2] + ex[3]
    gts = [e / den for e in ex]

    ti = lax.broadcasted_iota(I32, (tm, tm), 0)
    tj = lax.broadcasted_iota(I32, (tm, tm), 1)
    strict = (ti < tj).astype(BF16)
    before = _dot(hot.astype(BF16), strict)
    cnt = jnp.sum(hot, axis=1, keepdims=True)
    nchunks = jnp.ceil(cnt * (1.0 / MOE_CHUNK))
    ei = lax.broadcasted_iota(I32, (ne, ne), 0)
    ej = lax.broadcasted_iota(I32, (ne, ne), 1)
    earlier = (ej < ei).astype(BF16)
    start = _dot(earlier, jnp.broadcast_to(nchunks, (ne, LANES)).astype(BF16))[:, 0:1] * MOE_CHUNK
    where_to = before + start
    poss = [jnp.sum(jnp.where(s, where_to, 0.0), axis=0, keepdims=True).astype(I32) for s in sels]

    rb_rows = SORT_ROWS
    lane = lax.broadcasted_iota(I32, (rb_rows, LANES), 1)

    def sort_rows(i, carry):
        r0 = pl.multiple_of(i * rb_rows, rb_rows)
        pid = lax.broadcasted_iota(I32, (rb_rows, tm), 0)
        gsel = jnp.zeros((rb_rows, tm), F32)
        for k in range(TOP_K):
            gsel = jnp.where(pid == poss[k] - r0, gts[k], gsel)
        place = jnp.where(gsel != 0.0, 1.0, 0.0)
        hs_ref[pl.ds(r0, rb_rows), 0:d] = _dot(place.astype(BF16), h2b_s[...]).astype(BF16)
        g = jnp.sum(gsel, axis=1, keepdims=True)
        g_hi = g.astype(BF16).astype(F32)
        hs_ref[pl.ds(r0, rb_rows), d:d + LANES] = jnp.where(lane < LANES // 2, g_hi, g - g_hi).astype(BF16)
        return carry

    lax.fori_loop(0, cap // rb_rows, sort_rows, 0)

    row = lax.broadcasted_iota(I32, (SUBLANES, tm), 0)
    packed = jnp.zeros((SUBLANES, tm), I32)
    for k, p in enumerate(poss):
        packed = jnp.where(row == k, jnp.broadcast_to(p, (SUBLANES, tm)), packed)
    pos_ref[...] = packed
    cnt_ref[...] = jnp.broadcast_to(cnt, cnt_ref.shape)


def _merge(x, hl, ya, yb, w_m, w_ba, w_bb, w_out, gate2, shift3, scale4, g2, rwt, rb, tm):
    b, t, d = x.shape
    ne = rwt.shape[0]
    nt = t // tm
    tok = b * t
    cap = _sorted_rows(tm, ne)
    kern = functools.partial(_merge_kernel, tm=tm, d=d, ne=ne, cap=cap)
    row_spec = pl.BlockSpec((None, tm, d), lambda bi, i: (bi, i, 0))
    mod_spec = pl.BlockSpec((None, 1, d), lambda bi, i: (bi, 0, 0))
    flat = lambda bi, i: (bi * nt + i, 0)
    lane = lambda bi, i: (0, bi * nt + i)

    def whole(shape):
        return pl.BlockSpec(shape, lambda bi, i: (0,) * len(shape))

    return pl.pallas_call(
        kern,
        grid=(b, nt),
        in_specs=[
            row_spec, row_spec, row_spec, row_spec,
            whole((d, 2 * d)), whole((d, d)), whole((d, d)), whole((d, d)),
            mod_spec, mod_spec, mod_spec,
            whole((1, d)), whole((ne, d)), whole((ne, 1)),
        ],
        out_specs=[
            row_spec,
            pl.BlockSpec((cap, d + LANES), flat),
            pl.BlockSpec((SUBLANES, tm), lane),
            pl.BlockSpec((None, ne, LANES), lambda bi, i: (bi * nt + i, 0, 0)),
        ],
        out_shape=[
            jax.ShapeDtypeStruct((b, t, d), F32),
            jax.ShapeDtypeStruct((b * nt * cap, d + LANES), BF16),
            jax.ShapeDtypeStruct((SUBLANES, tok), I32),
            jax.ShapeDtypeStruct((b * nt, ne, LANES), F32),
        ],
        scratch_shapes=[pltpu.VMEM((tm, d), BF16)],
        compiler_params=_params(("arbitrary", "arbitrary"), 56),
        name="merge",
    )(x, hl, ya, yb, w_m, w_ba, w_bb, w_out, gate2, shift3, scale4, g2.reshape(1, d), rwt, rb)


def _experts_kernel(bexp_ref, bact_ref, src_ref, src1_ref, src2_ref, hs_hbm, w1_ref, b1_ref, w2_ref, b2_ref,
                    y_ref, xbuf, act_s, w1p_s, w2_s, gsem, *, blk, nblocks, d):
    j = pl.program_id(0)
    slot = lax.rem(j, MOE_RING)
    far = lax.rem(j + 2, MOE_RING)
    f = w2_s.shape[1]
    pw = 2 * LANES
    n1 = 2 * f // pw
    n2 = d // pw
    nch = blk // MOE_CHUNK
    active = bact_ref[j] == 1

    def request(idx_ref, buf_slot):
        for c in range(nch):
            src = pl.multiple_of(idx_ref[0, 0, c] * MOE_CHUNK, MOE_CHUNK)
            pltpu.make_async_copy(hs_hbm.at[pl.ds(src, MOE_CHUNK), :],
                                  xbuf.at[buf_slot, pl.ds(c * MOE_CHUNK, MOE_CHUNK), :], gsem.at[buf_slot]).start()

    @pl.when(j == 0)
    def _():
        request(src_ref, 0)
        request(src1_ref, 1)

    @pl.when(j + 2 < nblocks)
    def _():
        request(src2_ref, far)

    pltpu.make_async_copy(hs_hbm.at[pl.ds(0, blk), :], xbuf.at[slot], gsem.at[slot]).wait()

    new_expert = (j == 0) | (bexp_ref[j] != bexp_ref[jnp.maximum(j - 1, 0)])

    @pl.when(new_expert & active)
    def _():
        src = lax.broadcasted_iota(I32, (pw, pw), 0)
        dst = lax.broadcasted_iota(I32, (pw, pw), 1)
        want = jnp.where(dst < pw // 2, 2 * dst, 2 * (dst - pw // 2) + 1)
        sel = (src == want).astype(BF16)
        for cb in range(n1):
            w1p_s[cb] = _dot(w1_ref[:, cb * pw:(cb + 1) * pw].astype(BF16), sel).astype(BF16)
        for cb in range(n2):
            w2_s[cb] = w2_ref[:, cb * pw:(cb + 1) * pw].astype(BF16)

    @pl.when(active)
    def _():
        x = xbuf[slot, :, 0:d]
        gate = (xbuf[slot, :, d:d + 1].astype(F32) + xbuf[slot, :, d + LANES // 2:d + LANES // 2 + 1].astype(F32))
        for n in range(n1):
            h = _dot(x, w1p_s[n]) + b1_ref[n]
            hg = jnp.minimum(h[:, :LANES], SWIGLU_LIMIT)
            hu = jnp.clip(h[:, LANES:], -SWIGLU_LIMIT, SWIGLU_LIMIT)
            act_s[n] = (hg * _sigmoid(SWIGLU_ALPHA * hg) * (hu + 1.0)).astype(BF16)
        act = jnp.concatenate([act_s[i] for i in range(n1)], axis=1)
        for n in range(n2):
            y = _dot(act, w2_s[n]) + b2_ref[n]
            y_ref[:, n * pw:(n + 1) * pw] = (y * gate).astype(y_ref.dtype)

    @pl.when(jnp.logical_not(active))
    def _():
        y_ref[...] = jnp.zeros_like(y_ref)


def _experts(block_expert, block_active, src3, hs, w1, b1p, w2, b2, blk):
    nblocks = block_expert.shape[0]
    f, d = w2.shape[1], w2.shape[2]
    pw = 2 * LANES
    n1, n2 = 2 * f // pw, d // pw
    nch = blk // MOE_CHUNK
    b1p = b1p.reshape(-1, n1, 1, pw)
    b2 = b2.reshape(-1, n2, 1, pw)
    kern = functools.partial(_experts_kernel, blk=blk, nblocks=nblocks, d=d)
    wmap = lambda j, be, ba: (be[j], 0, 0)

    def idx_spec(ahead):
        return pl.BlockSpec((1, 1, nch), lambda j, be, ba: (jnp.minimum(j + ahead, nblocks - 1), 0, 0),
                            memory_space=pltpu.SMEM)

    grid_spec = pltpu.PrefetchScalarGridSpec(
        num_scalar_prefetch=2,
        grid=(nblocks,),
        in_specs=[
            idx_spec(0), idx_spec(1), idx_spec(2),
            pl.BlockSpec(memory_space=pl.ANY),
            pl.BlockSpec((None, d, 2 * f), wmap),
            pl.BlockSpec((None, n1, 1, pw), lambda j, be, ba: (be[j], 0, 0, 0)),
            pl.BlockSpec((None, f, d), wmap),
            pl.BlockSpec((None, n2, 1, pw), lambda j, be, ba: (be[j], 0, 0, 0)),
        ],
        out_specs=pl.BlockSpec((blk, d), lambda j, be, ba: (j, 0)),
        scratch_shapes=[
            pltpu.VMEM((MOE_RING, blk, d + LANES), BF16),
            pltpu.VMEM((n1, blk, LANES), BF16),
            pltpu.VMEM((n1, d, pw), BF16),
            pltpu.VMEM((n2, f, pw), BF16),
            pltpu.SemaphoreType.DMA((MOE_RING,)),
        ],
    )
    return pl.pallas_call(
        kern,
        grid_spec=grid_spec,
        out_shape=jax.ShapeDtypeStruct((nblocks * blk, d), BF16),
        compiler_params=_params(("arbitrary",), 56),
        name="experts",
    )(block_expert, block_active, src3, src3, src3, hs, w1, b1p, w2, b2)


def _combine_kernel(dst_ref, dstn_ref, y_hbm, x1_ref, pos_ref, gate_ref, fg_ref, o_ref, ybuf, sem,
                    *, tm, cap, ntiles):
    i = pl.program_id(0)
    slot = lax.rem(i, 2)
    nch = cap // MOE_CHUNK

    def request(idx_ref, buf_slot):
        def body(c, carry):
            src = pl.multiple_of(idx_ref[0, 0, c] * MOE_CHUNK, MOE_CHUNK)
            dst = pl.multiple_of(c * MOE_CHUNK, MOE_CHUNK)
            pltpu.make_async_copy(y_hbm.at[pl.ds(src, MOE_CHUNK), :],
                                  ybuf.at[buf_slot, pl.ds(dst, MOE_CHUNK), :], sem.at[buf_slot]).start()
            return carry
        lax.fori_loop(0, nch, body, 0, unroll=8)

    @pl.when(i == 0)
    def _():
        request(dst_ref, 0)

    @pl.when(i + 1 < ntiles)
    def _():
        request(dstn_ref, 1 - slot)

    pltpu.make_async_copy(y_hbm.at[pl.ds(0, cap), :], ybuf.at[slot], sem.at[slot]).wait()

    pos = pos_ref[...]
    acc = None
    for c0 in range(0, cap, SORT_ROWS):
        col = lax.broadcasted_iota(I32, (tm, SORT_ROWS), 1) + c0
        pick = jnp.zeros((tm, SORT_ROWS), F32)
        for k in range(TOP_K):
            pick = jnp.where(col == pos[:, k:k + 1], 1.0, pick)
        part = _dot(pick.astype(BF16), ybuf[slot, c0:c0 + SORT_ROWS, :])
        acc = part if acc is None else acc + part
    x2 = x1_ref[...] + gate_ref[...] * acc
    o_ref[...] = _rms(x2) * fg_ref[...]


def _combine(dst3, y, x1, pos_t, gate5, final_g, tm, cap):
    b, t, d = x1.shape
    nt = t // tm
    ntiles = b * nt
    nch = cap // MOE_CHUNK
    kern = functools.partial(_combine_kernel, tm=tm, cap=cap, ntiles=ntiles)
    row_map = lambda i: (i // nt, i % nt, 0)
    return pl.pallas_call(
        kern,
        grid=(ntiles,),
        in_specs=[
            pl.BlockSpec((1, 1, nch), lambda i: (i, 0, 0), memory_space=pltpu.SMEM),
            pl.BlockSpec((1, 1, nch), lambda i: (jnp.minimum(i + 1, ntiles - 1), 0, 0), memory_space=pltpu.SMEM),
            pl.BlockSpec(memory_space=pl.ANY),
            pl.BlockSpec((None, tm, d), row_map),
            pl.BlockSpec((tm, TOP_K), lambda i: (i, 0)),
            pl.BlockSpec((None, 1, d), lambda i: (i // nt, 0, 0)),
            pl.BlockSpec((1, d), lambda i: (0, 0)),
        ],
        out_specs=pl.BlockSpec((None, tm, d), row_map),
        out_shape=jax.ShapeDtypeStruct((b, t, d), F32),
        scratch_shapes=[
            pltpu.VMEM((2, cap, d), BF16),
            pltpu.SemaphoreType.DMA((2,)),
        ],
        compiler_params=_params(("arbitrary",), 48),
        name="combine",
    )(dst3, dst3, y, x1, pos_t, gate5, final_g.reshape(1, d))


def _sorted_rows(tm, ne):
    need = tm * TOP_K + ne * (MOE_CHUNK - 1) + MOE_CHUNK
    return -(-need // SORT_ROWS) * SORT_ROWS


def _pick_tile(n, pref):
    tm = pref
    while n % tm:
        tm //= 2
    return tm


def kernel(x, c, ctx, c_ctx, ada_w, ada_b, norm1_g, norm2_g, w_in, lru_conv_w, lru_conv_b, lru_wa, lru_ba, lru_wx, lru_bx, lru_lam, hg_lb_logits, hg_norm_g, w_branch_a, w_branch_b, w_out, router_w, router_b, moe_w1, moe_b1, moe_w2, moe_b2, final_g):
    b, t, d = x.shape
    cx = ctx.shape[1]
    layer = 0
    w_lru = lru_conv_w.shape[2]
    qk = (w_in.shape[2] - 2 * w_lru - 2 * d) // 5
    heads = qk // HG_DK
    ne = router_w.shape[2]

    pad = (-(b + 1)) % SUBLANES
    c_all = jnp.concatenate([c, c_ctx[None, :], jnp.zeros((pad, d), F32)], axis=0)
    mod = _modulation(c_all, ada_w[layer], ada_b[layer])
    mod_l = mod[:b].reshape(b, N_MOD, 1, d)
    mod_c = mod[b].reshape(N_MOD, 1, 1, d)

    w_in_b = w_in[layer].astype(BF16)
    hg0 = 2 * w_lru
    w_m = w_in_b[:, hg0 + 5 * qk:]
    g_blocks = w_lru // LRU_BLOCK
    wg = jnp.concatenate([lru_wa[layer, 0], lru_wx[layer, 0], lru_wa[layer, 1], lru_wx[layer, 1]], axis=-1)
    wg = (0.5 * wg).astype(BF16)
    bg = jnp.concatenate([lru_ba[layer, 0], lru_bx[layer, 0], lru_ba[layer, 1], lru_bx[layer, 1]], axis=-1)
    bg = (0.5 * bg).reshape(g_blocks, 1, 4 * LRU_BLOCK)
    lb_all = jnp.cumsum(jax.nn.softmax(hg_lb_logits.astype(F32), axis=0), axis=0)
    lb = lb_all[layer].reshape(heads, 1, HG_DK)

    tm_n = _pick_tile(t, 512)
    hl, axl = _norm_proj(x, mod_l[:, 0], mod_l[:, 1], norm1_g[layer], w_in_b, 2 * w_lru, tm_n)
    hc, axc = _norm_proj(ctx, mod_c[0], mod_c[1], norm1_g[layer], w_in_b, 2 * w_lru, _pick_tile(cx, 256))
    ya = _lru(axl, axc, lru_conv_w[layer], lru_conv_b[layer], wg, bg, lru_lam[layer], b, cx, t)
    yb = _hgrn(hc, hl, w_in_b, hg0, heads, lb, hg_norm_g[layer])

    rwt = router_w[layer].T
    rb = router_b[layer].reshape(ne, 1)
    tm = _pick_tile(t, 512)
    x1, hs, pos8, cnt = _merge(
        x, hl, ya, yb, w_m, w_branch_a[layer].astype(BF16), w_branch_b[layer].astype(BF16),
        w_out[layer].astype(BF16), mod_l[:, 2], mod_l[:, 3], mod_l[:, 4], norm2_g[layer], rwt, rb, tm)

    blk = MOE_BLOCK
    ch = MOE_CHUNK
    bpc = blk // ch
    tok = b * t
    ntile = tok // tm
    cap = _sorted_rows(tm, ne)
    tch = cap // ch
    run = (cnt[:, :, 0].astype(I32) + ch - 1) // ch
    loc_end = jnp.cumsum(run, axis=1)
    loc_start = loc_end - run
    upto = jnp.cumsum(run, axis=0)
    total = upto[-1]
    padded = (total + bpc - 1) // bpc * bpc
    gend = jnp.cumsum(padded)
    gstart = gend - padded
    where_run = gstart[None, :] + upto - run
    max_chunks = tok * TOP_K // ch + ntile * ne + ne * bpc
    nblocks = -(-max_chunks // bpc)

    eids = jnp.arange(ne, dtype=I32)
    tids = jnp.arange(ntile, dtype=I32)
    g = jnp.arange(nblocks * bpc, dtype=I32)
    g_e = jnp.minimum(jnp.sum(g[:, None] >= gend[None, :], axis=1), ne - 1).astype(I32)
    is_e = g_e[:, None] == eids[None, :]
    off = g - jnp.sum(jnp.where(is_e, gstart[None, :], 0), axis=1)
    g_total = jnp.sum(jnp.where(is_e, total[None, :], 0), axis=1)
    upto_g = jnp.sum(jnp.where(is_e[:, None, :], upto[None, :, :], 0), axis=2)
    g_i = jnp.minimum(jnp.sum(off[:, None] >= upto_g, axis=1), ntile - 1).astype(I32)
    both = (g_i[:, None] == tids[None, :])[:, :, None] & is_e[:, None, :]

    def at_run(table):
        return jnp.sum(jnp.where(both, table[None, :, :], 0), axis=(1, 2))

    within = off - (at_run(upto) - at_run(run))
    zero_chunk = tch - 1
    src = jnp.where((off < g_total) & (g < gend[-1]), g_i * tch + at_run(loc_start) + within, zero_chunk)

    block_first = jnp.arange(nblocks, dtype=I32) * bpc
    block_active = block_first < gend[-1]
    block_expert = jnp.minimum(jnp.sum(block_first[:, None] >= gend[None, :], axis=1), ne - 1).astype(I32)
    last_expert = jnp.max(jnp.where(block_active, block_expert, 0))
    block_expert = jnp.where(block_active, block_expert, last_expert).astype(I32)

    c = jnp.arange(tch, dtype=I32)
    c_e = jnp.minimum(jnp.sum(c[None, :, None] >= loc_end[:, None, :], axis=2), ne - 1).astype(I32)
    is_ce = c_e[:, :, None] == eids[None, None, :]
    back = jnp.sum(jnp.where(is_ce, (where_run - loc_start)[:, None, :], 0), axis=2) + c[None, :]
    back = jnp.where(c[None, :] < loc_end[:, -1:], back, 0)

    b1 = moe_b1[layer]
    f = b1.shape[1] // 2
    b1p = b1.reshape(ne, f // LANES, LANES, 2).transpose(0, 1, 3, 2).reshape(ne, 1, 2 * f)
    y = _experts(block_expert, block_active.astype(I32), src.reshape(nblocks, 1, bpc), hs, moe_w1[layer], b1p,
                 moe_w2[layer], moe_b2[layer].reshape(ne, 1, d), blk)
    return _combine(back.reshape(ntile, 1, tch), y, x1, pos8[:TOP_K].T, mod_l[:, 5], final_g, tm, cap)
```

```python
import functools

import jax
import jax.numpy as jnp
from jax import lax
from jax.experimental import pallas as pl
from jax.experimental.pallas import tpu as pltpu

F32 = jnp.float32
BF16 = jnp.bfloat16
I32 = jnp.int32
HIGHEST = lax.Precision.HIGHEST

EPS = 1e-6
N_MOD = 6
GRID_W = 64
CONV_LEFT = 2
CONV_WIDTH = 4
LRU_C = 8.0
LRU_BLOCK = 128
HG_DK = 128
HG_CHUNK = 64
TOP_K = 4
SWIGLU_LIMIT = 7.0
SWIGLU_ALPHA = 1.702
MOE_BLOCK = 512
MOE_CHUNK = 16
SORT_ROWS = 512
MOE_RING = 3
SUBLANES = 8
LANES = 128
HALO = 8


def _params(sem, vmem_mb):
    return pltpu.CompilerParams(dimension_semantics=sem, vmem_limit_bytes=vmem_mb * 1024 * 1024)


def _dot(a, b):
    return jnp.dot(a, b, preferred_element_type=F32)


def _dot_nt(a, b):
    return lax.dot_general(a, b, (((1,), (1,)), ((), ())), preferred_element_type=F32)


def _dot_tn(a, b):
    return lax.dot_general(a, b, (((0,), (0,)), ((), ())), preferred_element_type=F32)


def _sigmoid(x):
    return 0.5 * jnp.tanh(0.5 * x) + 0.5


def _silu(x):
    return x * _sigmoid(x)


def _rms(x):
    return x * lax.rsqrt(jnp.mean(x * x, axis=-1, keepdims=True) + EPS)


def _mod_kernel(c_ref, w_ref, b_ref, o_ref):
    s = _silu(c_ref[...])
    o_ref[...] = jnp.dot(s, w_ref[...], preferred_element_type=F32, precision=HIGHEST) + b_ref[...]


def _modulation(c_all, ada_w, ada_b):
    m, d = c_all.shape
    n = ada_w.shape[1]
    tn = 1024
    return pl.pallas_call(
        _mod_kernel,
        grid=(n // tn,),
        in_specs=[
            pl.BlockSpec((m, d), lambda j: (0, 0)),
            pl.BlockSpec((d, tn), lambda j: (0, j)),
            pl.BlockSpec((1, tn), lambda j: (0, j)),
        ],
        out_specs=pl.BlockSpec((m, tn), lambda j: (0, j)),
        out_shape=jax.ShapeDtypeStruct((m, n), F32),
        compiler_params=_params(("arbitrary",), 32),
        name="mod",
    )(c_all, ada_w, ada_b.reshape(1, n))


def _norm_proj_kernel(x_ref, shift_ref, scale_ref, g_ref, w_ref, h_ref, o_ref):
    y = _rms(x_ref[...]) * g_ref[...]
    h = (y * (1.0 + scale_ref[...]) + shift_ref[...]).astype(h_ref.dtype)
    h_ref[...] = h
    o_ref[...] = _dot(h, w_ref[...]).astype(o_ref.dtype)


def _norm_proj(x, shift, scale, g, w_in_b, n_out, tm):
    b, l, d = x.shape
    per_batch = shift.shape[0] == b and b > 1
    mod_map = (lambda bi, i: (bi, 0, 0)) if per_batch else (lambda bi, i: (0, 0, 0))
    return pl.pallas_call(
        _norm_proj_kernel,
        grid=(b, l // tm),
        in_specs=[
            pl.BlockSpec((None, tm, d), lambda bi, i: (bi, i, 0)),
            pl.BlockSpec((None, 1, d), mod_map),
            pl.BlockSpec((None, 1, d), mod_map),
            pl.BlockSpec((1, d), lambda bi, i: (0, 0)),
            pl.BlockSpec((d, n_out), lambda bi, i: (0, 0)),
        ],
        out_specs=[
            pl.BlockSpec((None, tm, d), lambda bi, i: (bi, i, 0)),
            pl.BlockSpec((None, tm, n_out), lambda bi, i: (bi, i, 0)),
        ],
        out_shape=[
            jax.ShapeDtypeStruct((b, l, d), BF16),
            jax.ShapeDtypeStruct((b, l, n_out), BF16),
        ],
        compiler_params=_params(("arbitrary", "arbitrary"), 40),
        name="norm_proj",
    )(x, shift, scale, g.reshape(1, d), w_in_b)


def _lru_kernel(axl_ref, agl_ref, axc_ref, cw_ref, cb_ref, wg_ref, bg_ref, lam_ref, o_ref,
                xc_ref, yacc_ref, sx_ref, sa_ref, sb_ref, sh_ref, *, nb, cx, t):
    tb = GRID_W
    ctx0 = HALO
    gw = 2 * LRU_BLOCK
    pre = CONV_LEFT
    steps = pre + tb + (HALO - pre)

    zeros_h = jnp.zeros((HALO, LRU_BLOCK), F32)
    for b in range(nb):
        xc_ref[b, 0:HALO, :] = zeros_h
        xc_ref[b, ctx0:ctx0 + cx, :] = axc_ref[b].astype(F32)
        xc_ref[b, ctx0 + cx:ctx0 + cx + HALO, :] = zeros_h

    cw = cw_ref[...]
    cb = cb_ref[...]
    lam = lam_ref[...]
    sp = jnp.maximum(-lam, 0.0) + jnp.log1p(jnp.exp(-jnp.abs(lam)))
    half_neg_c_sp = (-0.5 * LRU_C) * sp

    def zero_history(d):
        sx_ref[d, 0:pre * nb, :] = jnp.zeros((pre * nb, LRU_BLOCK), F32)
        sx_ref[d, (pre + tb) * nb:steps * nb, :] = jnp.zeros(((steps - pre - tb) * nb, LRU_BLOCK), F32)

    def load_ctx_block(d, base):
        for b in range(nb):
            sx_ref[d, pl.ds(b, steps, stride=nb), :] = xc_ref[b, pl.ds(base - pre, steps), :]

    def load_latent_block(d, r0):
        for b in range(nb):
            sx_ref[d, pl.ds(pre * nb + b, tb, stride=nb), :] = axl_ref[b, pl.ds(r0, tb), :].astype(F32)

    def fill_gates(d):
        u = jnp.broadcast_to(cb, (tb * nb, LRU_BLOCK))
        for k in range(CONV_WIDTH):
            u = u + sx_ref[d, k * nb:(k + tb) * nb, :] * cw[k:k + 1, :]
        th = jnp.tanh(_dot(u.astype(BF16), wg_ref[:, d * gw:(d + 1) * gw]) + bg_ref[:, d * gw:(d + 1) * gw])
        r2 = th[:, :LRU_BLOCK] + 1.0
        i2 = th[:, LRU_BLOCK:] + 1.0
        log_a = r2 * half_neg_c_sp[d:d + 1, :]
        a = jnp.exp(log_a)
        half_mult = jnp.sqrt(jnp.tanh(log_a) * (a * a + 1.0) * (-0.25))
        sa_ref[d] = a
        sb_ref[d] = half_mult * (i2 * u)

    def scan_pair(hf, hr):
        for s in range(tb):
            sr = tb - 1 - s
            hf = sa_ref[0, s * nb:(s + 1) * nb, :] * hf + sb_ref[0, s * nb:(s + 1) * nb, :]
            sh_ref[0, s * nb:(s + 1) * nb, :] = hf
            hr = sa_ref[1, sr * nb:(sr + 1) * nb, :] * hr + sb_ref[1, sr * nb:(sr + 1) * nb, :]
            sh_ref[1, sr * nb:(sr + 1) * nb, :] = hr
        return hf, hr

    n_c, n_l = cx // tb, t // tb

    def ctx_body(j, hs):
        load_ctx_block(0, pl.multiple_of(ctx0 + j * tb, SUBLANES))
        load_ctx_block(1, pl.multiple_of(ctx0 + (n_c - 1 - j) * tb, SUBLANES))
        fill_gates(0)
        fill_gates(1)
        return scan_pair(*hs)

    def emit(d, r0, final):
        for b in range(nb):
            yb = sh_ref[d, pl.ds(b, tb, stride=nb), :]
            if final:
                gate = jax.nn.gelu(agl_ref[b, pl.ds(r0, tb), :].astype(F32))
                o_ref[b, pl.ds(r0, tb), :] = ((yacc_ref[b, pl.ds(r0, tb), :] + yb) * gate).astype(o_ref.dtype)
            else:
                yacc_ref[b, pl.ds(r0, tb), :] = yb

    def lat_body(j, hs, final_f, final_r):
        rf = pl.multiple_of(j * tb, tb)
        rr = pl.multiple_of((n_l - 1 - j) * tb, tb)
        load_latent_block(0, rf)
        load_latent_block(1, rr)
        fill_gates(0)
        fill_gates(1)
        hs = scan_pair(*hs)
        emit(0, rf, final_f)
        emit(1, rr, final_r)
        return hs

    h0 = jnp.zeros((nb, LRU_BLOCK), F32)
    hs = lax.fori_loop(0, n_c, ctx_body, (h0, h0))
    zero_history(0)
    zero_history(1)
    half = n_l // 2
    hs = lax.fori_loop(0, half, functools.partial(lat_body, final_f=False, final_r=False), hs)
    if n_l % 2:
        hs = lat_body(jnp.int32(half), hs, False, True)
    lax.fori_loop(n_l - half, n_l, functools.partial(lat_body, final_f=True, final_r=True), hs)


def _lru(axl, axc, conv_w, conv_b, wg, bg, lam, nb, cx, t):
    w = conv_w.shape[1]
    g = w // LRU_BLOCK
    kern = functools.partial(_lru_kernel, nb=nb, cx=cx, t=t)
    return pl.pallas_call(
        kern,
        grid=(g,),
        in_specs=[
            pl.BlockSpec((nb, t, LRU_BLOCK), lambda j: (0, 0, j)),
            pl.BlockSpec((nb, t, LRU_BLOCK), lambda j: (0, 0, g + j)),
            pl.BlockSpec((nb, cx, LRU_BLOCK), lambda j: (0, 0, j)),
            pl.BlockSpec((CONV_WIDTH, LRU_BLOCK), lambda j: (0, j)),
            pl.BlockSpec((1, LRU_BLOCK), lambda j: (0, j)),
            pl.BlockSpec((None, LRU_BLOCK, 4 * LRU_BLOCK), lambda j: (j, 0, 0)),
            pl.BlockSpec((None, 1, 4 * LRU_BLOCK), lambda j: (j, 0, 0)),
            pl.BlockSpec((2, LRU_BLOCK), lambda j: (0, j)),
        ],
        out_specs=pl.BlockSpec((nb, t, LRU_BLOCK), lambda j: (0, 0, j)),
        out_shape=jax.ShapeDtypeStruct((nb, t, w), BF16),
        scratch_shapes=[
            pltpu.VMEM((nb, cx + 2 * HALO, LRU_BLOCK), F32),
            pltpu.VMEM((nb, t, LRU_BLOCK), F32),
            pltpu.VMEM((2, (GRID_W + HALO) * nb, LRU_BLOCK), F32),
            pltpu.VMEM((2, GRID_W * nb, LRU_BLOCK), F32),
            pltpu.VMEM((2, GRID_W * nb, LRU_BLOCK), F32),
            pltpu.VMEM((2, GRID_W * nb, LRU_BLOCK), F32),
        ],
        compiler_params=_params(("arbitrary",), 56),
        name="lru",
    )(axl, axl, axc, conv_w, conv_b.reshape(1, w), wg, bg, lam)


def _hgrn_kernel(hc_ref, hl_ref, wq_ref, wff_ref, wfb_ref, wv_ref, wg_ref, lb_ref, ng_ref, o_ref,
                 p_ref, sprev_s, q_s, v_s, k_s, hl_s, gc_s, bk_s, sc_s, aq_s, upd_ref, dec_ref,
                 *, cx, t):
    c = HG_CHUNK
    dk = HG_DK
    n_c, n_l = cx // c, t // c
    n_all = n_c + n_l

    w = jnp.concatenate([wq_ref[...], wff_ref[...], wfb_ref[...], wv_ref[...], wg_ref[...]], axis=1)
    p_ref[0:cx, :] = _dot(hc_ref[...], w)
    p_ref[cx:cx + t, :] = _dot(hl_ref[...], w)

    lb = lb_ref[...]
    f_half = 0.5 * (1.0 - lb)
    f_mid = lb + f_half
    ri = lax.broadcasted_iota(I32, (c, c), 0)
    ci = lax.broadcasted_iota(I32, (c, c), 1)
    keep = (ri >= ci, ci >= ri)
    mid = (c // 2 - 1, c // 2)
    end = (c - 1, 0)

    def stage_gates(j, carry):
        r0 = pl.multiple_of(j * c, c)
        q_s[pl.ds(r0, c), :] = _silu(p_ref[pl.ds(r0, c), 0:dk])
        v_s[pl.ds(r0, c), :] = p_ref[pl.ds(r0, c), 3 * dk:4 * dk].astype(BF16)
        for d in range(2):
            f = f_mid + f_half * jnp.tanh(0.5 * p_ref[pl.ds(r0, c), (1 + d) * dk:(2 + d) * dk])
            logf = jnp.log(f)
            hi = logf.astype(BF16)
            lo = (logf - hi.astype(F32)).astype(BF16)
            k_s[d, pl.ds(r0, c), :] = 1.0 - f
            hl_s[d, pl.ds(r0, c), :] = jnp.concatenate([hi, lo], axis=1)
        return carry

    lax.fori_loop(0, n_all, stage_gates, 0, unroll=4)

    def stage_cumsum(j, carry):
        r0 = pl.multiple_of(j * c, c)
        for d in range(2):
            s2 = _dot(keep[d].astype(BF16), hl_s[d, pl.ds(r0, c), :])
            gc_s[d, pl.ds(r0, c), :] = s2[:, :dk] + s2[:, dk:]
        return carry

    lax.fori_loop(0, n_all, stage_cumsum, 0, unroll=12)

    def stage_scores(j, carry):
        r0 = pl.multiple_of(j * c, c)
        q = q_s[pl.ds(r0, c), :]
        scores = None
        for d in range(2):
            gc = gc_s[d, pl.ds(r0, c), :]
            k = k_s[d, pl.ds(r0, c), :]
            g_mid = gc[mid[d]:mid[d] + 1, :]
            g_end = gc[end[d]:end[d] + 1, :]
            qa = q * jnp.exp(gc - g_mid)
            kb = k * jnp.exp(g_mid - gc)
            sc = jnp.where(keep[d], _dot_nt(qa.astype(BF16), kb.astype(BF16)), 0.0)
            scores = sc if scores is None else scores + sc
            bk_s[d, pl.ds(r0, c), :] = (kb * jnp.exp(g_end - g_mid)).astype(BF16)
            aq_s[pl.ds(r0, c), d * dk:(d + 1) * dk] = (qa * jnp.exp(g_mid)).astype(BF16)
            dec_ref[d, j] = jnp.broadcast_to(jnp.exp(g_end), (SUBLANES, dk))
        sc_s[pl.ds(r0, c), :] = scores.astype(BF16)
        return carry

    lax.fori_loop(0, n_all, stage_scores, 0, unroll=6)

    def stage_updates(j, carry):
        r0 = pl.multiple_of(j * c, c)
        v = v_s[pl.ds(r0, c), :]
        for d in range(2):
            upd_ref[d, j] = _dot_tn(v, bk_s[d, pl.ds(r0, c), :])
        return carry

    lax.fori_loop(0, n_all, stage_updates, 0, unroll=12)

    def advance(d, jj, st):
        return st * dec_ref[d, jj][0:1, :] + upd_ref[d, jj]

    def ctx_body(j, sts):
        return advance(0, j, sts[0]), advance(1, n_c - 1 - j, sts[1])

    def lat_body(j, sts):
        jf, jr = j, n_l - 1 - j
        sprev_s[jf, :, 0:dk] = sts[0].astype(BF16)
        sprev_s[jr, :, dk:2 * dk] = sts[1].astype(BF16)
        return advance(0, n_c + jf, sts[0]), advance(1, n_c + jr, sts[1])

    zero = jnp.zeros((dk, dk), F32)
    sts = lax.fori_loop(0, n_c, ctx_body, (zero, zero), unroll=2)
    lax.fori_loop(0, n_l, lat_body, sts, unroll=4)

    ng = ng_ref[...]

    def stage_outputs(j, carry):
        r0 = pl.multiple_of(j * c, c)
        rc = pl.multiple_of(cx + r0, c)
        o = _dot(sc_s[pl.ds(rc, c), :], v_s[pl.ds(rc, c), :]) + _dot_nt(aq_s[pl.ds(rc, c), :], sprev_s[j])
        g = p_ref[pl.ds(rc, c), 4 * dk:5 * dk]
        o_ref[pl.ds(r0, c), :] = (_rms(o) * ng * _silu(g)).astype(o_ref.dtype)
        return carry

    lax.fori_loop(0, n_l, stage_outputs, 0, unroll=16)


def _hgrn(hc, hl, w_in_b, col0, heads, lb, norm_g):
    b, cx, d = hc.shape
    t = hl.shape[1]
    dk = HG_DK
    n_all = (cx + t) // HG_CHUNK
    kern = functools.partial(_hgrn_kernel, cx=cx, t=t)

    def group(gi):
        return pl.BlockSpec((d, dk), lambda bi, h: (0, col0 // dk + gi * heads + h))

    return pl.pallas_call(
        kern,
        grid=(b, heads),
        in_specs=[
            pl.BlockSpec((None, cx, d), lambda bi, h: (bi, 0, 0)),
            pl.BlockSpec((None, t, d), lambda bi, h: (bi, 0, 0)),
            group(0), group(1), group(2), group(3), group(4),
            pl.BlockSpec((None, 1, dk), lambda bi, h: (h, 0, 0)),
            pl.BlockSpec((1, dk), lambda bi, h: (0, 0)),
        ],
        out_specs=pl.BlockSpec((None, t, dk), lambda bi, h: (bi, 0, h)),
        out_shape=jax.ShapeDtypeStruct((b, t, heads * dk), BF16),
        scratch_shapes=[
            pltpu.VMEM((cx + t, 5 * dk), F32),
            pltpu.VMEM((t // HG_CHUNK, dk, 2 * dk), BF16),
            pltpu.VMEM((cx + t, dk), F32),
            pltpu.VMEM((cx + t, dk), BF16),
            pltpu.VMEM((2, cx + t, dk), F32),
            pltpu.VMEM((2, cx + t, 2 * dk), BF16),
            pltpu.VMEM((2, cx + t, dk), F32),
            pltpu.VMEM((2, cx + t, dk), BF16),
            pltpu.VMEM((cx + t, HG_CHUNK), BF16),
            pltpu.VMEM((cx + t, 2 * dk), BF16),
            pltpu.VMEM((2, n_all, dk, dk), F32),
            pltpu.VMEM((2, n_all, SUBLANES, dk), F32),
        ],
        compiler_params=_params(("arbitrary", "arbitrary"), 48),
        name="hgrn",
    )(hc, hl, w_in_b, w_in_b, w_in_b, w_in_b, w_in_b, lb, norm_g.reshape(1, dk))


def _merge_kernel(x_ref, h_ref, ya_ref, yb_ref, wm_ref, wa_ref, wb_ref, wo_ref,
                  gate_ref, shift_ref, scale_ref, g2_ref, rwt_ref, rb_ref,
                  x1_ref, hs_ref, pos_ref, cnt_ref, h2b_s, *, tm, d, ne, cap):
    m = _dot(h_ref[...], wm_ref[...])
    za = _dot(ya_ref[...], wa_ref[...])
    zb = _dot(yb_ref[...], wb_ref[...])
    mix = _sigmoid(m[:, :d]) * za + _sigmoid(m[:, d:]) * zb
    out = _dot(mix.astype(BF16), wo_ref[...])
    x1 = x_ref[...] + gate_ref[...] * out
    x1_ref[...] = x1
    h2 = _rms(x1) * g2_ref[...] * (1.0 + scale_ref[...]) + shift_ref[...]
    h2b_s[...] = h2.astype(BF16)

    logits = lax.dot_general(rwt_ref[...], h2, (((1,), (1,)), ((), ())),
                             preferred_element_type=F32, precision=HIGHEST) + rb_ref[...]
    eid = lax.broadcasted_iota(I32, (ne, tm), 0)
    neg = jnp.float32(-jnp.inf)
    work = logits
    hot = jnp.zeros((ne, tm), F32)
    vals, sels = [], []
    for _ in range(TOP_K):
        mx = jnp.max(work, axis=0, keepdims=True)
        ix = jnp.min(jnp.where(work == mx, eid, ne), axis=0, keepdims=True)
        sel = eid == ix
        work = jnp.where(sel, neg, work)
        hot = hot + sel.astype(F32)
        vals.append(mx)
        sels.append(sel)
    ex = [jnp.exp(v - vals[0]) for v in vals]
    den = sum(ex[1:], ex[0])
    gts = [e / den for e in ex]

    ti = lax.broadcasted_iota(I32, (tm, tm), 0)
    tj = lax.broadcasted_iota(I32, (tm, tm), 1)
    strict = (ti < tj).astype(BF16)
    before = _dot(hot.astype(BF16), strict)
    cnt = jnp.sum(hot, axis=1, keepdims=True)
    nchunks = jnp.ceil(cnt * (1.0 / MOE_CHUNK))
    ei = lax.broadcasted_iota(I32, (ne, ne), 0)
    ej = lax.broadcasted_iota(I32, (ne, ne), 1)
    earlier = (ej < ei).astype(BF16)
    start = _dot(earlier, jnp.broadcast_to(nchunks, (ne, LANES)).astype(BF16))[:, 0:1] * MOE_CHUNK
    where_to = before + start
    poss = [jnp.sum(jnp.where(s, where_to, 0.0), axis=0, keepdims=True).astype(I32) for s in sels]

    rb_rows = SORT_ROWS
    lane = lax.broadcasted_iota(I32, (rb_rows, LANES), 1)

    def sort_rows(i, carry):
        r0 = pl.multiple_of(i * rb_rows, rb_rows)
        pid = lax.broadcasted_iota(I32, (rb_rows, tm), 0)
        gsel = jnp.zeros((rb_rows, tm), F32)
        for k in range(TOP_K):
            gsel = jnp.where(pid == poss[k] - r0, gts[k], gsel)
        place = jnp.where(gsel != 0.0, 1.0, 0.0)
        hs_ref[pl.ds(r0, rb_rows), 0:d] = _dot(place.astype(BF16), h2b_s[...]).astype(BF16)
        g = jnp.sum(gsel, axis=1, keepdims=True)
        g_hi = g.astype(BF16).astype(F32)
        hs_ref[pl.ds(r0, rb_rows), d:d + LANES] = jnp.where(lane < LANES // 2, g_hi, g - g_hi).astype(BF16)
        return carry

    lax.fori_loop(0, cap // rb_rows, sort_rows, 0)

    row = lax.broadcasted_iota(I32, (SUBLANES, tm), 0)
    packed = jnp.zeros((SUBLANES, tm), I32)
    for k, p in enumerate(poss):
        packed = jnp.where(row == k, jnp.broadcast_to(p, (SUBLANES, tm)), packed)
    pos_ref[...] = packed
    cnt_ref[...] = jnp.broadcast_to(cnt, cnt_ref.shape)


def _merge(x, hl, ya, yb, w_m, w_ba, w_bb, w_out, gate2, shift3, scale4, g2, rwt, rb, tm):
    b, t, d = x.shape
    ne = rwt.shape[0]
    nt = t // tm
    tok = b * t
    cap = _sorted_rows(tm, ne)
    kern = functools.partial(_merge_kernel, tm=tm, d=d, ne=ne, cap=cap)
    row_spec = pl.BlockSpec((None, tm, d), lambda bi, i: (bi, i, 0))
    mod_spec = pl.BlockSpec((None, 1, d), lambda bi, i: (bi, 0, 0))
    flat = lambda bi, i: (bi * nt + i, 0)
    lane = lambda bi, i: (0, bi * nt + i)

    def whole(shape):
        return pl.BlockSpec(shape, lambda bi, i: (0,) * len(shape))

    return pl.pallas_call(
        kern,
        grid=(b, nt),
        in_specs=[
            row_spec, row_spec, row_spec, row_spec,
            whole((d, 2 * d)), whole((d, d)), whole((d, d)), whole((d, d)),
            mod_spec, mod_spec, mod_spec,
            whole((1, d)), whole((ne, d)), whole((ne, 1)),
        ],
        out_specs=[
            row_spec,
            pl.BlockSpec((cap, d + LANES), flat),
            pl.BlockSpec((SUBLANES, tm), lane),
            pl.BlockSpec((None, ne, LANES), lambda bi, i: (bi * nt + i, 0, 0)),
        ],
        out_shape=[
            jax.ShapeDtypeStruct((b, t, d), F32),
            jax.ShapeDtypeStruct((b * nt * cap, d + LANES), BF16),
            jax.ShapeDtypeStruct((SUBLANES, tok), I32),
            jax.ShapeDtypeStruct((b * nt, ne, LANES), F32),
        ],
        scratch_shapes=[pltpu.VMEM((tm, d), BF16)],
        compiler_params=_params(("arbitrary", "arbitrary"), 56),
        name="merge",
    )(x, hl, ya, yb, w_m, w_ba, w_bb, w_out, gate2, shift3, scale4, g2.reshape(1, d), rwt, rb)


def _experts_kernel(bexp_ref, bact_ref, src_ref, src1_ref, src2_ref, hs_hbm, w1_ref, b1_ref, w2_ref, b2_ref,
                    y_ref, xbuf, act_s, w1p_s, w2_s, gsem, *, blk, nblocks, d):
    j = pl.program_id(0)
    slot = lax.rem(j, MOE_RING)
    far = lax.rem(j + 2, MOE_RING)
    f = w2_s.shape[1]
    pw = 2 * LANES
    n1 = 2 * f // pw
    n2 = d // pw
    nch = blk // MOE_CHUNK
    active = bact_ref[j] == 1

    def request(idx_ref, buf_slot):
        for c in range(nch):
            src = pl.multiple_of(idx_ref[0, 0, c] * MOE_CHUNK, MOE_CHUNK)
            pltpu.make_async_copy(hs_hbm.at[pl.ds(src, MOE_CHUNK), :],
                                  xbuf.at[buf_slot, pl.ds(c * MOE_CHUNK, MOE_CHUNK), :], gsem.at[buf_slot]).start()

    @pl.when(j == 0)
    def _():
        request(src_ref, 0)
        request(src1_ref, 1)

    @pl.when(j + 2 < nblocks)
    def _():
        request(src2_ref, far)

    pltpu.make_async_copy(hs_hbm.at[pl.ds(0, blk), :], xbuf.at[slot], gsem.at[slot]).wait()

    new_expert = (j == 0) | (bexp_ref[j] != bexp_ref[jnp.maximum(j - 1, 0)])

    @pl.when(new_expert & active)
    def _():
        src = lax.broadcasted_iota(I32, (pw, pw), 0)
        dst = lax.broadcasted_iota(I32, (pw, pw), 1)
        want = jnp.where(dst < pw // 2, 2 * dst, 2 * (dst - pw // 2) + 1)
        sel = (src == want).astype(BF16)
        for cb in range(n1):
            w1p_s[cb] = _dot(w1_ref[:, cb * pw:(cb + 1) * pw].astype(BF16), sel).astype(BF16)
        for cb in range(n2):
            w2_s[cb] = w2_ref[:, cb * pw:(cb + 1) * pw].astype(BF16)

    @pl.when(active)
    def _():
        x = xbuf[slot, :, 0:d]
        gate = (xbuf[slot, :, d:d + 1].astype(F32) + xbuf[slot, :, d + LANES // 2:d + LANES // 2 + 1].astype(F32))
        for n in range(n1):
            h = _dot(x, w1p_s[n]) + b1_ref[n]
            hg = jnp.minimum(h[:, :LANES], SWIGLU_LIMIT)
            hu = jnp.clip(h[:, LANES:], -SWIGLU_LIMIT, SWIGLU_LIMIT)
            act_s[n] = (hg * _sigmoid(SWIGLU_ALPHA * hg) * (hu + 1.0)).astype(BF16)
        act = jnp.concatenate([act_s[i] for i in range(n1)], axis=1)
        for n in range(n2):
            y = _dot(act, w2_s[n]) + b2_ref[n]
            y_ref[:, n * pw:(n + 1) * pw] = (y * gate).astype(y_ref.dtype)

    @pl.when(jnp.logical_not(active))
    def _():
        y_ref[...] = jnp.zeros_like(y_ref)


def _experts(block_expert, block_active, src3, hs, w1, b1p, w2, b2, blk):
    nblocks = block_expert.shape[0]
    f, d = w2.shape[1], w2.shape[2]
    pw = 2 * LANES
    n1, n2 = 2 * f // pw, d // pw
    nch = blk // MOE_CHUNK
    b1p = b1p.reshape(-1, n1, 1, pw)
    b2 = b2.reshape(-1, n2, 1, pw)
    kern = functools.partial(_experts_kernel, blk=blk, nblocks=nblocks, d=d)
    wmap = lambda j, be, ba: (be[j], 0, 0)

    def idx_spec(ahead):
        return pl.BlockSpec((1, 1, nch), lambda j, be, ba: (jnp.minimum(j + ahead, nblocks - 1), 0, 0),
                            memory_space=pltpu.SMEM)

    grid_spec = pltpu.PrefetchScalarGridSpec(
        num_scalar_prefetch=2,
        grid=(nblocks,),
        in_specs=[
            idx_spec(0), idx_spec(1), idx_spec(2),
            pl.BlockSpec(memory_space=pl.ANY),
            pl.BlockSpec((None, d, 2 * f), wmap),
            pl.BlockSpec((None, n1, 1, pw), lambda j, be, ba: (be[j], 0, 0, 0)),
            pl.BlockSpec((None, f, d), wmap),
            pl.BlockSpec((None, n2, 1, pw), lambda j, be, ba: (be[j], 0, 0, 0)),
        ],
        out_specs=pl.BlockSpec((blk, d), lambda j, be, ba: (j, 0)),
        scratch_shapes=[
            pltpu.VMEM((MOE_RING, blk, d + LANES), BF16),
            pltpu.VMEM((n1, blk, LANES), BF16),
            pltpu.VMEM((n1, d, pw), BF16),
            pltpu.VMEM((n2, f, pw), BF16),
            pltpu.SemaphoreType.DMA((MOE_RING,)),
        ],
    )
    return pl.pallas_call(
        kern,
        grid_spec=grid_spec,
        out_shape=jax.ShapeDtypeStruct((nblocks * blk, d), BF16),
        compiler_params=_params(("arbitrary",), 56),
        name="experts",
    )(block_expert, block_active, src3, src3, src3, hs, w1, b1p, w2, b2)


def _combine_kernel(dst_ref, dstn_ref, y_hbm, x1_ref, pos_ref, gate_ref, fg_ref, o_ref, ybuf, sem,
                    *, tm, cap, ntiles):
    i = pl.program_id(0)
    slot = lax.rem(i, 2)
    nch = cap // MOE_CHUNK

    def request(idx_ref, buf_slot):
        def body(c, carry):
            src = pl.multiple_of(idx_ref[0, 0, c] * MOE_CHUNK, MOE_CHUNK)
            dst = pl.multiple_of(c * MOE_CHUNK, MOE_CHUNK)
            pltpu.make_async_copy(y_hbm.at[pl.ds(src, MOE_CHUNK), :],
                                  ybuf.at[buf_slot, pl.ds(dst, MOE_CHUNK), :], sem.at[buf_slot]).start()
            return carry
        lax.fori_loop(0, nch, body, 0, unroll=8)

    @pl.when(i == 0)
    def _():
        request(dst_ref, 0)

    @pl.when(i + 1 < ntiles)
    def _():
        request(dstn_ref, 1 - slot)

    pltpu.make_async_copy(y_hbm.at[pl.ds(0, cap), :], ybuf.at[slot], sem.at[slot]).wait()

    pos = pos_ref[...]
    acc = None
    for c0 in range(0, cap, SORT_ROWS):
        col = lax.broadcasted_iota(I32, (tm, SORT_ROWS), 1) + c0
        pick = jnp.zeros((tm, SORT_ROWS), F32)
        for k in range(TOP_K):
            pick = jnp.where(col == pos[:, k:k + 1], 1.0, pick)
        part = _dot(pick.astype(BF16), ybuf[slot, c0:c0 + SORT_ROWS, :])
        acc = part if acc is None else acc + part
    x2 = x1_ref[...] + gate_ref[...] * acc
    o_ref[...] = _rms(x2) * fg_ref[...]


def _combine(dst3, y, x1, pos_t, gate5, final_g, tm, cap):
    b, t, d = x1.shape
    nt = t // tm
    ntiles = b * nt
    nch = cap // MOE_CHUNK
    kern = functools.partial(_combine_kernel, tm=tm, cap=cap, ntiles=ntiles)
    row_map = lambda i: (i // nt, i % nt, 0)
    return pl.pallas_call(
        kern,
        grid=(ntiles,),
        in_specs=[
            pl.BlockSpec((1, 1, nch), lambda i: (i, 0, 0), memory_space=pltpu.SMEM),
            pl.BlockSpec((1, 1, nch), lambda i: (jnp.minimum(i + 1, ntiles - 1), 0, 0), memory_space=pltpu.SMEM),
            pl.BlockSpec(memory_space=pl.ANY),
            pl.BlockSpec((None, tm, d), row_map),
            pl.BlockSpec((tm, TOP_K), lambda i: (i, 0)),
            pl.BlockSpec((None, 1, d), lambda i: (i // nt, 0, 0)),
            pl.BlockSpec((1, d), lambda i: (0, 0)),
        ],
        out_specs=pl.BlockSpec((None, tm, d), row_map),
        out_shape=jax.ShapeDtypeStruct((b, t, d), F32),
        scratch_shapes=[
            pltpu.VMEM((2, cap, d), BF16),
            pltpu.SemaphoreType.DMA((2,)),
        ],
        compiler_params=_params(("arbitrary",), 48),
        name="combine",
    )(dst3, dst3, y, x1, pos_t, gate5, final_g.reshape(1, d))


def _sorted_rows(tm, ne):
    need = tm * TOP_K + ne * (MOE_CHUNK - 1) + MOE_CHUNK
    return -(-need // SORT_ROWS) * SORT_ROWS


def _pick_tile(n, pref):
    tm = pref
    while n % tm:
        tm //= 2
    return tm


def kernel(x, c, ctx, c_ctx, ada_w, ada_b, norm1_g, norm2_g, w_in, lru_conv_w, lru_conv_b, lru_wa, lru_ba, lru_wx, lru_bx, lru_lam, hg_lb_logits, hg_norm_g, w_branch_a, w_branch_b, w_out, router_w, router_b, moe_w1, moe_b1, moe_w2, moe_b2, final_g):
    b, t, d = x.shape
    cx = ctx.shape[1]
    layer = 0
    w_lru = lru_conv_w.shape[2]
    qk = (w_in.shape[2] - 2 * w_lru - 2 * d) // 5
    heads = qk // HG_DK
    ne = router_w.shape[2]

    pad = (-(b + 1)) % SUBLANES
    c_all = jnp.concatenate([c, c_ctx[None, :], jnp.zeros((pad, d), F32)], axis=0)
    mod = _modulation(c_all, ada_w[layer], ada_b[layer])
    mod_l = mod[:b].reshape(b, N_MOD, 1, d)
    mod_c = mod[b].reshape(N_MOD, 1, 1, d)

    w_in_b = w_in[layer].astype(BF16)
    hg0 = 2 * w_lru
    w_m = w_in_b[:, hg0 + 5 * qk:]
    g_blocks = w_lru // LRU_BLOCK
    wg = jnp.concatenate([lru_wa[layer, 0], lru_wx[layer, 0], lru_wa[layer, 1], lru_wx[layer, 1]], axis=-1)
    wg = (0.5 * wg).astype(BF16)
    bg = jnp.concatenate([lru_ba[layer, 0], lru_bx[layer, 0], lru_ba[layer, 1], lru_bx[layer, 1]], axis=-1)
    bg = (0.5 * bg).reshape(g_blocks, 1, 4 * LRU_BLOCK)
    lb_all = jnp.cumsum(jax.nn.softmax(hg_lb_logits.astype(F32), axis=0), axis=0)
    lb = lb_all[layer].reshape(heads, 1, HG_DK)

    tm_n = _pick_tile(t, 512)
    hl, axl = _norm_proj(x, mod_l[:, 0], mod_l[:, 1], norm1_g[layer], w_in_b, 2 * w_lru, tm_n)
    hc, axc = _norm_proj(ctx, mod_c[0], mod_c[1], norm1_g[layer], w_in_b, 2 * w_lru, _pick_tile(cx, 256))
    ya = _lru(axl, axc, lru_conv_w[layer], lru_conv_b[layer], wg, bg, lru_lam[layer], b, cx, t)
    yb = _hgrn(hc, hl, w_in_b, hg0, heads, lb, hg_norm_g[layer])

    rwt = router_w[layer].T
    rb = router_b[layer].reshape(ne, 1)
    tm = _pick_tile(t, 512)
    x1, hs, pos8, cnt = _merge(
        x, hl, ya, yb, w_m, w_branch_a[layer].astype(BF16), w_branch_b[layer].astype(BF16),
        w_out[layer].astype(BF16), mod_l[:, 2], mod_l[:, 3], mod_l[:, 4], norm2_g[layer], rwt, rb, tm)

    blk = MOE_BLOCK
    ch = MOE_CHUNK
    bpc = blk // ch
    tok = b * t
    ntile = tok // tm
    cap = _sorted_rows(tm, ne)
    tch = cap // ch
    run = (cnt[:, :, 0].astype(I32) + ch - 1) // ch
    loc_end = jnp.cumsum(run, axis=1)
    loc_start = loc_end - run
    upto = jnp.cumsum(run, axis=0)
    total = upto[-1]
    padded = (total + bpc - 1) // bpc * bpc
    gend = jnp.cumsum(padded)
    gstart = gend - padded
    where_run = gstart[None, :] + upto - run
    max_chunks = tok * TOP_K // ch + ntile * ne + ne * bpc
    nblocks = -(-max_chunks // bpc)

    eids = jnp.arange(ne, dtype=I32)
    tids = jnp.arange(ntile, dtype=I32)
    g = jnp.arange(nblocks * bpc, dtype=I32)
    g_e = jnp.minimum(jnp.sum(g[:, None] >= gend[None, :], axis=1), ne - 1).astype(I32)
    is_e = g_e[:, None] == eids[None, :]
    off = g - jnp.sum(jnp.where(is_e, gstart[None, :], 0), axis=1)
    g_total = jnp.sum(jnp.where(is_e, total[None, :], 0), axis=1)
    upto_g = jnp.sum(jnp.where(is_e[:, None, :], upto[None, :, :], 0), axis=2)
    g_i = jnp.minimum(jnp.sum(off[:, None] >= upto_g, axis=1), ntile - 1).astype(I32)
    both = (g_i[:, None] == tids[None, :])[:, :, None] & is_e[:, None, :]

    def at_run(table):
        return jnp.sum(jnp.where(both, table[None, :, :], 0), axis=(1, 2))

    within = off - (at_run(upto) - at_run(run))
    zero_chunk = tch - 1
    src = jnp.where((off < g_total) & (g < gend[-1]), g_i * tch + at_run(loc_start) + within, zero_chunk)

    block_first = jnp.arange(nblocks, dtype=I32) * bpc
    block_active = block_first < gend[-1]
    block_expert = jnp.minimum(jnp.sum(block_first[:, None] >= gend[None, :], axis=1), ne - 1).astype(I32)
    last_expert = jnp.max(jnp.where(block_active, block_expert, 0))
    block_expert = jnp.where(block_active, block_expert, last_expert).astype(I32)

    c = jnp.arange(tch, dtype=I32)
    c_e = jnp.minimum(jnp.sum(c[None, :, None] >= loc_end[:, None, :], axis=2), ne - 1).astype(I32)
    is_ce = c_e[:, :, None] == eids[None, None, :]
    back = jnp.sum(jnp.where(is_ce, (where_run - loc_start)[:, None, :], 0), axis=2) + c[None, :]
    back = jnp.where(c[None, :] < loc_end[:, -1:], back, 0)

    b1 = moe_b1[layer]
    f = b1.shape[1] // 2
    b1p = b1.reshape(ne, f // LANES, LANES, 2).transpose(0, 1, 3, 2).reshape(ne, 1, 2 * f)
    y = _experts(block_expert, block_active.astype(I32), src.reshape(nblocks, 1, bpc), hs, moe_w1[layer], b1p,
                 moe_w2[layer], moe_b2[layer].reshape(ne, 1, d), blk)
    return _combine(back.reshape(ntile, 1, tch), y, x1, pos8[:TOP_K].T, mod_l[:, 5], final_g, tm, cap)
```

```python
import functools

import jax
import jax.numpy as jnp
from jax import lax
from jax.experimental import pallas as pl
from jax.experimental.pallas import tpu as pltpu

F32 = jnp.float32
BF16 = jnp.bfloat16
I32 = jnp.int32
HIGHEST = lax.Precision.HIGHEST

EPS = 1e-6
N_MOD = 6
GRID_W = 64
CONV_LEFT = 2
CONV_WIDTH = 4
LRU_C = 8.0
LRU_BLOCK = 128
HG_DK = 128
HG_CHUNK = 64
HG_HEADS_PER_STEP = 2
TOP_K = 4
SWIGLU_LIMIT = 7.0
SWIGLU_ALPHA = 1.702
MOE_BLOCK = 512
MOE_CHUNK = 16
SORT_ROWS = 512
MOE_RING = 3
SUBLANES = 8
LANES = 128
HALO = 8


def _params(sem, vmem_mb):
    return pltpu.CompilerParams(dimension_semantics=sem, vmem_limit_bytes=vmem_mb * 1024 * 1024)


def _dot(a, b):
    return jnp.dot(a, b, preferred_element_type=F32)


def _dot_nt(a, b):
    return lax.dot_general(a, b, (((1,), (1,)), ((), ())), preferred_element_type=F32)


def _dot_tn(a, b):
    return lax.dot_general(a, b, (((0,), (0,)), ((), ())), preferred_element_type=F32)


def _sigmoid(x):
    return 0.5 * jnp.tanh(0.5 * x) + 0.5


def _silu(x):
    return x * _sigmoid(x)


def _rms(x):
    return x * lax.rsqrt(jnp.mean(x * x, axis=-1, keepdims=True) + EPS)


def _mod_kernel(c_ref, w_ref, b_ref, o_ref):
    s = _silu(c_ref[...])
    o_ref[...] = jnp.dot(s, w_ref[...], preferred_element_type=F32, precision=HIGHEST) + b_ref[...]


def _modulation(c_all, ada_w, ada_b):
    m, d = c_all.shape
    n = ada_w.shape[1]
    tn = 1024
    return pl.pallas_call(
        _mod_kernel,
        grid=(n // tn,),
        in_specs=[
            pl.BlockSpec((m, d), lambda j: (0, 0)),
            pl.BlockSpec((d, tn), lambda j: (0, j)),
            pl.BlockSpec((1, tn), lambda j: (0, j)),
        ],
        out_specs=pl.BlockSpec((m, tn), lambda j: (0, j)),
        out_shape=jax.ShapeDtypeStruct((m, n), F32),
        compiler_params=_params(("arbitrary",), 32),
        name="mod",
    )(c_all, ada_w, ada_b.reshape(1, n))


def _norm_proj_kernel(x_ref, shift_ref, scale_ref, g_ref, w_ref, h_ref, o_ref):
    y = _rms(x_ref[...]) * g_ref[...]
    h = (y * (1.0 + scale_ref[...]) + shift_ref[...]).astype(h_ref.dtype)
    h_ref[...] = h
    o_ref[...] = _dot(h, w_ref[...]).astype(o_ref.dtype)


def _norm_proj(x, shift, scale, g, w_in_b, n_out, tm):
    b, l, d = x.shape
    per_batch = shift.shape[0] == b and b > 1
    mod_map = (lambda bi, i: (bi, 0, 0)) if per_batch else (lambda bi, i: (0, 0, 0))
    return pl.pallas_call(
        _norm_proj_kernel,
        grid=(b, l // tm),
        in_specs=[
            pl.BlockSpec((None, tm, d), lambda bi, i: (bi, i, 0)),
            pl.BlockSpec((None, 1, d), mod_map),
            pl.BlockSpec((None, 1, d), mod_map),
            pl.BlockSpec((1, d), lambda bi, i: (0, 0)),
            pl.BlockSpec((d, n_out), lambda bi, i: (0, 0)),
        ],
        out_specs=[
            pl.BlockSpec((None, tm, d), lambda bi, i: (bi, i, 0)),
            pl.BlockSpec((None, tm, n_out), lambda bi, i: (bi, i, 0)),
        ],
        out_shape=[
            jax.ShapeDtypeStruct((b, l, d), BF16),
            jax.ShapeDtypeStruct((b, l, n_out), BF16),
        ],
        compiler_params=_params(("arbitrary", "arbitrary"), 40),
        name="norm_proj",
    )(x, shift, scale, g.reshape(1, d), w_in_b)


def _lru_kernel(axl_ref, agl_ref, axc_ref, cw_ref, cb_ref, wg_ref, bg_ref, lam_ref, o_ref,
                xc_ref, yacc_ref, sx_ref, sa_ref, sb_ref, sh_ref, *, nb, cx, t):
    tb = GRID_W
    ctx0 = HALO
    gw = 2 * LRU_BLOCK
    pre = CONV_LEFT
    steps = pre + tb + (HALO - pre)

    zeros_h = jnp.zeros((HALO, LRU_BLOCK), F32)
    for b in range(nb):
        xc_ref[b, 0:HALO, :] = zeros_h
        xc_ref[b, ctx0:ctx0 + cx, :] = axc_ref[b].astype(F32)
        xc_ref[b, ctx0 + cx:ctx0 + cx + HALO, :] = zeros_h

    cw = cw_ref[...]
    cb = cb_ref[...]
    lam = lam_ref[...]
    sp = jnp.maximum(-lam, 0.0) + jnp.log1p(jnp.exp(-jnp.abs(lam)))
    half_neg_c_sp = (-0.5 * LRU_C) * sp

    def zero_history(d):
        sx_ref[d, 0:pre * nb, :] = jnp.zeros((pre * nb, LRU_BLOCK), F32)
        sx_ref[d, (pre + tb) * nb:steps * nb, :] = jnp.zeros(((steps - pre - tb) * nb, LRU_BLOCK), F32)

    def load_ctx_block(d, base):
        for b in range(nb):
            sx_ref[d, pl.ds(b, steps, stride=nb), :] = xc_ref[b, pl.ds(base - pre, steps), :]

    def load_latent_block(d, r0):
        for b in range(nb):
            sx_ref[d, pl.ds(pre * nb + b, tb, stride=nb), :] = axl_ref[b, pl.ds(r0, tb), :].astype(F32)

    def fill_gates(d):
        u = jnp.broadcast_to(cb, (tb * nb, LRU_BLOCK))
        for k in range(CONV_WIDTH):
            u = u + sx_ref[d, k * nb:(k + tb) * nb, :] * cw[k:k + 1, :]
        th = jnp.tanh(_dot(u.astype(BF16), wg_ref[:, d * gw:(d + 1) * gw]) + bg_ref[:, d * gw:(d + 1) * gw])
        r2 = th[:, :LRU_BLOCK] + 1.0
        i2 = th[:, LRU_BLOCK:] + 1.0
        log_a = r2 * half_neg_c_sp[d:d + 1, :]
        a = jnp.exp(log_a)
        half_mult = jnp.sqrt(jnp.tanh(log_a) * (a * a + 1.0) * (-0.25))
        sa_ref[d] = a
        sb_ref[d] = half_mult * (i2 * u)

    def scan_pair(hf, hr):
        for s in range(tb):
            sr = tb - 1 - s
            hf = sa_ref[0, s * nb:(s + 1) * nb, :] * hf + sb_ref[0, s * nb:(s + 1) * nb, :]
            sh_ref[0, s * nb:(s + 1) * nb, :] = hf
            hr = sa_ref[1, sr * nb:(sr + 1) * nb, :] * hr + sb_ref[1, sr * nb:(sr + 1) * nb, :]
            sh_ref[1, sr * nb:(sr + 1) * nb, :] = hr
        return hf, hr

    n_c, n_l = cx // tb, t // tb

    def ctx_body(j, hs):
        load_ctx_block(0, pl.multiple_of(ctx0 + j * tb, SUBLANES))
        load_ctx_block(1, pl.multiple_of(ctx0 + (n_c - 1 - j) * tb, SUBLANES))
        fill_gates(0)
        fill_gates(1)
        return scan_pair(*hs)

    def emit(d, r0, final):
        for b in range(nb):
            yb = sh_ref[d, pl.ds(b, tb, stride=nb), :]
            if final:
                gate = jax.nn.gelu(agl_ref[b, pl.ds(r0, tb), :].astype(F32))
                o_ref[b, pl.ds(r0, tb), :] = ((yacc_ref[b, pl.ds(r0, tb), :] + yb) * gate).astype(o_ref.dtype)
            else:
                yacc_ref[b, pl.ds(r0, tb), :] = yb

    def lat_body(j, hs, final_f, final_r):
        rf = pl.multiple_of(j * tb, tb)
        rr = pl.multiple_of((n_l - 1 - j) * tb, tb)
        load_latent_block(0, rf)
        load_latent_block(1, rr)
        fill_gates(0)
        fill_gates(1)
        hs = scan_pair(*hs)
        emit(0, rf, final_f)
        emit(1, rr, final_r)
        return hs

    h0 = jnp.zeros((nb, LRU_BLOCK), F32)
    hs = lax.fori_loop(0, n_c, ctx_body, (h0, h0))
    zero_history(0)
    zero_history(1)
    half = n_l // 2
    hs = lax.fori_loop(0, half, functools.partial(lat_body, final_f=False, final_r=False), hs)
    if n_l % 2:
        hs = lat_body(jnp.int32(half), hs, False, True)
    lax.fori_loop(n_l - half, n_l, functools.partial(lat_body, final_f=True, final_r=True), hs)


def _lru(axl, axc, conv_w, conv_b, wg, bg, lam, nb, cx, t):
    w = conv_w.shape[1]
    g = w // LRU_BLOCK
    kern = functools.partial(_lru_kernel, nb=nb, cx=cx, t=t)
    return pl.pallas_call(
        kern,
        grid=(g,),
        in_specs=[
            pl.BlockSpec((nb, t, LRU_BLOCK), lambda j: (0, 0, j)),
            pl.BlockSpec((nb, t, LRU_BLOCK), lambda j: (0, 0, g + j)),
            pl.BlockSpec((nb, cx, LRU_BLOCK), lambda j: (0, 0, j)),
            pl.BlockSpec((CONV_WIDTH, LRU_BLOCK), lambda j: (0, j)),
            pl.BlockSpec((1, LRU_BLOCK), lambda j: (0, j)),
            pl.BlockSpec((None, LRU_BLOCK, 4 * LRU_BLOCK), lambda j: (j, 0, 0)),
            pl.BlockSpec((None, 1, 4 * LRU_BLOCK), lambda j: (j, 0, 0)),
            pl.BlockSpec((2, LRU_BLOCK), lambda j: (0, j)),
        ],
        out_specs=pl.BlockSpec((nb, t, LRU_BLOCK), lambda j: (0, 0, j)),
        out_shape=jax.ShapeDtypeStruct((nb, t, w), BF16),
        scratch_shapes=[
            pltpu.VMEM((nb, cx + 2 * HALO, LRU_BLOCK), F32),
            pltpu.VMEM((nb, t, LRU_BLOCK), F32),
            pltpu.VMEM((2, (GRID_W + HALO) * nb, LRU_BLOCK), F32),
            pltpu.VMEM((2, GRID_W * nb, LRU_BLOCK), F32),
            pltpu.VMEM((2, GRID_W * nb, LRU_BLOCK), F32),
            pltpu.VMEM((2, GRID_W * nb, LRU_BLOCK), F32),
        ],
        compiler_params=_params(("arbitrary",), 56),
        name="lru",
    )(axl, axl, axc, conv_w, conv_b.reshape(1, w), wg, bg, lam)


def _hgrn_kernel(hc_ref, hl_ref, wq_ref, wff_ref, wfb_ref, wv_ref, wg_ref, lb_ref, ng_ref, o_ref,
                 p_ref, *scratch, cx, t):
    w = jnp.concatenate([wq_ref[...], wff_ref[...], wfb_ref[...], wv_ref[...], wg_ref[...]], axis=1)
    p_ref[0:cx, :] = _dot(hc_ref[...], w)
    p_ref[cx:cx + t, :] = _dot(hl_ref[...], w)
    for hh in range(HG_HEADS_PER_STEP):
        _hgrn_head(hh, lb_ref, ng_ref, o_ref, p_ref, *scratch, cx=cx, t=t)


def _hgrn_head(hh, lb_ref, ng_ref, o_ref,
               p_ref, sprev_s, q_s, v_s, k_s, hl_s, gc_s, bk_s, sc_s, aq_s, upd_ref, dec_ref,
               *, cx, t):
    c = HG_CHUNK
    dk = HG_DK
    n_c, n_l = cx // c, t // c
    n_all = n_c + n_l

    def col(group):
        start = (group * HG_HEADS_PER_STEP + hh) * dk
        return slice(start, start + dk)

    lb = lb_ref[hh]
    f_half = 0.5 * (1.0 - lb)
    f_mid = lb + f_half
    ri = lax.broadcasted_iota(I32, (c, c), 0)
    ci = lax.broadcasted_iota(I32, (c, c), 1)
    keep = (ri >= ci, ci >= ri)
    mid = (c // 2 - 1, c // 2)
    end = (c - 1, 0)

    def stage_gates(j, carry):
        r0 = pl.multiple_of(j * c, c)
        q_s[pl.ds(r0, c), :] = _silu(p_ref[pl.ds(r0, c), col(0)])
        v_s[pl.ds(r0, c), :] = p_ref[pl.ds(r0, c), col(3)].astype(BF16)
        for d in range(2):
            f = f_mid + f_half * jnp.tanh(0.5 * p_ref[pl.ds(r0, c), col(1 + d)])
            logf = jnp.log(f)
            hi = logf.astype(BF16)
            lo = (logf - hi.astype(F32)).astype(BF16)
            k_s[d, pl.ds(r0, c), :] = 1.0 - f
            hl_s[d, pl.ds(r0, c), :] = jnp.concatenate([hi, lo], axis=1)
        return carry

    lax.fori_loop(0, n_all, stage_gates, 0, unroll=4)

    def stage_cumsum(j, carry):
        r0 = pl.multiple_of(j * c, c)
        for d in range(2):
            s2 = _dot(keep[d].astype(BF16), hl_s[d, pl.ds(r0, c), :])
            gc_s[d, pl.ds(r0, c), :] = s2[:, :dk] + s2[:, dk:]
        return carry

    lax.fori_loop(0, n_all, stage_cumsum, 0, unroll=12)

    def stage_scores(j, carry):
        r0 = pl.multiple_of(j * c, c)
        q = q_s[pl.ds(r0, c), :]
        scores = None
        for d in range(2):
            gc = gc_s[d, pl.ds(r0, c), :]
            k = k_s[d, pl.ds(r0, c), :]
            g_mid = gc[mid[d]:mid[d] + 1, :]
            g_end = gc[end[d]:end[d] + 1, :]
            qa = q * jnp.exp(gc - g_mid)
            kb = k * jnp.exp(g_mid - gc)
            sc = jnp.where(keep[d], _dot_nt(qa.astype(BF16), kb.astype(BF16)), 0.0)
            scores = sc if scores is None else scores + sc
            bk_s[d, pl.ds(r0, c), :] = (kb * jnp.exp(g_end - g_mid)).astype(BF16)
            aq_s[pl.ds(r0, c), d * dk:(d + 1) * dk] = (qa * jnp.exp(g_mid)).astype(BF16)
            dec_ref[d, j] = jnp.broadcast_to(jnp.exp(g_end), (SUBLANES, dk))
        sc_s[pl.ds(r0, c), :] = scores.astype(BF16)
        return carry

    lax.fori_loop(0, n_all, stage_scores, 0, unroll=6)

    def stage_updates(j, carry):
        r0 = pl.multiple_of(j * c, c)
        v = v_s[pl.ds(r0, c), :]
        for d in range(2):
            upd_ref[d, j] = _dot_tn(v, bk_s[d, pl.ds(r0, c), :])
        return carry

    lax.fori_loop(0, n_all, stage_updates, 0, unroll=12)

    def advance(d, jj, st):
        return st * dec_ref[d, jj][0:1, :] + upd_ref[d, jj]

    def ctx_body(j, sts):
        return advance(0, j, sts[0]), advance(1, n_c - 1 - j, sts[1])

    def lat_body(j, sts):
        jf, jr = j, n_l - 1 - j
        sprev_s[jf, :, 0:dk] = sts[0].astype(BF16)
        sprev_s[jr, :, dk:2 * dk] = sts[1].astype(BF16)
        return advance(0, n_c + jf, sts[0]), advance(1, n_c + jr, sts[1])

    zero = jnp.zeros((dk, dk), F32)
    sts = lax.fori_loop(0, n_c, ctx_body, (zero, zero), unroll=2)
    lax.fori_loop(0, n_l, lat_body, sts, unroll=4)

    ng = ng_ref[...]

    def stage_outputs(j, carry):
        r0 = pl.multiple_of(j * c, c)
        rc = pl.multiple_of(cx + r0, c)
        o = _dot(sc_s[pl.ds(rc, c), :], v_s[pl.ds(rc, c), :]) + _dot_nt(aq_s[pl.ds(rc, c), :], sprev_s[j])
        g = p_ref[pl.ds(rc, c), col(4)]
        o_ref[pl.ds(r0, c), hh * dk:(hh + 1) * dk] = (_rms(o) * ng * _silu(g)).astype(o_ref.dtype)
        return carry

    lax.fori_loop(0, n_l, stage_outputs, 0, unroll=16)


def _hgrn(hc, hl, w_in_b, col0, heads, lb, norm_g):
    b, cx, d = hc.shape
    t = hl.shape[1]
    dk = HG_DK
    n_all = (cx + t) // HG_CHUNK
    kern = functools.partial(_hgrn_kernel, cx=cx, t=t)
    hps = HG_HEADS_PER_STEP
    wide = hps * dk
    assert heads % hps == 0 and col0 % wide == 0

    def group(gi):
        return pl.BlockSpec((d, wide), lambda bi, h: (0, col0 // wide + gi * (heads // hps) + h))

    return pl.pallas_call(
        kern,
        grid=(b, heads // hps),
        in_specs=[
            pl.BlockSpec((None, cx, d), lambda bi, h: (bi, 0, 0)),
            pl.BlockSpec((None, t, d), lambda bi, h: (bi, 0, 0)),
            group(0), group(1), group(2), group(3), group(4),
            pl.BlockSpec((hps, 1, dk), lambda bi, h: (h, 0, 0)),
            pl.BlockSpec((1, dk), lambda bi, h: (0, 0)),
        ],
        out_specs=pl.BlockSpec((None, t, wide), lambda bi, h: (bi, 0, h)),
        out_shape=jax.ShapeDtypeStruct((b, t, heads * dk), BF16),
        scratch_shapes=[
            pltpu.VMEM((cx + t, 5 * wide), F32),
            pltpu.VMEM((t // HG_CHUNK, dk, 2 * dk), BF16),
            pltpu.VMEM((cx + t, dk), F32),
            pltpu.VMEM((cx + t, dk), BF16),
            pltpu.VMEM((2, cx + t, dk), F32),
            pltpu.VMEM((2, cx + t, 2 * dk), BF16),
            pltpu.VMEM((2, cx + t, dk), F32),
            pltpu.VMEM((2, cx + t, dk), BF16),
            pltpu.VMEM((cx + t, HG_CHUNK), BF16),
            pltpu.VMEM((cx + t, 2 * dk), BF16),
            pltpu.VMEM((2, n_all, dk, dk), F32),
            pltpu.VMEM((2, n_all, SUBLANES, dk), F32),
        ],
        compiler_params=_params(("arbitrary", "arbitrary"), 56),
        name="hgrn",
    )(hc, hl, w_in_b, w_in_b, w_in_b, w_in_b, w_in_b, lb, norm_g.reshape(1, dk))


def _merge_kernel(x_ref, h_ref, ya_ref, yb_ref, wm_ref, wa_ref, wb_ref, wo_ref,
                  gate_ref, shift_ref, scale_ref, g2_ref, rwt_ref, rb_ref,
                  x1_ref, hs_ref, pos_ref, cnt_ref, h2b_s, *, tm, d, ne, cap):
    m = _dot(h_ref[...], wm_ref[...])
    za = _dot(ya_ref[...], wa_ref[...])
    zb = _dot(yb_ref[...], wb_ref[...])
    mix = _sigmoid(m[:, :d]) * za + _sigmoid(m[:, d:]) * zb
    out = _dot(mix.astype(BF16), wo_ref[...])
    x1 = x_ref[...] + gate_ref[...] * out
    x1_ref[...] = x1
    h2 = _rms(x1) * g2_ref[...] * (1.0 + scale_ref[...]) + shift_ref[...]
    h2b_s[...] = h2.astype(BF16)

    logits = lax.dot_general(rwt_ref[...], h2, (((1,), (1,)), ((), ())),
                             preferred_element_type=F32, precision=HIGHEST) + rb_ref[...]
    eid = lax.broadcasted_iota(I32, (ne, tm), 0)
    neg = jnp.float32(-jnp.inf)
    work = logits
    hot = jnp.zeros((ne, tm), F32)
    vals, sels = [], []
    for _ in range(TOP_K):
        mx = jnp.max(work, axis=0, keepdims=True)
        ix = jnp.min(jnp.where(work == mx, eid, ne), axis=0, keepdims=True)
        sel = eid == ix
        work = jnp.where(sel, neg, work)
        hot = hot + sel.astype(F32)
        vals.append(mx)
        sels.append(sel)
    ex = [jnp.exp(v - vals[0]) for v in vals]
    den = sum(ex[1:], ex[0])
    gts = [e / den for e in ex]

    ti = lax.broadcasted_iota(I32, (tm, tm), 0)
    tj = lax.broadcasted_iota(I32, (tm, tm), 1)
    strict = (ti < tj).astype(BF16)
    before = _dot(hot.astype(BF16), strict)
    cnt = jnp.sum(hot, axis=1, keepdims=True)
    nchunks = jnp.ceil(cnt * (1.0 / MOE_CHUNK))
    ei = lax.broadcasted_iota(I32, (ne, ne), 0)
    ej = lax.broadcasted_iota(I32, (ne, ne), 1)
    earlier = (ej < ei).astype(BF16)
    start = _dot(earlier, jnp.broadcast_to(nchunks, (ne, LANES)).astype(BF16))[:, 0:1] * MOE_CHUNK
    where_to = before + start
    poss = [jnp.sum(jnp.where(s, where_to, 0.0), axis=0, keepdims=True).astype(I32) for s in sels]

    rb_rows = SORT_ROWS
    lane = lax.broadcasted_iota(I32, (rb_rows, LANES), 1)

    def sort_rows(i, carry):
        r0 = pl.multiple_of(i * rb_rows, rb_rows)
        pid = lax.broadcasted_iota(I32, (rb_rows, tm), 0)
        gsel = jnp.zeros((rb_rows, tm), F32)
        for k in range(TOP_K):
            gsel = jnp.where(pid == poss[k] - r0, gts[k], gsel)
        place = jnp.where(gsel != 0.0, 1.0, 0.0)
        hs_ref[pl.ds(r0, rb_rows), 0:d] = _dot(place.astype(BF16), h2b_s[...]).astype(BF16)
        g = jnp.sum(gsel, axis=1, keepdims=True)
        g_hi = g.astype(BF16).astype(F32)
        hs_ref[pl.ds(r0, rb_rows), d:d + LANES] = jnp.where(lane < LANES // 2, g_hi, g - g_hi).astype(BF16)
        return carry

    lax.fori_loop(0, cap // rb_rows, sort_rows, 0)

    row = lax.broadcasted_iota(I32, (SUBLANES, tm), 0)
    packed = jnp.zeros((SUBLANES, tm), I32)
    for k, p in enumerate(poss):
        packed = jnp.where(row == k, jnp.broadcast_to(p, (SUBLANES, tm)), packed)
    pos_ref[...] = packed
    cnt_ref[...] = jnp.broadcast_to(cnt, cnt_ref.shape)


def _merge(x, hl, ya, yb, w_m, w_ba, w_bb, w_out, gate2, shift3, scale4, g2, rwt, rb, tm):
    b, t, d = x.shape
    ne = rwt.shape[0]
    nt = t // tm
    tok = b * t
    cap = _sorted_rows(tm, ne)
    kern = functools.partial(_merge_kernel, tm=tm, d=d, ne=ne, cap=cap)
    row_spec = pl.BlockSpec((None, tm, d), lambda bi, i: (bi, i, 0))
    mod_spec = pl.BlockSpec((None, 1, d), lambda bi, i: (bi, 0, 0))
    flat = lambda bi, i: (bi * nt + i, 0)
    lane = lambda bi, i: (0, bi * nt + i)

    def whole(shape):
        return pl.BlockSpec(shape, lambda bi, i: (0,) * len(shape))

    return pl.pallas_call(
        kern,
        grid=(b, nt),
        in_specs=[
            row_spec, row_spec, row_spec, row_spec,
            whole((d, 2 * d)), whole((d, d)), whole((d, d)), whole((d, d)),
            mod_spec, mod_spec, mod_spec,
            whole((1, d)), whole((ne, d)), whole((ne, 1)),
        ],
        out_specs=[
            row_spec,
            pl.BlockSpec((cap, d + LANES), flat),
            pl.BlockSpec((SUBLANES, tm), lane),
            pl.BlockSpec((None, ne, LANES), lambda bi, i: (bi * nt + i, 0, 0)),
        ],
        out_shape=[
            jax.ShapeDtypeStruct((b, t, d), F32),
            jax.ShapeDtypeStruct((b * nt * cap, d + LANES), BF16),
            jax.ShapeDtypeStruct((SUBLANES, tok), I32),
            jax.ShapeDtypeStruct((b * nt, ne, LANES), F32),
        ],
        scratch_shapes=[pltpu.VMEM((tm, d), BF16)],
        compiler_params=_params(("arbitrary", "arbitrary"), 56),
        name="merge",
    )(x, hl, ya, yb, w_m, w_ba, w_bb, w_out, gate2, shift3, scale4, g2.reshape(1, d), rwt, rb)


def _experts_kernel(bexp_ref, bact_ref, src_ref, src1_ref, src2_ref, hs_hbm, w1_ref, b1_ref, w2_ref, b2_ref,
                    y_ref, xbuf, act_s, w1p_s, w2_s, gsem, *, blk, nblocks, d):
    j = pl.program_id(0)
    slot = lax.rem(j, MOE_RING)
    far = lax.rem(j + 2, MOE_RING)
    f = w2_s.shape[1]
    pw = 2 * LANES
    n1 = 2 * f // pw
    n2 = d // pw
    nch = blk // MOE_CHUNK
    active = bact_ref[j] == 1

    def request(idx_ref, buf_slot):
        for c in range(nch):
            src = pl.multiple_of(idx_ref[0, 0, c] * MOE_CHUNK, MOE_CHUNK)
            pltpu.make_async_copy(hs_hbm.at[pl.ds(src, MOE_CHUNK), :],
                                  xbuf.at[buf_slot, pl.ds(c * MOE_CHUNK, MOE_CHUNK), :], gsem.at[buf_slot]).start()

    @pl.when(j == 0)
    def _():
        request(src_ref, 0)
        request(src1_ref, 1)

    @pl.when(j + 2 < nblocks)
    def _():
        request(src2_ref, far)

    pltpu.make_async_copy(hs_hbm.at[pl.ds(0, blk), :], xbuf.at[slot], gsem.at[slot]).wait()

    new_expert = (j == 0) | (bexp_ref[j] != bexp_ref[jnp.maximum(j - 1, 0)])

    @pl.when(new_expert & active)
    def _():
        src = lax.broadcasted_iota(I32, (pw, pw), 0)
        dst = lax.broadcasted_iota(I32, (pw, pw), 1)
        want = jnp.where(dst < pw // 2, 2 * dst, 2 * (dst - pw // 2) + 1)
        sel = (src == want).astype(BF16)
        for cb in range(n1):
            w1p_s[cb] = _dot(w1_ref[:, cb * pw:(cb + 1) * pw].astype(BF16), sel).astype(BF16)
        for cb in range(n2):
            w2_s[cb] = w2_ref[:, cb * pw:(cb + 1) * pw].astype(BF16)

    @pl.when(active)
    def _():
        x = xbuf[slot, :, 0:d]
        gate = (xbuf[slot, :, d:d + 1].astype(F32) + xbuf[slot, :, d + LANES // 2:d + LANES // 2 + 1].astype(F32))
        for n in range(n1):
            h = _dot(x, w1p_s[n]) + b1_ref[n]
            hg = jnp.minimum(h[:, :LANES], SWIGLU_LIMIT)
            hu = jnp.clip(h[:, LANES:], -SWIGLU_LIMIT, SWIGLU_LIMIT)
            act_s[n] = (hg * _sigmoid(SWIGLU_ALPHA * hg) * (hu + 1.0)).astype(BF16)
        act = jnp.concatenate([act_s[i] for i in range(n1)], axis=1)
        for n in range(n2):
            y = _dot(act, w2_s[n]) + b2_ref[n]
            y_ref[:, n * pw:(n + 1) * pw] = (y * gate).astype(y_ref.dtype)

    @pl.when(jnp.logical_not(active))
    def _():
        y_ref[...] = jnp.zeros_like(y_ref)


def _experts(block_expert, block_active, src3, hs, w1, b1p, w2, b2, blk):
    nblocks = block_expert.shape[0]
    f, d = w2.shape[1], w2.shape[2]
    pw = 2 * LANES
    n1, n2 = 2 * f // pw, d // pw
    nch = blk // MOE_CHUNK
    b1p = b1p.reshape(-1, n1, 1, pw)
    b2 = b2.reshape(-1, n2, 1, pw)
    kern = functools.partial(_experts_kernel, blk=blk, nblocks=nblocks, d=d)
    wmap = lambda j, be, ba: (be[j], 0, 0)

    def idx_spec(ahead):
        return pl.BlockSpec((1, 1, nch), lambda j, be, ba: (jnp.minimum(j + ahead, nblocks - 1), 0, 0),
                            memory_space=pltpu.SMEM)

    grid_spec = pltpu.PrefetchScalarGridSpec(
        num_scalar_prefetch=2,
        grid=(nblocks,),
        in_specs=[
            idx_spec(0), idx_spec(1), idx_spec(2),
            pl.BlockSpec(memory_space=pl.ANY),
            pl.BlockSpec((None, d, 2 * f), wmap),
            pl.BlockSpec((None, n1, 1, pw), lambda j, be, ba: (be[j], 0, 0, 0)),
            pl.BlockSpec((None, f, d), wmap),
            pl.BlockSpec((None, n2, 1, pw), lambda j, be, ba: (be[j], 0, 0, 0)),
        ],
        out_specs=pl.BlockSpec((blk, d), lambda j, be, ba: (j, 0)),
        scratch_shapes=[
            pltpu.VMEM((MOE_RING, blk, d + LANES), BF16),
            pltpu.VMEM((n1, blk, LANES), BF16),
            pltpu.VMEM((n1, d, pw), BF16),
            pltpu.VMEM((n2, f, pw), BF16),
            pltpu.SemaphoreType.DMA((MOE_RING,)),
        ],
    )
    return pl.pallas_call(
        kern,
        grid_spec=grid_spec,
        out_shape=jax.ShapeDtypeStruct((nblocks * blk, d), BF16),
        compiler_params=_params(("arbitrary",), 56),
        name="experts",
    )(block_expert, block_active, src3, src3, src3, hs, w1, b1p, w2, b2)


def _combine_kernel(dst_ref, dstn_ref, y_hbm, x1_ref, pos_ref, gate_ref, fg_ref, o_ref, ybuf, sem,
                    *, tm, cap, ntiles):
    i = pl.program_id(0)
    slot = lax.rem(i, 2)
    nch = cap // MOE_CHUNK

    def request(idx_ref, buf_slot):
        def body(c, carry):
            src = pl.multiple_of(idx_ref[0, 0, c] * MOE_CHUNK, MOE_CHUNK)
            dst = pl.multiple_of(c * MOE_CHUNK, MOE_CHUNK)
            pltpu.make_async_copy(y_hbm.at[pl.ds(src, MOE_CHUNK), :],
                                  ybuf.at[buf_slot, pl.ds(dst, MOE_CHUNK), :], sem.at[buf_slot]).start()
            return carry
        lax.fori_loop(0, nch, body, 0, unroll=8)

    @pl.when(i == 0)
    def _():
        request(dst_ref, 0)

    @pl.when(i + 1 < ntiles)
    def _():
        request(dstn_ref, 1 - slot)

    pltpu.make_async_copy(y_hbm.at[pl.ds(0, cap), :], ybuf.at[slot], sem.at[slot]).wait()

    pos = pos_ref[...]
    acc = None
    for c0 in range(0, cap, SORT_ROWS):
        col = lax.broadcasted_iota(I32, (tm, SORT_ROWS), 1) + c0
        pick = jnp.zeros((tm, SORT_ROWS), F32)
        for k in range(TOP_K):
            pick = jnp.where(col == pos[:, k:k + 1], 1.0, pick)
        part = _dot(pick.astype(BF16), ybuf[slot, c0:c0 + SORT_ROWS, :])
        acc = part if acc is None else acc + part
    x2 = x1_ref[...] + gate_ref[...] * acc
    o_ref[...] = _rms(x2) * fg_ref[...]


def _combine(dst3, y, x1, pos_t, gate5, final_g, tm, cap):
    b, t, d = x1.shape
    nt = t // tm
    ntiles = b * nt
    nch = cap // MOE_CHUNK
    kern = functools.partial(_combine_kernel, tm=tm, cap=cap, ntiles=ntiles)
    row_map = lambda i: (i // nt, i % nt, 0)
    return pl.pallas_call(
        kern,
        grid=(ntiles,),
        in_specs=[
            pl.BlockSpec((1, 1, nch), lambda i: (i, 0, 0), memory_space=pltpu.SMEM),
            pl.BlockSpec((1, 1, nch), lambda i: (jnp.minimum(i + 1, ntiles - 1), 0, 0), memory_space=pltpu.SMEM),
            pl.BlockSpec(memory_space=pl.ANY),
            pl.BlockSpec((None, tm, d), row_map),
            pl.BlockSpec((tm, TOP_K), lambda i: (i, 0)),
            pl.BlockSpec((None, 1, d), lambda i: (i // nt, 0, 0)),
            pl.BlockSpec((1, d), lambda i: (0, 0)),
        ],
        out_specs=pl.BlockSpec((None, tm, d), row_map),
        out_shape=jax.ShapeDtypeStruct((b, t, d), F32),
        scratch_shapes=[
            pltpu.VMEM((2, cap, d), BF16),
            pltpu.SemaphoreType.DMA((2,)),
        ],
        compiler_params=_params(("arbitrary",), 48),
        name="combine",
    )(dst3, dst3, y, x1, pos_t, gate5, final_g.reshape(1, d))


def _sorted_rows(tm, ne):
    need = tm * TOP_K + ne * (MOE_CHUNK - 1) + MOE_CHUNK
    return -(-need // SORT_ROWS) * SORT_ROWS


def _pick_tile(n, pref):
    tm = pref
    while n % tm:
        tm //= 2
    return tm


def kernel(x, c, ctx, c_ctx, ada_w, ada_b, norm1_g, norm2_g, w_in, lru_conv_w, lru_conv_b, lru_wa, lru_ba, lru_wx, lru_bx, lru_lam, hg_lb_logits, hg_norm_g, w_branch_a, w_branch_b, w_out, router_w, router_b, moe_w1, moe_b1, moe_w2, moe_b2, final_g):
    b, t, d = x.shape
    cx = ctx.shape[1]
    layer = 0
    w_lru = lru_conv_w.shape[2]
    qk = (w_in.shape[2] - 2 * w_lru - 2 * d) // 5
    heads = qk // HG_DK
    ne = router_w.shape[2]

    pad = (-(b + 1)) % SUBLANES
    c_all = jnp.concatenate([c, c_ctx[None, :], jnp.zeros((pad, d), F32)], axis=0)
    mod = _modulation(c_all, ada_w[layer], ada_b[layer])
    mod_l = mod[:b].reshape(b, N_MOD, 1, d)
    mod_c = mod[b].reshape(N_MOD, 1, 1, d)

    w_in_b = w_in[layer].astype(BF16)
    hg0 = 2 * w_lru
    w_m = w_in_b[:, hg0 + 5 * qk:]
    g_blocks = w_lru // LRU_BLOCK
    wg = jnp.concatenate([lru_wa[layer, 0], lru_wx[layer, 0], lru_wa[layer, 1], lru_wx[layer, 1]], axis=-1)
    wg = (0.5 * wg).astype(BF16)
    bg = jnp.concatenate([lru_ba[layer, 0], lru_bx[layer, 0], lru_ba[layer, 1], lru_bx[layer, 1]], axis=-1)
    bg = (0.5 * bg).reshape(g_blocks, 1, 4 * LRU_BLOCK)
    lb_all = jnp.cumsum(jax.nn.softmax(hg_lb_logits.astype(F32), axis=0), axis=0)
    lb = lb_all[layer].reshape(heads, 1, HG_DK)

    tm_n = _pick_tile(t, 512)
    hl, axl = _norm_proj(x, mod_l[:, 0], mod_l[:, 1], norm1_g[layer], w_in_b, 2 * w_lru, tm_n)
    hc, axc = _norm_proj(ctx, mod_c[0], mod_c[1], norm1_g[layer], w_in_b, 2 * w_lru, _pick_tile(cx, 256))
    ya = _lru(axl, axc, lru_conv_w[layer], lru_conv_b[layer], wg, bg, lru_lam[layer], b, cx, t)
    yb = _hgrn(hc, hl, w_in_b, hg0, heads, lb, hg_norm_g[layer])

    rwt = router_w[layer].T
    rb = router_b[layer].reshape(ne, 1)
    tm = _pick_tile(t, 512)
    x1, hs, pos8, cnt = _merge(
        x, hl, ya, yb, w_m, w_branch_a[layer].astype(BF16), w_branch_b[layer].astype(BF16),
        w_out[layer].astype(BF16), mod_l[:, 2], mod_l[:, 3], mod_l[:, 4], norm2_g[layer], rwt, rb, tm)

    blk = MOE_BLOCK
    ch = MOE_CHUNK
    bpc = blk // ch
    tok = b * t
    ntile = tok // tm
    cap = _sorted_rows(tm, ne)
    tch = cap // ch
    run = (cnt[:, :, 0].astype(I32) + ch - 1) // ch
    loc_end = jnp.cumsum(run, axis=1)
    loc_start = loc_end - run
    upto = jnp.cumsum(run, axis=0)
    total = upto[-1]
    padded = (total + bpc - 1) // bpc * bpc
    gend = jnp.cumsum(padded)
    gstart = gend - padded
    where_run = gstart[None, :] + upto - run
    max_chunks = tok * TOP_K // ch + ntile * ne + ne * bpc
    nblocks = -(-max_chunks // bpc)

    eids = jnp.arange(ne, dtype=I32)
    tids = jnp.arange(ntile, dtype=I32)
    g = jnp.arange(nblocks * bpc, dtype=I32)
    g_e = jnp.minimum(jnp.sum(g[:, None] >= gend[None, :], axis=1), ne - 1).astype(I32)
    is_e = g_e[:, None] == eids[None, :]
    off = g - jnp.sum(jnp.where(is_e, gstart[None, :], 0), axis=1)
    g_total = jnp.sum(jnp.where(is_e, total[None, :], 0), axis=1)
    upto_g = jnp.sum(jnp.where(is_e[:, None, :], upto[None, :, :], 0), axis=2)
    g_i = jnp.minimum(jnp.sum(off[:, None] >= upto_g, axis=1), ntile - 1).astype(I32)
    both = (g_i[:, None] == tids[None, :])[:, :, None] & is_e[:, None, :]

    def at_run(table):
        return jnp.sum(jnp.where(both, table[None, :, :], 0), axis=(1, 2))

    within = off - (at_run(upto) - at_run(run))
    zero_chunk = tch - 1
    src = jnp.where((off < g_total) & (g < gend[-1]), g_i * tch + at_run(loc_start) + within, zero_chunk)

    block_first = jnp.arange(nblocks, dtype=I32) * bpc
    block_active = block_first < gend[-1]
    block_expert = jnp.minimum(jnp.sum(block_first[:, None] >= gend[None, :], axis=1), ne - 1).astype(I32)
    last_expert = jnp.max(jnp.where(block_active, block_expert, 0))
    block_expert = jnp.where(block_active, block_expert, last_expert).astype(I32)

    c = jnp.arange(tch, dtype=I32)
    c_e = jnp.minimum(jnp.sum(c[None, :, None] >= loc_end[:, None, :], axis=2), ne - 1).astype(I32)
    is_ce = c_e[:, :, None] == eids[None, None, :]
    back = jnp.sum(jnp.where(is_ce, (where_run - loc_start)[:, None, :], 0), axis=2) + c[None, :]
    back = jnp.where(c[None, :] < loc_end[:, -1:], back, 0)

    b1 = moe_b1[layer]
    f = b1.shape[1] // 2
    b1p = b1.reshape(ne, f // LANES, LANES, 2).transpose(0, 1, 3, 2).reshape(ne, 1, 2 * f)
    y = _experts(block_expert, block_active.astype(I32), src.reshape(nblocks, 1, bpc), hs, moe_w1[layer], b1p,
                 moe_w2[layer], moe_b2[layer].reshape(ne, 1, d), blk)
    return _combine(back.reshape(ntile, 1, tch), y, x1, pos8[:TOP_K].T, mod_l[:, 5], final_g, tm, cap)
```

```python
import functools

import jax
import jax.numpy as jnp
from jax import lax
from jax.experimental import pallas as pl
from jax.experimental.pallas import tpu as pltpu

F32 = jnp.float32
BF16 = jnp.bfloat16
I32 = jnp.int32
HIGHEST = lax.Precision.HIGHEST

EPS = 1e-6
N_MOD = 6
GRID_W = 64
CONV_LEFT = 2
CONV_WIDTH = 4
LRU_C = 8.0
LRU_BLOCK = 128
HG_DK = 128
HG_CHUNK = 64
HG_HEADS_PER_STEP = 2
TOP_K = 4
SWIGLU_LIMIT = 7.0
SWIGLU_ALPHA = 1.702
MOE_BLOCK = 512
MOE_CHUNK = 16
SORT_ROWS = 512
MOE_RING = 3
SUBLANES = 8
LANES = 128
HALO = 8


def _params(sem, vmem_mb):
    return pltpu.CompilerParams(dimension_semantics=sem, vmem_limit_bytes=vmem_mb * 1024 * 1024)


def _dot(a, b):
    return jnp.dot(a, b, preferred_element_type=F32)


def _dot_nt(a, b):
    return lax.dot_general(a, b, (((1,), (1,)), ((), ())), preferred_element_type=F32)


def _dot_tn(a, b):
    return lax.dot_general(a, b, (((0,), (0,)), ((), ())), preferred_element_type=F32)


def _sigmoid(x):
    return 0.5 * jnp.tanh(0.5 * x) + 0.5


def _silu(x):
    return x * _sigmoid(x)


def _rms(x):
    return x * lax.rsqrt(jnp.mean(x * x, axis=-1, keepdims=True) + EPS)


def _mod_kernel(c_ref, w_ref, b_ref, o_ref):
    s = _silu(c_ref[...])
    o_ref[...] = jnp.dot(s, w_ref[...], preferred_element_type=F32, precision=HIGHEST) + b_ref[...]


def _modulation(c_all, ada_w, ada_b):
    m, d = c_all.shape
    n = ada_w.shape[1]
    tn = 1024
    return pl.pallas_call(
        _mod_kernel,
        grid=(n // tn,),
        in_specs=[
            pl.BlockSpec((m, d), lambda j: (0, 0)),
            pl.BlockSpec((d, tn), lambda j: (0, j)),
            pl.BlockSpec((1, tn), lambda j: (0, j)),
        ],
        out_specs=pl.BlockSpec((m, tn), lambda j: (0, j)),
        out_shape=jax.ShapeDtypeStruct((m, n), F32),
        compiler_params=_params(("arbitrary",), 32),
        name="mod",
    )(c_all, ada_w, ada_b.reshape(1, n))


def _norm_proj_kernel(x_ref, shift_ref, scale_ref, g_ref, w_ref, h_ref, o_ref):
    y = _rms(x_ref[...]) * g_ref[...]
    h = (y * (1.0 + scale_ref[...]) + shift_ref[...]).astype(h_ref.dtype)
    h_ref[...] = h
    o_ref[...] = _dot(h, w_ref[...]).astype(o_ref.dtype)


def _norm_proj(x, shift, scale, g, w_in_b, n_out, tm):
    b, l, d = x.shape
    per_batch = shift.shape[0] == b and b > 1
    mod_map = (lambda bi, i: (bi, 0, 0)) if per_batch else (lambda bi, i: (0, 0, 0))
    return pl.pallas_call(
        _norm_proj_kernel,
        grid=(b, l // tm),
        in_specs=[
            pl.BlockSpec((None, tm, d), lambda bi, i: (bi, i, 0)),
            pl.BlockSpec((None, 1, d), mod_map),
            pl.BlockSpec((None, 1, d), mod_map),
            pl.BlockSpec((1, d), lambda bi, i: (0, 0)),
            pl.BlockSpec((d, n_out), lambda bi, i: (0, 0)),
        ],
        out_specs=[
            pl.BlockSpec((None, tm, d), lambda bi, i: (bi, i, 0)),
            pl.BlockSpec((None, tm, n_out), lambda bi, i: (bi, i, 0)),
        ],
        out_shape=[
            jax.ShapeDtypeStruct((b, l, d), BF16),
            jax.ShapeDtypeStruct((b, l, n_out), BF16),
        ],
        compiler_params=_params(("arbitrary", "arbitrary"), 40),
        name="norm_proj",
    )(x, shift, scale, g.reshape(1, d), w_in_b)


def _lru_kernel(axl_ref, agl_ref, axc_ref, cw_ref, cb_ref, wg_ref, bg_ref, lam_ref, o_ref,
                xc_ref, yacc_ref, sx_ref, sa_ref, sb_ref, sh_ref, *, nb, cx, t):
    tb = GRID_W
    ctx0 = HALO
    gw = 2 * LRU_BLOCK
    pre = CONV_LEFT
    steps = pre + tb + (HALO - pre)

    zeros_h = jnp.zeros((HALO, LRU_BLOCK), F32)
    for b in range(nb):
        xc_ref[b, 0:HALO, :] = zeros_h
        xc_ref[b, ctx0:ctx0 + cx, :] = axc_ref[b].astype(F32)
        xc_ref[b, ctx0 + cx:ctx0 + cx + HALO, :] = zeros_h

    cw = cw_ref[...]
    cb = cb_ref[...]
    lam = lam_ref[...]
    sp = jnp.maximum(-lam, 0.0) + jnp.log1p(jnp.exp(-jnp.abs(lam)))
    half_neg_c_sp = (-0.5 * LRU_C) * sp

    def zero_history(d):
        sx_ref[d, 0:pre * nb, :] = jnp.zeros((pre * nb, LRU_BLOCK), F32)
        sx_ref[d, (pre + tb) * nb:steps * nb, :] = jnp.zeros(((steps - pre - tb) * nb, LRU_BLOCK), F32)

    def load_ctx_block(d, base):
        for b in range(nb):
            sx_ref[d, pl.ds(b, steps, stride=nb), :] = xc_ref[b, pl.ds(base - pre, steps), :]

    def load_latent_block(d, r0):
        for b in range(nb):
            sx_ref[d, pl.ds(pre * nb + b, tb, stride=nb), :] = axl_ref[b, pl.ds(r0, tb), :].astype(F32)

    def fill_gates(d):
        u = jnp.broadcast_to(cb, (tb * nb, LRU_BLOCK))
        for k in range(CONV_WIDTH):
            u = u + sx_ref[d, k * nb:(k + tb) * nb, :] * cw[k:k + 1, :]
        th = jnp.tanh(_dot(u.astype(BF16), wg_ref[:, d * gw:(d + 1) * gw]) + bg_ref[:, d * gw:(d + 1) * gw])
        r2 = th[:, :LRU_BLOCK] + 1.0
        i2 = th[:, LRU_BLOCK:] + 1.0
        log_a = r2 * half_neg_c_sp[d:d + 1, :]
        a = jnp.exp(log_a)
        half_mult = jnp.sqrt(jnp.tanh(log_a) * (a * a + 1.0) * (-0.25))
        sa_ref[d] = a
        sb_ref[d] = half_mult * (i2 * u)

    def scan_pair(hf, hr):
        for s in range(tb):
            sr = tb - 1 - s
            hf = sa_ref[0, s * nb:(s + 1) * nb, :] * hf + sb_ref[0, s * nb:(s + 1) * nb, :]
            sh_ref[0, s * nb:(s + 1) * nb, :] = hf
            hr = sa_ref[1, sr * nb:(sr + 1) * nb, :] * hr + sb_ref[1, sr * nb:(sr + 1) * nb, :]
            sh_ref[1, sr * nb:(sr + 1) * nb, :] = hr
        return hf, hr

    n_c, n_l = cx // tb, t // tb

    def ctx_body(j, hs):
        load_ctx_block(0, pl.multiple_of(ctx0 + j * tb, SUBLANES))
        load_ctx_block(1, pl.multiple_of(ctx0 + (n_c - 1 - j) * tb, SUBLANES))
        fill_gates(0)
        fill_gates(1)
        return scan_pair(*hs)

    def emit(d, r0, final):
        for b in range(nb):
            yb = sh_ref[d, pl.ds(b, tb, stride=nb), :]
            if final:
                gate = jax.nn.gelu(agl_ref[b, pl.ds(r0, tb), :].astype(F32))
                o_ref[b, pl.ds(r0, tb), :] = ((yacc_ref[b, pl.ds(r0, tb), :] + yb) * gate).astype(o_ref.dtype)
            else:
                yacc_ref[b, pl.ds(r0, tb), :] = yb

    def lat_body(j, hs, final_f, final_r):
        rf = pl.multiple_of(j * tb, tb)
        rr = pl.multiple_of((n_l - 1 - j) * tb, tb)
        load_latent_block(0, rf)
        load_latent_block(1, rr)
        fill_gates(0)
        fill_gates(1)
        hs = scan_pair(*hs)
        emit(0, rf, final_f)
        emit(1, rr, final_r)
        return hs

    h0 = jnp.zeros((nb, LRU_BLOCK), F32)
    hs = lax.fori_loop(0, n_c, ctx_body, (h0, h0))
    zero_history(0)
    zero_history(1)
    half = n_l // 2
    hs = lax.fori_loop(0, half, functools.partial(lat_body, final_f=False, final_r=False), hs)
    if n_l % 2:
        hs = lat_body(jnp.int32(half), hs, False, True)
    lax.fori_loop(n_l - half, n_l, functools.partial(lat_body, final_f=True, final_r=True), hs)


def _lru(axl, axc, conv_w, conv_b, wg, bg, lam, nb, cx, t):
    w = conv_w.shape[1]
    g = w // LRU_BLOCK
    kern = functools.partial(_lru_kernel, nb=nb, cx=cx, t=t)
    return pl.pallas_call(
        kern,
        grid=(g,),
        in_specs=[
            pl.BlockSpec((nb, t, LRU_BLOCK), lambda j: (0, 0, j)),
            pl.BlockSpec((nb, t, LRU_BLOCK), lambda j: (0, 0, g + j)),
            pl.BlockSpec((nb, cx, LRU_BLOCK), lambda j: (0, 0, j)),
            pl.BlockSpec((CONV_WIDTH, LRU_BLOCK), lambda j: (0, j)),
            pl.BlockSpec((1, LRU_BLOCK), lambda j: (0, j)),
            pl.BlockSpec((None, LRU_BLOCK, 4 * LRU_BLOCK), lambda j: (j, 0, 0)),
            pl.BlockSpec((None, 1, 4 * LRU_BLOCK), lambda j: (j, 0, 0)),
            pl.BlockSpec((2, LRU_BLOCK), lambda j: (0, j)),
        ],
        out_specs=pl.BlockSpec((nb, t, LRU_BLOCK), lambda j: (0, 0, j)),
        out_shape=jax.ShapeDtypeStruct((nb, t, w), BF16),
        scratch_shapes=[
            pltpu.VMEM((nb, cx + 2 * HALO, LRU_BLOCK), F32),
            pltpu.VMEM((nb, t, LRU_BLOCK), F32),
            pltpu.VMEM((2, (GRID_W + HALO) * nb, LRU_BLOCK), F32),
            pltpu.VMEM((2, GRID_W * nb, LRU_BLOCK), F32),
            pltpu.VMEM((2, GRID_W * nb, LRU_BLOCK), F32),
            pltpu.VMEM((2, GRID_W * nb, LRU_BLOCK), F32),
        ],
        compiler_params=_params(("arbitrary",), 56),
        name="lru",
    )(axl, axl, axc, conv_w, conv_b.reshape(1, w), wg, bg, lam)


def _hgrn_kernel(hc_ref, hl_ref, wq_ref, wff_ref, wfb_ref, wv_ref, wg_ref, lb_ref, ng_ref, o_ref,
                 p_ref, *scratch, cx, t):
    w = jnp.concatenate([wq_ref[...], wff_ref[...], wfb_ref[...], wv_ref[...], wg_ref[...]], axis=1)
    p_ref[0:cx, :] = _dot(hc_ref[...], w)
    p_ref[cx:cx + t, :] = _dot(hl_ref[...], w)
    for hh in range(HG_HEADS_PER_STEP):
        _hgrn_head(hh, lb_ref, ng_ref, o_ref, p_ref, *scratch, cx=cx, t=t)


def _hgrn_head(hh, lb_ref, ng_ref, o_ref,
               p_ref, sprev_s, q_s, v_s, k_s, hl_s, gc_s, bk_s, sc_s, aq_s, upd_ref, dec_ref,
               *, cx, t):
    c = HG_CHUNK
    dk = HG_DK
    n_c, n_l = cx // c, t // c
    n_all = n_c + n_l

    def col(group):
        start = (group * HG_HEADS_PER_STEP + hh) * dk
        return slice(start, start + dk)

    lb = lb_ref[hh]
    f_half = 0.5 * (1.0 - lb)
    f_mid = lb + f_half
    ri = lax.broadcasted_iota(I32, (c, c), 0)
    ci = lax.broadcasted_iota(I32, (c, c), 1)
    keep = (ri >= ci, ci >= ri)
    mid = (c // 2 - 1, c // 2)
    end = (c - 1, 0)

    def stage_gates(j, carry):
        r0 = pl.multiple_of(j * c, c)
        q_s[pl.ds(r0, c), :] = _silu(p_ref[pl.ds(r0, c), col(0)])
        v_s[pl.ds(r0, c), :] = p_ref[pl.ds(r0, c), col(3)].astype(BF16)
        for d in range(2):
            f = f_mid + f_half * jnp.tanh(0.5 * p_ref[pl.ds(r0, c), col(1 + d)])
            logf = jnp.log(f)
            hi = logf.astype(BF16)
            lo = (logf - hi.astype(F32)).astype(BF16)
            k_s[d, pl.ds(r0, c), :] = 1.0 - f
            hl_s[d, pl.ds(r0, c), :] = jnp.concatenate([hi, lo], axis=1)
        return carry

    lax.fori_loop(0, n_all, stage_gates, 0, unroll=4)

    def stage_cumsum(j, carry):
        r0 = pl.multiple_of(j * c, c)
        for d in range(2):
            s2 = _dot(keep[d].astype(BF16), hl_s[d, pl.ds(r0, c), :])
            gc_s[d, pl.ds(r0, c), :] = s2[:, :dk] + s2[:, dk:]
        return carry

    lax.fori_loop(0, n_all, stage_cumsum, 0, unroll=18)

    def stage_scores(j, carry):
        r0 = pl.multiple_of(j * c, c)
        q = q_s[pl.ds(r0, c), :]
        scores = None
        for d in range(2):
            gc = gc_s[d, pl.ds(r0, c), :]
            k = k_s[d, pl.ds(r0, c), :]
            g_mid = gc[mid[d]:mid[d] + 1, :]
            g_end = gc[end[d]:end[d] + 1, :]
            qa = q * jnp.exp(gc - g_mid)
            kb = k * jnp.exp(g_mid - gc)
            sc = jnp.where(keep[d], _dot_nt(qa.astype(BF16), kb.astype(BF16)), 0.0)
            scores = sc if scores is None else scores + sc
            bk_s[d, pl.ds(r0, c), :] = (kb * jnp.exp(g_end - g_mid)).astype(BF16)
            aq_s[pl.ds(r0, c), d * dk:(d + 1) * dk] = (qa * jnp.exp(g_mid)).astype(BF16)
            dec_ref[d, j] = jnp.broadcast_to(jnp.exp(g_end), (SUBLANES, dk))
        sc_s[pl.ds(r0, c), :] = scores.astype(BF16)
        return carry

    lax.fori_loop(0, n_all, stage_scores, 0, unroll=12)

    def stage_updates(j, carry):
        r0 = pl.multiple_of(j * c, c)
        v = v_s[pl.ds(r0, c), :]
        for d in range(2):
            upd_ref[d, j] = _dot_tn(v, bk_s[d, pl.ds(r0, c), :])
        return carry

    lax.fori_loop(0, n_all, stage_updates, 0, unroll=18)

    def advance(d, jj, st):
        return st * dec_ref[d, jj][0:1, :] + upd_ref[d, jj]

    def ctx_body(j, sts):
        return advance(0, j, sts[0]), advance(1, n_c - 1 - j, sts[1])

    def lat_body(j, sts):
        jf, jr = j, n_l - 1 - j
        sprev_s[jf, :, 0:dk] = sts[0].astype(BF16)
        sprev_s[jr, :, dk:2 * dk] = sts[1].astype(BF16)
        return advance(0, n_c + jf, sts[0]), advance(1, n_c + jr, sts[1])

    zero = jnp.zeros((dk, dk), F32)
    sts = lax.fori_loop(0, n_c, ctx_body, (zero, zero), unroll=2)
    lax.fori_loop(0, n_l, lat_body, sts, unroll=4)

    ng = ng_ref[...]

    def stage_outputs(j, carry):
        r0 = pl.multiple_of(j * c, c)
        rc = pl.multiple_of(cx + r0, c)
        o = _dot(sc_s[pl.ds(rc, c), :], v_s[pl.ds(rc, c), :]) + _dot_nt(aq_s[pl.ds(rc, c), :], sprev_s[j])
        g = p_ref[pl.ds(rc, c), col(4)]
        o_ref[pl.ds(r0, c), hh * dk:(hh + 1) * dk] = (_rms(o) * ng * _silu(g)).astype(o_ref.dtype)
        return carry

    lax.fori_loop(0, n_l, stage_outputs, 0, unroll=32)


def _hgrn(hc, hl, w_in_b, col0, heads, lb, norm_g):
    b, cx, d = hc.shape
    t = hl.shape[1]
    dk = HG_DK
    n_all = (cx + t) // HG_CHUNK
    kern = functools.partial(_hgrn_kernel, cx=cx, t=t)
    hps = HG_HEADS_PER_STEP
    wide = hps * dk
    assert heads % hps == 0 and col0 % wide == 0

    def group(gi):
        return pl.BlockSpec((d, wide), lambda bi, h: (0, col0 // wide + gi * (heads // hps) + h))

    return pl.pallas_call(
        kern,
        grid=(b, heads // hps),
        in_specs=[
            pl.BlockSpec((None, cx, d), lambda bi, h: (bi, 0, 0)),
            pl.BlockSpec((None, t, d), lambda bi, h: (bi, 0, 0)),
            group(0), group(1), group(2), group(3), group(4),
            pl.BlockSpec((hps, 1, dk), lambda bi, h: (h, 0, 0)),
            pl.BlockSpec((1, dk), lambda bi, h: (0, 0)),
        ],
        out_specs=pl.BlockSpec((None, t, wide), lambda bi, h: (bi, 0, h)),
        out_shape=jax.ShapeDtypeStruct((b, t, heads * dk), BF16),
        scratch_shapes=[
            pltpu.VMEM((cx + t, 5 * wide), F32),
            pltpu.VMEM((t // HG_CHUNK, dk, 2 * dk), BF16),
            pltpu.VMEM((cx + t, dk), F32),
            pltpu.VMEM((cx + t, dk), BF16),
            pltpu.VMEM((2, cx + t, dk), F32),
            pltpu.VMEM((2, cx + t, 2 * dk), BF16),
            pltpu.VMEM((2, cx + t, dk), F32),
            pltpu.VMEM((2, cx + t, dk), BF16),
            pltpu.VMEM((cx + t, HG_CHUNK), BF16),
            pltpu.VMEM((cx + t, 2 * dk), BF16),
            pltpu.VMEM((2, n_all, dk, dk), F32),
            pltpu.VMEM((2, n_all, SUBLANES, dk), F32),
        ],
        compiler_params=_params(("arbitrary", "arbitrary"), 56),
        name="hgrn",
    )(hc, hl, w_in_b, w_in_b, w_in_b, w_in_b, w_in_b, lb, norm_g.reshape(1, dk))


def _merge_kernel(x_ref, h_ref, ya_ref, yb_ref, wm_ref, wa_ref, wb_ref, wo_ref,
                  gate_ref, shift_ref, scale_ref, g2_ref, rwt_ref, rb_ref,
                  x1_ref, hs_ref, pos_ref, cnt_ref, h2b_s, *, tm, d, ne, cap):
    m = _dot(h_ref[...], wm_ref[...])
    za = _dot(ya_ref[...], wa_ref[...])
    zb = _dot(yb_ref[...], wb_ref[...])
    mix = _sigmoid(m[:, :d]) * za + _sigmoid(m[:, d:]) * zb
    out = _dot(mix.astype(BF16), wo_ref[...])
    x1 = x_ref[...] + gate_ref[...] * out
    x1_ref[...] = x1
    h2 = _rms(x1) * g2_ref[...] * (1.0 + scale_ref[...]) + shift_ref[...]
    h2b_s[...] = h2.astype(BF16)

    logits = lax.dot_general(rwt_ref[...], h2, (((1,), (1,)), ((), ())),
                             preferred_element_type=F32, precision=HIGHEST) + rb_ref[...]
    eid = lax.broadcasted_iota(I32, (ne, tm), 0)
    neg = jnp.float32(-jnp.inf)
    work = logits
    hot = jnp.zeros((ne, tm), F32)
    vals, sels = [], []
    for _ in range(TOP_K):
        mx = jnp.max(work, axis=0, keepdims=True)
        ix = jnp.min(jnp.where(work == mx, eid, ne), axis=0, keepdims=True)
        sel = eid == ix
        work = jnp.where(sel, neg, work)
        hot = hot + sel.astype(F32)
        vals.append(mx)
        sels.append(sel)
    ex = [jnp.exp(v - vals[0]) for v in vals]
    den = sum(ex[1:], ex[0])
    gts = [e / den for e in ex]

    ti = lax.broadcasted_iota(I32, (tm, tm), 0)
    tj = lax.broadcasted_iota(I32, (tm, tm), 1)
    strict = (ti < tj).astype(BF16)
    before = _dot(hot.astype(BF16), strict)
    cnt = jnp.sum(hot, axis=1, keepdims=True)
    nchunks = jnp.ceil(cnt * (1.0 / MOE_CHUNK))
    ei = lax.broadcasted_iota(I32, (ne, ne), 0)
    ej = lax.broadcasted_iota(I32, (ne, ne), 1)
    earlier = (ej < ei).astype(BF16)
    start = _dot(earlier, jnp.broadcast_to(nchunks, (ne, LANES)).astype(BF16))[:, 0:1] * MOE_CHUNK
    where_to = before + start
    poss = [jnp.sum(jnp.where(s, where_to, 0.0), axis=0, keepdims=True).astype(I32) for s in sels]

    rb_rows = SORT_ROWS
    lane = lax.broadcasted_iota(I32, (rb_rows, LANES), 1)

    def sort_rows(i, carry):
        r0 = pl.multiple_of(i * rb_rows, rb_rows)
        pid = lax.broadcasted_iota(I32, (rb_rows, tm), 0)
        gsel = jnp.zeros((rb_rows, tm), F32)
        for k in range(TOP_K):
            gsel = jnp.where(pid == poss[k] - r0, gts[k], gsel)
        place = jnp.where(gsel != 0.0, 1.0, 0.0)
        hs_ref[pl.ds(r0, rb_rows), 0:d] = _dot(place.astype(BF16), h2b_s[...]).astype(BF16)
        g = jnp.sum(gsel, axis=1, keepdims=True)
        g_hi = g.astype(BF16).astype(F32)
        hs_ref[pl.ds(r0, rb_rows), d:d + LANES] = jnp.where(lane < LANES // 2, g_hi, g - g_hi).astype(BF16)
        return carry

    lax.fori_loop(0, cap // rb_rows, sort_rows, 0)

    row = lax.broadcasted_iota(I32, (SUBLANES, tm), 0)
    packed = jnp.zeros((SUBLANES, tm), I32)
    for k, p in enumerate(poss):
        packed = jnp.where(row == k, jnp.broadcast_to(p, (SUBLANES, tm)), packed)
    pos_ref[...] = packed
    cnt_ref[...] = jnp.broadcast_to(cnt, cnt_ref.shape)


def _merge(x, hl, ya, yb, w_m, w_ba, w_bb, w_out, gate2, shift3, scale4, g2, rwt, rb, tm):
    b, t, d = x.shape
    ne = rwt.shape[0]
    nt = t // tm
    tok = b * t
    cap = _sorted_rows(tm, ne)
    kern = functools.partial(_merge_kernel, tm=tm, d=d, ne=ne, cap=cap)
    row_spec = pl.BlockSpec((None, tm, d), lambda bi, i: (bi, i, 0))
    mod_spec = pl.BlockSpec((None, 1, d), lambda bi, i: (bi, 0, 0))
    flat = lambda bi, i: (bi * nt + i, 0)
    lane = lambda bi, i: (0, bi * nt + i)

    def whole(shape):
        return pl.BlockSpec(shape, lambda bi, i: (0,) * len(shape))

    return pl.pallas_call(
        kern,
        grid=(b, nt),
        in_specs=[
            row_spec, row_spec, row_spec, row_spec,
            whole((d, 2 * d)), whole((d, d)), whole((d, d)), whole((d, d)),
            mod_spec, mod_spec, mod_spec,
            whole((1, d)), whole((ne, d)), whole((ne, 1)),
        ],
        out_specs=[
            row_spec,
            pl.BlockSpec((cap, d + LANES), flat),
            pl.BlockSpec((SUBLANES, tm), lane),
            pl.BlockSpec((None, ne, LANES), lambda bi, i: (bi * nt + i, 0, 0)),
        ],
        out_shape=[
            jax.ShapeDtypeStruct((b, t, d), F32),
            jax.ShapeDtypeStruct((b * nt * cap, d + LANES), BF16),
            jax.ShapeDtypeStruct((SUBLANES, tok), I32),
            jax.ShapeDtypeStruct((b * nt, ne, LANES), F32),
        ],
        scratch_shapes=[pltpu.VMEM((tm, d), BF16)],
        compiler_params=_params(("arbitrary", "arbitrary"), 56),
        name="merge",
    )(x, hl, ya, yb, w_m, w_ba, w_bb, w_out, gate2, shift3, scale4, g2.reshape(1, d), rwt, rb)


def _experts_kernel(bexp_ref, bact_ref, src_ref, src1_ref, src2_ref, hs_hbm, w1_ref, b1_ref, w2_ref, b2_ref,
                    y_ref, xbuf, act_s, w1p_s, w2_s, gsem, *, blk, nblocks, d):
    j = pl.program_id(0)
    slot = lax.rem(j, MOE_RING)
    far = lax.rem(j + 2, MOE_RING)
    f = w2_s.shape[1]
    pw = 2 * LANES
    n1 = 2 * f // pw
    n2 = d // pw
    nch = blk // MOE_CHUNK
    active = bact_ref[j] == 1

    def request(idx_ref, buf_slot):
        for c in range(nch):
            src = pl.multiple_of(idx_ref[0, 0, c] * MOE_CHUNK, MOE_CHUNK)
            pltpu.make_async_copy(hs_hbm.at[pl.ds(src, MOE_CHUNK), :],
                                  xbuf.at[buf_slot, pl.ds(c * MOE_CHUNK, MOE_CHUNK), :], gsem.at[buf_slot]).start()

    @pl.when(j == 0)
    def _():
        request(src_ref, 0)
        request(src1_ref, 1)

    @pl.when(j + 2 < nblocks)
    def _():
        request(src2_ref, far)

    pltpu.make_async_copy(hs_hbm.at[pl.ds(0, blk), :], xbuf.at[slot], gsem.at[slot]).wait()

    new_expert = (j == 0) | (bexp_ref[j] != bexp_ref[jnp.maximum(j - 1, 0)])

    @pl.when(new_expert & active)
    def _():
        src = lax.broadcasted_iota(I32, (pw, pw), 0)
        dst = lax.broadcasted_iota(I32, (pw, pw), 1)
        want = jnp.where(dst < pw // 2, 2 * dst, 2 * (dst - pw // 2) + 1)
        sel = (src == want).astype(BF16)
        for cb in range(n1):
            w1p_s[cb] = _dot(w1_ref[:, cb * pw:(cb + 1) * pw].astype(BF16), sel).astype(BF16)
        for cb in range(n2):
            w2_s[cb] = w2_ref[:, cb * pw:(cb + 1) * pw].astype(BF16)

    @pl.when(active)
    def _():
        x = xbuf[slot, :, 0:d]
        gate = (xbuf[slot, :, d:d + 1].astype(F32) + xbuf[slot, :, d + LANES // 2:d + LANES // 2 + 1].astype(F32))
        for n in range(n1):
            h = _dot(x, w1p_s[n]) + b1_ref[n]
            hg = jnp.minimum(h[:, :LANES], SWIGLU_LIMIT)
            hu = jnp.clip(h[:, LANES:], -SWIGLU_LIMIT, SWIGLU_LIMIT)
            act_s[n] = (hg * _sigmoid(SWIGLU_ALPHA * hg) * (hu + 1.0)).astype(BF16)
        act = jnp.concatenate([act_s[i] for i in range(n1)], axis=1)
        for n in range(n2):
            y = _dot(act, w2_s[n]) + b2_ref[n]
            y_ref[:, n * pw:(n + 1) * pw] = (y * gate).astype(y_ref.dtype)

    @pl.when(jnp.logical_not(active))
    def _():
        y_ref[...] = jnp.zeros_like(y_ref)


def _experts(block_expert, block_active, src3, hs, w1, b1p, w2, b2, blk):
    nblocks = block_expert.shape[0]
    f, d = w2.shape[1], w2.shape[2]
    pw = 2 * LANES
    n1, n2 = 2 * f // pw, d // pw
    nch = blk // MOE_CHUNK
    b1p = b1p.reshape(-1, n1, 1, pw)
    b2 = b2.reshape(-1, n2, 1, pw)
    kern = functools.partial(_experts_kernel, blk=blk, nblocks=nblocks, d=d)
    wmap = lambda j, be, ba: (be[j], 0, 0)

    def idx_spec(ahead):
        return pl.BlockSpec((1, 1, nch), lambda j, be, ba: (jnp.minimum(j + ahead, nblocks - 1), 0, 0),
                            memory_space=pltpu.SMEM)

    grid_spec = pltpu.PrefetchScalarGridSpec(
        num_scalar_prefetch=2,
        grid=(nblocks,),
        in_specs=[
            idx_spec(0), idx_spec(1), idx_spec(2),
            pl.BlockSpec(memory_space=pl.ANY),
            pl.BlockSpec((None, d, 2 * f), wmap),
            pl.BlockSpec((None, n1, 1, pw), lambda j, be, ba: (be[j], 0, 0, 0)),
            pl.BlockSpec((None, f, d), wmap),
            pl.BlockSpec((None, n2, 1, pw), lambda j, be, ba: (be[j], 0, 0, 0)),
        ],
        out_specs=pl.BlockSpec((blk, d), lambda j, be, ba: (j, 0)),
        scratch_shapes=[
            pltpu.VMEM((MOE_RING, blk, d + LANES), BF16),
            pltpu.VMEM((n1, blk, LANES), BF16),
            pltpu.VMEM((n1, d, pw), BF16),
            pltpu.VMEM((n2, f, pw), BF16),
            pltpu.SemaphoreType.DMA((MOE_RING,)),
        ],
    )
    return pl.pallas_call(
        kern,
        grid_spec=grid_spec,
        out_shape=jax.ShapeDtypeStruct((nblocks * blk, d), BF16),
        compiler_params=_params(("arbitrary",), 56),
        name="experts",
    )(block_expert, block_active, src3, src3, src3, hs, w1, b1p, w2, b2)


def _combine_kernel(dst_ref, dstn_ref, y_hbm, x1_ref, pos_ref, gate_ref, fg_ref, o_ref, ybuf, sem,
                    *, tm, cap, ntiles):
    i = pl.program_id(0)
    slot = lax.rem(i, 2)
    nch = cap // MOE_CHUNK

    def request(idx_ref, buf_slot):
        def body(c, carry):
            src = pl.multiple_of(idx_ref[0, 0, c] * MOE_CHUNK, MOE_CHUNK)
            dst = pl.multiple_of(c * MOE_CHUNK, MOE_CHUNK)
            pltpu.make_async_copy(y_hbm.at[pl.ds(src, MOE_CHUNK), :],
                                  ybuf.at[buf_slot, pl.ds(dst, MOE_CHUNK), :], sem.at[buf_slot]).start()
            return carry
        lax.fori_loop(0, nch, body, 0, unroll=8)

    @pl.when(i == 0)
    def _():
        request(dst_ref, 0)

    @pl.when(i + 1 < ntiles)
    def _():
        request(dstn_ref, 1 - slot)

    pltpu.make_async_copy(y_hbm.at[pl.ds(0, cap), :], ybuf.at[slot], sem.at[slot]).wait()

    pos = pos_ref[...]
    acc = None
    for c0 in range(0, cap, SORT_ROWS):
        col = lax.broadcasted_iota(I32, (tm, SORT_ROWS), 1) + c0
        pick = jnp.zeros((tm, SORT_ROWS), F32)
        for k in range(TOP_K):
            pick = jnp.where(col == pos[:, k:k + 1], 1.0, pick)
        part = _dot(pick.astype(BF16), ybuf[slot, c0:c0 + SORT_ROWS, :])
        acc = part if acc is None else acc + part
    x2 = x1_ref[...] + gate_ref[...] * acc
    o_ref[...] = _rms(x2) * fg_ref[...]


def _combine(dst3, y, x1, pos_t, gate5, final_g, tm, cap):
    b, t, d = x1.shape
    nt = t // tm
    ntiles = b * nt
    nch = cap // MOE_CHUNK
    kern = functools.partial(_combine_kernel, tm=tm, cap=cap, ntiles=ntiles)
    row_map = lambda i: (i // nt, i % nt, 0)
    return pl.pallas_call(
        kern,
        grid=(ntiles,),
        in_specs=[
            pl.BlockSpec((1, 1, nch), lambda i: (i, 0, 0), memory_space=pltpu.SMEM),
            pl.BlockSpec((1, 1, nch), lambda i: (jnp.minimum(i + 1, ntiles - 1), 0, 0), memory_space=pltpu.SMEM),
            pl.BlockSpec(memory_space=pl.ANY),
            pl.BlockSpec((None, tm, d), row_map),
            pl.BlockSpec((tm, TOP_K), lambda i: (i, 0)),
            pl.BlockSpec((None, 1, d), lambda i: (i // nt, 0, 0)),
            pl.BlockSpec((1, d), lambda i: (0, 0)),
        ],
        out_specs=pl.BlockSpec((None, tm, d), row_map),
        out_shape=jax.ShapeDtypeStruct((b, t, d), F32),
        scratch_shapes=[
            pltpu.VMEM((2, cap, d), BF16),
            pltpu.SemaphoreType.DMA((2,)),
        ],
        compiler_params=_params(("arbitrary",), 48),
        name="combine",
    )(dst3, dst3, y, x1, pos_t, gate5, final_g.reshape(1, d))


def _sorted_rows(tm, ne):
    need = tm * TOP_K + ne * (MOE_CHUNK - 1) + MOE_CHUNK
    return -(-need // SORT_ROWS) * SORT_ROWS


def _pick_tile(n, pref):
    tm = pref
    while n % tm:
        tm //= 2
    return tm


def kernel(x, c, ctx, c_ctx, ada_w, ada_b, norm1_g, norm2_g, w_in, lru_conv_w, lru_conv_b, lru_wa, lru_ba, lru_wx, lru_bx, lru_lam, hg_lb_logits, hg_norm_g, w_branch_a, w_branch_b, w_out, router_w, router_b, moe_w1, moe_b1, moe_w2, moe_b2, final_g):
    b, t, d = x.shape
    cx = ctx.shape[1]
    layer = 0
    w_lru = lru_conv_w.shape[2]
    qk = (w_in.shape[2] - 2 * w_lru - 2 * d) // 5
    heads = qk // HG_DK
    ne = router_w.shape[2]

    pad = (-(b + 1)) % SUBLANES
    c_all = jnp.concatenate([c, c_ctx[None, :], jnp.zeros((pad, d), F32)], axis=0)
    mod = _modulation(c_all, ada_w[layer], ada_b[layer])
    mod_l = mod[:b].reshape(b, N_MOD, 1, d)
    mod_c = mod[b].reshape(N_MOD, 1, 1, d)

    w_in_b = w_in[layer].astype(BF16)
    hg0 = 2 * w_lru
    w_m = w_in_b[:, hg0 + 5 * qk:]
    g_blocks = w_lru // LRU_BLOCK
    wg = jnp.concatenate([lru_wa[layer, 0], lru_wx[layer, 0], lru_wa[layer, 1], lru_wx[layer, 1]], axis=-1)
    wg = (0.5 * wg).astype(BF16)
    bg = jnp.concatenate([lru_ba[layer, 0], lru_bx[layer, 0], lru_ba[layer, 1], lru_bx[layer, 1]], axis=-1)
    bg = (0.5 * bg).reshape(g_blocks, 1, 4 * LRU_BLOCK)
    lb_all = jnp.cumsum(jax.nn.softmax(hg_lb_logits.astype(F32), axis=0), axis=0)
    lb = lb_all[layer].reshape(heads, 1, HG_DK)

    tm_n = _pick_tile(t, 512)
    hl, axl = _norm_proj(x, mod_l[:, 0], mod_l[:, 1], norm1_g[layer], w_in_b, 2 * w_lru, tm_n)
    hc, axc = _norm_proj(ctx, mod_c[0], mod_c[1], norm1_g[layer], w_in_b, 2 * w_lru, _pick_tile(cx, 256))
    ya = _lru(axl, axc, lru_conv_w[layer], lru_conv_b[layer], wg, bg, lru_lam[layer], b, cx, t)
    yb = _hgrn(hc, hl, w_in_b, hg0, heads, lb, hg_norm_g[layer])

    rwt = router_w[layer].T
    rb = router_b[layer].reshape(ne, 1)
    tm = _pick_tile(t, 512)
    x1, hs, pos8, cnt = _merge(
        x, hl, ya, yb, w_m, w_branch_a[layer].astype(BF16), w_branch_b[layer].astype(BF16),
        w_out[layer].astype(BF16), mod_l[:, 2], mod_l[:, 3], mod_l[:, 4], norm2_g[layer], rwt, rb, tm)

    blk = MOE_BLOCK
    ch = MOE_CHUNK
    bpc = blk // ch
    tok = b * t
    ntile = tok // tm
    cap = _sorted_rows(tm, ne)
    tch = cap // ch
    run = (cnt[:, :, 0].astype(I32) + ch - 1) // ch
    loc_end = jnp.cumsum(run, axis=1)
    loc_start = loc_end - run
    upto = jnp.cumsum(run, axis=0)
    total = upto[-1]
    padded = (total + bpc - 1) // bpc * bpc
    gend = jnp.cumsum(padded)
    gstart = gend - padded
    where_run = gstart[None, :] + upto - run
    max_chunks = tok * TOP_K // ch + ntile * ne + ne * bpc
    nblocks = -(-max_chunks // bpc)

    eids = jnp.arange(ne, dtype=I32)
    tids = jnp.arange(ntile, dtype=I32)
    g = jnp.arange(nblocks * bpc, dtype=I32)
    g_e = jnp.minimum(jnp.sum(g[:, None] >= gend[None, :], axis=1), ne - 1).astype(I32)
    is_e = g_e[:, None] == eids[None, :]
    off = g - jnp.sum(jnp.where(is_e, gstart[None, :], 0), axis=1)
    g_total = jnp.sum(jnp.where(is_e, total[None, :], 0), axis=1)
    upto_g = jnp.sum(jnp.where(is_e[:, None, :], upto[None, :, :], 0), axis=2)
    g_i = jnp.minimum(jnp.sum(off[:, None] >= upto_g, axis=1), ntile - 1).astype(I32)
    both = (g_i[:, None] == tids[None, :])[:, :, None] & is_e[:, None, :]

    def at_run(table):
        return jnp.sum(jnp.where(both, table[None, :, :], 0), axis=(1, 2))

    within = off - (at_run(upto) - at_run(run))
    zero_chunk = tch - 1
    src = jnp.where((off < g_total) & (g < gend[-1]), g_i * tch + at_run(loc_start) + within, zero_chunk)

    block_first = jnp.arange(nblocks, dtype=I32) * bpc
    block_active = block_first < gend[-1]
    block_expert = jnp.minimum(jnp.sum(block_first[:, None] >= gend[None, :], axis=1), ne - 1).astype(I32)
    last_expert = jnp.max(jnp.where(block_active, block_expert, 0))
    block_expert = jnp.where(block_active, block_expert, last_expert).astype(I32)

    c = jnp.arange(tch, dtype=I32)
    c_e = jnp.minimum(jnp.sum(c[None, :, None] >= loc_end[:, None, :], axis=2), ne - 1).astype(I32)
    is_ce = c_e[:, :, None] == eids[None, None, :]
    back = jnp.sum(jnp.where(is_ce, (where_run - loc_start)[:, None, :], 0), axis=2) + c[None, :]
    back = jnp.where(c[None, :] < loc_end[:, -1:], back, 0)

    b1 = moe_b1[layer]
    f = b1.shape[1] // 2
    b1p = b1.reshape(ne, f // LANES, LANES, 2).transpose(0, 1, 3, 2).reshape(ne, 1, 2 * f)
    y = _experts(block_expert, block_active.astype(I32), src.reshape(nblocks, 1, bpc), hs, moe_w1[layer], b1p,
                 moe_w2[layer], moe_b2[layer].reshape(ne, 1, d), blk)
    return _combine(back.reshape(ntile, 1, tch), y, x1, pos8[:TOP_K].T, mod_l[:, 5], final_g, tm, cap)
```

```python
import functools

import jax
import jax.numpy as jnp
from jax import lax
from jax.experimental import pallas as pl
from jax.experimental.pallas import tpu as pltpu

F32 = jnp.float32
BF16 = jnp.bfloat16
I32 = jnp.int32
HIGHEST = lax.Precision.HIGHEST

EPS = 1e-6
N_MOD = 6
GRID_W = 64
CONV_LEFT = 2
CONV_WIDTH = 4
LRU_C = 8.0
LRU_BLOCK = 128
HG_DK = 128
HG_CHUNK = 64
HG_HEADS_PER_STEP = 2
TOP_K = 4
SWIGLU_LIMIT = 7.0
SWIGLU_ALPHA = 1.702
MOE_BLOCK = 512
MOE_CHUNK = 16
SORT_ROWS = 512
MOE_RING = 3
SUBLANES = 8
LANES = 128
HALO = 8


def _params(sem, vmem_mb):
    return pltpu.CompilerParams(dimension_semantics=sem, vmem_limit_bytes=vmem_mb * 1024 * 1024)


def _dot(a, b):
    return jnp.dot(a, b, preferred_element_type=F32)


def _dot_nt(a, b):
    return lax.dot_general(a, b, (((1,), (1,)), ((), ())), preferred_element_type=F32)


def _dot_tn(a, b):
    return lax.dot_general(a, b, (((0,), (0,)), ((), ())), preferred_element_type=F32)


def _sigmoid(x):
    return 0.5 * jnp.tanh(0.5 * x) + 0.5


def _silu(x):
    return x * _sigmoid(x)


def _rms(x):
    return x * lax.rsqrt(jnp.mean(x * x, axis=-1, keepdims=True) + EPS)


def _mod_kernel(c_ref, w_ref, b_ref, o_ref):
    s = _silu(c_ref[...])
    o_ref[...] = jnp.dot(s, w_ref[...], preferred_element_type=F32, precision=HIGHEST) + b_ref[...]


def _modulation(c_all, ada_w, ada_b):
    m, d = c_all.shape
    n = ada_w.shape[1]
    tn = 1024
    return pl.pallas_call(
        _mod_kernel,
        grid=(n // tn,),
        in_specs=[
            pl.BlockSpec((m, d), lambda j: (0, 0)),
            pl.BlockSpec((d, tn), lambda j: (0, j)),
            pl.BlockSpec((1, tn), lambda j: (0, j)),
        ],
        out_specs=pl.BlockSpec((m, tn), lambda j: (0, j)),
        out_shape=jax.ShapeDtypeStruct((m, n), F32),
        compiler_params=_params(("arbitrary",), 32),
        name="mod",
    )(c_all, ada_w, ada_b.reshape(1, n))


def _norm_proj_kernel(x_ref, shift_ref, scale_ref, g_ref, w_ref, h_ref, o_ref):
    y = _rms(x_ref[...]) * g_ref[...]
    h = (y * (1.0 + scale_ref[...]) + shift_ref[...]).astype(h_ref.dtype)
    h_ref[...] = h
    o_ref[...] = _dot(h, w_ref[...]).astype(o_ref.dtype)


def _norm_proj(x, shift, scale, g, w_in_b, n_out, tm):
    b, l, d = x.shape
    per_batch = shift.shape[0] == b and b > 1
    mod_map = (lambda bi, i: (bi, 0, 0)) if per_batch else (lambda bi, i: (0, 0, 0))
    return pl.pallas_call(
        _norm_proj_kernel,
        grid=(b, l // tm),
        in_specs=[
            pl.BlockSpec((None, tm, d), lambda bi, i: (bi, i, 0)),
            pl.BlockSpec((None, 1, d), mod_map),
            pl.BlockSpec((None, 1, d), mod_map),
            pl.BlockSpec((1, d), lambda bi, i: (0, 0)),
            pl.BlockSpec((d, n_out), lambda bi, i: (0, 0)),
        ],
        out_specs=[
            pl.BlockSpec((None, tm, d), lambda bi, i: (bi, i, 0)),
            pl.BlockSpec((None, tm, n_out), lambda bi, i: (bi, i, 0)),
        ],
        out_shape=[
            jax.ShapeDtypeStruct((b, l, d), BF16),
            jax.ShapeDtypeStruct((b, l, n_out), BF16),
        ],
        compiler_params=_params(("arbitrary", "arbitrary"), 40),
        name="norm_proj",
    )(x, shift, scale, g.reshape(1, d), w_in_b)


def _lru_kernel(axl_ref, agl_ref, axc_ref, cw_ref, cb_ref, wg_ref, bg_ref, lam_ref, o_ref,
                xc_ref, yacc_ref, sx_ref, sa_ref, sb_ref, sh_ref, *, nb, cx, t):
    tb = GRID_W
    ctx0 = HALO
    gw = 2 * LRU_BLOCK
    pre = CONV_LEFT
    steps = pre + tb + (HALO - pre)

    zeros_h = jnp.zeros((HALO, LRU_BLOCK), F32)
    for b in range(nb):
        xc_ref[b, 0:HALO, :] = zeros_h
        xc_ref[b, ctx0:ctx0 + cx, :] = axc_ref[b].astype(F32)
        xc_ref[b, ctx0 + cx:ctx0 + cx + HALO, :] = zeros_h

    cw = cw_ref[...]
    cb = cb_ref[...]
    lam = lam_ref[...]
    sp = jnp.maximum(-lam, 0.0) + jnp.log1p(jnp.exp(-jnp.abs(lam)))
    half_neg_c_sp = (-0.5 * LRU_C) * sp

    def zero_history(d):
        sx_ref[d, 0:pre * nb, :] = jnp.zeros((pre * nb, LRU_BLOCK), F32)
        sx_ref[d, (pre + tb) * nb:steps * nb, :] = jnp.zeros(((steps - pre - tb) * nb, LRU_BLOCK), F32)

    def load_ctx_block(d, base):
        for b in range(nb):
            sx_ref[d, pl.ds(b, steps, stride=nb), :] = xc_ref[b, pl.ds(base - pre, steps), :]

    def load_latent_block(d, r0):
        for b in range(nb):
            sx_ref[d, pl.ds(pre * nb + b, tb, stride=nb), :] = axl_ref[b, pl.ds(r0, tb), :].astype(F32)

    def fill_gates(d):
        u = jnp.broadcast_to(cb, (tb * nb, LRU_BLOCK))
        for k in range(CONV_WIDTH):
            u = u + sx_ref[d, k * nb:(k + tb) * nb, :] * cw[k:k + 1, :]
        th = jnp.tanh(_dot(u.astype(BF16), wg_ref[:, d * gw:(d + 1) * gw]) + bg_ref[:, d * gw:(d + 1) * gw])
        r2 = th[:, :LRU_BLOCK] + 1.0
        i2 = th[:, LRU_BLOCK:] + 1.0
        log_a = r2 * half_neg_c_sp[d:d + 1, :]
        a = jnp.exp(log_a)
        half_mult = jnp.sqrt(jnp.tanh(log_a) * (a * a + 1.0) * (-0.25))
        sa_ref[d] = a
        sb_ref[d] = half_mult * (i2 * u)

    def scan_pair(hf, hr):
        for s in range(tb):
            sr = tb - 1 - s
            hf = sa_ref[0, s * nb:(s + 1) * nb, :] * hf + sb_ref[0, s * nb:(s + 1) * nb, :]
            sh_ref[0, s * nb:(s + 1) * nb, :] = hf
            hr = sa_ref[1, sr * nb:(sr + 1) * nb, :] * hr + sb_ref[1, sr * nb:(sr + 1) * nb, :]
            sh_ref[1, sr * nb:(sr + 1) * nb, :] = hr
        return hf, hr

    n_c, n_l = cx // tb, t // tb

    def ctx_body(j, hs):
        load_ctx_block(0, pl.multiple_of(ctx0 + j * tb, SUBLANES))
        load_ctx_block(1, pl.multiple_of(ctx0 + (n_c - 1 - j) * tb, SUBLANES))
        fill_gates(0)
        fill_gates(1)
        return scan_pair(*hs)

    def emit(d, r0, final):
        for b in range(nb):
            yb = sh_ref[d, pl.ds(b, tb, stride=nb), :]
            if final:
                gate = jax.nn.gelu(agl_ref[b, pl.ds(r0, tb), :].astype(F32))
                o_ref[b, pl.ds(r0, tb), :] = ((yacc_ref[b, pl.ds(r0, tb), :] + yb) * gate).astype(o_ref.dtype)
            else:
                yacc_ref[b, pl.ds(r0, tb), :] = yb

    def lat_body(j, hs, final_f, final_r):
        rf = pl.multiple_of(j * tb, tb)
        rr = pl.multiple_of((n_l - 1 - j) * tb, tb)
        load_latent_block(0, rf)
        load_latent_block(1, rr)
        fill_gates(0)
        fill_gates(1)
        hs = scan_pair(*hs)
        emit(0, rf, final_f)
        emit(1, rr, final_r)
        return hs

    h0 = jnp.zeros((nb, LRU_BLOCK), F32)
    hs = lax.fori_loop(0, n_c, ctx_body, (h0, h0))
    zero_history(0)
    zero_history(1)
    half = n_l // 2
    hs = lax.fori_loop(0, half, functools.partial(lat_body, final_f=False, final_r=False), hs)
    if n_l % 2:
        hs = lat_body(jnp.int32(half), hs, False, True)
    lax.fori_loop(n_l - half, n_l, functools.partial(lat_body, final_f=True, final_r=True), hs)


def _lru(axl, axc, conv_w, conv_b, wg, bg, lam, nb, cx, t):
    w = conv_w.shape[1]
    g = w // LRU_BLOCK
    kern = functools.partial(_lru_kernel, nb=nb, cx=cx, t=t)
    return pl.pallas_call(
        kern,
        grid=(g,),
        in_specs=[
            pl.BlockSpec((nb, t, LRU_BLOCK), lambda j: (0, 0, j)),
            pl.BlockSpec((nb, t, LRU_BLOCK), lambda j: (0, 0, g + j)),
            pl.BlockSpec((nb, cx, LRU_BLOCK), lambda j: (0, 0, j)),
            pl.BlockSpec((CONV_WIDTH, LRU_BLOCK), lambda j: (0, j)),
            pl.BlockSpec((1, LRU_BLOCK), lambda j: (0, j)),
            pl.BlockSpec((None, LRU_BLOCK, 4 * LRU_BLOCK), lambda j: (j, 0, 0)),
            pl.BlockSpec((None, 1, 4 * LRU_BLOCK), lambda j: (j, 0, 0)),
            pl.BlockSpec((2, LRU_BLOCK), lambda j: (0, j)),
        ],
        out_specs=pl.BlockSpec((nb, t, LRU_BLOCK), lambda j: (0, 0, j)),
        out_shape=jax.ShapeDtypeStruct((nb, t, w), BF16),
        scratch_shapes=[
            pltpu.VMEM((nb, cx + 2 * HALO, LRU_BLOCK), F32),
            pltpu.VMEM((nb, t, LRU_BLOCK), F32),
            pltpu.VMEM((2, (GRID_W + HALO) * nb, LRU_BLOCK), F32),
            pltpu.VMEM((2, GRID_W * nb, LRU_BLOCK), F32),
            pltpu.VMEM((2, GRID_W * nb, LRU_BLOCK), F32),
            pltpu.VMEM((2, GRID_W * nb, LRU_BLOCK), F32),
        ],
        compiler_params=_params(("arbitrary",), 56),
        name="lru",
    )(axl, axl, axc, conv_w, conv_b.reshape(1, w), wg, bg, lam)


def _hgrn_kernel(hc_ref, hl_ref, wq_ref, wff_ref, wfb_ref, wv_ref, wg_ref, lb_ref, ng_ref, o_ref,
                 p_ref, *scratch, cx, t):
    w = jnp.concatenate([wq_ref[...], wff_ref[...], wfb_ref[...], wv_ref[...], wg_ref[...]], axis=1)
    p_ref[0:cx, :] = _dot(hc_ref[...], w)
    p_ref[cx:cx + t, :] = _dot(hl_ref[...], w)
    for hh in range(HG_HEADS_PER_STEP):
        _hgrn_head(hh, lb_ref, ng_ref, o_ref, p_ref, *scratch, cx=cx, t=t)


def _hgrn_head(hh, lb_ref, ng_ref, o_ref,
               p_ref, sprev_s, q_s, v_s, k_s, hl_s, gc_s, bk_s, sc_s, aq_s, upd_ref, dec_ref,
               *, cx, t):
    c = HG_CHUNK
    dk = HG_DK
    n_c, n_l = cx // c, t // c
    n_all = n_c + n_l

    def col(group):
        start = (group * HG_HEADS_PER_STEP + hh) * dk
        return slice(start, start + dk)

    lb = lb_ref[hh]
    f_half = 0.5 * (1.0 - lb)
    f_mid = lb + f_half
    ri = lax.broadcasted_iota(I32, (c, c), 0)
    ci = lax.broadcasted_iota(I32, (c, c), 1)
    keep = (ri >= ci, ci >= ri)
    mid = (c // 2 - 1, c // 2)
    end = (c - 1, 0)

    def stage_gates(j, carry):
        r0 = pl.multiple_of(j * c, c)
        q_s[pl.ds(r0, c), :] = _silu(p_ref[pl.ds(r0, c), col(0)])
        v_s[pl.ds(r0, c), :] = p_ref[pl.ds(r0, c), col(3)].astype(BF16)
        for d in range(2):
            f = f_mid + f_half * jnp.tanh(p_ref[pl.ds(r0, c), col(1 + d)])
            logf = jnp.log(f)
            hi = logf.astype(BF16)
            lo = (logf - hi.astype(F32)).astype(BF16)
            k_s[d, pl.ds(r0, c), :] = 1.0 - f
            hl_s[d, pl.ds(r0, c), :] = jnp.concatenate([hi, lo], axis=1)
        return carry

    lax.fori_loop(0, n_all, stage_gates, 0, unroll=4)

    def stage_cumsum(j, carry):
        r0 = pl.multiple_of(j * c, c)
        for d in range(2):
            s2 = _dot(keep[d].astype(BF16), hl_s[d, pl.ds(r0, c), :])
            gc_s[d, pl.ds(r0, c), :] = s2[:, :dk] + s2[:, dk:]
        return carry

    lax.fori_loop(0, n_all, stage_cumsum, 0, unroll=18)

    def stage_scores(j, carry):
        r0 = pl.multiple_of(j * c, c)
        q = q_s[pl.ds(r0, c), :]
        scores = None
        for d in range(2):
            gc = gc_s[d, pl.ds(r0, c), :]
            k = k_s[d, pl.ds(r0, c), :]
            g_mid = gc[mid[d]:mid[d] + 1, :]
            g_end = gc[end[d]:end[d] + 1, :]
            qa = q * jnp.exp(gc - g_mid)
            kb = k * jnp.exp(g_mid - gc)
            sc = jnp.where(keep[d], _dot_nt(qa.astype(BF16), kb.astype(BF16)), 0.0)
            scores = sc if scores is None else scores + sc
            bk_s[d, pl.ds(r0, c), :] = (kb * jnp.exp(g_end - g_mid)).astype(BF16)
            aq_s[pl.ds(r0, c), d * dk:(d + 1) * dk] = (qa * jnp.exp(g_mid)).astype(BF16)
            dec_ref[d, j] = jnp.broadcast_to(jnp.exp(g_end), (SUBLANES, dk))
        sc_s[pl.ds(r0, c), :] = scores.astype(BF16)
        return carry

    lax.fori_loop(0, n_all, stage_scores, 0, unroll=12)

    def stage_updates(j, carry):
        r0 = pl.multiple_of(j * c, c)
        v = v_s[pl.ds(r0, c), :]
        for d in range(2):
            upd_ref[d, j] = _dot_tn(v, bk_s[d, pl.ds(r0, c), :])
        return carry

    lax.fori_loop(0, n_all, stage_updates, 0, unroll=18)

    def advance(d, jj, st):
        return st * dec_ref[d, jj][0:1, :] + upd_ref[d, jj]

    def ctx_body(j, sts):
        return advance(0, j, sts[0]), advance(1, n_c - 1 - j, sts[1])

    def lat_body(j, sts):
        jf, jr = j, n_l - 1 - j
        sprev_s[jf, :, 0:dk] = sts[0].astype(BF16)
        sprev_s[jr, :, dk:2 * dk] = sts[1].astype(BF16)
        return advance(0, n_c + jf, sts[0]), advance(1, n_c + jr, sts[1])

    zero = jnp.zeros((dk, dk), F32)
    sts = lax.fori_loop(0, n_c, ctx_body, (zero, zero), unroll=2)
    lax.fori_loop(0, n_l, lat_body, sts, unroll=4)

    ng = ng_ref[...]

    def stage_outputs(j, carry):
        r0 = pl.multiple_of(j * c, c)
        rc = pl.multiple_of(cx + r0, c)
        o = _dot(sc_s[pl.ds(rc, c), :], v_s[pl.ds(rc, c), :]) + _dot_nt(aq_s[pl.ds(rc, c), :], sprev_s[j])
        g = p_ref[pl.ds(rc, c), col(4)]
        o_ref[pl.ds(r0, c), hh * dk:(hh + 1) * dk] = (_rms(o) * ng * _silu(g)).astype(o_ref.dtype)
        return carry

    lax.fori_loop(0, n_l, stage_outputs, 0, unroll=32)


def _hgrn(hc, hl, w_in_b, col0, heads, lb, norm_g):
    b, cx, d = hc.shape
    t = hl.shape[1]
    dk = HG_DK
    n_all = (cx + t) // HG_CHUNK
    kern = functools.partial(_hgrn_kernel, cx=cx, t=t)
    hps = HG_HEADS_PER_STEP
    wide = hps * dk
    assert heads % hps == 0 and col0 % wide == 0

    def group(gi):
        return pl.BlockSpec((d, wide), lambda bi, h: (0, col0 // wide + gi * (heads // hps) + h))

    return pl.pallas_call(
        kern,
        grid=(b, heads // hps),
        in_specs=[
            pl.BlockSpec((None, cx, d), lambda bi, h: (bi, 0, 0)),
            pl.BlockSpec((None, t, d), lambda bi, h: (bi, 0, 0)),
            group(0), group(1), group(2), group(3), group(4),
            pl.BlockSpec((hps, 1, dk), lambda bi, h: (h, 0, 0)),
            pl.BlockSpec((1, dk), lambda bi, h: (0, 0)),
        ],
        out_specs=pl.BlockSpec((None, t, wide), lambda bi, h: (bi, 0, h)),
        out_shape=jax.ShapeDtypeStruct((b, t, heads * dk), BF16),
        scratch_shapes=[
            pltpu.VMEM((cx + t, 5 * wide), F32),
            pltpu.VMEM((t // HG_CHUNK, dk, 2 * dk), BF16),
            pltpu.VMEM((cx + t, dk), F32),
            pltpu.VMEM((cx + t, dk), BF16),
            pltpu.VMEM((2, cx + t, dk), F32),
            pltpu.VMEM((2, cx + t, 2 * dk), BF16),
            pltpu.VMEM((2, cx + t, dk), F32),
            pltpu.VMEM((2, cx + t, dk), BF16),
            pltpu.VMEM((cx + t, HG_CHUNK), BF16),
            pltpu.VMEM((cx + t, 2 * dk), BF16),
            pltpu.VMEM((2, n_all, dk, dk), F32),
            pltpu.VMEM((2, n_all, SUBLANES, dk), F32),
        ],
        compiler_params=_params(("arbitrary", "arbitrary"), 56),
        name="hgrn",
    )(hc, hl, w_in_b, w_in_b, w_in_b, w_in_b, w_in_b, lb, norm_g.reshape(1, dk))


def _merge_kernel(x_ref, h_ref, ya_ref, yb_ref, wm_ref, wa_ref, wb_ref, wo_ref,
                  gate_ref, shift_ref, scale_ref, g2_ref, rwt_ref, rb_ref,
                  x1_ref, hs_ref, pos_ref, cnt_ref, h2b_s, *, tm, d, ne, cap):
    m = _dot(h_ref[...], wm_ref[...])
    za = _dot(ya_ref[...], wa_ref[...])
    zb = _dot(yb_ref[...], wb_ref[...])
    mix = _sigmoid(m[:, :d]) * za + _sigmoid(m[:, d:]) * zb
    out = _dot(mix.astype(BF16), wo_ref[...])
    x1 = x_ref[...] + gate_ref[...] * out
    x1_ref[...] = x1
    h2 = _rms(x1) * g2_ref[...] * (1.0 + scale_ref[...]) + shift_ref[...]
    h2b_s[...] = h2.astype(BF16)

    logits = lax.dot_general(rwt_ref[...], h2, (((1,), (1,)), ((), ())),
                             preferred_element_type=F32, precision=HIGHEST) + rb_ref[...]
    eid = lax.broadcasted_iota(I32, (ne, tm), 0)
    neg = jnp.float32(-jnp.inf)
    work = logits
    hot = jnp.zeros((ne, tm), F32)
    vals, sels = [], []
    for _ in range(TOP_K):
        mx = jnp.max(work, axis=0, keepdims=True)
        ix = jnp.min(jnp.where(work == mx, eid, ne), axis=0, keepdims=True)
        sel = eid == ix
        work = jnp.where(sel, neg, work)
        hot = hot + sel.astype(F32)
        vals.append(mx)
        sels.append(sel)
    ex = [jnp.exp(v - vals[0]) for v in vals]
    den = sum(ex[1:], ex[0])
    gts = [e / den for e in ex]

    ti = lax.broadcasted_iota(I32, (tm, tm), 0)
    tj = lax.broadcasted_iota(I32, (tm, tm), 1)
    strict = (ti < tj).astype(BF16)
    before = _dot(hot.astype(BF16), strict)
    cnt = jnp.sum(hot, axis=1, keepdims=True)
    nchunks = jnp.ceil(cnt * (1.0 / MOE_CHUNK))
    ei = lax.broadcasted_iota(I32, (ne, ne), 0)
    ej = lax.broadcasted_iota(I32, (ne, ne), 1)
    earlier = (ej < ei).astype(BF16)
    start = _dot(earlier, jnp.broadcast_to(nchunks, (ne, LANES)).astype(BF16))[:, 0:1] * MOE_CHUNK
    where_to = before + start
    poss = [jnp.sum(jnp.where(s, where_to, 0.0), axis=0, keepdims=True).astype(I32) for s in sels]

    rb_rows = SORT_ROWS
    lane = lax.broadcasted_iota(I32, (rb_rows, LANES), 1)

    def sort_rows(i, carry):
        r0 = pl.multiple_of(i * rb_rows, rb_rows)
        pid = lax.broadcasted_iota(I32, (rb_rows, tm), 0)
        gsel = jnp.zeros((rb_rows, tm), F32)
        for k in range(TOP_K):
            gsel = jnp.where(pid == poss[k] - r0, gts[k], gsel)
        place = jnp.where(gsel != 0.0, 1.0, 0.0)
        hs_ref[pl.ds(r0, rb_rows), 0:d] = _dot(place.astype(BF16), h2b_s[...]).astype(BF16)
        g = jnp.sum(gsel, axis=1, keepdims=True)
        g_hi = g.astype(BF16).astype(F32)
        hs_ref[pl.ds(r0, rb_rows), d:d + LANES] = jnp.where(lane < LANES // 2, g_hi, g - g_hi).astype(BF16)
        return carry

    lax.fori_loop(0, cap // rb_rows, sort_rows, 0)

    row = lax.broadcasted_iota(I32, (SUBLANES, tm), 0)
    packed = jnp.zeros((SUBLANES, tm), I32)
    for k, p in enumerate(poss):
        packed = jnp.where(row == k, jnp.broadcast_to(p, (SUBLANES, tm)), packed)
    pos_ref[...] = packed
    cnt_ref[...] = jnp.broadcast_to(cnt, cnt_ref.shape)


def _merge(x, hl, ya, yb, w_m, w_ba, w_bb, w_out, gate2, shift3, scale4, g2, rwt, rb, tm):
    b, t, d = x.shape
    ne = rwt.shape[0]
    nt = t // tm
    tok = b * t
    cap = _sorted_rows(tm, ne)
    kern = functools.partial(_merge_kernel, tm=tm, d=d, ne=ne, cap=cap)
    row_spec = pl.BlockSpec((None, tm, d), lambda bi, i: (bi, i, 0))
    mod_spec = pl.BlockSpec((None, 1, d), lambda bi, i: (bi, 0, 0))
    flat = lambda bi, i: (bi * nt + i, 0)
    lane = lambda bi, i: (0, bi * nt + i)

    def whole(shape):
        return pl.BlockSpec(shape, lambda bi, i: (0,) * len(shape))

    return pl.pallas_call(
        kern,
        grid=(b, nt),
        in_specs=[
            row_spec, row_spec, row_spec, row_spec,
            whole((d, 2 * d)), whole((d, d)), whole((d, d)), whole((d, d)),
            mod_spec, mod_spec, mod_spec,
            whole((1, d)), whole((ne, d)), whole((ne, 1)),
        ],
        out_specs=[
            row_spec,
            pl.BlockSpec((cap, d + LANES), flat),
            pl.BlockSpec((SUBLANES, tm), lane),
            pl.BlockSpec((None, ne, LANES), lambda bi, i: (bi * nt + i, 0, 0)),
        ],
        out_shape=[
            jax.ShapeDtypeStruct((b, t, d), F32),
            jax.ShapeDtypeStruct((b * nt * cap, d + LANES), BF16),
            jax.ShapeDtypeStruct((SUBLANES, tok), I32),
            jax.ShapeDtypeStruct((b * nt, ne, LANES), F32),
        ],
        scratch_shapes=[pltpu.VMEM((tm, d), BF16)],
        compiler_params=_params(("arbitrary", "arbitrary"), 56),
        name="merge",
    )(x, hl, ya, yb, w_m, w_ba, w_bb, w_out, gate2, shift3, scale4, g2.reshape(1, d), rwt, rb)


def _experts_kernel(bexp_ref, bact_ref, src_ref, src1_ref, src2_ref, hs_hbm, w1_ref, b1_ref, w2_ref, b2_ref,
                    y_ref, xbuf, act_s, w1p_s, w2_s, gsem, *, blk, nblocks, d):
    j = pl.program_id(0)
    slot = lax.rem(j, MOE_RING)
    far = lax.rem(j + 2, MOE_RING)
    f = w2_s.shape[1]
    pw = 2 * LANES
    n1 = 2 * f // pw
    n2 = d // pw
    nch = blk // MOE_CHUNK
    active = bact_ref[j] == 1

    def request(idx_ref, buf_slot):
        for c in range(nch):
            src = pl.multiple_of(idx_ref[0, 0, c] * MOE_CHUNK, MOE_CHUNK)
            pltpu.make_async_copy(hs_hbm.at[pl.ds(src, MOE_CHUNK), :],
                                  xbuf.at[buf_slot, pl.ds(c * MOE_CHUNK, MOE_CHUNK), :], gsem.at[buf_slot]).start()

    @pl.when(j == 0)
    def _():
        request(src_ref, 0)
        request(src1_ref, 1)

    @pl.when(j + 2 < nblocks)
    def _():
        request(src2_ref, far)

    pltpu.make_async_copy(hs_hbm.at[pl.ds(0, blk), :], xbuf.at[slot], gsem.at[slot]).wait()

    new_expert = (j == 0) | (bexp_ref[j] != bexp_ref[jnp.maximum(j - 1, 0)])

    @pl.when(new_expert & active)
    def _():
        src = lax.broadcasted_iota(I32, (pw, pw), 0)
        dst = lax.broadcasted_iota(I32, (pw, pw), 1)
        want = jnp.where(dst < pw // 2, 2 * dst, 2 * (dst - pw // 2) + 1)
        sel = (src == want).astype(BF16)
        for cb in range(n1):
            w1p_s[cb] = _dot(w1_ref[:, cb * pw:(cb + 1) * pw].astype(BF16), sel).astype(BF16)
        for cb in range(n2):
            w2_s[cb] = w2_ref[:, cb * pw:(cb + 1) * pw].astype(BF16)

    @pl.when(active)
    def _():
        x = xbuf[slot, :, 0:d]
        gate = (xbuf[slot, :, d:d + 1].astype(F32) + xbuf[slot, :, d + LANES // 2:d + LANES // 2 + 1].astype(F32))
        for n in range(n1):
            h = _dot(x, w1p_s[n]) + b1_ref[n]
            hg = jnp.minimum(h[:, :LANES], SWIGLU_LIMIT)
            hu = jnp.clip(h[:, LANES:], -SWIGLU_LIMIT, SWIGLU_LIMIT)
            act_s[n] = (hg * _sigmoid(SWIGLU_ALPHA * hg) * (hu + 1.0)).astype(BF16)
        act = jnp.concatenate([act_s[i] for i in range(n1)], axis=1)
        for n in range(n2):
            y = _dot(act, w2_s[n]) + b2_ref[n]
            y_ref[:, n * pw:(n + 1) * pw] = (y * gate).astype(y_ref.dtype)

    @pl.when(jnp.logical_not(active))
    def _():
        y_ref[...] = jnp.zeros_like(y_ref)


def _experts(block_expert, block_active, src3, hs, w1, b1p, w2, b2, blk):
    nblocks = block_expert.shape[0]
    f, d = w2.shape[1], w2.shape[2]
    pw = 2 * LANES
    n1, n2 = 2 * f // pw, d // pw
    nch = blk // MOE_CHUNK
    b1p = b1p.reshape(-1, n1, 1, pw)
    b2 = b2.reshape(-1, n2, 1, pw)
    kern = functools.partial(_experts_kernel, blk=blk, nblocks=nblocks, d=d)
    wmap = lambda j, be, ba: (be[j], 0, 0)

    def idx_spec(ahead):
        return pl.BlockSpec((1, 1, nch), lambda j, be, ba: (jnp.minimum(j + ahead, nblocks - 1), 0, 0),
                            memory_space=pltpu.SMEM)

    grid_spec = pltpu.PrefetchScalarGridSpec(
        num_scalar_prefetch=2,
        grid=(nblocks,),
        in_specs=[
            idx_spec(0), idx_spec(1), idx_spec(2),
            pl.BlockSpec(memory_space=pl.ANY),
            pl.BlockSpec((None, d, 2 * f), wmap),
            pl.BlockSpec((None, n1, 1, pw), lambda j, be, ba: (be[j], 0, 0, 0)),
            pl.BlockSpec((None, f, d), wmap),
            pl.BlockSpec((None, n2, 1, pw), lambda j, be, ba: (be[j], 0, 0, 0)),
        ],
        out_specs=pl.BlockSpec((blk, d), lambda j, be, ba: (j, 0)),
        scratch_shapes=[
            pltpu.VMEM((MOE_RING, blk, d + LANES), BF16),
            pltpu.VMEM((n1, blk, LANES), BF16),
            pltpu.VMEM((n1, d, pw), BF16),
            pltpu.VMEM((n2, f, pw), BF16),
            pltpu.SemaphoreType.DMA((MOE_RING,)),
        ],
    )
    return pl.pallas_call(
        kern,
        grid_spec=grid_spec,
        out_shape=jax.ShapeDtypeStruct((nblocks * blk, d), BF16),
        compiler_params=_params(("arbitrary",), 56),
        name="experts",
    )(block_expert, block_active, src3, src3, src3, hs, w1, b1p, w2, b2)


def _combine_kernel(dst_ref, dstn_ref, y_hbm, x1_ref, pos_ref, gate_ref, fg_ref, o_ref, ybuf, sem,
                    *, tm, cap, ntiles):
    i = pl.program_id(0)
    slot = lax.rem(i, 2)
    nch = cap // MOE_CHUNK

    def request(idx_ref, buf_slot):
        def body(c, carry):
            src = pl.multiple_of(idx_ref[0, 0, c] * MOE_CHUNK, MOE_CHUNK)
            dst = pl.multiple_of(c * MOE_CHUNK, MOE_CHUNK)
            pltpu.make_async_copy(y_hbm.at[pl.ds(src, MOE_CHUNK), :],
                                  ybuf.at[buf_slot, pl.ds(dst, MOE_CHUNK), :], sem.at[buf_slot]).start()
            return carry
        lax.fori_loop(0, nch, body, 0, unroll=8)

    @pl.when(i == 0)
    def _():
        request(dst_ref, 0)

    @pl.when(i + 1 < ntiles)
    def _():
        request(dstn_ref, 1 - slot)

    pltpu.make_async_copy(y_hbm.at[pl.ds(0, cap), :], ybuf.at[slot], sem.at[slot]).wait()

    pos = pos_ref[...]
    acc = None
    for c0 in range(0, cap, SORT_ROWS):
        col = lax.broadcasted_iota(I32, (tm, SORT_ROWS), 1) + c0
        pick = jnp.zeros((tm, SORT_ROWS), F32)
        for k in range(TOP_K):
            pick = jnp.where(col == pos[:, k:k + 1], 1.0, pick)
        part = _dot(pick.astype(BF16), ybuf[slot, c0:c0 + SORT_ROWS, :])
        acc = part if acc is None else acc + part
    x2 = x1_ref[...] + gate_ref[...] * acc
    o_ref[...] = _rms(x2) * fg_ref[...]


def _combine(dst3, y, x1, pos_t, gate5, final_g, tm, cap):
    b, t, d = x1.shape
    nt = t // tm
    ntiles = b * nt
    nch = cap // MOE_CHUNK
    kern = functools.partial(_combine_kernel, tm=tm, cap=cap, ntiles=ntiles)
    row_map = lambda i: (i // nt, i % nt, 0)
    return pl.pallas_call(
        kern,
        grid=(ntiles,),
        in_specs=[
            pl.BlockSpec((1, 1, nch), lambda i: (i, 0, 0), memory_space=pltpu.SMEM),
            pl.BlockSpec((1, 1, nch), lambda i: (jnp.minimum(i + 1, ntiles - 1), 0, 0), memory_space=pltpu.SMEM),
            pl.BlockSpec(memory_space=pl.ANY),
            pl.BlockSpec((None, tm, d), row_map),
            pl.BlockSpec((tm, TOP_K), lambda i: (i, 0)),
            pl.BlockSpec((None, 1, d), lambda i: (i // nt, 0, 0)),
            pl.BlockSpec((1, d), lambda i: (0, 0)),
        ],
        out_specs=pl.BlockSpec((None, tm, d), row_map),
        out_shape=jax.ShapeDtypeStruct((b, t, d), F32),
        scratch_shapes=[
            pltpu.VMEM((2, cap, d), BF16),
            pltpu.SemaphoreType.DMA((2,)),
        ],
        compiler_params=_params(("arbitrary",), 48),
        name="combine",
    )(dst3, dst3, y, x1, pos_t, gate5, final_g.reshape(1, d))


def _sorted_rows(tm, ne):
    need = tm * TOP_K + ne * (MOE_CHUNK - 1) + MOE_CHUNK
    return -(-need // SORT_ROWS) * SORT_ROWS


def _pick_tile(n, pref):
    tm = pref
    while n % tm:
        tm //= 2
    return tm


def kernel(x, c, ctx, c_ctx, ada_w, ada_b, norm1_g, norm2_g, w_in, lru_conv_w, lru_conv_b, lru_wa, lru_ba, lru_wx, lru_bx, lru_lam, hg_lb_logits, hg_norm_g, w_branch_a, w_branch_b, w_out, router_w, router_b, moe_w1, moe_b1, moe_w2, moe_b2, final_g):
    b, t, d = x.shape
    cx = ctx.shape[1]
    layer = 0
    w_lru = lru_conv_w.shape[2]
    qk = (w_in.shape[2] - 2 * w_lru - 2 * d) // 5
    heads = qk // HG_DK
    ne = router_w.shape[2]

    pad = (-(b + 1)) % SUBLANES
    c_all = jnp.concatenate([c, c_ctx[None, :], jnp.zeros((pad, d), F32)], axis=0)
    mod = _modulation(c_all, ada_w[layer], ada_b[layer])
    mod_l = mod[:b].reshape(b, N_MOD, 1, d)
    mod_c = mod[b].reshape(N_MOD, 1, 1, d)

    hg0 = 2 * w_lru
    col_ids = jnp.arange(w_in.shape[2])
    halve = (col_ids >= hg0 + qk) & (col_ids < hg0 + 3 * qk)
    w_in_b = (w_in[layer] * jnp.where(halve, 0.5, 1.0)[None, :]).astype(BF16)
    w_m = w_in_b[:, hg0 + 5 * qk:]
    g_blocks = w_lru // LRU_BLOCK
    wg = jnp.concatenate([lru_wa[layer, 0], lru_wx[layer, 0], lru_wa[layer, 1], lru_wx[layer, 1]], axis=-1)
    wg = (0.5 * wg).astype(BF16)
    bg = jnp.concatenate([lru_ba[layer, 0], lru_bx[layer, 0], lru_ba[layer, 1], lru_bx[layer, 1]], axis=-1)
    bg = (0.5 * bg).reshape(g_blocks, 1, 4 * LRU_BLOCK)
    lb_all = jnp.cumsum(jax.nn.softmax(hg_lb_logits.astype(F32), axis=0), axis=0)
    lb = lb_all[layer].reshape(heads, 1, HG_DK)

    tm_n = _pick_tile(t, 512)
    hl, axl = _norm_proj(x, mod_l[:, 0], mod_l[:, 1], norm1_g[layer], w_in_b, 2 * w_lru, tm_n)
    hc, axc = _norm_proj(ctx, mod_c[0], mod_c[1], norm1_g[layer], w_in_b, 2 * w_lru, _pick_tile(cx, 256))
    ya = _lru(axl, axc, lru_conv_w[layer], lru_conv_b[layer], wg, bg, lru_lam[layer], b, cx, t)
    yb = _hgrn(hc, hl, w_in_b, hg0, heads, lb, hg_norm_g[layer])

    rwt = router_w[layer].T
    rb = router_b[layer].reshape(ne, 1)
    tm = _pick_tile(t, 512)
    x1, hs, pos8, cnt = _merge(
        x, hl, ya, yb, w_m, w_branch_a[layer].astype(BF16), w_branch_b[layer].astype(BF16),
        w_out[layer].astype(BF16), mod_l[:, 2], mod_l[:, 3], mod_l[:, 4], norm2_g[layer], rwt, rb, tm)

    blk = MOE_BLOCK
    ch = MOE_CHUNK
    bpc = blk // ch
    tok = b * t
    ntile = tok // tm
    cap = _sorted_rows(tm, ne)
    tch = cap // ch
    run = (cnt[:, :, 0].astype(I32) + ch - 1) // ch
    loc_end = jnp.cumsum(run, axis=1)
    loc_start = loc_end - run
    upto = jnp.cumsum(run, axis=0)
    total = upto[-1]
    padded = (total + bpc - 1) // bpc * bpc
    gend = jnp.cumsum(padded)
    gstart = gend - padded
    where_run = gstart[None, :] + upto - run
    max_chunks = tok * TOP_K // ch + ntile * ne + ne * bpc
    nblocks = -(-max_chunks // bpc)

    eids = jnp.arange(ne, dtype=I32)
    tids = jnp.arange(ntile, dtype=I32)
    g = jnp.arange(nblocks * bpc, dtype=I32)
    g_e = jnp.minimum(jnp.sum(g[:, None] >= gend[None, :], axis=1), ne - 1).astype(I32)
    is_e = g_e[:, None] == eids[None, :]
    off = g - jnp.sum(jnp.where(is_e, gstart[None, :], 0), axis=1)
    g_total = jnp.sum(jnp.where(is_e, total[None, :], 0), axis=1)
    upto_g = jnp.sum(jnp.where(is_e[:, None, :], upto[None, :, :], 0), axis=2)
    g_i = jnp.minimum(jnp.sum(off[:, None] >= upto_g, axis=1), ntile - 1).astype(I32)
    both = (g_i[:, None] == tids[None, :])[:, :, None] & is_e[:, None, :]

    def at_run(table):
        return jnp.sum(jnp.where(both, table[None, :, :], 0), axis=(1, 2))

    within = off - (at_run(upto) - at_run(run))
    zero_chunk = tch - 1
    src = jnp.where((off < g_total) & (g < gend[-1]), g_i * tch + at_run(loc_start) + within, zero_chunk)

    block_first = jnp.arange(nblocks, dtype=I32) * bpc
    block_active = block_first < gend[-1]
    block_expert = jnp.minimum(jnp.sum(block_first[:, None] >= gend[None, :], axis=1), ne - 1).astype(I32)
    last_expert = jnp.max(jnp.where(block_active, block_expert, 0))
    block_expert = jnp.where(block_active, block_expert, last_expert).astype(I32)

    c = jnp.arange(tch, dtype=I32)
    c_e = jnp.minimum(jnp.sum(c[None, :, None] >= loc_end[:, None, :], axis=2), ne - 1).astype(I32)
    is_ce = c_e[:, :, None] == eids[None, None, :]
    back = jnp.sum(jnp.where(is_ce, (where_run - loc_start)[:, None, :], 0), axis=2) + c[None, :]
    back = jnp.where(c[None, :] < loc_end[:, -1:], back, 0)

    b1 = moe_b1[layer]
    f = b1.shape[1] // 2
    b1p = b1.reshape(ne, f // LANES, LANES, 2).transpose(0, 1, 3, 2).reshape(ne, 1, 2 * f)
    y = _experts(block_expert, block_active.astype(I32), src.reshape(nblocks, 1, bpc), hs, moe_w1[layer], b1p,
                 moe_w2[layer], moe_b2[layer].reshape(ne, 1, d), blk)
    return _combine(back.reshape(ntile, 1, tch), y, x1, pos8[:TOP_K].T, mod_l[:, 5], final_g, tm, cap)
```

```python
import functools

import jax
import jax.numpy as jnp
from jax import lax
from jax.experimental import pallas as pl
from jax.experimental.pallas import tpu as pltpu

F32 = jnp.float32
BF16 = jnp.bfloat16
I32 = jnp.int32
HIGHEST = lax.Precision.HIGHEST

EPS = 1e-6
N_MOD = 6
GRID_W = 64
CONV_LEFT = 2
CONV_WIDTH = 4
LRU_C = 8.0
LRU_BLOCK = 128
HG_DK = 128
HG_CHUNK = 64
HG_HEADS_PER_STEP = 2
TOP_K = 4
SWIGLU_LIMIT = 7.0
SWIGLU_ALPHA = 1.702
MOE_BLOCK = 384
MOE_CHUNK = 16
SORT_ROWS = 512
MOE_RING = 3
SUBLANES = 8
LANES = 128
HALO = 8


def _params(sem, vmem_mb):
    return pltpu.CompilerParams(dimension_semantics=sem, vmem_limit_bytes=vmem_mb * 1024 * 1024)


def _dot(a, b):
    return jnp.dot(a, b, preferred_element_type=F32)


def _dot_nt(a, b):
    return lax.dot_general(a, b, (((1,), (1,)), ((), ())), preferred_element_type=F32)


def _dot_tn(a, b):
    return lax.dot_general(a, b, (((0,), (0,)), ((), ())), preferred_element_type=F32)


def _sigmoid(x):
    return 0.5 * jnp.tanh(0.5 * x) + 0.5


def _silu(x):
    return x * _sigmoid(x)


def _rms(x):
    return x * lax.rsqrt(jnp.mean(x * x, axis=-1, keepdims=True) + EPS)


def _mod_kernel(c_ref, w_ref, b_ref, o_ref):
    s = _silu(c_ref[...])
    o_ref[...] = jnp.dot(s, w_ref[...], preferred_element_type=F32, precision=HIGHEST) + b_ref[...]


def _modulation(c_all, ada_w, ada_b):
    m, d = c_all.shape
    n = ada_w.shape[1]
    tn = 1024
    return pl.pallas_call(
        _mod_kernel,
        grid=(n // tn,),
        in_specs=[
            pl.BlockSpec((m, d), lambda j: (0, 0)),
            pl.BlockSpec((d, tn), lambda j: (0, j)),
            pl.BlockSpec((1, tn), lambda j: (0, j)),
        ],
        out_specs=pl.BlockSpec((m, tn), lambda j: (0, j)),
        out_shape=jax.ShapeDtypeStruct((m, n), F32),
        compiler_params=_params(("arbitrary",), 32),
        name="mod",
    )(c_all, ada_w, ada_b.reshape(1, n))


def _norm_proj_kernel(x_ref, shift_ref, scale_ref, g_ref, w_ref, h_ref, o_ref):
    y = _rms(x_ref[...]) * g_ref[...]
    h = (y * (1.0 + scale_ref[...]) + shift_ref[...]).astype(h_ref.dtype)
    h_ref[...] = h
    o_ref[...] = _dot(h, w_ref[...]).astype(o_ref.dtype)


def _norm_proj(x, shift, scale, g, w_in_b, n_out, tm):
    b, l, d = x.shape
    per_batch = shift.shape[0] == b and b > 1
    mod_map = (lambda bi, i: (bi, 0, 0)) if per_batch else (lambda bi, i: (0, 0, 0))
    return pl.pallas_call(
        _norm_proj_kernel,
        grid=(b, l // tm),
        in_specs=[
            pl.BlockSpec((None, tm, d), lambda bi, i: (bi, i, 0)),
            pl.BlockSpec((None, 1, d), mod_map),
            pl.BlockSpec((None, 1, d), mod_map),
            pl.BlockSpec((1, d), lambda bi, i: (0, 0)),
            pl.BlockSpec((d, n_out), lambda bi, i: (0, 0)),
        ],
        out_specs=[
            pl.BlockSpec((None, tm, d), lambda bi, i: (bi, i, 0)),
            pl.BlockSpec((None, tm, n_out), lambda bi, i: (bi, i, 0)),
        ],
        out_shape=[
            jax.ShapeDtypeStruct((b, l, d), BF16),
            jax.ShapeDtypeStruct((b, l, n_out), BF16),
        ],
        compiler_params=_params(("arbitrary", "arbitrary"), 40),
        name="norm_proj",
    )(x, shift, scale, g.reshape(1, d), w_in_b)


def _lru_kernel(axl_ref, agl_ref, axc_ref, cw_ref, cb_ref, wg_ref, bg_ref, lam_ref, o_ref,
                xc_ref, yacc_ref, sx_ref, sa_ref, sb_ref, sh_ref, *, nb, cx, t):
    tb = GRID_W
    ctx0 = HALO
    gw = 2 * LRU_BLOCK
    pre = CONV_LEFT
    steps = pre + tb + (HALO - pre)

    zeros_h = jnp.zeros((HALO, LRU_BLOCK), F32)
    for b in range(nb):
        xc_ref[b, 0:HALO, :] = zeros_h
        xc_ref[b, ctx0:ctx0 + cx, :] = axc_ref[b].astype(F32)
        xc_ref[b, ctx0 + cx:ctx0 + cx + HALO, :] = zeros_h

    cw = cw_ref[...]
    cb = cb_ref[...]
    lam = lam_ref[...]
    sp = jnp.maximum(-lam, 0.0) + jnp.log1p(jnp.exp(-jnp.abs(lam)))
    half_neg_c_sp = (-0.5 * LRU_C) * sp

    def zero_history(d):
        sx_ref[d, 0:pre * nb, :] = jnp.zeros((pre * nb, LRU_BLOCK), F32)
        sx_ref[d, (pre + tb) * nb:steps * nb, :] = jnp.zeros(((steps - pre - tb) * nb, LRU_BLOCK), F32)

    def load_ctx_block(d, base):
        for b in range(nb):
            sx_ref[d, pl.ds(b, steps, stride=nb), :] = xc_ref[b, pl.ds(base - pre, steps), :]

    def load_latent_block(d, r0):
        for b in range(nb):
            sx_ref[d, pl.ds(pre * nb + b, tb, stride=nb), :] = axl_ref[b, pl.ds(r0, tb), :].astype(F32)

    def fill_gates(d):
        u = jnp.broadcast_to(cb, (tb * nb, LRU_BLOCK))
        for k in range(CONV_WIDTH):
            u = u + sx_ref[d, k * nb:(k + tb) * nb, :] * cw[k:k + 1, :]
        th = jnp.tanh(_dot(u.astype(BF16), wg_ref[:, d * gw:(d + 1) * gw]) + bg_ref[:, d * gw:(d + 1) * gw])
        r2 = th[:, :LRU_BLOCK] + 1.0
        i2 = th[:, LRU_BLOCK:] + 1.0
        log_a = r2 * half_neg_c_sp[d:d + 1, :]
        a = jnp.exp(log_a)
        half_mult = jnp.sqrt(jnp.tanh(log_a) * (a * a + 1.0) * (-0.25))
        sa_ref[d] = a
        sb_ref[d] = half_mult * (i2 * u)

    def scan_pair(hf, hr):
        for s in range(tb):
            sr = tb - 1 - s
            hf = sa_ref[0, s * nb:(s + 1) * nb, :] * hf + sb_ref[0, s * nb:(s + 1) * nb, :]
            sh_ref[0, s * nb:(s + 1) * nb, :] = hf
            hr = sa_ref[1, sr * nb:(sr + 1) * nb, :] * hr + sb_ref[1, sr * nb:(sr + 1) * nb, :]
            sh_ref[1, sr * nb:(sr + 1) * nb, :] = hr
        return hf, hr

    n_c, n_l = cx // tb, t // tb

    def ctx_body(j, hs):
        load_ctx_block(0, pl.multiple_of(ctx0 + j * tb, SUBLANES))
        load_ctx_block(1, pl.multiple_of(ctx0 + (n_c - 1 - j) * tb, SUBLANES))
        fill_gates(0)
        fill_gates(1)
        return scan_pair(*hs)

    def emit(d, r0, final):
        for b in range(nb):
            yb = sh_ref[d, pl.ds(b, tb, stride=nb), :]
            if final:
                gate = jax.nn.gelu(agl_ref[b, pl.ds(r0, tb), :].astype(F32))
                o_ref[b, pl.ds(r0, tb), :] = ((yacc_ref[b, pl.ds(r0, tb), :] + yb) * gate).astype(o_ref.dtype)
            else:
                yacc_ref[b, pl.ds(r0, tb), :] = yb

    def lat_body(j, hs, final_f, final_r):
        rf = pl.multiple_of(j * tb, tb)
        rr = pl.multiple_of((n_l - 1 - j) * tb, tb)
        load_latent_block(0, rf)
        load_latent_block(1, rr)
        fill_gates(0)
        fill_gates(1)
        hs = scan_pair(*hs)
        emit(0, rf, final_f)
        emit(1, rr, final_r)
        return hs

    h0 = jnp.zeros((nb, LRU_BLOCK), F32)
    hs = lax.fori_loop(0, n_c, ctx_body, (h0, h0))
    zero_history(0)
    zero_history(1)
    half = n_l // 2
    hs = lax.fori_loop(0, half, functools.partial(lat_body, final_f=False, final_r=False), hs)
    if n_l % 2:
        hs = lat_body(jnp.int32(half), hs, False, True)
    lax.fori_loop(n_l - half, n_l, functools.partial(lat_body, final_f=True, final_r=True), hs)


def _lru(axl, axc, conv_w, conv_b, wg, bg, lam, nb, cx, t):
    w = conv_w.shape[1]
    g = w // LRU_BLOCK
    kern = functools.partial(_lru_kernel, nb=nb, cx=cx, t=t)
    return pl.pallas_call(
        kern,
        grid=(g,),
        in_specs=[
            pl.BlockSpec((nb, t, LRU_BLOCK), lambda j: (0, 0, j)),
            pl.BlockSpec((nb, t, LRU_BLOCK), lambda j: (0, 0, g + j)),
            pl.BlockSpec((nb, cx, LRU_BLOCK), lambda j: (0, 0, j)),
            pl.BlockSpec((CONV_WIDTH, LRU_BLOCK), lambda j: (0, j)),
            pl.BlockSpec((1, LRU_BLOCK), lambda j: (0, j)),
            pl.BlockSpec((None, LRU_BLOCK, 4 * LRU_BLOCK), lambda j: (j, 0, 0)),
            pl.BlockSpec((None, 1, 4 * LRU_BLOCK), lambda j: (j, 0, 0)),
            pl.BlockSpec((2, LRU_BLOCK), lambda j: (0, j)),
        ],
        out_specs=pl.BlockSpec((nb, t, LRU_BLOCK), lambda j: (0, 0, j)),
        out_shape=jax.ShapeDtypeStruct((nb, t, w), BF16),
        scratch_shapes=[
            pltpu.VMEM((nb, cx + 2 * HALO, LRU_BLOCK), F32),
            pltpu.VMEM((nb, t, LRU_BLOCK), F32),
            pltpu.VMEM((2, (GRID_W + HALO) * nb, LRU_BLOCK), F32),
            pltpu.VMEM((2, GRID_W * nb, LRU_BLOCK), F32),
            pltpu.VMEM((2, GRID_W * nb, LRU_BLOCK), F32),
            pltpu.VMEM((2, GRID_W * nb, LRU_BLOCK), F32),
        ],
        compiler_params=_params(("arbitrary",), 56),
        name="lru",
    )(axl, axl, axc, conv_w, conv_b.reshape(1, w), wg, bg, lam)


def _hgrn_kernel(hc_ref, hl_ref, wq_ref, wff_ref, wfb_ref, wv_ref, wg_ref, lb_ref, ng_ref, o_ref,
                 p_ref, *scratch, cx, t):
    w = jnp.concatenate([wq_ref[...], wff_ref[...], wfb_ref[...], wv_ref[...], wg_ref[...]], axis=1)
    p_ref[0:cx, :] = _dot(hc_ref[...], w)
    p_ref[cx:cx + t, :] = _dot(hl_ref[...], w)
    for hh in range(HG_HEADS_PER_STEP):
        _hgrn_head(hh, lb_ref, ng_ref, o_ref, p_ref, *scratch, cx=cx, t=t)


def _hgrn_head(hh, lb_ref, ng_ref, o_ref,
               p_ref, sprev_s, q_s, v_s, k_s, hl_s, gc_s, bk_s, sc_s, aq_s, upd_ref, dec_ref,
               *, cx, t):
    c = HG_CHUNK
    dk = HG_DK
    n_c, n_l = cx // c, t // c
    n_all = n_c + n_l

    def col(group):
        start = (group * HG_HEADS_PER_STEP + hh) * dk
        return slice(start, start + dk)

    lb = lb_ref[hh]
    f_half = 0.5 * (1.0 - lb)
    f_mid = lb + f_half
    ri = lax.broadcasted_iota(I32, (c, c), 0)
    ci = lax.broadcasted_iota(I32, (c, c), 1)
    keep = (ri >= ci, ci >= ri)
    mid = (c // 2 - 1, c // 2)
    end = (c - 1, 0)

    def stage_gates(j, carry):
        r0 = pl.multiple_of(j * c, c)
        q_s[pl.ds(r0, c), :] = _silu(p_ref[pl.ds(r0, c), col(0)])
        v_s[pl.ds(r0, c), :] = p_ref[pl.ds(r0, c), col(3)].astype(BF16)
        for d in range(2):
            f = f_mid + f_half * jnp.tanh(0.5 * p_ref[pl.ds(r0, c), col(1 + d)])
            logf = jnp.log(f)
            hi = logf.astype(BF16)
            lo = (logf - hi.astype(F32)).astype(BF16)
            k_s[d, pl.ds(r0, c), :] = 1.0 - f
            hl_s[d, pl.ds(r0, c), :] = jnp.concatenate([hi, lo], axis=1)
        return carry

    lax.fori_loop(0, n_all, stage_gates, 0, unroll=4)

    def stage_cumsum(j, carry):
        r0 = pl.multiple_of(j * c, c)
        for d in range(2):
            s2 = _dot(keep[d].astype(BF16), hl_s[d, pl.ds(r0, c), :])
            gc_s[d, pl.ds(r0, c), :] = s2[:, :dk] + s2[:, dk:]
        return carry

    lax.fori_loop(0, n_all, stage_cumsum, 0, unroll=18)

    def stage_scores(j, carry):
        r0 = pl.multiple_of(j * c, c)
        q = q_s[pl.ds(r0, c), :]
        scores = None
        for d in range(2):
            gc = gc_s[d, pl.ds(r0, c), :]
            k = k_s[d, pl.ds(r0, c), :]
            g_mid = gc[mid[d]:mid[d] + 1, :]
            g_end = gc[end[d]:end[d] + 1, :]
            qa = q * jnp.exp(gc - g_mid)
            kb = k * jnp.exp(g_mid - gc)
            sc = jnp.where(keep[d], _dot_nt(qa.astype(BF16), kb.astype(BF16)), 0.0)
            scores = sc if scores is None else scores + sc
            bk_s[d, pl.ds(r0, c), :] = (kb * jnp.exp(g_end - g_mid)).astype(BF16)
            aq_s[pl.ds(r0, c), d * dk:(d + 1) * dk] = (qa * jnp.exp(g_mid)).astype(BF16)
            dec_ref[d, j] = jnp.broadcast_to(jnp.exp(g_end), (SUBLANES, dk))
        sc_s[pl.ds(r0, c), :] = scores.astype(BF16)
        return carry

    lax.fori_loop(0, n_all, stage_scores, 0, unroll=12)

    def stage_updates(j, carry):
        r0 = pl.multiple_of(j * c, c)
        v = v_s[pl.ds(r0, c), :]
        for d in range(2):
            upd_ref[d, j] = _dot_tn(v, bk_s[d, pl.ds(r0, c), :])
        return carry

    lax.fori_loop(0, n_all, stage_updates, 0, unroll=18)

    def advance(d, jj, st):
        return st * dec_ref[d, jj][0:1, :] + upd_ref[d, jj]

    def ctx_body(j, sts):
        return advance(0, j, sts[0]), advance(1, n_c - 1 - j, sts[1])

    def lat_body(j, sts):
        jf, jr = j, n_l - 1 - j
        sprev_s[jf, :, 0:dk] = sts[0].astype(BF16)
        sprev_s[jr, :, dk:2 * dk] = sts[1].astype(BF16)
        return advance(0, n_c + jf, sts[0]), advance(1, n_c + jr, sts[1])

    zero = jnp.zeros((dk, dk), F32)
    sts = lax.fori_loop(0, n_c, ctx_body, (zero, zero), unroll=2)
    lax.fori_loop(0, n_l, lat_body, sts, unroll=4)

    ng = ng_ref[...]

    def stage_outputs(j, carry):
        r0 = pl.multiple_of(j * c, c)
        rc = pl.multiple_of(cx + r0, c)
        o = _dot(sc_s[pl.ds(rc, c), :], v_s[pl.ds(rc, c), :]) + _dot_nt(aq_s[pl.ds(rc, c), :], sprev_s[j])
        g = p_ref[pl.ds(rc, c), col(4)]
        o_ref[pl.ds(r0, c), hh * dk:(hh + 1) * dk] = (_rms(o) * ng * _silu(g)).astype(o_ref.dtype)
        return carry

    lax.fori_loop(0, n_l, stage_outputs, 0, unroll=32)


def _hgrn(hc, hl, w_in_b, col0, heads, lb, norm_g):
    b, cx, d = hc.shape
    t = hl.shape[1]
    dk = HG_DK
    n_all = (cx + t) // HG_CHUNK
    kern = functools.partial(_hgrn_kernel, cx=cx, t=t)
    hps = HG_HEADS_PER_STEP
    wide = hps * dk
    assert heads % hps == 0 and col0 % wide == 0

    def group(gi):
        return pl.BlockSpec((d, wide), lambda bi, h: (0, col0 // wide + gi * (heads // hps) + h))

    return pl.pallas_call(
        kern,
        grid=(b, heads // hps),
        in_specs=[
            pl.BlockSpec((None, cx, d), lambda bi, h: (bi, 0, 0)),
            pl.BlockSpec((None, t, d), lambda bi, h: (bi, 0, 0)),
            group(0), group(1), group(2), group(3), group(4),
            pl.BlockSpec((hps, 1, dk), lambda bi, h: (h, 0, 0)),
            pl.BlockSpec((1, dk), lambda bi, h: (0, 0)),
        ],
        out_specs=pl.BlockSpec((None, t, wide), lambda bi, h: (bi, 0, h)),
        out_shape=jax.ShapeDtypeStruct((b, t, heads * dk), BF16),
        scratch_shapes=[
            pltpu.VMEM((cx + t, 5 * wide), F32),
            pltpu.VMEM((t // HG_CHUNK, dk, 2 * dk), BF16),
            pltpu.VMEM((cx + t, dk), F32),
            pltpu.VMEM((cx + t, dk), BF16),
            pltpu.VMEM((2, cx + t, dk), F32),
            pltpu.VMEM((2, cx + t, 2 * dk), BF16),
            pltpu.VMEM((2, cx + t, dk), F32),
            pltpu.VMEM((2, cx + t, dk), BF16),
            pltpu.VMEM((cx + t, HG_CHUNK), BF16),
            pltpu.VMEM((cx + t, 2 * dk), BF16),
            pltpu.VMEM((2, n_all, dk, dk), F32),
            pltpu.VMEM((2, n_all, SUBLANES, dk), F32),
        ],
        compiler_params=_params(("arbitrary", "arbitrary"), 56),
        name="hgrn",
    )(hc, hl, w_in_b, w_in_b, w_in_b, w_in_b, w_in_b, lb, norm_g.reshape(1, dk))


def _merge_kernel(x_ref, h_ref, ya_ref, yb_ref, wm_ref, wa_ref, wb_ref, wo_ref,
                  gate_ref, shift_ref, scale_ref, g2_ref, rwt_ref, rb_ref,
                  x1_ref, hs_ref, pos_ref, cnt_ref, h2b_s, *, tm, d, ne, cap):
    m = _dot(h_ref[...], wm_ref[...])
    za = _dot(ya_ref[...], wa_ref[...])
    zb = _dot(yb_ref[...], wb_ref[...])
    mix = _sigmoid(m[:, :d]) * za + _sigmoid(m[:, d:]) * zb
    out = _dot(mix.astype(BF16), wo_ref[...])
    x1 = x_ref[...] + gate_ref[...] * out
    x1_ref[...] = x1
    h2 = _rms(x1) * g2_ref[...] * (1.0 + scale_ref[...]) + shift_ref[...]
    h2b_s[...] = h2.astype(BF16)

    logits = lax.dot_general(rwt_ref[...], h2, (((1,), (1,)), ((), ())),
                             preferred_element_type=F32, precision=HIGHEST) + rb_ref[...]
    eid = lax.broadcasted_iota(I32, (ne, tm), 0)
    neg = jnp.float32(-jnp.inf)
    work = logits
    hot = jnp.zeros((ne, tm), F32)
    vals, sels = [], []
    for _ in range(TOP_K):
        mx = jnp.max(work, axis=0, keepdims=True)
        ix = jnp.min(jnp.where(work == mx, eid, ne), axis=0, keepdims=True)
        sel = eid == ix
        work = jnp.where(sel, neg, work)
        hot = hot + sel.astype(F32)
        vals.append(mx)
        sels.append(sel)
    ex = [jnp.exp(v - vals[0]) for v in vals]
    den = sum(ex[1:], ex[0])
    gts = [e / den for e in ex]

    ti = lax.broadcasted_iota(I32, (tm, tm), 0)
    tj = lax.broadcasted_iota(I32, (tm, tm), 1)
    strict = (ti < tj).astype(BF16)
    before = _dot(hot.astype(BF16), strict)
    cnt = jnp.sum(hot, axis=1, keepdims=True)
    nchunks = jnp.ceil(cnt * (1.0 / MOE_CHUNK))
    ei = lax.broadcasted_iota(I32, (ne, ne), 0)
    ej = lax.broadcasted_iota(I32, (ne, ne), 1)
    earlier = (ej < ei).astype(BF16)
    start = _dot(earlier, jnp.broadcast_to(nchunks, (ne, LANES)).astype(BF16))[:, 0:1] * MOE_CHUNK
    where_to = before + start
    poss = [jnp.sum(jnp.where(s, where_to, 0.0), axis=0, keepdims=True).astype(I32) for s in sels]

    rb_rows = SORT_ROWS
    lane = lax.broadcasted_iota(I32, (rb_rows, LANES), 1)

    def sort_rows(i, carry):
        r0 = pl.multiple_of(i * rb_rows, rb_rows)
        pid = lax.broadcasted_iota(I32, (rb_rows, tm), 0)
        gsel = jnp.zeros((rb_rows, tm), F32)
        for k in range(TOP_K):
            gsel = jnp.where(pid == poss[k] - r0, gts[k], gsel)
        place = jnp.where(gsel != 0.0, 1.0, 0.0)
        hs_ref[pl.ds(r0, rb_rows), 0:d] = _dot(place.astype(BF16), h2b_s[...]).astype(BF16)
        g = jnp.sum(gsel, axis=1, keepdims=True)
        g_hi = g.astype(BF16).astype(F32)
        hs_ref[pl.ds(r0, rb_rows), d:d + LANES] = jnp.where(lane < LANES // 2, g_hi, g - g_hi).astype(BF16)
        return carry

    lax.fori_loop(0, cap // rb_rows, sort_rows, 0)

    row = lax.broadcasted_iota(I32, (SUBLANES, tm), 0)
    packed = jnp.zeros((SUBLANES, tm), I32)
    for k, p in enumerate(poss):
        packed = jnp.where(row == k, jnp.broadcast_to(p, (SUBLANES, tm)), packed)
    pos_ref[...] = packed
    cnt_ref[...] = jnp.broadcast_to(cnt, cnt_ref.shape)


def _merge(x, hl, ya, yb, w_m, w_ba, w_bb, w_out, gate2, shift3, scale4, g2, rwt, rb, tm):
    b, t, d = x.shape
    ne = rwt.shape[0]
    nt = t // tm
    tok = b * t
    cap = _sorted_rows(tm, ne)
    kern = functools.partial(_merge_kernel, tm=tm, d=d, ne=ne, cap=cap)
    row_spec = pl.BlockSpec((None, tm, d), lambda bi, i: (bi, i, 0))
    mod_spec = pl.BlockSpec((None, 1, d), lambda bi, i: (bi, 0, 0))
    flat = lambda bi, i: (bi * nt + i, 0)
    lane = lambda bi, i: (0, bi * nt + i)

    def whole(shape):
        return pl.BlockSpec(shape, lambda bi, i: (0,) * len(shape))

    return pl.pallas_call(
        kern,
        grid=(b, nt),
        in_specs=[
            row_spec, row_spec, row_spec, row_spec,
            whole((d, 2 * d)), whole((d, d)), whole((d, d)), whole((d, d)),
            mod_spec, mod_spec, mod_spec,
            whole((1, d)), whole((ne, d)), whole((ne, 1)),
        ],
        out_specs=[
            row_spec,
            pl.BlockSpec((cap, d + LANES), flat),
            pl.BlockSpec((SUBLANES, tm), lane),
            pl.BlockSpec((None, ne, LANES), lambda bi, i: (bi * nt + i, 0, 0)),
        ],
        out_shape=[
            jax.ShapeDtypeStruct((b, t, d), F32),
            jax.ShapeDtypeStruct((b * nt * cap, d + LANES), BF16),
            jax.ShapeDtypeStruct((SUBLANES, tok), I32),
            jax.ShapeDtypeStruct((b * nt, ne, LANES), F32),
        ],
        scratch_shapes=[pltpu.VMEM((tm, d), BF16)],
        compiler_params=_params(("arbitrary", "arbitrary"), 56),
        name="merge",
    )(x, hl, ya, yb, w_m, w_ba, w_bb, w_out, gate2, shift3, scale4, g2.reshape(1, d), rwt, rb)


def _experts_kernel(bexp_ref, bact_ref, src_ref, src1_ref, src2_ref, hs_hbm, w1_ref, b1_ref, w2_ref, b2_ref,
                    y_ref, xbuf, act_s, w1p_s, w2_s, gsem, *, blk, nblocks, d):
    j = pl.program_id(0)
    slot = lax.rem(j, MOE_RING)
    far = lax.rem(j + 2, MOE_RING)
    f = w2_s.shape[1]
    pw = 2 * LANES
    n1 = 2 * f // pw
    n2 = d // pw
    nch = blk // MOE_CHUNK
    active = bact_ref[j] == 1

    def request(idx_ref, buf_slot):
        for c in range(nch):
            src = pl.multiple_of(idx_ref[0, 0, c] * MOE_CHUNK, MOE_CHUNK)
            pltpu.make_async_copy(hs_hbm.at[pl.ds(src, MOE_CHUNK), :],
                                  xbuf.at[buf_slot, pl.ds(c * MOE_CHUNK, MOE_CHUNK), :], gsem.at[buf_slot]).start()

    @pl.when(j == 0)
    def _():
        request(src_ref, 0)
        request(src1_ref, 1)

    @pl.when(j + 2 < nblocks)
    def _():
        request(src2_ref, far)

    pltpu.make_async_copy(hs_hbm.at[pl.ds(0, blk), :], xbuf.at[slot], gsem.at[slot]).wait()

    new_expert = (j == 0) | (bexp_ref[j] != bexp_ref[jnp.maximum(j - 1, 0)])

    @pl.when(new_expert & active)
    def _():
        src = lax.broadcasted_iota(I32, (pw, pw), 0)
        dst = lax.broadcasted_iota(I32, (pw, pw), 1)
        want = jnp.where(dst < pw // 2, 2 * dst, 2 * (dst - pw // 2) + 1)
        sel = (src == want).astype(BF16)
        for cb in range(n1):
            w1p_s[cb] = _dot(w1_ref[:, cb * pw:(cb + 1) * pw].astype(BF16), sel).astype(BF16)
        for cb in range(n2):
            w2_s[cb] = w2_ref[:, cb * pw:(cb + 1) * pw].astype(BF16)

    @pl.when(active)
    def _():
        x = xbuf[slot, :, 0:d]
        gate = (xbuf[slot, :, d:d + 1].astype(F32) + xbuf[slot, :, d + LANES // 2:d + LANES // 2 + 1].astype(F32))
        for n in range(n1):
            h = _dot(x, w1p_s[n]) + b1_ref[n]
            hg = jnp.minimum(h[:, :LANES], SWIGLU_LIMIT)
            hu = jnp.clip(h[:, LANES:], -SWIGLU_LIMIT, SWIGLU_LIMIT)
            act_s[n] = (hg * _sigmoid(SWIGLU_ALPHA * hg) * (hu + 1.0)).astype(BF16)
        act = jnp.concatenate([act_s[i] for i in range(n1)], axis=1)
        for n in range(n2):
            y = _dot(act, w2_s[n]) + b2_ref[n]
            y_ref[:, n * pw:(n + 1) * pw] = (y * gate).astype(y_ref.dtype)

    @pl.when(jnp.logical_not(active))
    def _():
        y_ref[...] = jnp.zeros_like(y_ref)


def _experts(block_expert, block_active, src3, hs, w1, b1p, w2, b2, blk):
    nblocks = block_expert.shape[0]
    f, d = w2.shape[1], w2.shape[2]
    pw = 2 * LANES
    n1, n2 = 2 * f // pw, d // pw
    nch = blk // MOE_CHUNK
    b1p = b1p.reshape(-1, n1, 1, pw)
    b2 = b2.reshape(-1, n2, 1, pw)
    kern = functools.partial(_experts_kernel, blk=blk, nblocks=nblocks, d=d)
    wmap = lambda j, be, ba: (be[j], 0, 0)

    def idx_spec(ahead):
        return pl.BlockSpec((1, 1, nch), lambda j, be, ba: (jnp.minimum(j + ahead, nblocks - 1), 0, 0),
                            memory_space=pltpu.SMEM)

    grid_spec = pltpu.PrefetchScalarGridSpec(
        num_scalar_prefetch=2,
        grid=(nblocks,),
        in_specs=[
            idx_spec(0), idx_spec(1), idx_spec(2),
            pl.BlockSpec(memory_space=pl.ANY),
            pl.BlockSpec((None, d, 2 * f), wmap),
            pl.BlockSpec((None, n1, 1, pw), lambda j, be, ba: (be[j], 0, 0, 0)),
            pl.BlockSpec((None, f, d), wmap),
            pl.BlockSpec((None, n2, 1, pw), lambda j, be, ba: (be[j], 0, 0, 0)),
        ],
        out_specs=pl.BlockSpec((blk, d), lambda j, be, ba: (j, 0)),
        scratch_shapes=[
            pltpu.VMEM((MOE_RING, blk, d + LANES), BF16),
            pltpu.VMEM((n1, blk, LANES), BF16),
            pltpu.VMEM((n1, d, pw), BF16),
            pltpu.VMEM((n2, f, pw), BF16),
            pltpu.SemaphoreType.DMA((MOE_RING,)),
        ],
    )
    return pl.pallas_call(
        kern,
        grid_spec=grid_spec,
        out_shape=jax.ShapeDtypeStruct((nblocks * blk, d), BF16),
        compiler_params=_params(("arbitrary",), 56),
        name="experts",
    )(block_expert, block_active, src3, src3, src3, hs, w1, b1p, w2, b2)


def _combine_kernel(dst_ref, dstn_ref, y_hbm, x1_ref, pos_ref, gate_ref, fg_ref, o_ref, ybuf, sem,
                    *, tm, cap, ntiles):
    i = pl.program_id(0)
    slot = lax.rem(i, 2)
    nch = cap // MOE_CHUNK

    def request(idx_ref, buf_slot):
        def body(c, carry):
            src = pl.multiple_of(idx_ref[0, 0, c] * MOE_CHUNK, MOE_CHUNK)
            dst = pl.multiple_of(c * MOE_CHUNK, MOE_CHUNK)
            pltpu.make_async_copy(y_hbm.at[pl.ds(src, MOE_CHUNK), :],
                                  ybuf.at[buf_slot, pl.ds(dst, MOE_CHUNK), :], sem.at[buf_slot]).start()
            return carry
        lax.fori_loop(0, nch, body, 0, unroll=8)

    @pl.when(i == 0)
    def _():
        request(dst_ref, 0)

    @pl.when(i + 1 < ntiles)
    def _():
        request(dstn_ref, 1 - slot)

    pltpu.make_async_copy(y_hbm.at[pl.ds(0, cap), :], ybuf.at[slot], sem.at[slot]).wait()

    pos = pos_ref[...]
    acc = None
    for c0 in range(0, cap, SORT_ROWS):
        col = lax.broadcasted_iota(I32, (tm, SORT_ROWS), 1) + c0
        pick = jnp.zeros((tm, SORT_ROWS), F32)
        for k in range(TOP_K):
            pick = jnp.where(col == pos[:, k:k + 1], 1.0, pick)
        part = _dot(pick.astype(BF16), ybuf[slot, c0:c0 + SORT_ROWS, :])
        acc = part if acc is None else acc + part
    x2 = x1_ref[...] + gate_ref[...] * acc
    o_ref[...] = _rms(x2) * fg_ref[...]


def _combine(dst3, y, x1, pos_t, gate5, final_g, tm, cap):
    b, t, d = x1.shape
    nt = t // tm
    ntiles = b * nt
    nch = cap // MOE_CHUNK
    kern = functools.partial(_combine_kernel, tm=tm, cap=cap, ntiles=ntiles)
    row_map = lambda i: (i // nt, i % nt, 0)
    return pl.pallas_call(
        kern,
        grid=(ntiles,),
        in_specs=[
            pl.BlockSpec((1, 1, nch), lambda i: (i, 0, 0), memory_space=pltpu.SMEM),
            pl.BlockSpec((1, 1, nch), lambda i: (jnp.minimum(i + 1, ntiles - 1), 0, 0), memory_space=pltpu.SMEM),
            pl.BlockSpec(memory_space=pl.ANY),
            pl.BlockSpec((None, tm, d), row_map),
            pl.BlockSpec((tm, TOP_K), lambda i: (i, 0)),
            pl.BlockSpec((None, 1, d), lambda i: (i // nt, 0, 0)),
            pl.BlockSpec((1, d), lambda i: (0, 0)),
        ],
        out_specs=pl.BlockSpec((None, tm, d), row_map),
        out_shape=jax.ShapeDtypeStruct((b, t, d), F32),
        scratch_shapes=[
            pltpu.VMEM((2, cap, d), BF16),
            pltpu.SemaphoreType.DMA((2,)),
        ],
        compiler_params=_params(("arbitrary",), 48),
        name="combine",
    )(dst3, dst3, y, x1, pos_t, gate5, final_g.reshape(1, d))


def _sorted_rows(tm, ne):
    need = tm * TOP_K + ne * (MOE_CHUNK - 1) + MOE_CHUNK
    return -(-need // SORT_ROWS) * SORT_ROWS


def _pick_tile(n, pref):
    tm = pref
    while n % tm:
        tm //= 2
    return tm


def kernel(x, c, ctx, c_ctx, ada_w, ada_b, norm1_g, norm2_g, w_in, lru_conv_w, lru_conv_b, lru_wa, lru_ba, lru_wx, lru_bx, lru_lam, hg_lb_logits, hg_norm_g, w_branch_a, w_branch_b, w_out, router_w, router_b, moe_w1, moe_b1, moe_w2, moe_b2, final_g):
    b, t, d = x.shape
    cx = ctx.shape[1]
    layer = 0
    w_lru = lru_conv_w.shape[2]
    qk = (w_in.shape[2] - 2 * w_lru - 2 * d) // 5
    heads = qk // HG_DK
    ne = router_w.shape[2]

    pad = (-(b + 1)) % SUBLANES
    c_all = jnp.concatenate([c, c_ctx[None, :], jnp.zeros((pad, d), F32)], axis=0)
    mod = _modulation(c_all, ada_w[layer], ada_b[layer])
    mod_l = mod[:b].reshape(b, N_MOD, 1, d)
    mod_c = mod[b].reshape(N_MOD, 1, 1, d)

    w_in_b = w_in[layer].astype(BF16)
    hg0 = 2 * w_lru
    w_m = w_in_b[:, hg0 + 5 * qk:]
    g_blocks = w_lru // LRU_BLOCK
    wg = jnp.concatenate([lru_wa[layer, 0], lru_wx[layer, 0], lru_wa[layer, 1], lru_wx[layer, 1]], axis=-1)
    wg = (0.5 * wg).astype(BF16)
    bg = jnp.concatenate([lru_ba[layer, 0], lru_bx[layer, 0], lru_ba[layer, 1], lru_bx[layer, 1]], axis=-1)
    bg = (0.5 * bg).reshape(g_blocks, 1, 4 * LRU_BLOCK)
    lb_all = jnp.cumsum(jax.nn.softmax(hg_lb_logits.astype(F32), axis=0), axis=0)
    lb = lb_all[layer].reshape(heads, 1, HG_DK)

    tm_n = _pick_tile(t, 512)
    hl, axl = _norm_proj(x, mod_l[:, 0], mod_l[:, 1], norm1_g[layer], w_in_b, 2 * w_lru, tm_n)
    hc, axc = _norm_proj(ctx, mod_c[0], mod_c[1], norm1_g[layer], w_in_b, 2 * w_lru, _pick_tile(cx, 256))
    ya = _lru(axl, axc, lru_conv_w[layer], lru_conv_b[layer], wg, bg, lru_lam[layer], b, cx, t)
    yb = _hgrn(hc, hl, w_in_b, hg0, heads, lb, hg_norm_g[layer])

    rwt = router_w[layer].T
    rb = router_b[layer].reshape(ne, 1)
    tm = _pick_tile(t, 512)
    x1, hs, pos8, cnt = _merge(
        x, hl, ya, yb, w_m, w_branch_a[layer].astype(BF16), w_branch_b[layer].astype(BF16),
        w_out[layer].astype(BF16), mod_l[:, 2], mod_l[:, 3], mod_l[:, 4], norm2_g[layer], rwt, rb, tm)

    blk = MOE_BLOCK
    ch = MOE_CHUNK
    bpc = blk // ch
    tok = b * t
    ntile = tok // tm
    cap = _sorted_rows(tm, ne)
    tch = cap // ch
    run = (cnt[:, :, 0].astype(I32) + ch - 1) // ch
    loc_end = jnp.cumsum(run, axis=1)
    loc_start = loc_end - run
    upto = jnp.cumsum(run, axis=0)
    total = upto[-1]
    padded = (total + bpc - 1) // bpc * bpc
    gend = jnp.cumsum(padded)
    gstart = gend - padded
    where_run = gstart[None, :] + upto - run
    max_chunks = tok * TOP_K // ch + ntile * ne + ne * bpc
    nblocks = -(-max_chunks // bpc)

    eids = jnp.arange(ne, dtype=I32)
    tids = jnp.arange(ntile, dtype=I32)
    g = jnp.arange(nblocks * bpc, dtype=I32)
    g_e = jnp.minimum(jnp.sum(g[:, None] >= gend[None, :], axis=1), ne - 1).astype(I32)
    is_e = g_e[:, None] == eids[None, :]
    off = g - jnp.sum(jnp.where(is_e, gstart[None, :], 0), axis=1)
    g_total = jnp.sum(jnp.where(is_e, total[None, :], 0), axis=1)
    upto_g = jnp.sum(jnp.where(is_e[:, None, :], upto[None, :, :], 0), axis=2)
    g_i = jnp.minimum(jnp.sum(off[:, None] >= upto_g, axis=1), ntile - 1).astype(I32)
    both = (g_i[:, None] == tids[None, :])[:, :, None] & is_e[:, None, :]

    def at_run(table):
        return jnp.sum(jnp.where(both, table[None, :, :], 0), axis=(1, 2))

    within = off - (at_run(upto) - at_run(run))
    zero_chunk = tch - 1
    src = jnp.where((off < g_total) & (g < gend[-1]), g_i * tch + at_run(loc_start) + within, zero_chunk)

    block_first = jnp.arange(nblocks, dtype=I32) * bpc
    block_active = block_first < gend[-1]
    block_expert = jnp.minimum(jnp.sum(block_first[:, None] >= gend[None, :], axis=1), ne - 1).astype(I32)
    last_expert = jnp.max(jnp.where(block_active, block_expert, 0))
    block_expert = jnp.where(block_active, block_expert, last_expert).astype(I32)

    c = jnp.arange(tch, dtype=I32)
    c_e = jnp.minimum(jnp.sum(c[None, :, None] >= loc_end[:, None, :], axis=2), ne - 1).astype(I32)
    is_ce = c_e[:, :, None] == eids[None, None, :]
    back = jnp.sum(jnp.where(is_ce, (where_run - loc_start)[:, None, :], 0), axis=2) + c[None, :]
    back = jnp.where(c[None, :] < loc_end[:, -1:], back, 0)

    b1 = moe_b1[layer]
    f = b1.shape[1] // 2
    b1p = b1.reshape(ne, f // LANES, LANES, 2).transpose(0, 1, 3, 2).reshape(ne, 1, 2 * f)
    y = _experts(block_expert, block_active.astype(I32), src.reshape(nblocks, 1, bpc), hs, moe_w1[layer], b1p,
                 moe_w2[layer], moe_b2[layer].reshape(ne, 1, d), blk)
    return _combine(back.reshape(ntile, 1, tch), y, x1, pos8[:TOP_K].T, mod_l[:, 5], final_g, tm, cap)
```

```python
import functools

import jax
import jax.numpy as jnp
from jax import lax
from jax.experimental import pallas as pl
from jax.experimental.pallas import tpu as pltpu

F32 = jnp.float32
BF16 = jnp.bfloat16
I32 = jnp.int32
HIGHEST = lax.Precision.HIGHEST

EPS = 1e-6
N_MOD = 6
GRID_W = 64
CONV_LEFT = 2
CONV_WIDTH = 4
LRU_C = 8.0
LRU_BLOCK = 128
HG_DK = 128
HG_CHUNK = 64
HG_HEADS_PER_STEP = 2
TOP_K = 4
SWIGLU_LIMIT = 7.0
SWIGLU_ALPHA = 1.702
MOE_BLOCK = 640
MOE_CHUNK = 16
SORT_ROWS = 512
MOE_RING = 3
SUBLANES = 8
LANES = 128
HALO = 8


def _params(sem, vmem_mb):
    return pltpu.CompilerParams(dimension_semantics=sem, vmem_limit_bytes=vmem_mb * 1024 * 1024)


def _dot(a, b):
    return jnp.dot(a, b, preferred_element_type=F32)


def _dot_nt(a, b):
    return lax.dot_general(a, b, (((1,), (1,)), ((), ())), preferred_element_type=F32)


def _dot_tn(a, b):
    return lax.dot_general(a, b, (((0,), (0,)), ((), ())), preferred_element_type=F32)


def _sigmoid(x):
    return 0.5 * jnp.tanh(0.5 * x) + 0.5


def _silu(x):
    return x * _sigmoid(x)


def _rms(x):
    return x * lax.rsqrt(jnp.mean(x * x, axis=-1, keepdims=True) + EPS)


def _mod_kernel(c_ref, w_ref, b_ref, o_ref):
    s = _silu(c_ref[...])
    o_ref[...] = jnp.dot(s, w_ref[...], preferred_element_type=F32, precision=HIGHEST) + b_ref[...]


def _modulation(c_all, ada_w, ada_b):
    m, d = c_all.shape
    n = ada_w.shape[1]
    tn = 1024
    return pl.pallas_call(
        _mod_kernel,
        grid=(n // tn,),
        in_specs=[
            pl.BlockSpec((m, d), lambda j: (0, 0)),
            pl.BlockSpec((d, tn), lambda j: (0, j)),
            pl.BlockSpec((1, tn), lambda j: (0, j)),
        ],
        out_specs=pl.BlockSpec((m, tn), lambda j: (0, j)),
        out_shape=jax.ShapeDtypeStruct((m, n), F32),
        compiler_params=_params(("arbitrary",), 32),
        name="mod",
    )(c_all, ada_w, ada_b.reshape(1, n))


def _norm_proj_kernel(x_ref, shift_ref, scale_ref, g_ref, w_ref, h_ref, o_ref):
    y = _rms(x_ref[...]) * g_ref[...]
    h = (y * (1.0 + scale_ref[...]) + shift_ref[...]).astype(h_ref.dtype)
    h_ref[...] = h
    o_ref[...] = _dot(h, w_ref[...]).astype(o_ref.dtype)


def _norm_proj(x, shift, scale, g, w_in_b, n_out, tm):
    b, l, d = x.shape
    per_batch = shift.shape[0] == b and b > 1
    mod_map = (lambda bi, i: (bi, 0, 0)) if per_batch else (lambda bi, i: (0, 0, 0))
    return pl.pallas_call(
        _norm_proj_kernel,
        grid=(b, l // tm),
        in_specs=[
            pl.BlockSpec((None, tm, d), lambda bi, i: (bi, i, 0)),
            pl.BlockSpec((None, 1, d), mod_map),
            pl.BlockSpec((None, 1, d), mod_map),
            pl.BlockSpec((1, d), lambda bi, i: (0, 0)),
            pl.BlockSpec((d, n_out), lambda bi, i: (0, 0)),
        ],
        out_specs=[
            pl.BlockSpec((None, tm, d), lambda bi, i: (bi, i, 0)),
            pl.BlockSpec((None, tm, n_out), lambda bi, i: (bi, i, 0)),
        ],
        out_shape=[
            jax.ShapeDtypeStruct((b, l, d), BF16),
            jax.ShapeDtypeStruct((b, l, n_out), BF16),
        ],
        compiler_params=_params(("arbitrary", "arbitrary"), 40),
        name="norm_proj",
    )(x, shift, scale, g.reshape(1, d), w_in_b)


def _lru_kernel(axl_ref, agl_ref, axc_ref, cw_ref, cb_ref, wg_ref, bg_ref, lam_ref, o_ref,
                xc_ref, yacc_ref, sx_ref, sa_ref, sb_ref, sh_ref, *, nb, cx, t):
    tb = GRID_W
    ctx0 = HALO
    gw = 2 * LRU_BLOCK
    pre = CONV_LEFT
    steps = pre + tb + (HALO - pre)

    zeros_h = jnp.zeros((HALO, LRU_BLOCK), F32)
    for b in range(nb):
        xc_ref[b, 0:HALO, :] = zeros_h
        xc_ref[b, ctx0:ctx0 + cx, :] = axc_ref[b].astype(F32)
        xc_ref[b, ctx0 + cx:ctx0 + cx + HALO, :] = zeros_h

    cw = cw_ref[...]
    cb = cb_ref[...]
    lam = lam_ref[...]
    sp = jnp.maximum(-lam, 0.0) + jnp.log1p(jnp.exp(-jnp.abs(lam)))
    half_neg_c_sp = (-0.5 * LRU_C) * sp

    def zero_history(d):
        sx_ref[d, 0:pre * nb, :] = jnp.zeros((pre * nb, LRU_BLOCK), F32)
        sx_ref[d, (pre + tb) * nb:steps * nb, :] = jnp.zeros(((steps - pre - tb) * nb, LRU_BLOCK), F32)

    def load_ctx_block(d, base):
        for b in range(nb):
            sx_ref[d, pl.ds(b, steps, stride=nb), :] = xc_ref[b, pl.ds(base - pre, steps), :]

    def load_latent_block(d, r0):
        for b in range(nb):
            sx_ref[d, pl.ds(pre * nb + b, tb, stride=nb), :] = axl_ref[b, pl.ds(r0, tb), :].astype(F32)

    def fill_gates(d):
        u = jnp.broadcast_to(cb, (tb * nb, LRU_BLOCK))
        for k in range(CONV_WIDTH):
            u = u + sx_ref[d, k * nb:(k + tb) * nb, :] * cw[k:k + 1, :]
        th = jnp.tanh(_dot(u.astype(BF16), wg_ref[:, d * gw:(d + 1) * gw]) + bg_ref[:, d * gw:(d + 1) * gw])
        r2 = th[:, :LRU_BLOCK] + 1.0
        i2 = th[:, LRU_BLOCK:] + 1.0
        log_a = r2 * half_neg_c_sp[d:d + 1, :]
        a = jnp.exp(log_a)
        half_mult = jnp.sqrt(jnp.tanh(log_a) * (a * a + 1.0) * (-0.25))
        sa_ref[d] = a
        sb_ref[d] = half_mult * (i2 * u)

    def scan_pair(hf, hr):
        for s in range(tb):
            sr = tb - 1 - s
            hf = sa_ref[0, s * nb:(s + 1) * nb, :] * hf + sb_ref[0, s * nb:(s + 1) * nb, :]
            sh_ref[0, s * nb:(s + 1) * nb, :] = hf
            hr = sa_ref[1, sr * nb:(sr + 1) * nb, :] * hr + sb_ref[1, sr * nb:(sr + 1) * nb, :]
            sh_ref[1, sr * nb:(sr + 1) * nb, :] = hr
        return hf, hr

    n_c, n_l = cx // tb, t // tb

    def ctx_body(j, hs):
        load_ctx_block(0, pl.multiple_of(ctx0 + j * tb, SUBLANES))
        load_ctx_block(1, pl.multiple_of(ctx0 + (n_c - 1 - j) * tb, SUBLANES))
        fill_gates(0)
        fill_gates(1)
        return scan_pair(*hs)

    def emit(d, r0, final):
        for b in range(nb):
            yb = sh_ref[d, pl.ds(b, tb, stride=nb), :]
            if final:
                gate = jax.nn.gelu(agl_ref[b, pl.ds(r0, tb), :].astype(F32))
                o_ref[b, pl.ds(r0, tb), :] = ((yacc_ref[b, pl.ds(r0, tb), :] + yb) * gate).astype(o_ref.dtype)
            else:
                yacc_ref[b, pl.ds(r0, tb), :] = yb

    def lat_body(j, hs, final_f, final_r):
        rf = pl.multiple_of(j * tb, tb)
        rr = pl.multiple_of((n_l - 1 - j) * tb, tb)
        load_latent_block(0, rf)
        load_latent_block(1, rr)
        fill_gates(0)
        fill_gates(1)
        hs = scan_pair(*hs)
        emit(0, rf, final_f)
        emit(1, rr, final_r)
        return hs

    h0 = jnp.zeros((nb, LRU_BLOCK), F32)
    hs = lax.fori_loop(0, n_c, ctx_body, (h0, h0))
    zero_history(0)
    zero_history(1)
    half = n_l // 2
    hs = lax.fori_loop(0, half, functools.partial(lat_body, final_f=False, final_r=False), hs)
    if n_l % 2:
        hs = lat_body(jnp.int32(half), hs, False, True)
    lax.fori_loop(n_l - half, n_l, functools.partial(lat_body, final_f=True, final_r=True), hs)


def _lru(axl, axc, conv_w, conv_b, wg, bg, lam, nb, cx, t):
    w = conv_w.shape[1]
    g = w // LRU_BLOCK
    kern = functools.partial(_lru_kernel, nb=nb, cx=cx, t=t)
    return pl.pallas_call(
        kern,
        grid=(g,),
        in_specs=[
            pl.BlockSpec((nb, t, LRU_BLOCK), lambda j: (0, 0, j)),
            pl.BlockSpec((nb, t, LRU_BLOCK), lambda j: (0, 0, g + j)),
            pl.BlockSpec((nb, cx, LRU_BLOCK), lambda j: (0, 0, j)),
            pl.BlockSpec((CONV_WIDTH, LRU_BLOCK), lambda j: (0, j)),
            pl.BlockSpec((1, LRU_BLOCK), lambda j: (0, j)),
            pl.BlockSpec((None, LRU_BLOCK, 4 * LRU_BLOCK), lambda j: (j, 0, 0)),
            pl.BlockSpec((None, 1, 4 * LRU_BLOCK), lambda j: (j, 0, 0)),
            pl.BlockSpec((2, LRU_BLOCK), lambda j: (0, j)),
        ],
        out_specs=pl.BlockSpec((nb, t, LRU_BLOCK), lambda j: (0, 0, j)),
        out_shape=jax.ShapeDtypeStruct((nb, t, w), BF16),
        scratch_shapes=[
            pltpu.VMEM((nb, cx + 2 * HALO, LRU_BLOCK), F32),
            pltpu.VMEM((nb, t, LRU_BLOCK), F32),
            pltpu.VMEM((2, (GRID_W + HALO) * nb, LRU_BLOCK), F32),
            pltpu.VMEM((2, GRID_W * nb, LRU_BLOCK), F32),
            pltpu.VMEM((2, GRID_W * nb, LRU_BLOCK), F32),
            pltpu.VMEM((2, GRID_W * nb, LRU_BLOCK), F32),
        ],
        compiler_params=_params(("arbitrary",), 56),
        name="lru",
    )(axl, axl, axc, conv_w, conv_b.reshape(1, w), wg, bg, lam)


def _hgrn_kernel(hc_ref, hl_ref, wq_ref, wff_ref, wfb_ref, wv_ref, wg_ref, lb_ref, ng_ref, o_ref,
                 p_ref, *scratch, cx, t):
    w = jnp.concatenate([wq_ref[...], wff_ref[...], wfb_ref[...], wv_ref[...], wg_ref[...]], axis=1)
    p_ref[0:cx, :] = _dot(hc_ref[...], w)
    p_ref[cx:cx + t, :] = _dot(hl_ref[...], w)
    for hh in range(HG_HEADS_PER_STEP):
        _hgrn_head(hh, lb_ref, ng_ref, o_ref, p_ref, *scratch, cx=cx, t=t)


def _hgrn_head(hh, lb_ref, ng_ref, o_ref,
               p_ref, sprev_s, q_s, v_s, k_s, hl_s, gc_s, bk_s, sc_s, aq_s, upd_ref, dec_ref,
               *, cx, t):
    c = HG_CHUNK
    dk = HG_DK
    n_c, n_l = cx // c, t // c
    n_all = n_c + n_l

    def col(group):
        start = (group * HG_HEADS_PER_STEP + hh) * dk
        return slice(start, start + dk)

    lb = lb_ref[hh]
    f_half = 0.5 * (1.0 - lb)
    f_mid = lb + f_half
    ri = lax.broadcasted_iota(I32, (c, c), 0)
    ci = lax.broadcasted_iota(I32, (c, c), 1)
    keep = (ri >= ci, ci >= ri)
    mid = (c // 2 - 1, c // 2)
    end = (c - 1, 0)

    def stage_gates(j, carry):
        r0 = pl.multiple_of(j * c, c)
        q_s[pl.ds(r0, c), :] = _silu(p_ref[pl.ds(r0, c), col(0)])
        v_s[pl.ds(r0, c), :] = p_ref[pl.ds(r0, c), col(3)].astype(BF16)
        for d in range(2):
            f = f_mid + f_half * jnp.tanh(0.5 * p_ref[pl.ds(r0, c), col(1 + d)])
            logf = jnp.log(f)
            hi = logf.astype(BF16)
            lo = (logf - hi.astype(F32)).astype(BF16)
            k_s[d, pl.ds(r0, c), :] = 1.0 - f
            hl_s[d, pl.ds(r0, c), :] = jnp.concatenate([hi, lo], axis=1)
        return carry

    lax.fori_loop(0, n_all, stage_gates, 0, unroll=4)

    def stage_cumsum(j, carry):
        r0 = pl.multiple_of(j * c, c)
        for d in range(2):
            s2 = _dot(keep[d].astype(BF16), hl_s[d, pl.ds(r0, c), :])
            gc_s[d, pl.ds(r0, c), :] = s2[:, :dk] + s2[:, dk:]
        return carry

    lax.fori_loop(0, n_all, stage_cumsum, 0, unroll=18)

    def stage_scores(j, carry):
        r0 = pl.multiple_of(j * c, c)
        q = q_s[pl.ds(r0, c), :]
        scores = None
        for d in range(2):
            gc = gc_s[d, pl.ds(r0, c), :]
            k = k_s[d, pl.ds(r0, c), :]
            g_mid = gc[mid[d]:mid[d] + 1, :]
            g_end = gc[end[d]:end[d] + 1, :]
            qa = q * jnp.exp(gc - g_mid)
            kb = k * jnp.exp(g_mid - gc)
            sc = jnp.where(keep[d], _dot_nt(qa.astype(BF16), kb.astype(BF16)), 0.0)
            scores = sc if scores is None else scores + sc
            bk_s[d, pl.ds(r0, c), :] = (kb * jnp.exp(g_end - g_mid)).astype(BF16)
            aq_s[pl.ds(r0, c), d * dk:(d + 1) * dk] = (qa * jnp.exp(g_mid)).astype(BF16)
            dec_ref[d, j] = jnp.broadcast_to(jnp.exp(g_end), (SUBLANES, dk))
        sc_s[pl.ds(r0, c), :] = scores.astype(BF16)
        return carry

    lax.fori_loop(0, n_all, stage_scores, 0, unroll=12)

    def stage_updates(j, carry):
        r0 = pl.multiple_of(j * c, c)
        v = v_s[pl.ds(r0, c), :]
        for d in range(2):
            upd_ref[d, j] = _dot_tn(v, bk_s[d, pl.ds(r0, c), :])
        return carry

    lax.fori_loop(0, n_all, stage_updates, 0, unroll=18)

    def advance(d, jj, st):
        return st * dec_ref[d, jj][0:1, :] + upd_ref[d, jj]

    def ctx_body(j, sts):
        return advance(0, j, sts[0]), advance(1, n_c - 1 - j, sts[1])

    def lat_body(j, sts):
        jf, jr = j, n_l - 1 - j
        sprev_s[jf, :, 0:dk] = sts[0].astype(BF16)
        sprev_s[jr, :, dk:2 * dk] = sts[1].astype(BF16)
        return advance(0, n_c + jf, sts[0]), advance(1, n_c + jr, sts[1])

    zero = jnp.zeros((dk, dk), F32)
    sts = lax.fori_loop(0, n_c, ctx_body, (zero, zero), unroll=2)
    lax.fori_loop(0, n_l, lat_body, sts, unroll=4)

    ng = ng_ref[...]

    def stage_outputs(j, carry):
        r0 = pl.multiple_of(j * c, c)
        rc = pl.multiple_of(cx + r0, c)
        o = _dot(sc_s[pl.ds(rc, c), :], v_s[pl.ds(rc, c), :]) + _dot_nt(aq_s[pl.ds(rc, c), :], sprev_s[j])
        g = p_ref[pl.ds(rc, c), col(4)]
        o_ref[pl.ds(r0, c), hh * dk:(hh + 1) * dk] = (_rms(o) * ng * _silu(g)).astype(o_ref.dtype)
        return carry

    lax.fori_loop(0, n_l, stage_outputs, 0, unroll=32)


def _hgrn(hc, hl, w_in_b, col0, heads, lb, norm_g):
    b, cx, d = hc.shape
    t = hl.shape[1]
    dk = HG_DK
    n_all = (cx + t) // HG_CHUNK
    kern = functools.partial(_hgrn_kernel, cx=cx, t=t)
    hps = HG_HEADS_PER_STEP
    wide = hps * dk
    assert heads % hps == 0 and col0 % wide == 0

    def group(gi):
        return pl.BlockSpec((d, wide), lambda bi, h: (0, col0 // wide + gi * (heads // hps) + h))

    return pl.pallas_call(
        kern,
        grid=(b, heads // hps),
        in_specs=[
            pl.BlockSpec((None, cx, d), lambda bi, h: (bi, 0, 0)),
            pl.BlockSpec((None, t, d), lambda bi, h: (bi, 0, 0)),
            group(0), group(1), group(2), group(3), group(4),
            pl.BlockSpec((hps, 1, dk), lambda bi, h: (h, 0, 0)),
            pl.BlockSpec((1, dk), lambda bi, h: (0, 0)),
        ],
        out_specs=pl.BlockSpec((None, t, wide), lambda bi, h: (bi, 0, h)),
        out_shape=jax.ShapeDtypeStruct((b, t, heads * dk), BF16),
        scratch_shapes=[
            pltpu.VMEM((cx + t, 5 * wide), F32),
            pltpu.VMEM((t // HG_CHUNK, dk, 2 * dk), BF16),
            pltpu.VMEM((cx + t, dk), F32),
            pltpu.VMEM((cx + t, dk), BF16),
            pltpu.VMEM((2, cx + t, dk), F32),
            pltpu.VMEM((2, cx + t, 2 * dk), BF16),
            pltpu.VMEM((2, cx + t, dk), F32),
            pltpu.VMEM((2, cx + t, dk), BF16),
            pltpu.VMEM((cx + t, HG_CHUNK), BF16),
            pltpu.VMEM((cx + t, 2 * dk), BF16),
            pltpu.VMEM((2, n_all, dk, dk), F32),
            pltpu.VMEM((2, n_all, SUBLANES, dk), F32),
        ],
        compiler_params=_params(("arbitrary", "arbitrary"), 56),
        name="hgrn",
    )(hc, hl, w_in_b, w_in_b, w_in_b, w_in_b, w_in_b, lb, norm_g.reshape(1, dk))


def _merge_kernel(x_ref, h_ref, ya_ref, yb_ref, wm_ref, wa_ref, wb_ref, wo_ref,
                  gate_ref, shift_ref, scale_ref, g2_ref, rwt_ref, rb_ref,
                  x1_ref, hs_ref, pos_ref, cnt_ref, h2b_s, *, tm, d, ne, cap):
    m = _dot(h_ref[...], wm_ref[...])
    za = _dot(ya_ref[...], wa_ref[...])
    zb = _dot(yb_ref[...], wb_ref[...])
    mix = _sigmoid(m[:, :d]) * za + _sigmoid(m[:, d:]) * zb
    out = _dot(mix.astype(BF16), wo_ref[...])
    x1 = x_ref[...] + gate_ref[...] * out
    x1_ref[...] = x1
    h2 = _rms(x1) * g2_ref[...] * (1.0 + scale_ref[...]) + shift_ref[...]
    h2b_s[...] = h2.astype(BF16)

    logits = lax.dot_general(rwt_ref[...], h2, (((1,), (1,)), ((), ())),
                             preferred_element_type=F32, precision=HIGHEST) + rb_ref[...]
    eid = lax.broadcasted_iota(I32, (ne, tm), 0)
    neg = jnp.float32(-jnp.inf)
    work = logits
    hot = jnp.zeros((ne, tm), F32)
    vals, sels = [], []
    for _ in range(TOP_K):
        mx = jnp.max(work, axis=0, keepdims=True)
        ix = jnp.min(jnp.where(work == mx, eid, ne), axis=0, keepdims=True)
        sel = eid == ix
        work = jnp.where(sel, neg, work)
        hot = hot + sel.astype(F32)
        vals.append(mx)
        sels.append(sel)
    ex = [jnp.exp(v - vals[0]) for v in vals]
    den = sum(ex[1:], ex[0])
    gts = [e / den for e in ex]

    ti = lax.broadcasted_iota(I32, (tm, tm), 0)
    tj = lax.broadcasted_iota(I32, (tm, tm), 1)
    strict = (ti < tj).astype(BF16)
    before = _dot(hot.astype(BF16), strict)
    cnt = jnp.sum(hot, axis=1, keepdims=True)
    nchunks = jnp.ceil(cnt * (1.0 / MOE_CHUNK))
    ei = lax.broadcasted_iota(I32, (ne, ne), 0)
    ej = lax.broadcasted_iota(I32, (ne, ne), 1)
    earlier = (ej < ei).astype(BF16)
    start = _dot(earlier, jnp.broadcast_to(nchunks, (ne, LANES)).astype(BF16))[:, 0:1] * MOE_CHUNK
    where_to = before + start
    poss = [jnp.sum(jnp.where(s, where_to, 0.0), axis=0, keepdims=True).astype(I32) for s in sels]

    rb_rows = SORT_ROWS
    lane = lax.broadcasted_iota(I32, (rb_rows, LANES), 1)

    def sort_rows(i, carry):
        r0 = pl.multiple_of(i * rb_rows, rb_rows)
        pid = lax.broadcasted_iota(I32, (rb_rows, tm), 0)
        gsel = jnp.zeros((rb_rows, tm), F32)
        for k in range(TOP_K):
            gsel = jnp.where(pid == poss[k] - r0, gts[k], gsel)
        place = jnp.where(gsel != 0.0, 1.0, 0.0)
        hs_ref[pl.ds(r0, rb_rows), 0:d] = _dot(place.astype(BF16), h2b_s[...]).astype(BF16)
        g = jnp.sum(gsel, axis=1, keepdims=True)
        g_hi = g.astype(BF16).astype(F32)
        hs_ref[pl.ds(r0, rb_rows), d:d + LANES] = jnp.where(lane < LANES // 2, g_hi, g - g_hi).astype(BF16)
        return carry

    lax.fori_loop(0, cap // rb_rows, sort_rows, 0)

    row = lax.broadcasted_iota(I32, (SUBLANES, tm), 0)
    packed = jnp.zeros((SUBLANES, tm), I32)
    for k, p in enumerate(poss):
        packed = jnp.where(row == k, jnp.broadcast_to(p, (SUBLANES, tm)), packed)
    pos_ref[...] = packed
    cnt_ref[...] = jnp.broadcast_to(cnt, cnt_ref.shape)


def _merge(x, hl, ya, yb, w_m, w_ba, w_bb, w_out, gate2, shift3, scale4, g2, rwt, rb, tm):
    b, t, d = x.shape
    ne = rwt.shape[0]
    nt = t // tm
    tok = b * t
    cap = _sorted_rows(tm, ne)
    kern = functools.partial(_merge_kernel, tm=tm, d=d, ne=ne, cap=cap)
    row_spec = pl.BlockSpec((None, tm, d), lambda bi, i: (bi, i, 0))
    mod_spec = pl.BlockSpec((None, 1, d), lambda bi, i: (bi, 0, 0))
    flat = lambda bi, i: (bi * nt + i, 0)
    lane = lambda bi, i: (0, bi * nt + i)

    def whole(shape):
        return pl.BlockSpec(shape, lambda bi, i: (0,) * len(shape))

    return pl.pallas_call(
        kern,
        grid=(b, nt),
        in_specs=[
            row_spec, row_spec, row_spec, row_spec,
            whole((d, 2 * d)), whole((d, d)), whole((d, d)), whole((d, d)),
            mod_spec, mod_spec, mod_spec,
            whole((1, d)), whole((ne, d)), whole((ne, 1)),
        ],
        out_specs=[
            row_spec,
            pl.BlockSpec((cap, d + LANES), flat),
            pl.BlockSpec((SUBLANES, tm), lane),
            pl.BlockSpec((None, ne, LANES), lambda bi, i: (bi * nt + i, 0, 0)),
        ],
        out_shape=[
            jax.ShapeDtypeStruct((b, t, d), F32),
            jax.ShapeDtypeStruct((b * nt * cap, d + LANES), BF16),
            jax.ShapeDtypeStruct((SUBLANES, tok), I32),
            jax.ShapeDtypeStruct((b * nt, ne, LANES), F32),
        ],
        scratch_shapes=[pltpu.VMEM((tm, d), BF16)],
        compiler_params=_params(("arbitrary", "arbitrary"), 56),
        name="merge",
    )(x, hl, ya, yb, w_m, w_ba, w_bb, w_out, gate2, shift3, scale4, g2.reshape(1, d), rwt, rb)


def _experts_kernel(bexp_ref, bact_ref, src_ref, src1_ref, src2_ref, hs_hbm, w1_ref, b1_ref, w2_ref, b2_ref,
                    y_ref, xbuf, act_s, w1p_s, w2_s, gsem, *, blk, nblocks, d):
    j = pl.program_id(0)
    slot = lax.rem(j, MOE_RING)
    far = lax.rem(j + 2, MOE_RING)
    f = w2_s.shape[1]
    pw = 2 * LANES
    n1 = 2 * f // pw
    n2 = d // pw
    nch = blk // MOE_CHUNK
    active = bact_ref[j] == 1

    def request(idx_ref, buf_slot):
        for c in range(nch):
            src = pl.multiple_of(idx_ref[0, 0, c] * MOE_CHUNK, MOE_CHUNK)
            pltpu.make_async_copy(hs_hbm.at[pl.ds(src, MOE_CHUNK), :],
                                  xbuf.at[buf_slot, pl.ds(c * MOE_CHUNK, MOE_CHUNK), :], gsem.at[buf_slot]).start()

    @pl.when(j == 0)
    def _():
        request(src_ref, 0)
        request(src1_ref, 1)

    @pl.when(j + 2 < nblocks)
    def _():
        request(src2_ref, far)

    pltpu.make_async_copy(hs_hbm.at[pl.ds(0, blk), :], xbuf.at[slot], gsem.at[slot]).wait()

    new_expert = (j == 0) | (bexp_ref[j] != bexp_ref[jnp.maximum(j - 1, 0)])

    @pl.when(new_expert & active)
    def _():
        src = lax.broadcasted_iota(I32, (pw, pw), 0)
        dst = lax.broadcasted_iota(I32, (pw, pw), 1)
        want = jnp.where(dst < pw // 2, 2 * dst, 2 * (dst - pw // 2) + 1)
        sel = (src == want).astype(BF16)
        for cb in range(n1):
            w1p_s[cb] = _dot(w1_ref[:, cb * pw:(cb + 1) * pw].astype(BF16), sel).astype(BF16)
        for cb in range(n2):
            w2_s[cb] = w2_ref[:, cb * pw:(cb + 1) * pw].astype(BF16)

    @pl.when(active)
    def _():
        x = xbuf[slot, :, 0:d]
        gate = (xbuf[slot, :, d:d + 1].astype(F32) + xbuf[slot, :, d + LANES // 2:d + LANES // 2 + 1].astype(F32))
        for n in range(n1):
            h = _dot(x, w1p_s[n]) + b1_ref[n]
            hg = jnp.minimum(h[:, :LANES], SWIGLU_LIMIT)
            hu = jnp.clip(h[:, LANES:], -SWIGLU_LIMIT, SWIGLU_LIMIT)
            act_s[n] = (hg * _sigmoid(SWIGLU_ALPHA * hg) * (hu + 1.0)).astype(BF16)
        act = jnp.concatenate([act_s[i] for i in range(n1)], axis=1)
        for n in range(n2):
            y = _dot(act, w2_s[n]) + b2_ref[n]
            y_ref[:, n * pw:(n + 1) * pw] = (y * gate).astype(y_ref.dtype)

    @pl.when(jnp.logical_not(active))
    def _():
        y_ref[...] = jnp.zeros_like(y_ref)


def _experts(block_expert, block_active, src3, hs, w1, b1p, w2, b2, blk):
    nblocks = block_expert.shape[0]
    f, d = w2.shape[1], w2.shape[2]
    pw = 2 * LANES
    n1, n2 = 2 * f // pw, d // pw
    nch = blk // MOE_CHUNK
    b1p = b1p.reshape(-1, n1, 1, pw)
    b2 = b2.reshape(-1, n2, 1, pw)
    kern = functools.partial(_experts_kernel, blk=blk, nblocks=nblocks, d=d)
    wmap = lambda j, be, ba: (be[j], 0, 0)

    def idx_spec(ahead):
        return pl.BlockSpec((1, 1, nch), lambda j, be, ba: (jnp.minimum(j + ahead, nblocks - 1), 0, 0),
                            memory_space=pltpu.SMEM)

    grid_spec = pltpu.PrefetchScalarGridSpec(
        num_scalar_prefetch=2,
        grid=(nblocks,),
        in_specs=[
            idx_spec(0), idx_spec(1), idx_spec(2),
            pl.BlockSpec(memory_space=pl.ANY),
            pl.BlockSpec((None, d, 2 * f), wmap),
            pl.BlockSpec((None, n1, 1, pw), lambda j, be, ba: (be[j], 0, 0, 0)),
            pl.BlockSpec((None, f, d), wmap),
            pl.BlockSpec((None, n2, 1, pw), lambda j, be, ba: (be[j], 0, 0, 0)),
        ],
        out_specs=pl.BlockSpec((blk, d), lambda j, be, ba: (j, 0)),
        scratch_shapes=[
            pltpu.VMEM((MOE_RING, blk, d + LANES), BF16),
            pltpu.VMEM((n1, blk, LANES), BF16),
            pltpu.VMEM((n1, d, pw), BF16),
            pltpu.VMEM((n2, f, pw), BF16),
            pltpu.SemaphoreType.DMA((MOE_RING,)),
        ],
    )
    return pl.pallas_call(
        kern,
        grid_spec=grid_spec,
        out_shape=jax.ShapeDtypeStruct((nblocks * blk, d), BF16),
        compiler_params=_params(("arbitrary",), 56),
        name="experts",
    )(block_expert, block_active, src3, src3, src3, hs, w1, b1p, w2, b2)


def _combine_kernel(dst_ref, dstn_ref, y_hbm, x1_ref, pos_ref, gate_ref, fg_ref, o_ref, ybuf, sem,
                    *, tm, cap, ntiles):
    i = pl.program_id(0)
    slot = lax.rem(i, 2)
    nch = cap // MOE_CHUNK

    def request(idx_ref, buf_slot):
        def body(c, carry):
            src = pl.multiple_of(idx_ref[0, 0, c] * MOE_CHUNK, MOE_CHUNK)
            dst = pl.multiple_of(c * MOE_CHUNK, MOE_CHUNK)
            pltpu.make_async_copy(y_hbm.at[pl.ds(src, MOE_CHUNK), :],
                                  ybuf.at[buf_slot, pl.ds(dst, MOE_CHUNK), :], sem.at[buf_slot]).start()
            return carry
        lax.fori_loop(0, nch, body, 0, unroll=8)

    @pl.when(i == 0)
    def _():
        request(dst_ref, 0)

    @pl.when(i + 1 < ntiles)
    def _():
        request(dstn_ref, 1 - slot)

    pltpu.make_async_copy(y_hbm.at[pl.ds(0, cap), :], ybuf.at[slot], sem.at[slot]).wait()

    pos = pos_ref[...]
    acc = None
    for c0 in range(0, cap, SORT_ROWS):
        col = lax.broadcasted_iota(I32, (tm, SORT_ROWS), 1) + c0
        pick = jnp.zeros((tm, SORT_ROWS), F32)
        for k in range(TOP_K):
            pick = jnp.where(col == pos[:, k:k + 1], 1.0, pick)
        part = _dot(pick.astype(BF16), ybuf[slot, c0:c0 + SORT_ROWS, :])
        acc = part if acc is None else acc + part
    x2 = x1_ref[...] + gate_ref[...] * acc
    o_ref[...] = _rms(x2) * fg_ref[...]


def _combine(dst3, y, x1, pos_t, gate5, final_g, tm, cap):
    b, t, d = x1.shape
    nt = t // tm
    ntiles = b * nt
    nch = cap // MOE_CHUNK
    kern = functools.partial(_combine_kernel, tm=tm, cap=cap, ntiles=ntiles)
    row_map = lambda i: (i // nt, i % nt, 0)
    return pl.pallas_call(
        kern,
        grid=(ntiles,),
        in_specs=[
            pl.BlockSpec((1, 1, nch), lambda i: (i, 0, 0), memory_space=pltpu.SMEM),
            pl.BlockSpec((1, 1, nch), lambda i: (jnp.minimum(i + 1, ntiles - 1), 0, 0), memory_space=pltpu.SMEM),
            pl.BlockSpec(memory_space=pl.ANY),
            pl.BlockSpec((None, tm, d), row_map),
            pl.BlockSpec((tm, TOP_K), lambda i: (i, 0)),
            pl.BlockSpec((None, 1, d), lambda i: (i // nt, 0, 0)),
            pl.BlockSpec((1, d), lambda i: (0, 0)),
        ],
        out_specs=pl.BlockSpec((None, tm, d), row_map),
        out_shape=jax.ShapeDtypeStruct((b, t, d), F32),
        scratch_shapes=[
            pltpu.VMEM((2, cap, d), BF16),
            pltpu.SemaphoreType.DMA((2,)),
        ],
        compiler_params=_params(("arbitrary",), 48),
        name="combine",
    )(dst3, dst3, y, x1, pos_t, gate5, final_g.reshape(1, d))


def _sorted_rows(tm, ne):
    need = tm * TOP_K + ne * (MOE_CHUNK - 1) + MOE_CHUNK
    return -(-need // SORT_ROWS) * SORT_ROWS


def _pick_tile(n, pref):
    tm = pref
    while n % tm:
        tm //= 2
    return tm


def kernel(x, c, ctx, c_ctx, ada_w, ada_b, norm1_g, norm2_g, w_in, lru_conv_w, lru_conv_b, lru_wa, lru_ba, lru_wx, lru_bx, lru_lam, hg_lb_logits, hg_norm_g, w_branch_a, w_branch_b, w_out, router_w, router_b, moe_w1, moe_b1, moe_w2, moe_b2, final_g):
    b, t, d = x.shape
    cx = ctx.shape[1]
    layer = 0
    w_lru = lru_conv_w.shape[2]
    qk = (w_in.shape[2] - 2 * w_lru - 2 * d) // 5
    heads = qk // HG_DK
    ne = router_w.shape[2]

    pad = (-(b + 1)) % SUBLANES
    c_all = jnp.concatenate([c, c_ctx[None, :], jnp.zeros((pad, d), F32)], axis=0)
    mod = _modulation(c_all, ada_w[layer], ada_b[layer])
    mod_l = mod[:b].reshape(b, N_MOD, 1, d)
    mod_c = mod[b].reshape(N_MOD, 1, 1, d)

    w_in_b = w_in[layer].astype(BF16)
    hg0 = 2 * w_lru
    w_m = w_in_b[:, hg0 + 5 * qk:]
    g_blocks = w_lru // LRU_BLOCK
    wg = jnp.concatenate([lru_wa[layer, 0], lru_wx[layer, 0], lru_wa[layer, 1], lru_wx[layer, 1]], axis=-1)
    wg = (0.5 * wg).astype(BF16)
    bg = jnp.concatenate([lru_ba[layer, 0], lru_bx[layer, 0], lru_ba[layer, 1], lru_bx[layer, 1]], axis=-1)
    bg = (0.5 * bg).reshape(g_blocks, 1, 4 * LRU_BLOCK)
    lb_all = jnp.cumsum(jax.nn.softmax(hg_lb_logits.astype(F32), axis=0), axis=0)
    lb = lb_all[layer].reshape(heads, 1, HG_DK)

    tm_n = _pick_tile(t, 512)
    hl, axl = _norm_proj(x, mod_l[:, 0], mod_l[:, 1], norm1_g[layer], w_in_b, 2 * w_lru, tm_n)
    hc, axc = _norm_proj(ctx, mod_c[0], mod_c[1], norm1_g[layer], w_in_b, 2 * w_lru, _pick_tile(cx, 256))
    ya = _lru(axl, axc, lru_conv_w[layer], lru_conv_b[layer], wg, bg, lru_lam[layer], b, cx, t)
    yb = _hgrn(hc, hl, w_in_b, hg0, heads, lb, hg_norm_g[layer])

    rwt = router_w[layer].T
    rb = router_b[layer].reshape(ne, 1)
    tm = _pick_tile(t, 512)
    x1, hs, pos8, cnt = _merge(
        x, hl, ya, yb, w_m, w_branch_a[layer].astype(BF16), w_branch_b[layer].astype(BF16),
        w_out[layer].astype(BF16), mod_l[:, 2], mod_l[:, 3], mod_l[:, 4], norm2_g[layer], rwt, rb, tm)

    blk = MOE_BLOCK
    ch = MOE_CHUNK
    bpc = blk // ch
    tok = b * t
    ntile = tok // tm
    cap = _sorted_rows(tm, ne)
    tch = cap // ch
    run = (cnt[:, :, 0].astype(I32) + ch - 1) // ch
    loc_end = jnp.cumsum(run, axis=1)
    loc_start = loc_end - run
    upto = jnp.cumsum(run, axis=0)
    total = upto[-1]
    padded = (total + bpc - 1) // bpc * bpc
    gend = jnp.cumsum(padded)
    gstart = gend - padded
    where_run = gstart[None, :] + upto - run
    max_chunks = tok * TOP_K // ch + ntile * ne + ne * bpc
    nblocks = -(-max_chunks // bpc)

    eids = jnp.arange(ne, dtype=I32)
    tids = jnp.arange(ntile, dtype=I32)
    g = jnp.arange(nblocks * bpc, dtype=I32)
    g_e = jnp.minimum(jnp.sum(g[:, None] >= gend[None, :], axis=1), ne - 1).astype(I32)
    is_e = g_e[:, None] == eids[None, :]
    off = g - jnp.sum(jnp.where(is_e, gstart[None, :], 0), axis=1)
    g_total = jnp.sum(jnp.where(is_e, total[None, :], 0), axis=1)
    upto_g = jnp.sum(jnp.where(is_e[:, None, :], upto[None, :, :], 0), axis=2)
    g_i = jnp.minimum(jnp.sum(off[:, None] >= upto_g, axis=1), ntile - 1).astype(I32)
    both = (g_i[:, None] == tids[None, :])[:, :, None] & is_e[:, None, :]

    def at_run(table):
        return jnp.sum(jnp.where(both, table[None, :, :], 0), axis=(1, 2))

    within = off - (at_run(upto) - at_run(run))
    zero_chunk = tch - 1
    src = jnp.where((off < g_total) & (g < gend[-1]), g_i * tch + at_run(loc_start) + within, zero_chunk)

    block_first = jnp.arange(nblocks, dtype=I32) * bpc
    block_active = block_first < gend[-1]
    block_expert = jnp.minimum(jnp.sum(block_first[:, None] >= gend[None, :], axis=1), ne - 1).astype(I32)
    last_expert = jnp.max(jnp.where(block_active, block_expert, 0))
    block_expert = jnp.where(block_active, block_expert, last_expert).astype(I32)

    c = jnp.arange(tch, dtype=I32)
    c_e = jnp.minimum(jnp.sum(c[None, :, None] >= loc_end[:, None, :], axis=2), ne - 1).astype(I32)
    is_ce = c_e[:, :, None] == eids[None, None, :]
    back = jnp.sum(jnp.where(is_ce, (where_run - loc_start)[:, None, :], 0), axis=2) + c[None, :]
    back = jnp.where(c[None, :] < loc_end[:, -1:], back, 0)

    b1 = moe_b1[layer]
    f = b1.shape[1] // 2
    b1p = b1.reshape(ne, f // LANES, LANES, 2).transpose(0, 1, 3, 2).reshape(ne, 1, 2 * f)
    y = _experts(block_expert, block_active.astype(I32), src.reshape(nblocks, 1, bpc), hs, moe_w1[layer], b1p,
                 moe_w2[layer], moe_b2[layer].reshape(ne, 1, d), blk)
    return _combine(back.reshape(ntile, 1, tch), y, x1, pos8[:TOP_K].T, mod_l[:, 5], final_g, tm, cap)
```
